```python
import math
import jax, jax.numpy as jnp
from jax import lax
import numpy as np

D_MODEL = 2048
BATCH = 8
SEQ = 4096
DEPTH = 4

CHUNK = 64
Q_BLOCK = 128
HEAD_DIM = 128
FOX_HEADS = 8
FOX_WIDTH = FOX_HEADS * HEAD_DIM
SGU_GROUPS = 8
SGU_WIDTH = SGU_GROUPS * HEAD_DIM
SGU_GROUP_DIM = SGU_WIDTH // SGU_GROUPS
SGU_SPAN = 128
GDN_HEADS = 8
GDN_WIDTH = GDN_HEADS * HEAD_DIM
GDN_CONV = 4
N_BRANCH = 3
D_FF = 5504
FFN_CONV = 3
DEEPNORM_ALPHA = (2 * DEPTH) ** 0.25
DEEPNORM_BETA = (8 * DEPTH) ** -0.25
LN_EPS = 1e-5
RMS_EPS = 1e-6

IN_SIZES = (3 * FOX_WIDTH,
            FOX_HEADS,
            2 * SGU_WIDTH,
            3 * GDN_WIDTH,
            GDN_HEADS,
            GDN_HEADS,
            GDN_WIDTH,
            N_BRANCH * D_MODEL)
IN_OFFSETS = tuple(sum(IN_SIZES[:i]) for i in range(1, len(IN_SIZES)))
N_IN = sum(IN_SIZES)
FOX_F_OFFSET = 3 * FOX_WIDTH

kernel_name = "hybrid_fox_sgu_gdn_convffn_block"


def layer_norm(x, g, b):
    xf = x.astype(jnp.float32)
    mu = jnp.mean(xf, axis=-1, keepdims=True)
    var = jnp.mean(jnp.square(xf - mu), axis=-1, keepdims=True)
    return ((xf - mu) * lax.rsqrt(var + LN_EPS) * g + b).astype(x.dtype)


def causal_depthwise_conv(x, w):
    k = w.shape[0]
    c = x.shape[-1]
    return lax.conv_general_dilated(
        x, w[:, None, :].astype(x.dtype), window_strides=(1,),
        padding=[(k - 1, 0)], dimension_numbers=('NWC', 'WIO', 'NWC'),
        feature_group_count=c)


def forgetting_attention(q, k, v, log_f):
    b, s, h, dh = q.shape
    nb = s // Q_BLOCK
    c = jnp.cumsum(log_f, axis=1).transpose(0, 2, 1)
    qb = q.reshape(b, nb, Q_BLOCK, h, dh).transpose(1, 0, 3, 2, 4)
    cqb = c.reshape(b, h, nb, Q_BLOCK).transpose(2, 0, 1, 3)
    kpos = jnp.arange(s)
    scale = dh ** -0.5

    def block(args):
        qi, cqi, i = args
        logits = jnp.einsum('bhqd,bkhd->bhqk', qi, k).astype(jnp.float32) * scale
        logits = logits + cqi[..., None] - c[:, :, None, :]
        qpos = i * Q_BLOCK + jnp.arange(Q_BLOCK)
        mask = qpos[:, None] >= kpos[None, :]
        p = jax.nn.softmax(jnp.where(mask, logits, -jnp.inf), axis=-1)
        return jnp.einsum('bhqk,bkhd->bqhd', p.astype(v.dtype), v)

    out = lax.map(block, (qb, cqb, jnp.arange(nb)))
    return out.transpose(1, 0, 2, 3, 4).reshape(b, s, h * dh)


def spatial_gating(u, v, ln_g, ln_b, w_s, b_s):
    b, s, w = v.shape
    n = s // SGU_SPAN
    shape5 = (b, n, SGU_SPAN, SGU_GROUPS, SGU_GROUP_DIM)
    vg = layer_norm(v.reshape(shape5), ln_g.reshape(SGU_GROUPS, SGU_GROUP_DIM),
                    ln_b.reshape(SGU_GROUPS, SGU_GROUP_DIM))
    pos = jnp.arange(SGU_SPAN) // CHUNK
    mask = pos[:, None] >= pos[None, :]
    mixed = jnp.einsum('gts,bnsgc->bntgc', jnp.where(mask, w_s, 0.0).astype(vg.dtype), vg)
    mixed = mixed + b_s.T[:, :, None]
    return (u.reshape(shape5) * mixed).reshape(b, s, w)


def l2_normalize(t):
    return t * lax.rsqrt(jnp.sum(jnp.square(t), axis=-1, keepdims=True) + RMS_EPS)


def gated_delta_rule(q, k, v, g, beta):
    b, s, h, dh = q.shape
    n = s // CHUNK
    to_c = lambda t: t.transpose(0, 2, 1, 3).reshape(b, h, n, CHUNK, t.shape[-1])
    q, k, v = to_c(q) * dh ** -0.5, to_c(k), to_c(v)
    g = g.transpose(0, 2, 1).reshape(b, h, n, CHUNK)
    beta = beta.transpose(0, 2, 1).reshape(b, h, n, CHUNK)
    gc = jnp.cumsum(g, axis=-1)
    causal = jnp.tril(jnp.ones((CHUNK, CHUNK), bool))
    strict = jnp.tril(jnp.ones((CHUNK, CHUNK), bool), -1)
    decay = jnp.exp(jnp.where(causal, gc[..., :, None] - gc[..., None, :], -jnp.inf))
    k_beta = k * beta[..., None]
    a_kk = jnp.where(strict, jnp.einsum('bhnid,bhnjd->bhnij', k_beta, k) * decay, 0.0)
    eye = jnp.eye(CHUNK, dtype=q.dtype)
    rhs = jnp.concatenate([v * beta[..., None], k_beta * jnp.exp(gc)[..., None]], axis=-1)
    sol = lax.linalg.triangular_solve(eye + a_kk, rhs, left_side=True, lower=True)
    u, w = sol[..., :dh], sol[..., dh:]
    qk = jnp.where(causal, jnp.einsum('bhnid,bhnjd->bhnij', q, k) * decay, 0.0)
    g_last = gc[..., -1]
    k_dec = k * jnp.exp(g_last[..., None] - gc)[..., None]
    q_dec = q * jnp.exp(gc)[..., None]

    def step(state, xs):
        qd, kd, qk_i, u_i, w_i, gl = xs
        v_new = u_i - jnp.einsum('bhcd,bhde->bhce', w_i, state)
        o = jnp.einsum('bhcd,bhde->bhce', qd, state) + jnp.einsum('bhij,bhje->bhie', qk_i, v_new)
        state = state * jnp.exp(gl)[..., None, None] + jnp.einsum('bhcd,bhce->bhde', kd, v_new)
        return state, o

    xs = tuple(jnp.moveaxis(t, 2, 0) for t in (q_dec, k_dec, qk, u, w, g_last))
    state0 = jnp.zeros((b, h, dh, dh), jnp.float32)
    _, o = lax.scan(step, state0, xs)
    return jnp.moveaxis(o, 0, 2).reshape(b, h, s, dh).transpose(0, 2, 1, 3)


def gdn_mixer(qkv, a, beta_logit, gate, conv_w, a_log, dt_bias, norm_g):
    b, s, _ = qkv.shape
    qkv = jax.nn.silu(causal_depthwise_conv(qkv, conv_w)).astype(jnp.float32)
    q, k, v = [t.reshape(b, s, GDN_HEADS, HEAD_DIM) for t in jnp.split(qkv, 3, axis=-1)]
    q, k = l2_normalize(q), l2_normalize(k)
    g = -jnp.exp(a_log.astype(jnp.float32)) * jax.nn.softplus((a + dt_bias).astype(jnp.float32))
    beta = jax.nn.sigmoid(beta_logit.astype(jnp.float32))
    o = gated_delta_rule(q, k, v, g, beta)
    o = o * lax.rsqrt(jnp.mean(jnp.square(o), axis=-1, keepdims=True) + RMS_EPS) * norm_g
    o = o * jax.nn.silu(gate.reshape(b, s, GDN_HEADS, HEAD_DIM).astype(jnp.float32))
    return o.astype(gate.dtype).reshape(b, s, GDN_WIDTH)


def token_mixing(x, w_in, b_in, sgu_ln_g, sgu_ln_b, sgu_w, sgu_b, gdn_conv_w,
                 gdn_a_log, gdn_dt_bias, gdn_norm_g, w_proj_a, w_proj_b, w_proj_c, w_out):
    bsz, s, _ = x.shape
    proj = x @ w_in + b_in
    fox_qkv, fox_f, sgu_uv, gdn_qkv, gdn_a, gdn_b, gdn_gate, gates = jnp.split(
        proj, list(IN_OFFSETS), axis=-1)
    fq, fk, fv = [t.reshape(bsz, s, FOX_HEADS, HEAD_DIM) for t in jnp.split(fox_qkv, 3, axis=-1)]
    y_a = forgetting_attention(fq, fk, fv, jax.nn.log_sigmoid(fox_f.astype(jnp.float32)))
    su, sv = jnp.split(sgu_uv, 2, axis=-1)
    y_b = spatial_gating(su, sv, sgu_ln_g, sgu_ln_b, sgu_w, sgu_b)
    y_c = gdn_mixer(gdn_qkv, gdn_a, gdn_b, gdn_gate, gdn_conv_w, gdn_a_log, gdn_dt_bias, gdn_norm_g)
    gt = jax.nn.sigmoid(gates).reshape(bsz, s, N_BRANCH, D_MODEL)
    merged = (gt[:, :, 0] * (y_a @ w_proj_a) + gt[:, :, 1] * (y_b @ w_proj_b)
              + gt[:, :, 2] * (y_c @ w_proj_c))
    return merged @ w_out


def conv_ffn(x, w_up, conv_w, conv_b, w_down):
    h = causal_depthwise_conv(x @ w_up, conv_w) + conv_b
    h_gate, h_val = jnp.split(h, 2, axis=-1)
    return (jax.nn.silu(h_gate) * h_val) @ w_down


def _fwd_setup_inputs(seed: int = 0) -> dict:
    key = jax.random.key(seed)
    ks = jax.random.split(key, 24)
    L, D = DEPTH, D_MODEL
    f32 = jnp.float32
    nrm = lambda k, shape, scale: jax.random.normal(k, shape, f32) * scale
    x = nrm(ks[0], (BATCH, SEQ, D), 1.0)
    w_in = nrm(ks[1], (L, D, N_IN), D ** -0.5)
    b_in = nrm(ks[2], (L, N_IN), 0.01)
    fox_fb = jax.random.uniform(ks[3], (L, FOX_HEADS), f32, 1.0, 5.0)
    b_in = b_in.at[:, FOX_F_OFFSET:FOX_F_OFFSET + FOX_HEADS].add(fox_fb)
    sgu_ln_g = 1.0 + nrm(ks[4], (L, SGU_WIDTH), 0.01)
    sgu_ln_b = nrm(ks[5], (L, SGU_WIDTH), 0.01)
    sgu_w = nrm(ks[6], (L, SGU_GROUPS, SGU_SPAN, SGU_SPAN), SGU_SPAN ** -0.5)
    sgu_b = 1.0 + nrm(ks[7], (L, SGU_GROUPS, SGU_SPAN), 0.01)
    gdn_conv_w = nrm(ks[8], (L, GDN_CONV, 3 * GDN_WIDTH), GDN_CONV ** -0.5)
    gdn_a_log = jnp.log(jax.random.uniform(ks[9], (L, GDN_HEADS), f32, 1.0, 16.0))
    dt = jnp.exp(jax.random.uniform(ks[10], (L, GDN_HEADS), f32, math.log(1e-3), math.log(1e-1)))
    gdn_dt_bias = dt + jnp.log(-jnp.expm1(-dt))
    gdn_norm_g = 1.0 + nrm(ks[11], (L, HEAD_DIM), 0.01)
    w_proj_a = nrm(ks[12], (L, FOX_WIDTH, D), FOX_WIDTH ** -0.5 * DEEPNORM_BETA)
    w_proj_b = nrm(ks[13], (L, SGU_WIDTH, D), SGU_WIDTH ** -0.5 * DEEPNORM_BETA)
    w_proj_c = nrm(ks[14], (L, GDN_WIDTH, D), GDN_WIDTH ** -0.5 * DEEPNORM_BETA)
    w_out = nrm(ks[15], (L, D, D), D ** -0.5 * DEEPNORM_BETA)
    ln1_g = 1.0 + nrm(ks[16], (L, D), 0.01)
    ln1_b = nrm(ks[17], (L, D), 0.01)
    ffn_w_up = nrm(ks[18], (L, D, 2 * D_FF), D ** -0.5)
    ffn_conv_w = nrm(ks[19], (L, FFN_CONV, 2 * D_FF), FFN_CONV ** -0.5)
    ffn_conv_b = nrm(ks[20], (L, 2 * D_FF), 0.01)
    ffn_w_down = nrm(ks[21], (L, D_FF, D), D_FF ** -0.5 * DEEPNORM_BETA)
    ln2_g = 1.0 + nrm(ks[22], (L, D), 0.01)
    ln2_b = nrm(ks[23], (L, D), 0.01)
    return {"x": x, "w_in": w_in, "b_in": b_in, "sgu_ln_g": sgu_ln_g, "sgu_ln_b": sgu_ln_b,
            "sgu_w": sgu_w, "sgu_b": sgu_b, "gdn_conv_w": gdn_conv_w, "gdn_a_log": gdn_a_log,
            "gdn_dt_bias": gdn_dt_bias, "gdn_norm_g": gdn_norm_g, "w_proj_a": w_proj_a,
            "w_proj_b": w_proj_b, "w_proj_c": w_proj_c, "w_out": w_out, "ln1_g": ln1_g,
            "ln1_b": ln1_b, "ffn_w_up": ffn_w_up, "ffn_conv_w": ffn_conv_w,
            "ffn_conv_b": ffn_conv_b, "ffn_w_down": ffn_w_down, "ln2_g": ln2_g, "ln2_b": ln2_b}


def _fwd_reference(x, w_in, b_in, sgu_ln_g, sgu_ln_b, sgu_w, sgu_b, gdn_conv_w, gdn_a_log,
              gdn_dt_bias, gdn_norm_g, w_proj_a, w_proj_b, w_proj_c, w_out, ln1_g, ln1_b,
              ffn_w_up, ffn_conv_w, ffn_conv_b, ffn_w_down, ln2_g, ln2_b):
    for l in range(DEPTH):
        mix = token_mixing(x, w_in[l], b_in[l], sgu_ln_g[l], sgu_ln_b[l], sgu_w[l], sgu_b[l],
                           gdn_conv_w[l], gdn_a_log[l], gdn_dt_bias[l], gdn_norm_g[l],
                           w_proj_a[l], w_proj_b[l], w_proj_c[l], w_out[l])
        x = layer_norm(DEEPNORM_ALPHA * x + mix, ln1_g[l], ln1_b[l])
        ffn = conv_ffn(x, ffn_w_up[l], ffn_conv_w[l], ffn_conv_b[l], ffn_w_down[l])
        x = layer_norm(DEEPNORM_ALPHA * x + ffn, ln2_g[l], ln2_b[l])
    return x


import jax as _jax
import jax.numpy as _jnp

TWIN_FORMAT = 'train_step'
FWD_PARAMS = ['x', 'w_in', 'b_in', 'sgu_ln_g', 'sgu_ln_b', 'sgu_w', 'sgu_b', 'gdn_conv_w', 'gdn_a_log', 'gdn_dt_bias', 'gdn_norm_g', 'w_proj_a', 'w_proj_b', 'w_proj_c', 'w_out', 'ln1_g', 'ln1_b', 'ffn_w_up', 'ffn_conv_w', 'ffn_conv_b', 'ffn_w_down', 'ln2_g', 'ln2_b']
TWIN_WEIGHTS = ['w_in', 'b_in', 'sgu_ln_g', 'sgu_ln_b', 'sgu_w', 'sgu_b', 'gdn_conv_w', 'gdn_a_log', 'gdn_dt_bias', 'gdn_norm_g', 'w_proj_a', 'w_proj_b', 'w_proj_c', 'w_out', 'ln1_g', 'ln1_b', 'ffn_w_up', 'ffn_conv_w', 'ffn_conv_b', 'ffn_w_down', 'ln2_g', 'ln2_b']
TWIN_DIFF_INPUT = 'x'
TWIN_INPUTS = ['x', 'w_in', 'b_in', 'sgu_ln_g', 'sgu_ln_b', 'sgu_w', 'sgu_b', 'gdn_conv_w', 'gdn_a_log', 'gdn_dt_bias', 'gdn_norm_g', 'w_proj_a', 'w_proj_b', 'w_proj_c', 'w_out', 'ln1_g', 'ln1_b', 'ffn_w_up', 'ffn_conv_w', 'ffn_conv_b', 'ffn_w_down', 'ln2_g', 'ln2_b', 'loss_target', 'm_w_in', 'm_b_in', 'm_sgu_ln_g', 'm_sgu_ln_b', 'm_sgu_w', 'm_sgu_b', 'm_gdn_conv_w', 'm_gdn_a_log', 'm_gdn_dt_bias', 'm_gdn_norm_g', 'm_w_proj_a', 'm_w_proj_b', 'm_w_proj_c', 'm_w_out', 'm_ln1_g', 'm_ln1_b', 'm_ffn_w_up', 'm_ffn_conv_w', 'm_ffn_conv_b', 'm_ffn_w_down', 'm_ln2_g', 'm_ln2_b', 'v_w_in', 'v_b_in', 'v_sgu_ln_g', 'v_sgu_ln_b', 'v_sgu_w', 'v_sgu_b', 'v_gdn_conv_w', 'v_gdn_a_log', 'v_gdn_dt_bias', 'v_gdn_norm_g', 'v_w_proj_a', 'v_w_proj_b', 'v_w_proj_c', 'v_w_out', 'v_ln1_g', 'v_ln1_b', 'v_ffn_w_up', 'v_ffn_conv_w', 'v_ffn_conv_b', 'v_ffn_w_down', 'v_ln2_g', 'v_ln2_b']
TWIN_OUTPUTS = ['loss', 'grad_x', 'grad_w_in', 'grad_b_in', 'grad_sgu_ln_g', 'grad_sgu_ln_b', 'grad_sgu_w', 'grad_sgu_b', 'grad_gdn_conv_w', 'grad_gdn_a_log', 'grad_gdn_dt_bias', 'grad_gdn_norm_g', 'grad_w_proj_a', 'grad_w_proj_b', 'grad_w_proj_c', 'grad_w_out', 'grad_ln1_g', 'grad_ln1_b', 'grad_ffn_w_up', 'grad_ffn_conv_w', 'grad_ffn_conv_b', 'grad_ffn_w_down', 'grad_ln2_g', 'grad_ln2_b', 'delta_w_in', 'delta_b_in', 'delta_sgu_ln_g', 'delta_sgu_ln_b', 'delta_sgu_w', 'delta_sgu_b', 'delta_gdn_conv_w', 'delta_gdn_a_log', 'delta_gdn_dt_bias', 'delta_gdn_norm_g', 'delta_w_proj_a', 'delta_w_proj_b', 'delta_w_proj_c', 'delta_w_out', 'delta_ln1_g', 'delta_ln1_b', 'delta_ffn_w_up', 'delta_ffn_conv_w', 'delta_ffn_conv_b', 'delta_ffn_w_down', 'delta_ln2_g', 'delta_ln2_b', 'new_m_w_in', 'new_m_b_in', 'new_m_sgu_ln_g', 'new_m_sgu_ln_b', 'new_m_sgu_w', 'new_m_sgu_b', 'new_m_gdn_conv_w', 'new_m_gdn_a_log', 'new_m_gdn_dt_bias', 'new_m_gdn_norm_g', 'new_m_w_proj_a', 'new_m_w_proj_b', 'new_m_w_proj_c', 'new_m_w_out', 'new_m_ln1_g', 'new_m_ln1_b', 'new_m_ffn_w_up', 'new_m_ffn_conv_w', 'new_m_ffn_conv_b', 'new_m_ffn_w_down', 'new_m_ln2_g', 'new_m_ln2_b', 'new_v_w_in', 'new_v_b_in', 'new_v_sgu_ln_g', 'new_v_sgu_ln_b', 'new_v_sgu_w', 'new_v_sgu_b', 'new_v_gdn_conv_w', 'new_v_gdn_a_log', 'new_v_gdn_dt_bias', 'new_v_gdn_norm_g', 'new_v_w_proj_a', 'new_v_w_proj_b', 'new_v_w_proj_c', 'new_v_w_out', 'new_v_ln1_g', 'new_v_ln1_b', 'new_v_ffn_w_up', 'new_v_ffn_conv_w', 'new_v_ffn_conv_b', 'new_v_ffn_w_down', 'new_v_ln2_g', 'new_v_ln2_b']
TWIN_LEAF_KINDS = {'loss': 'loss', 'grad_x': 'grad_x', 'grad_w_in': 'grad_w', 'grad_b_in': 'grad_w', 'grad_sgu_ln_g': 'grad_w', 'grad_sgu_ln_b': 'grad_w', 'grad_sgu_w': 'grad_w', 'grad_sgu_b': 'grad_w', 'grad_gdn_conv_w': 'grad_w', 'grad_gdn_a_log': 'grad_w', 'grad_gdn_dt_bias': 'grad_w', 'grad_gdn_norm_g': 'grad_w', 'grad_w_proj_a': 'grad_w', 'grad_w_proj_b': 'grad_w', 'grad_w_proj_c': 'grad_w', 'grad_w_out': 'grad_w', 'grad_ln1_g': 'grad_w', 'grad_ln1_b': 'grad_w', 'grad_ffn_w_up': 'grad_w', 'grad_ffn_conv_w': 'grad_w', 'grad_ffn_conv_b': 'grad_w', 'grad_ffn_w_down': 'grad_w', 'grad_ln2_g': 'grad_w', 'grad_ln2_b': 'grad_w', 'delta_w_in': 'delta_w', 'delta_b_in': 'delta_w', 'delta_sgu_ln_g': 'delta_w', 'delta_sgu_ln_b': 'delta_w', 'delta_sgu_w': 'delta_w', 'delta_sgu_b': 'delta_w', 'delta_gdn_conv_w': 'delta_w', 'delta_gdn_a_log': 'delta_w', 'delta_gdn_dt_bias': 'delta_w', 'delta_gdn_norm_g': 'delta_w', 'delta_w_proj_a': 'delta_w', 'delta_w_proj_b': 'delta_w', 'delta_w_proj_c': 'delta_w', 'delta_w_out': 'delta_w', 'delta_ln1_g': 'delta_w', 'delta_ln1_b': 'delta_w', 'delta_ffn_w_up': 'delta_w', 'delta_ffn_conv_w': 'delta_w', 'delta_ffn_conv_b': 'delta_w', 'delta_ffn_w_down': 'delta_w', 'delta_ln2_g': 'delta_w', 'delta_ln2_b': 'delta_w', 'new_m_w_in': 'new_m', 'new_m_b_in': 'new_m', 'new_m_sgu_ln_g': 'new_m', 'new_m_sgu_ln_b': 'new_m', 'new_m_sgu_w': 'new_m', 'new_m_sgu_b': 'new_m', 'new_m_gdn_conv_w': 'new_m', 'new_m_gdn_a_log': 'new_m', 'new_m_gdn_dt_bias': 'new_m', 'new_m_gdn_norm_g': 'new_m', 'new_m_w_proj_a': 'new_m', 'new_m_w_proj_b': 'new_m', 'new_m_w_proj_c': 'new_m', 'new_m_w_out': 'new_m', 'new_m_ln1_g': 'new_m', 'new_m_ln1_b': 'new_m', 'new_m_ffn_w_up': 'new_m', 'new_m_ffn_conv_w': 'new_m', 'new_m_ffn_conv_b': 'new_m', 'new_m_ffn_w_down': 'new_m', 'new_m_ln2_g': 'new_m', 'new_m_ln2_b': 'new_m', 'new_v_w_in': 'new_v', 'new_v_b_in': 'new_v', 'new_v_sgu_ln_g': 'new_v', 'new_v_sgu_ln_b': 'new_v', 'new_v_sgu_w': 'new_v', 'new_v_sgu_b': 'new_v', 'new_v_gdn_conv_w': 'new_v', 'new_v_gdn_a_log': 'new_v', 'new_v_gdn_dt_bias': 'new_v', 'new_v_gdn_norm_g': 'new_v', 'new_v_w_proj_a': 'new_v', 'new_v_w_proj_b': 'new_v', 'new_v_w_proj_c': 'new_v', 'new_v_w_out': 'new_v', 'new_v_ln1_g': 'new_v', 'new_v_ln1_b': 'new_v', 'new_v_ffn_w_up': 'new_v', 'new_v_ffn_conv_w': 'new_v', 'new_v_ffn_conv_b': 'new_v', 'new_v_ffn_w_down': 'new_v', 'new_v_ln2_g': 'new_v', 'new_v_ln2_b': 'new_v'}


def _forward(args):
    return _fwd_reference(*[args[k] for k in FWD_PARAMS])


def _output_shape():
    def fwd():
        inp = _fwd_setup_inputs(0)
        return _fwd_reference(*[inp[k] for k in FWD_PARAMS])
    out = _jax.eval_shape(fwd)
    return out.shape, out.dtype

N_MICROBATCH = 1
ADAM_LR = 0.001
ADAM_B1 = 0.9
ADAM_B2 = 0.999
ADAM_EPS = 1e-08
ADAM_WD = 0.01
ADAM_STEP = 10
PER_EXAMPLE_BATCH_AXIS = {'x': 0, 'loss_target': 0}
SHARED_INPUTS = []
_WEIGHT_DTYPES = {'w_in': _jnp.float32, 'b_in': _jnp.float32, 'sgu_ln_g': _jnp.float32, 'sgu_ln_b': _jnp.float32, 'sgu_w': _jnp.float32, 'sgu_b': _jnp.float32, 'gdn_conv_w': _jnp.float32, 'gdn_a_log': _jnp.float32, 'gdn_dt_bias': _jnp.float32, 'gdn_norm_g': _jnp.float32, 'w_proj_a': _jnp.float32, 'w_proj_b': _jnp.float32, 'w_proj_c': _jnp.float32, 'w_out': _jnp.float32, 'ln1_g': _jnp.float32, 'ln1_b': _jnp.float32, 'ffn_w_up': _jnp.float32, 'ffn_conv_w': _jnp.float32, 'ffn_conv_b': _jnp.float32, 'ffn_w_down': _jnp.float32, 'ln2_g': _jnp.float32, 'ln2_b': _jnp.float32}
MOMENT_SCALE = {'w_in': 3.844143e-03, 'b_in': 5.427781e-03, 'sgu_ln_g': 6.448307e-03, 'sgu_ln_b': 6.297402e-03, 'sgu_w': 6.381696e-03, 'sgu_b': 7.553268e-03, 'gdn_conv_w': 3.387716e-03, 'gdn_a_log': 1.852057e-02, 'gdn_dt_bias': 1.802906e-02, 'gdn_norm_g': 1.203543e-02, 'w_proj_a': 4.591051e-03, 'w_proj_b': 1.637293e-02, 'w_proj_c': 7.467437e-03, 'w_out': 1.845478e-02, 'ln1_g': 2.570233e-01, 'ln1_b': 1.548573e-01, 'ffn_w_up': 8.501726e-03, 'ffn_conv_w': 8.475049e-03, 'ffn_conv_b': 8.725374e-03, 'ffn_w_down': 3.261050e-02, 'ln2_g': 8.005853e+00, 'ln2_b': 2.546708e-01}


def _to_microbatches(a, axis):
    t = _jnp.moveaxis(a, axis, 0)
    t = t.reshape((N_MICROBATCH, t.shape[0] // N_MICROBATCH) + t.shape[1:])
    return _jnp.moveaxis(t, 1, axis + 1)


def setup_inputs(seed: int = 0) -> dict:
    inp = _fwd_setup_inputs(seed)
    key = _jax.random.fold_in(_jax.random.key(seed), 7919)
    shape, _ = _output_shape()
    out = dict(inp)
    out["loss_target"] = _jax.random.normal(_jax.random.fold_in(key, 0), shape, _jnp.float32)
    for i, name in enumerate(TWIN_WEIGHTS):
        w = inp[name].astype(_jnp.float32)
        if MOMENT_SCALE is None:
            s = _jnp.sqrt(_jnp.mean(_jnp.square(w)) + 1e-30)
        else:
            s = MOMENT_SCALE[name]
        km, kv = _jax.random.split(_jax.random.fold_in(key, i + 1))
        out[name] = w
        out["m_" + name] = s * _jax.random.normal(km, w.shape, _jnp.float32)
        out["v_" + name] = (s * s) * _jax.random.uniform(kv, w.shape, _jnp.float32, 0.5, 1.5)
    if N_MICROBATCH > 1:
        for name, axis in PER_EXAMPLE_BATCH_AXIS.items():
            out[name] = _to_microbatches(out[name], axis)
    return {'x': out['x'], 'w_in': out['w_in'], 'b_in': out['b_in'], 'sgu_ln_g': out['sgu_ln_g'], 'sgu_ln_b': out['sgu_ln_b'], 'sgu_w': out['sgu_w'], 'sgu_b': out['sgu_b'], 'gdn_conv_w': out['gdn_conv_w'], 'gdn_a_log': out['gdn_a_log'], 'gdn_dt_bias': out['gdn_dt_bias'], 'gdn_norm_g': out['gdn_norm_g'], 'w_proj_a': out['w_proj_a'], 'w_proj_b': out['w_proj_b'], 'w_proj_c': out['w_proj_c'], 'w_out': out['w_out'], 'ln1_g': out['ln1_g'], 'ln1_b': out['ln1_b'], 'ffn_w_up': out['ffn_w_up'], 'ffn_conv_w': out['ffn_conv_w'], 'ffn_conv_b': out['ffn_conv_b'], 'ffn_w_down': out['ffn_w_down'], 'ln2_g': out['ln2_g'], 'ln2_b': out['ln2_b'], 'loss_target': out['loss_target'], 'm_w_in': out['m_w_in'], 'm_b_in': out['m_b_in'], 'm_sgu_ln_g': out['m_sgu_ln_g'], 'm_sgu_ln_b': out['m_sgu_ln_b'], 'm_sgu_w': out['m_sgu_w'], 'm_sgu_b': out['m_sgu_b'], 'm_gdn_conv_w': out['m_gdn_conv_w'], 'm_gdn_a_log': out['m_gdn_a_log'], 'm_gdn_dt_bias': out['m_gdn_dt_bias'], 'm_gdn_norm_g': out['m_gdn_norm_g'], 'm_w_proj_a': out['m_w_proj_a'], 'm_w_proj_b': out['m_w_proj_b'], 'm_w_proj_c': out['m_w_proj_c'], 'm_w_out': out['m_w_out'], 'm_ln1_g': out['m_ln1_g'], 'm_ln1_b': out['m_ln1_b'], 'm_ffn_w_up': out['m_ffn_w_up'], 'm_ffn_conv_w': out['m_ffn_conv_w'], 'm_ffn_conv_b': out['m_ffn_conv_b'], 'm_ffn_w_down': out['m_ffn_w_down'], 'm_ln2_g': out['m_ln2_g'], 'm_ln2_b': out['m_ln2_b'], 'v_w_in': out['v_w_in'], 'v_b_in': out['v_b_in'], 'v_sgu_ln_g': out['v_sgu_ln_g'], 'v_sgu_ln_b': out['v_sgu_ln_b'], 'v_sgu_w': out['v_sgu_w'], 'v_sgu_b': out['v_sgu_b'], 'v_gdn_conv_w': out['v_gdn_conv_w'], 'v_gdn_a_log': out['v_gdn_a_log'], 'v_gdn_dt_bias': out['v_gdn_dt_bias'], 'v_gdn_norm_g': out['v_gdn_norm_g'], 'v_w_proj_a': out['v_w_proj_a'], 'v_w_proj_b': out['v_w_proj_b'], 'v_w_proj_c': out['v_w_proj_c'], 'v_w_out': out['v_w_out'], 'v_ln1_g': out['v_ln1_g'], 'v_ln1_b': out['v_ln1_b'], 'v_ffn_w_up': out['v_ffn_w_up'], 'v_ffn_conv_w': out['v_ffn_conv_w'], 'v_ffn_conv_b': out['v_ffn_conv_b'], 'v_ffn_w_down': out['v_ffn_w_down'], 'v_ln2_g': out['v_ln2_g'], 'v_ln2_b': out['v_ln2_b']}


def _loss(weights, diff, rest, loss_target):
    with _jax.named_scope("forward"):
        args = {**rest, TWIN_DIFF_INPUT: diff, **{k: w.astype(_WEIGHT_DTYPES[k]) for k, w in weights.items()}}
        y = _forward(args)
    with _jax.named_scope("loss_head"):
        err = _jnp.square(y.astype(_jnp.float32) - loss_target)
        return 0.5 * _jnp.sum(_jnp.mean(err, axis=-1)) if err.ndim else 0.5 * err


def _adamw(w, g, m, v):
    m = ADAM_B1 * m + (1.0 - ADAM_B1) * g
    v = ADAM_B2 * v + (1.0 - ADAM_B2) * _jnp.square(g)
    m_hat = m / (1.0 - ADAM_B1 ** ADAM_STEP)
    v_hat = v / (1.0 - ADAM_B2 ** ADAM_STEP)
    delta = -ADAM_LR * (m_hat / (_jnp.sqrt(v_hat) + ADAM_EPS) + ADAM_WD * w)
    return delta, m, v


def reference(x, w_in, b_in, sgu_ln_g, sgu_ln_b, sgu_w, sgu_b, gdn_conv_w, gdn_a_log, gdn_dt_bias, gdn_norm_g, w_proj_a, w_proj_b, w_proj_c, w_out, ln1_g, ln1_b, ffn_w_up, ffn_conv_w, ffn_conv_b, ffn_w_down, ln2_g, ln2_b, loss_target, m_w_in, m_b_in, m_sgu_ln_g, m_sgu_ln_b, m_sgu_w, m_sgu_b, m_gdn_conv_w, m_gdn_a_log, m_gdn_dt_bias, m_gdn_norm_g, m_w_proj_a, m_w_proj_b, m_w_proj_c, m_w_out, m_ln1_g, m_ln1_b, m_ffn_w_up, m_ffn_conv_w, m_ffn_conv_b, m_ffn_w_down, m_ln2_g, m_ln2_b, v_w_in, v_b_in, v_sgu_ln_g, v_sgu_ln_b, v_sgu_w, v_sgu_b, v_gdn_conv_w, v_gdn_a_log, v_gdn_dt_bias, v_gdn_norm_g, v_w_proj_a, v_w_proj_b, v_w_proj_c, v_w_out, v_ln1_g, v_ln1_b, v_ffn_w_up, v_ffn_conv_w, v_ffn_conv_b, v_ffn_w_down, v_ln2_g, v_ln2_b):
    given = dict(x=x, w_in=w_in, b_in=b_in, sgu_ln_g=sgu_ln_g, sgu_ln_b=sgu_ln_b, sgu_w=sgu_w, sgu_b=sgu_b, gdn_conv_w=gdn_conv_w, gdn_a_log=gdn_a_log, gdn_dt_bias=gdn_dt_bias, gdn_norm_g=gdn_norm_g, w_proj_a=w_proj_a, w_proj_b=w_proj_b, w_proj_c=w_proj_c, w_out=w_out, ln1_g=ln1_g, ln1_b=ln1_b, ffn_w_up=ffn_w_up, ffn_conv_w=ffn_conv_w, ffn_conv_b=ffn_conv_b, ffn_w_down=ffn_w_down, ln2_g=ln2_g, ln2_b=ln2_b, loss_target=loss_target, m_w_in=m_w_in, m_b_in=m_b_in, m_sgu_ln_g=m_sgu_ln_g, m_sgu_ln_b=m_sgu_ln_b, m_sgu_w=m_sgu_w, m_sgu_b=m_sgu_b, m_gdn_conv_w=m_gdn_conv_w, m_gdn_a_log=m_gdn_a_log, m_gdn_dt_bias=m_gdn_dt_bias, m_gdn_norm_g=m_gdn_norm_g, m_w_proj_a=m_w_proj_a, m_w_proj_b=m_w_proj_b, m_w_proj_c=m_w_proj_c, m_w_out=m_w_out, m_ln1_g=m_ln1_g, m_ln1_b=m_ln1_b, m_ffn_w_up=m_ffn_w_up, m_ffn_conv_w=m_ffn_conv_w, m_ffn_conv_b=m_ffn_conv_b, m_ffn_w_down=m_ffn_w_down, m_ln2_g=m_ln2_g, m_ln2_b=m_ln2_b, v_w_in=v_w_in, v_b_in=v_b_in, v_sgu_ln_g=v_sgu_ln_g, v_sgu_ln_b=v_sgu_ln_b, v_sgu_w=v_sgu_w, v_sgu_b=v_sgu_b, v_gdn_conv_w=v_gdn_conv_w, v_gdn_a_log=v_gdn_a_log, v_gdn_dt_bias=v_gdn_dt_bias, v_gdn_norm_g=v_gdn_norm_g, v_w_proj_a=v_w_proj_a, v_w_proj_b=v_w_proj_b, v_w_proj_c=v_w_proj_c, v_w_out=v_w_out, v_ln1_g=v_ln1_g, v_ln1_b=v_ln1_b, v_ffn_w_up=v_ffn_w_up, v_ffn_conv_w=v_ffn_conv_w, v_ffn_conv_b=v_ffn_conv_b, v_ffn_w_down=v_ffn_w_down, v_ln2_g=v_ln2_g, v_ln2_b=v_ln2_b)
    weights = {n: given[n] for n in TWIN_WEIGHTS}
    shared = {n: given[n] for n in SHARED_INPUTS}
    per_example = {n: given[n] for n in ['x']}
    grad_fn = _jax.value_and_grad(_loss, argnums=(0, 1))

    def one_microbatch(ex, loss_target):
        ex = dict(ex)
        diff = ex.pop(TWIN_DIFF_INPUT)
        return grad_fn(weights, diff, {**shared, **ex}, loss_target)

    if N_MICROBATCH == 1:
        loss, (grad_w, grad_x) = one_microbatch(per_example, given["loss_target"])
    else:
        def body(carry, xs):
            loss_sum, grad_sum = carry
            l_k, (gw_k, gx_k) = one_microbatch(xs[0], xs[1])
            with _jax.named_scope("update"):
                return (loss_sum + l_k, _jax.tree.map(_jnp.add, grad_sum, gw_k)), gx_k

        init = (_jnp.zeros((), _jnp.float32), _jax.tree.map(_jnp.zeros_like, weights))
        (loss, grad_w), grad_x = _jax.lax.scan(body, init, (per_example, given["loss_target"]))
    with _jax.named_scope("update"):
        delta_w, new_m, new_v = {}, {}, {}
        for n in TWIN_WEIGHTS:
            delta_w[n], new_m[n], new_v[n] = _adamw(weights[n], grad_w[n], given["m_" + n], given["v_" + n])
    return (loss, grad_x, *[grad_w[n] for n in TWIN_WEIGHTS], *[delta_w[n] for n in TWIN_WEIGHTS],
            *[new_m[n] for n in TWIN_WEIGHTS], *[new_v[n] for n in TWIN_WEIGHTS])
```

```python
import functools
import math

import jax
import jax.numpy as jnp
from jax import lax
from jax.experimental import pallas as pl
from jax.experimental.pallas import tpu as pltpu

f32 = jnp.float32
bf16 = jnp.bfloat16
HIGHEST = lax.Precision.HIGHEST
MESH = pl.DeviceIdType.MESH

HEAD_DIM = 128
CHUNK = 64
SGU_SPAN = 128
GDN_CONV = 4
FFN_CONV = 3
N_CHIPS = 4
N_DEV = 8
LN_EPS = 1e-5
RMS_EPS = 1e-6
ADAM_LR = 0.001
ADAM_B1 = 0.9
ADAM_B2 = 0.999
ADAM_EPS = 1e-08
ADAM_WD = 0.01
ADAM_STEP = 10
NEG_BIG = -1e30
LANES = 128
PACK_COLS = 1024
FF_ALIGN = 512
SUM_ROWS = 256

BIG = ("w_in", "w_proj_a", "w_proj_b", "w_proj_c", "w_out", "ffn_w_up", "ffn_w_down")
ROW_SHARDED = ("w_out", "ffn_w_down")
SMALL = ("b_in", "sgu_ln_g", "sgu_ln_b", "sgu_w", "sgu_b", "gdn_conv_w", "gdn_a_log", "gdn_dt_bias", "gdn_norm_g",
         "ln1_g", "ln1_b", "ffn_conv_w", "ffn_conv_b", "ln2_g", "ln2_b")
WEIGHTS = ("w_in", "b_in", "sgu_ln_g", "sgu_ln_b", "sgu_w", "sgu_b", "gdn_conv_w", "gdn_a_log", "gdn_dt_bias", "gdn_norm_g",
           "w_proj_a", "w_proj_b", "w_proj_c", "w_out", "ln1_g", "ln1_b", "ffn_w_up", "ffn_conv_w", "ffn_conv_b", "ffn_w_down",
           "ln2_g", "ln2_b")

ANY = pl.BlockSpec(memory_space=pl.ANY)


def _tile(dim, pref):
    t = pref
    while t > 128 and dim % t:
        t //= 2
    return min(t, dim) if dim % min(t, dim) == 0 else dim


def _params(*sem):
    return pltpu.CompilerParams(dimension_semantics=sem)


_DIMS = {"nn": (((1,), (0,)), ((), ())), "nt": (((1,), (1,)), ((), ())), "tn": (((0,), (0,)), ((), ()))}


def _matmul(a, b, mode, name, bias=None, add=None, tm=512, tn=1024, tk=1024):
    if mode == "nn":
        (M, K), (_, N) = a.shape, b.shape
    elif mode == "nt":
        (M, K), (N, _) = a.shape, b.shape
    else:
        (K, M), (_, N) = a.shape, b.shape
    tm, tn, tk = _tile(M, tm), _tile(N, tn), _tile(K, tk)
    nk = K // tk
    has_bias, has_add = bias is not None, add is not None

    def body(*refs):
        a_ref, b_ref = refs[0], refs[1]
        pos = 2
        bias_ref = add_ref = None
        if has_bias:
            bias_ref, pos = refs[pos], pos + 1
        if has_add:
            add_ref, pos = refs[pos], pos + 1
        o_ref, acc_ref = refs[pos], refs[pos + 1]
        k = pl.program_id(2)

        @pl.when(k == 0)
        def _():
            acc_ref[...] = jnp.zeros_like(acc_ref)

        acc_ref[...] += lax.dot_general(a_ref[...].astype(bf16), b_ref[...].astype(bf16), _DIMS[mode],
                                        preferred_element_type=f32)

        @pl.when(k == nk - 1)
        def _():
            r = acc_ref[...]
            if has_bias:
                r = r + bias_ref[...]
            if has_add:
                r = r + add_ref[...]
            o_ref[...] = r

    if mode == "nn":
        specs = [pl.BlockSpec((tm, tk), lambda i, j, k: (i, k)), pl.BlockSpec((tk, tn), lambda i, j, k: (k, j))]
    elif mode == "nt":
        specs = [pl.BlockSpec((tm, tk), lambda i, j, k: (i, k)), pl.BlockSpec((tn, tk), lambda i, j, k: (j, k))]
    else:
        specs = [pl.BlockSpec((tk, tm), lambda i, j, k: (k, i)), pl.BlockSpec((tk, tn), lambda i, j, k: (k, j))]
    ops = [a, b]
    if has_bias:
        specs.append(pl.BlockSpec((1, tn), lambda i, j, k: (0, j)))
        ops.append(bias)
    if has_add:
        specs.append(pl.BlockSpec((tm, tn), lambda i, j, k: (i, j)))
        ops.append(add)
    return pl.pallas_call(
        body, name=name, grid=(M // tm, N // tn, nk), in_specs=specs,
        out_specs=pl.BlockSpec((tm, tn), lambda i, j, k: (i, j)),
        out_shape=jax.ShapeDtypeStruct((M, N), f32), scratch_shapes=[pltpu.VMEM((tm, tn), f32)],
        compiler_params=_params("parallel", "parallel", "arbitrary"))(*ops)


def _colsum(a, name):
    S, N = a.shape
    tn = _tile(N, 512)

    def body(a_ref, o_ref):
        o_ref[...] = jnp.sum(a_ref[...], axis=0, keepdims=True)

    return pl.pallas_call(body, name=name, grid=(N // tn,), in_specs=[pl.BlockSpec((S, tn), lambda j: (0, j))],
                          out_specs=pl.BlockSpec((1, tn), lambda j: (0, j)), out_shape=jax.ShapeDtypeStruct((1, N), f32),
                          compiler_params=_params("parallel"))(a)


def _ln_fn(alpha, x, y, g, b):
    z = alpha * x + y
    mu = jnp.mean(z, axis=-1, keepdims=True)
    zc = z - mu
    var = jnp.mean(zc * zc, axis=-1, keepdims=True)
    return zc * lax.rsqrt(var + LN_EPS) * g + b


def _ln_fwd(x, y, g, b, alpha, name):
    S, D = x.shape
    tr = _tile(S, 256)

    def body(x_ref, y_ref, g_ref, b_ref, o_ref):
        o_ref[...] = _ln_fn(alpha, x_ref[...], y_ref[...], g_ref[...], b_ref[...])

    row = pl.BlockSpec((tr, D), lambda i: (i, 0))
    par = pl.BlockSpec((1, D), lambda i: (0, 0))
    return pl.pallas_call(body, name=name, grid=(S // tr,), in_specs=[row, row, par, par], out_specs=row,
                          out_shape=jax.ShapeDtypeStruct((S, D), f32), compiler_params=_params("parallel"))(x, y, g, b)


def _ln_bwd(x, y, g, b, dout, alpha, name):
    S, D = x.shape
    tr = _tile(S, 256)

    def body(x_ref, y_ref, g_ref, b_ref, d_ref, dx_ref, dy_ref, dg_ref, db_ref):
        _, vjp = jax.vjp(functools.partial(_ln_fn, alpha), x_ref[...], y_ref[...], g_ref[...], b_ref[...])
        dx, dy, dg, db = vjp(d_ref[...])
        dx_ref[...] = dx
        dy_ref[...] = dy

        @pl.when(pl.program_id(0) == 0)
        def _():
            dg_ref[...] = jnp.zeros_like(dg_ref)
            db_ref[...] = jnp.zeros_like(db_ref)

        dg_ref[...] += dg
        db_ref[...] += db

    row = pl.BlockSpec((tr, D), lambda i: (i, 0))
    par = pl.BlockSpec((1, D), lambda i: (0, 0))
    sd = jax.ShapeDtypeStruct
    return pl.pallas_call(body, name=name, grid=(S // tr,), in_specs=[row, row, par, par, row],
                          out_specs=[row, row, par, par],
                          out_shape=[sd((S, D), f32), sd((S, D), f32), sd((1, D), f32), sd((1, D), f32)],
                          compiler_params=_params("arbitrary"))(x, y, g, b, dout)


def _loss_head(y, t, name):
    S, D = y.shape
    tr = _tile(S, 256)

    def body(y_ref, t_ref, l_ref, d_ref):
        e = y_ref[...] - t_ref[...]
        d_ref[...] = e / D

        @pl.when(pl.program_id(0) == 0)
        def _():
            l_ref[...] = jnp.zeros_like(l_ref)

        l_ref[...] += 0.5 * jnp.sum(jnp.mean(e * e, axis=-1, keepdims=True))

    row = pl.BlockSpec((tr, D), lambda i: (i, 0))
    return pl.pallas_call(body, name=name, grid=(S // tr,), in_specs=[row, row],
                          out_specs=[pl.BlockSpec((8, LANES), lambda i: (0, 0)), row],
                          out_shape=[jax.ShapeDtypeStruct((8, LANES), f32), jax.ShapeDtypeStruct((S, D), f32)],
                          compiler_params=_params("arbitrary"))(y, t)


def _merge_fn(g0, g1, g2, pa, pb, pc):
    return jax.nn.sigmoid(g0) * pa + jax.nn.sigmoid(g1) * pb + jax.nn.sigmoid(g2) * pc


def _merge_specs(S, D, gate_off):
    tr = _tile(S, 512)
    tc = _tile(math.gcd(gate_off, D), 512)
    gates = [pl.BlockSpec((tr, tc), functools.partial(lambda k, i, j: (i, (gate_off + k * D) // tc + j), k)) for k in range(3)]
    tile = pl.BlockSpec((tr, tc), lambda i, j: (i, j))
    return tr, tc, gates, tile


def _merge_fwd(proj, pa, pb, pc, gate_off, name):
    S, D = pa.shape
    tr, tc, gates, tile = _merge_specs(S, D, gate_off)

    def body(g0, g1, g2, a, b, c, o_ref):
        o_ref[...] = _merge_fn(g0[...], g1[...], g2[...], a[...], b[...], c[...])

    return pl.pallas_call(body, name=name, grid=(S // tr, D // tc), in_specs=gates + [tile] * 3, out_specs=tile,
                          out_shape=jax.ShapeDtypeStruct((S, D), f32),
                          compiler_params=_params("parallel", "parallel"))(proj, proj, proj, pa, pb, pc)


def _merge_bwd(proj, pa, pb, pc, dm, gate_off, name):
    S, D = pa.shape
    tr, tc, gates, tile = _merge_specs(S, D, gate_off)

    def body(g0, g1, g2, a, b, c, d, dg0, dg1, dg2, da, db, dc):
        _, vjp = jax.vjp(_merge_fn, g0[...], g1[...], g2[...], a[...], b[...], c[...])
        for ref, val in zip((dg0, dg1, dg2, da, db, dc), vjp(d[...])):
            ref[...] = val

    sd = jax.ShapeDtypeStruct
    return pl.pallas_call(body, name=name, grid=(S // tr, D // tc), in_specs=gates + [tile] * 4,
                          out_specs=[tile] * 6, out_shape=[sd((S, D), f32)] * 6,
                          compiler_params=_params("parallel", "parallel"))(proj, proj, proj, pa, pb, pc, dm)


def _sgu_fn(nb, u, v, ln_g, ln_b, w_s, b_s):
    mu = jnp.mean(v, axis=-1, keepdims=True)
    vc = v - mu
    var = jnp.mean(vc * vc, axis=-1, keepdims=True)
    vn = vc * lax.rsqrt(var + LN_EPS) * ln_g + ln_b
    r = lax.broadcasted_iota(jnp.int32, (SGU_SPAN, SGU_SPAN), 0) // CHUNK
    c = lax.broadcasted_iota(jnp.int32, (SGU_SPAN, SGU_SPAN), 1) // CHUNK
    wm = jnp.where(r >= c, w_s, 0.0)
    vn3 = vn.reshape(nb, SGU_SPAN, HEAD_DIM)
    mixed = lax.dot_general(jnp.broadcast_to(wm, (nb, SGU_SPAN, SGU_SPAN)), vn3, (((2,), (1,)), ((0,), (0,))),
                            preferred_element_type=f32)
    mixed = mixed + b_s
    return u * mixed.reshape(nb * SGU_SPAN, HEAD_DIM)


def _sgu_specs(S, G, u_off, v_off):
    nb = max(1, min(8, S // SGU_SPAN))
    rows = nb * SGU_SPAN
    ub = pl.BlockSpec((rows, HEAD_DIM), lambda g, n: (n, u_off // HEAD_DIM + g))
    vb = pl.BlockSpec((rows, HEAD_DIM), lambda g, n: (n, v_off // HEAD_DIM + g))
    lnb = pl.BlockSpec((1, HEAD_DIM), lambda g, n: (0, g))
    wb = pl.BlockSpec((None, SGU_SPAN, SGU_SPAN), lambda g, n: (g, 0, 0))
    bb = pl.BlockSpec((None, SGU_SPAN, 1), lambda g, n: (g, 0, 0))
    return nb, rows, ub, vb, lnb, wb, bb


def _sgu_fwd(proj, ln_g, ln_b, w_s, b_s, u_off, v_off, name):
    S = proj.shape[0]
    G = w_s.shape[0]
    nb, rows, ub, vb, lnb, wb, bb = _sgu_specs(S, G, u_off, v_off)

    def body(u, v, lg, lb, w, b, o_ref):
        o_ref[...] = _sgu_fn(nb, u[...], v[...], lg[...], lb[...], w[...], b[...])

    return pl.pallas_call(body, name=name, grid=(G, S // rows), in_specs=[ub, vb, lnb, lnb, wb, bb],
                          out_specs=pl.BlockSpec((rows, HEAD_DIM), lambda g, n: (n, g)),
                          out_shape=jax.ShapeDtypeStruct((S, G * HEAD_DIM), f32),
                          compiler_params=_params("parallel", "parallel"))(proj, proj, ln_g, ln_b, w_s, b_s)


def _sgu_bwd(proj, ln_g, ln_b, w_s, b_s, dy, u_off, v_off, name):
    S = proj.shape[0]
    G = w_s.shape[0]
    nb, rows, ub, vb, lnb, wb, bb = _sgu_specs(S, G, u_off, v_off)

    def body(u, v, lg, lb, w, b, d, du, dv, dlg, dlb, dw, db):
        _, vjp = jax.vjp(functools.partial(_sgu_fn, nb), u[...], v[...], lg[...], lb[...], w[...], b[...])
        gu, gv, glg, glb, gw, gb = vjp(d[...])
        du[...] = gu
        dv[...] = gv

        @pl.when(pl.program_id(1) == 0)
        def _():
            for ref in (dlg, dlb, dw, db):
                ref[...] = jnp.zeros_like(ref)

        dlg[...] += glg
        dlb[...] += glb
        dw[...] += gw
        db[...] += gb

    tile = pl.BlockSpec((rows, HEAD_DIM), lambda g, n: (n, g))
    sd = jax.ShapeDtypeStruct
    W = G * HEAD_DIM
    return pl.pallas_call(body, name=name, grid=(G, S // rows), in_specs=[ub, vb, lnb, lnb, wb, bb, tile],
                          out_specs=[tile, tile, lnb, lnb, wb, bb],
                          out_shape=[sd((S, W), f32), sd((S, W), f32), sd((1, W), f32), sd((1, W), f32),
                                     sd((G, SGU_SPAN, SGU_SPAN), f32), sd((G, SGU_SPAN, 1), f32)],
                          compiler_params=_params("parallel", "arbitrary"))(proj, proj, ln_g, ln_b, w_s, b_s, dy)


def _shift_down(x, k):
    if k == 0:
        return x
    rows = lax.broadcasted_iota(jnp.int32, x.shape, 0)
    return jnp.where(rows >= k, pltpu.roll(x, k, 0), 0.0)


def _shift_up(x, k):
    if k == 0:
        return x
    n = x.shape[0]
    rows = lax.broadcasted_iota(jnp.int32, x.shape, 0)
    return jnp.where(rows < n - k, pltpu.roll(x, n - k, 0), 0.0)


def _conv(x, w_ref, width):
    out = w_ref[width - 1] * x
    for j in range(width - 1):
        out = out + w_ref[j] * _shift_down(x, width - 1 - j)
    return out


def _conv_bwd(x, dz, w_ref, dw_ref, width):
    dx = w_ref[width - 1] * dz
    dw_ref[width - 1] = jnp.sum(dz * x, axis=0, keepdims=True)
    for j in range(width - 1):
        k = width - 1 - j
        dx = dx + w_ref[j] * _shift_up(dz, k)
        dw_ref[j] = jnp.sum(dz * _shift_down(x, k), axis=0, keepdims=True)
    return dx


def _silu(z):
    return z * jax.nn.sigmoid(z)


def _dsilu(z):
    s = jax.nn.sigmoid(z)
    return s * (1.0 + z * (1.0 - s))


def _ffn_act_fwd(upg, upv, cwg, cwv, cbg, cbv, name):
    S, Fp = upg.shape

    def body(g_ref, v_ref, wg, wv, bg, bv, o_ref):
        hg = _conv(g_ref[...], wg, FFN_CONV) + bg[...]
        hv = _conv(v_ref[...], wv, FFN_CONV) + bv[...]
        o_ref[...] = _silu(hg) * hv

    col = pl.BlockSpec((S, LANES), lambda j: (0, j))
    wsp = pl.BlockSpec((FFN_CONV, 1, LANES), lambda j: (0, 0, j))
    bsp = pl.BlockSpec((1, LANES), lambda j: (0, j))
    return pl.pallas_call(body, name=name, grid=(Fp // LANES,), in_specs=[col, col, wsp, wsp, bsp, bsp], out_specs=col,
                          out_shape=jax.ShapeDtypeStruct((S, Fp), f32),
                          compiler_params=_params("parallel"))(upg, upv, cwg, cwv, cbg, cbv)


def _ffn_act_bwd(upg, upv, cwg, cwv, cbg, cbv, dact, name):
    S, Fp = upg.shape

    def body(g_ref, v_ref, wg, wv, bg, bv, d_ref, dg_ref, dv_ref, dwg, dwv, dbg, dbv):
        xg, xv, d = g_ref[...], v_ref[...], d_ref[...]
        hg = _conv(xg, wg, FFN_CONV) + bg[...]
        hv = _conv(xv, wv, FFN_CONV) + bv[...]
        dhg = d * hv * _dsilu(hg)
        dhv = d * _silu(hg)
        dbg[...] = jnp.sum(dhg, axis=0, keepdims=True)
        dbv[...] = jnp.sum(dhv, axis=0, keepdims=True)
        dg_ref[...] = _conv_bwd(xg, dhg, wg, dwg, FFN_CONV)
        dv_ref[...] = _conv_bwd(xv, dhv, wv, dwv, FFN_CONV)

    col = pl.BlockSpec((S, LANES), lambda j: (0, j))
    wsp = pl.BlockSpec((FFN_CONV, 1, LANES), lambda j: (0, 0, j))
    bsp = pl.BlockSpec((1, LANES), lambda j: (0, j))
    sd = jax.ShapeDtypeStruct
    return pl.pallas_call(body, name=name, grid=(Fp // LANES,), in_specs=[col, col, wsp, wsp, bsp, bsp, col],
                          out_specs=[col, col, wsp, wsp, bsp, bsp],
                          out_shape=[sd((S, Fp), f32), sd((S, Fp), f32), sd((FFN_CONV, 1, Fp), f32), sd((FFN_CONV, 1, Fp), f32),
                                     sd((1, Fp), f32), sd((1, Fp), f32)],
                          compiler_params=_params("parallel"))(upg, upv, cwg, cwv, cbg, cbv, dact)


def _gdn_pre_fwd(proj, cw, x_off, n_norm, name):
    S = proj.shape[0]
    C = cw.shape[2]

    def body(x_ref, w_ref, o_ref):
        s = _silu(_conv(x_ref[...], w_ref, GDN_CONV))
        r = lax.rsqrt(jnp.sum(s * s, axis=-1, keepdims=True) + RMS_EPS)
        o_ref[...] = jnp.where(pl.program_id(0) < n_norm, s * r, s)

    xs = pl.BlockSpec((S, LANES), lambda j: (0, x_off // LANES + j))
    col = pl.BlockSpec((S, LANES), lambda j: (0, j))
    wsp = pl.BlockSpec((GDN_CONV, 1, LANES), lambda j: (0, 0, j))
    return pl.pallas_call(body, name=name, grid=(C // LANES,), in_specs=[xs, wsp], out_specs=col,
                          out_shape=jax.ShapeDtypeStruct((S, C), f32), compiler_params=_params("parallel"))(proj, cw)


def _gdn_pre_bwd(proj, cw, dout, x_off, n_norm, name):
    S = proj.shape[0]
    C = cw.shape[2]

    def body(x_ref, w_ref, d_ref, dx_ref, dw_ref):
        x, d = x_ref[...], d_ref[...]
        z = _conv(x, w_ref, GDN_CONV)
        s = _silu(z)
        r = lax.rsqrt(jnp.sum(s * s, axis=-1, keepdims=True) + RMS_EPS)
        ds_norm = d * r - s * (r * r * r) * jnp.sum(d * s, axis=-1, keepdims=True)
        ds = jnp.where(pl.program_id(0) < n_norm, ds_norm, d)
        dz = ds * _dsilu(z)
        dx_ref[...] = _conv_bwd(x, dz, w_ref, dw_ref, GDN_CONV)

    xs = pl.BlockSpec((S, LANES), lambda j: (0, x_off // LANES + j))
    col = pl.BlockSpec((S, LANES), lambda j: (0, j))
    wsp = pl.BlockSpec((GDN_CONV, 1, LANES), lambda j: (0, 0, j))
    return pl.pallas_call(body, name=name, grid=(C // LANES,), in_specs=[xs, wsp, col], out_specs=[col, wsp],
                          out_shape=[jax.ShapeDtypeStruct((S, C), f32), jax.ShapeDtypeStruct((GDN_CONV, 1, C), f32)],
                          compiler_params=_params("parallel"))(proj, cw, dout)


def _bmm(a, b, prec=None):
    return lax.dot_general(a, b, (((2,), (1,)), ((0,), (0,))), precision=prec, preferred_element_type=f32)


def _bmm_nt(a, b, prec=None):
    return lax.dot_general(a, b, (((2,), (2,)), ((0,), (0,))), precision=prec, preferred_element_type=f32)


def _bmm_tn(a, b, prec=None):
    return lax.dot_general(a, b, (((1,), (1,)), ((0,), (0,))), precision=prec, preferred_element_type=f32)


def _softplus(x):
    return jnp.maximum(x, 0.0) + jnp.log1p(jnp.exp(-jnp.abs(x)))


def _gdn_chunk(q, k, v, al, bl, gate, a_log, dt_bias, norm_g, state):
    H, C, Dh = q.shape
    r = lax.broadcasted_iota(jnp.int32, (H, C, C), 1)
    c = lax.broadcasted_iota(jnp.int32, (H, C, C), 2)
    tril = r >= c
    strict = r > c
    lower = tril.astype(f32)
    upper = (r <= c).astype(f32)
    ones = jnp.ones((H, C, C), f32)
    g = -jnp.exp(a_log) * _softplus(al + dt_bias)
    beta = jax.nn.sigmoid(bl)
    g_lanes = jnp.broadcast_to(g, (H, C, Dh))
    g_sq = jnp.broadcast_to(g, (H, C, C))
    gc = _bmm(lower, g_lanes, HIGHEST)
    gc_i = _bmm(lower, g_sq, HIGHEST)
    gc_j = _bmm(ones, g_sq * upper, HIGHEST)
    decay = jnp.where(tril, jnp.exp(jnp.where(tril, gc_i - gc_j, 0.0)), 0.0)
    qs = q * (Dh ** -0.5)
    kb = k * beta
    a_kk = jnp.where(strict, _bmm_nt(kb, k) * decay, 0.0)
    rhs_u = v * beta
    rhs_w = kb * jnp.exp(gc)
    p = -a_kk
    inv = (r == c).astype(f32) + p
    for _ in range(int(math.log2(C)) - 1):
        p = _bmm(p, p, HIGHEST)
        inv = inv + _bmm(inv, p, HIGHEST)
    u = _bmm(inv, rhs_u, HIGHEST)
    w = _bmm(inv, rhs_w, HIGHEST)
    qk = jnp.where(tril, _bmm_nt(qs, k) * decay, 0.0)
    g_last = jnp.sum(g, axis=1, keepdims=True)
    k_dec = k * jnp.exp(g_last - gc)
    q_dec = qs * jnp.exp(gc)
    v_new = u - _bmm(w, state)
    o = _bmm(q_dec, state) + _bmm(qk, v_new)
    new_state = state * jnp.exp(g_last) + _bmm_tn(k_dec, v_new)
    y = o * lax.rsqrt(jnp.mean(o * o, axis=-1, keepdims=True) + RMS_EPS) * norm_g * _silu(gate)
    return y, new_state


def _heads(ref, off, H):
    return jnp.stack([ref[:, off + h * HEAD_DIM: off + (h + 1) * HEAD_DIM] for h in range(H)])


def _gdn_scan_fwd(qkvc, al, bl, proj, a_log, dt_bias, norm_g, gate_off, name):
    S = qkvc.shape[0]
    H = al.shape[0]
    W = H * HEAD_DIM
    n = S // CHUNK

    def body(x_ref, al_ref, bl_ref, gate_ref, alog_ref, dt_ref, ng_ref, y_ref, st_ref, state):
        @pl.when(pl.program_id(0) == 0)
        def _():
            state[...] = jnp.zeros_like(state)

        st_ref[...] = state[...]
        y, new = _gdn_chunk(_heads(x_ref, 0, H), _heads(x_ref, W, H), _heads(x_ref, 2 * W, H), al_ref[...], bl_ref[...],
                            _heads(gate_ref, 0, H), alog_ref[...], dt_ref[...], ng_ref[...], state[...])
        state[...] = new
        for h in range(H):
            y_ref[:, h * HEAD_DIM:(h + 1) * HEAD_DIM] = y[h]

    sd = jax.ShapeDtypeStruct
    col = pl.BlockSpec((H, CHUNK, 1), lambda i: (0, i, 0))
    par = pl.BlockSpec((H, 1, 1), lambda i: (0, 0, 0))
    return pl.pallas_call(
        body, name=name, grid=(n,),
        in_specs=[pl.BlockSpec((CHUNK, 3 * W), lambda i: (i, 0)), col, col,
                  pl.BlockSpec((CHUNK, W), lambda i: (i, gate_off // W)), par, par,
                  pl.BlockSpec((1, 1, HEAD_DIM), lambda i: (0, 0, 0))],
        out_specs=[pl.BlockSpec((CHUNK, W), lambda i: (i, 0)),
                   pl.BlockSpec((None, H, HEAD_DIM, HEAD_DIM), lambda i: (i, 0, 0, 0))],
        out_shape=[sd((S, W), f32), sd((n, H, HEAD_DIM, HEAD_DIM), f32)],
        scratch_shapes=[pltpu.VMEM((H, HEAD_DIM, HEAD_DIM), f32)],
        compiler_params=_params("arbitrary"))(qkvc, al, bl, proj, a_log, dt_bias, norm_g)


def _gdn_scan_bwd(qkvc, al, bl, proj, a_log, dt_bias, norm_g, states, dy, gate_off, name):
    S = qkvc.shape[0]
    H = al.shape[0]
    W = H * HEAD_DIM
    n = S // CHUNK

    def body(x_ref, al_ref, bl_ref, gate_ref, alog_ref, dt_ref, ng_ref, st_ref, dy_ref,
             dx_ref, dal_ref, dbl_ref, dgate_ref, dalog_ref, ddt_ref, dng_ref, dstate):
        @pl.when(pl.program_id(0) == 0)
        def _():
            dstate[...] = jnp.zeros_like(dstate)
            for ref in (dalog_ref, ddt_ref, dng_ref):
                ref[...] = jnp.zeros_like(ref)

        _, vjp = jax.vjp(_gdn_chunk, _heads(x_ref, 0, H), _heads(x_ref, W, H), _heads(x_ref, 2 * W, H), al_ref[...],
                         bl_ref[...], _heads(gate_ref, 0, H), alog_ref[...], dt_ref[...], ng_ref[...], st_ref[...])
        dq, dk, dv, dal, dbl, dgate, dalog, ddt, dng, dst = vjp((_heads(dy_ref, 0, H), dstate[...]))
        dstate[...] = dst
        for h in range(H):
            lo, hi = h * HEAD_DIM, (h + 1) * HEAD_DIM
            dx_ref[:, lo:hi] = dq[h]
            dx_ref[:, W + lo:W + hi] = dk[h]
            dx_ref[:, 2 * W + lo:2 * W + hi] = dv[h]
            dgate_ref[:, lo:hi] = dgate[h]
        dal_ref[...] = dal
        dbl_ref[...] = dbl
        dalog_ref[...] += dalog
        ddt_ref[...] += ddt
        dng_ref[...] += dng

    sd = jax.ShapeDtypeStruct
    rev = lambda i: n - 1 - i
    col = pl.BlockSpec((H, CHUNK, 1), lambda i: (0, rev(i), 0))
    par = pl.BlockSpec((H, 1, 1), lambda i: (0, 0, 0))
    ng = pl.BlockSpec((1, 1, HEAD_DIM), lambda i: (0, 0, 0))
    xs = pl.BlockSpec((CHUNK, 3 * W), lambda i: (rev(i), 0))
    ws = pl.BlockSpec((CHUNK, W), lambda i: (rev(i), 0))
    return pl.pallas_call(
        body, name=name, grid=(n,),
        in_specs=[xs, col, col, pl.BlockSpec((CHUNK, W), lambda i: (rev(i), gate_off // W)), par, par, ng,
                  pl.BlockSpec((None, H, HEAD_DIM, HEAD_DIM), lambda i: (rev(i), 0, 0, 0)), ws],
        out_specs=[xs, col, col, ws, par, par, ng],
        out_shape=[sd((S, 3 * W), f32), sd((H, S, 1), f32), sd((H, S, 1), f32), sd((S, W), f32), sd((H, 1, 1), f32),
                   sd((H, 1, 1), f32), sd((1, 1, HEAD_DIM), f32)],
        scratch_shapes=[pltpu.VMEM((H, HEAD_DIM, HEAD_DIM), f32)],
        compiler_params=_params("arbitrary"))(qkvc, al, bl, proj, a_log, dt_bias, norm_g, states, dy)


def _tri(n, upper):
    r = lax.broadcasted_iota(jnp.int32, (n, n), 0)
    c = lax.broadcasted_iota(jnp.int32, (n, n), 1)
    return (r <= c if upper else r >= c).astype(f32)


def _fox_prep_fwd(tail, name):
    S = tail.shape[0]
    tb = _tile(S, 512)

    def body(x_ref, o_ref, carry):
        @pl.when(pl.program_id(0) == 0)
        def _():
            carry[...] = jnp.zeros_like(carry)

        x = x_ref[...]
        lf = jnp.minimum(x, 0.0) - jnp.log1p(jnp.exp(-jnp.abs(x)))
        o_ref[...] = jnp.dot(_tri(tb, False), lf, precision=HIGHEST, preferred_element_type=f32) + carry[...]
        carry[...] += jnp.sum(lf, axis=0, keepdims=True)

    blk = pl.BlockSpec((tb, LANES), lambda i: (i, 0))
    return pl.pallas_call(body, name=name, grid=(S // tb,), in_specs=[blk], out_specs=blk,
                          out_shape=jax.ShapeDtypeStruct((S, LANES), f32), scratch_shapes=[pltpu.VMEM((1, LANES), f32)],
                          compiler_params=_params("arbitrary"))(tail)


def _fox_prep_bwd(tail, dc_q, dc_k, name):
    S = tail.shape[0]
    tb = _tile(S, 512)
    nb = S // tb

    def body(x_ref, dq_ref, d_ref, o_ref, carry):
        @pl.when(pl.program_id(0) == 0)
        def _():
            carry[...] = jnp.zeros_like(carry)

        d = d_ref[...] + dq_ref[...]
        dlf = jnp.dot(_tri(tb, True), d, precision=HIGHEST, preferred_element_type=f32) + carry[...]
        carry[...] += jnp.sum(d, axis=0, keepdims=True)
        o_ref[...] = dlf * jax.nn.sigmoid(-x_ref[...])

    blk = pl.BlockSpec((tb, LANES), lambda i: (nb - 1 - i, 0))
    return pl.pallas_call(body, name=name, grid=(nb,), in_specs=[blk, blk, blk], out_specs=blk,
                          out_shape=jax.ShapeDtypeStruct((S, LANES), f32), scratch_shapes=[pltpu.VMEM((1, LANES), f32)],
                          compiler_params=_params("arbitrary"))(tail, dc_q, dc_k)


def _dot_nt(a, b):
    return lax.dot_general(a.astype(bf16), b.astype(bf16), _DIMS["nt"], preferred_element_type=f32)


def _dot_tn(a, b):
    return lax.dot_general(a.astype(bf16), b.astype(bf16), _DIMS["tn"], preferred_element_type=f32)


def _dot_nn(a, b):
    return lax.dot_general(a.astype(bf16), b.astype(bf16), _DIMS["nn"], preferred_element_type=f32)


def _fox_logits(q, k, cc, cr, i, j, T):
    s = _dot_nt(q, k) * (HEAD_DIM ** -0.5) + cc - cr
    qpos = i * T + lax.broadcasted_iota(jnp.int32, (T, T), 0)
    kpos = j * T + lax.broadcasted_iota(jnp.int32, (T, T), 1)
    return jnp.where(qpos >= kpos, s, NEG_BIG)


def _fox_fwd(proj, c_col, c_row, H, name):
    S = proj.shape[0]
    T = _tile(S, 512)
    nt = S // T

    def body(q_ref, k_ref, v_ref, cc_ref, cr_ref, o_ref, lse_ref, m_s, l_s, acc_s):
        i, j = pl.program_id(1), pl.program_id(2)

        @pl.when(j == 0)
        def _():
            m_s[...] = jnp.full_like(m_s, NEG_BIG)
            l_s[...] = jnp.zeros_like(l_s)
            acc_s[...] = jnp.zeros_like(acc_s)

        @pl.when(j <= i)
        def _():
            s = _fox_logits(q_ref[...], k_ref[...], cc_ref[...], cr_ref[...], i, j, T)
            m_new = jnp.maximum(m_s[...], jnp.max(s, axis=-1, keepdims=True))
            p = jnp.exp(s - m_new)
            corr = jnp.exp(m_s[...] - m_new)
            l_s[...] = corr * l_s[...] + jnp.sum(p, axis=-1, keepdims=True)
            acc_s[...] = corr * acc_s[...] + _dot_nn(p, v_ref[...])
            m_s[...] = m_new

        @pl.when(j == i)
        def _():
            o_ref[...] = acc_s[...] / l_s[...]
            lse_ref[...] = m_s[...] + jnp.log(l_s[...])

    sd = jax.ShapeDtypeStruct
    return pl.pallas_call(
        body, name=name, grid=(H, nt, nt),
        in_specs=[pl.BlockSpec((T, HEAD_DIM), lambda h, i, j: (i, h)),
                  pl.BlockSpec((T, HEAD_DIM), lambda h, i, j: (jnp.minimum(j, i), H + h)),
                  pl.BlockSpec((T, HEAD_DIM), lambda h, i, j: (jnp.minimum(j, i), 2 * H + h)),
                  pl.BlockSpec((None, T, 1), lambda h, i, j: (h, i, 0)),
                  pl.BlockSpec((None, 1, T), lambda h, i, j: (h, 0, jnp.minimum(j, i)))],
        out_specs=[pl.BlockSpec((T, HEAD_DIM), lambda h, i, j: (i, h)), pl.BlockSpec((None, T, 1), lambda h, i, j: (h, i, 0))],
        out_shape=[sd((S, H * HEAD_DIM), f32), sd((H, S, 1), f32)],
        scratch_shapes=[pltpu.VMEM((T, 1), f32), pltpu.VMEM((T, 1), f32), pltpu.VMEM((T, HEAD_DIM), f32)],
        compiler_params=_params("parallel", "parallel", "arbitrary"))(proj, proj, proj, c_col, c_row)


def _fox_bwd_q(proj, c_col, c_row, o, do, lse, H, name):
    S = proj.shape[0]
    T = _tile(S, 512)
    nt = S // T

    def body(q_ref, k_ref, v_ref, cc_ref, cr_ref, o_ref, do_ref, lse_ref, dq_ref, dc_ref, acc_s, dc_s):
        i, j = pl.program_id(1), pl.program_id(2)

        @pl.when(j == 0)
        def _():
            acc_s[...] = jnp.zeros_like(acc_s)
            dc_s[...] = jnp.zeros_like(dc_s)

        @pl.when(j <= i)
        def _():
            s = _fox_logits(q_ref[...], k_ref[...], cc_ref[...], cr_ref[...], i, j, T)
            p = jnp.exp(s - lse_ref[...])
            do_ = do_ref[...]
            delta = jnp.sum(o_ref[...] * do_, axis=-1, keepdims=True)
            ds = p * (_dot_nt(do_, v_ref[...]) - delta)
            acc_s[...] += _dot_nn(ds, k_ref[...])
            dc_s[...] += jnp.sum(ds, axis=-1, keepdims=True)

        @pl.when(j == i)
        def _():
            dq_ref[...] = acc_s[...] * (HEAD_DIM ** -0.5)
            dc_ref[...] = dc_s[...]

    qb = pl.BlockSpec((T, HEAD_DIM), lambda h, i, j: (i, h))
    col = pl.BlockSpec((None, T, 1), lambda h, i, j: (h, i, 0))
    return pl.pallas_call(
        body, name=name, grid=(H, nt, nt),
        in_specs=[qb, pl.BlockSpec((T, HEAD_DIM), lambda h, i, j: (jnp.minimum(j, i), H + h)),
                  pl.BlockSpec((T, HEAD_DIM), lambda h, i, j: (jnp.minimum(j, i), 2 * H + h)),
                  col, pl.BlockSpec((None, 1, T), lambda h, i, j: (h, 0, jnp.minimum(j, i))), qb, qb, col],
        out_specs=[qb, col], out_shape=[jax.ShapeDtypeStruct((S, H * HEAD_DIM), f32), jax.ShapeDtypeStruct((H, S, 1), f32)],
        scratch_shapes=[pltpu.VMEM((T, HEAD_DIM), f32), pltpu.VMEM((T, 1), f32)],
        compiler_params=_params("parallel", "parallel", "arbitrary"))(proj, proj, proj, c_col, c_row, o, do, lse)


def _fox_bwd_kv(proj, c_col, c_row, o, do, lse, H, name):
    S = proj.shape[0]
    T = _tile(S, 512)
    nt = S // T

    def body(q_ref, k_ref, v_ref, cc_ref, cr_ref, o_ref, do_ref, lse_ref, dk_ref, dv_ref, dc_ref, dk_s, dv_s, dc_s):
        j, i = pl.program_id(1), pl.program_id(2)

        @pl.when(i == 0)
        def _():
            dk_s[...] = jnp.zeros_like(dk_s)
            dv_s[...] = jnp.zeros_like(dv_s)
            dc_s[...] = jnp.zeros_like(dc_s)

        @pl.when(i >= j)
        def _():
            s = _fox_logits(q_ref[...], k_ref[...], cc_ref[...], cr_ref[...], i, j, T)
            p = jnp.exp(s - lse_ref[...])
            do_ = do_ref[...]
            delta = jnp.sum(o_ref[...] * do_, axis=-1, keepdims=True)
            ds = p * (_dot_nt(do_, v_ref[...]) - delta)
            dv_s[...] += _dot_tn(p, do_)
            dk_s[...] += _dot_tn(ds, q_ref[...])
            dc_s[...] -= jnp.sum(ds, axis=0, keepdims=True)

        @pl.when(i == nt - 1)
        def _():
            dk_ref[...] = dk_s[...] * (HEAD_DIM ** -0.5)
            dv_ref[...] = dv_s[...]
            dc_ref[...] = dc_s[...]

    qb = pl.BlockSpec((T, HEAD_DIM), lambda h, j, i: (jnp.maximum(i, j), h))
    col = pl.BlockSpec((None, T, 1), lambda h, j, i: (h, jnp.maximum(i, j), 0))
    kb = pl.BlockSpec((T, HEAD_DIM), lambda h, j, i: (j, h))
    sd = jax.ShapeDtypeStruct
    return pl.pallas_call(
        body, name=name, grid=(H, nt, nt),
        in_specs=[qb, pl.BlockSpec((T, HEAD_DIM), lambda h, j, i: (j, H + h)),
                  pl.BlockSpec((T, HEAD_DIM), lambda h, j, i: (j, 2 * H + h)),
                  col, pl.BlockSpec((None, 1, T), lambda h, j, i: (h, 0, j)), qb, qb, col],
        out_specs=[kb, kb, pl.BlockSpec((None, 1, T), lambda h, j, i: (h, 0, j))],
        out_shape=[sd((S, H * HEAD_DIM), f32), sd((S, H * HEAD_DIM), f32), sd((H, 1, S), f32)],
        scratch_shapes=[pltpu.VMEM((T, HEAD_DIM), f32), pltpu.VMEM((T, HEAD_DIM), f32), pltpu.VMEM((1, T), f32)],
        compiler_params=_params("parallel", "parallel", "arbitrary"))(proj, proj, proj, c_col, c_row, o, do, lse)


def _adamw(w, g, m, v, name):
    shape = w.shape
    cols = shape[-1]
    rows = w.size // cols
    ops = [t.reshape(rows, cols) for t in (w, g, m, v)]
    tr = rows
    if rows % 8 == 0:
        tr = 8
        while tr * 2 <= rows and rows % (tr * 2) == 0 and tr * 2 * cols * 4 <= (1 << 20):
            tr *= 2

    def body(w_ref, g_ref, m_ref, v_ref, d_ref, mo_ref, vo_ref):
        g_ = g_ref[...]
        m_ = ADAM_B1 * m_ref[...] + (1.0 - ADAM_B1) * g_
        v_ = ADAM_B2 * v_ref[...] + (1.0 - ADAM_B2) * (g_ * g_)
        m_hat = m_ / (1.0 - ADAM_B1 ** ADAM_STEP)
        v_hat = v_ / (1.0 - ADAM_B2 ** ADAM_STEP)
        d_ref[...] = -ADAM_LR * (m_hat / (jnp.sqrt(v_hat) + ADAM_EPS) + ADAM_WD * w_ref[...])
        mo_ref[...] = m_
        vo_ref[...] = v_

    blk = pl.BlockSpec((tr, cols), lambda i: (i, 0))
    outs = pl.pallas_call(body, name=name, grid=(rows // tr,), in_specs=[blk] * 4, out_specs=[blk] * 3,
                          out_shape=[jax.ShapeDtypeStruct((rows, cols), f32)] * 3, compiler_params=_params("parallel"))(*ops)
    return [o.reshape(shape) for o in outs]


def _sum_leading(a, name):
    n, R, C = a.shape
    tr = _tile(R, SUM_ROWS)

    def body(a_ref, o_ref):
        acc = a_ref[0]
        for k in range(1, n):
            acc = acc + a_ref[k]
        o_ref[...] = acc

    return pl.pallas_call(body, name=name, grid=(R // tr,), in_specs=[pl.BlockSpec((n, tr, C), lambda i: (0, i, 0))],
                          out_specs=pl.BlockSpec((tr, C), lambda i: (i, 0)), out_shape=jax.ShapeDtypeStruct((R, C), f32),
                          compiler_params=_params("parallel"))(a)


def _add2(a, b, name):
    n, R, C = a.shape
    tr = _tile(R, SUM_ROWS)

    def body(a_ref, b_ref, o_ref):
        o_ref[...] = a_ref[...] + b_ref[...]

    blk = pl.BlockSpec((None, tr, C), lambda k, i: (k, i, 0))
    return pl.pallas_call(body, name=name, grid=(n, R // tr), in_specs=[blk, blk], out_specs=blk,
                          out_shape=jax.ShapeDtypeStruct((n, R, C), f32), compiler_params=_params("parallel", "parallel"))(a, b)


def _place():
    x, y, c = lax.axis_index("x"), lax.axis_index("y"), lax.axis_index("c")
    chips = [(1 - x, y), (x, 1 - y), (1 - x, 1 - y)]
    return x, y, c, chips


def _gather_weights(ws, name):
    _, R, C = ws.shape

    def body(ws_ref, out_ref, send_sems, recv_sems, local_sem):
        x, y, c, chips = _place()
        sibling = (x, y, 1 - c)

        def half(px, py):
            return out_ref.at[2 * px + py, c]

        def copy(k, src, dst, to):
            return pltpu.make_async_remote_copy(src_ref=src, dst_ref=dst, send_sem=send_sems.at[k], recv_sem=recv_sems.at[k],
                                                device_id=to, device_id_type=MESH)

        mine = pltpu.make_async_copy(ws_ref, out_ref.at[2 * x + y], local_sem)
        mine.start()
        first = [copy(j, ws_ref.at[c], half(x, y), (*chip, c)) for j, chip in enumerate(chips)]
        for cp in first:
            cp.start()
        passed = [copy(3 + j, half(*chip), half(*chip), sibling) for j, chip in enumerate(chips)]
        for j, chip in enumerate(chips):
            copy(j, ws_ref.at[c], half(*chip), (*chip, c)).wait_recv()
            passed[j].start()
        for j, chip in enumerate(chips):
            pltpu.make_async_remote_copy(src_ref=half(*chip), dst_ref=out_ref.at[2 * chip[0] + chip[1], 1 - c],
                                         send_sem=send_sems.at[3 + j], recv_sem=recv_sems.at[3 + j], device_id=sibling,
                                         device_id_type=MESH).wait_recv()
        for cp in first + passed:
            cp.wait_send()
        mine.wait()

    return pl.pallas_call(body, name=name, out_shape=jax.ShapeDtypeStruct((N_CHIPS, 2, R, C), ws.dtype), in_specs=[ANY],
                          out_specs=ANY,
                          scratch_shapes=[pltpu.SemaphoreType.DMA((6,)), pltpu.SemaphoreType.DMA((6,)), pltpu.SemaphoreType.DMA(())])(ws)


def _pair_exchange(g, name):
    _, n, R, C = g.shape

    def body(g_ref, out_ref, send_sem, recv_sem):
        x, y, c, _ = _place()
        cp = pltpu.make_async_remote_copy(src_ref=g_ref.at[1 - c], dst_ref=out_ref, send_sem=send_sem, recv_sem=recv_sem,
                                          device_id=(x, y, 1 - c), device_id_type=MESH)
        cp.start()
        cp.wait()

    return pl.pallas_call(body, name=name, out_shape=jax.ShapeDtypeStruct((n, R, C), g.dtype), in_specs=[ANY], out_specs=ANY,
                          scratch_shapes=[pltpu.SemaphoreType.DMA(()), pltpu.SemaphoreType.DMA(())])(g)


def _chip_exchange(a, name):
    n, R, C = a.shape

    def body(a_ref, out_ref, send_sems, recv_sems, local_sem):
        x, y, c, chips = _place()
        me = 2 * x + y
        mine = pltpu.make_async_copy(a_ref.at[me], out_ref.at[me], local_sem)
        mine.start()
        sends = [pltpu.make_async_remote_copy(src_ref=a_ref.at[2 * chip[0] + chip[1]], dst_ref=out_ref.at[me],
                                              send_sem=send_sems.at[j], recv_sem=recv_sems.at[j], device_id=(*chip, c),
                                              device_id_type=MESH) for j, chip in enumerate(chips)]
        for cp in sends:
            cp.start()
        for j, chip in enumerate(chips):
            pltpu.make_async_remote_copy(src_ref=a_ref.at[me], dst_ref=out_ref.at[2 * chip[0] + chip[1]],
                                         send_sem=send_sems.at[j], recv_sem=recv_sems.at[j], device_id=(*chip, c),
                                         device_id_type=MESH).wait_recv()
        for cp in sends:
            cp.wait_send()
        mine.wait()

    return pl.pallas_call(body, name=name, out_shape=jax.ShapeDtypeStruct((n, R, C), a.dtype), in_specs=[ANY], out_specs=ANY,
                          scratch_shapes=[pltpu.SemaphoreType.DMA((3,)), pltpu.SemaphoreType.DMA((3,)), pltpu.SemaphoreType.DMA(())])(a)


def _pair_gather(r, name):
    R, C = r.shape

    def body(r_ref, out_ref, send_sem, recv_sem, local_sem):
        x, y, c, _ = _place()
        mine = pltpu.make_async_copy(r_ref, out_ref.at[c], local_sem)
        mine.start()
        cp = pltpu.make_async_remote_copy(src_ref=r_ref, dst_ref=out_ref.at[c], send_sem=send_sem, recv_sem=recv_sem,
                                          device_id=(x, y, 1 - c), device_id_type=MESH)
        cp.start()
        pltpu.make_async_remote_copy(src_ref=r_ref, dst_ref=out_ref.at[1 - c], send_sem=send_sem, recv_sem=recv_sem,
                                     device_id=(x, y, 1 - c), device_id_type=MESH).wait_recv()
        cp.wait_send()
        mine.wait()

    return pl.pallas_call(body, name=name, out_shape=jax.ShapeDtypeStruct((2, R, C), r.dtype), in_specs=[ANY], out_specs=ANY,
                          scratch_shapes=[pltpu.SemaphoreType.DMA(()), pltpu.SemaphoreType.DMA(()), pltpu.SemaphoreType.DMA(())])(r)


def _all_gather8(v, name):
    R, C = v.shape

    def body(v_ref, out_ref, send_sems, recv_sems, local_sem):
        x, y, c, _ = _place()
        me = 4 * x + 2 * y + c
        mine = pltpu.make_async_copy(v_ref, out_ref.at[me], local_sem)
        mine.start()
        peers = [(x ^ (k >> 2), y ^ ((k >> 1) & 1), c ^ (k & 1)) for k in range(1, N_DEV)]
        sends = [pltpu.make_async_remote_copy(src_ref=v_ref, dst_ref=out_ref.at[me], send_sem=send_sems.at[k], recv_sem=recv_sems.at[k],
                                              device_id=peer, device_id_type=MESH) for k, peer in enumerate(peers)]
        for cp in sends:
            cp.start()
        for k, (px, py, pc) in enumerate(peers):
            pltpu.make_async_remote_copy(src_ref=v_ref, dst_ref=out_ref.at[4 * px + 2 * py + pc], send_sem=send_sems.at[k],
                                         recv_sem=recv_sems.at[k], device_id=(px, py, pc), device_id_type=MESH).wait_recv()
        for cp in sends:
            cp.wait_send()
        mine.wait()

    return pl.pallas_call(body, name=name, out_shape=jax.ShapeDtypeStruct((N_DEV, R, C), v.dtype), in_specs=[ANY], out_specs=ANY,
                          scratch_shapes=[pltpu.SemaphoreType.DMA((7,)), pltpu.SemaphoreType.DMA((7,)), pltpu.SemaphoreType.DMA(())])(v)


def _all_reduce8(v, name):
    n = v.shape[0]
    rows = -(-n // (LANES * SUM_ROWS)) * SUM_ROWS
    padded = jnp.pad(v, (0, rows * LANES - n)).reshape(rows, LANES)
    return _sum_leading(_all_gather8(padded, name + "_gather"), name + "_sum").reshape(-1)[:n]


def _pack_rows(n):
    return -(-n // (2 * PACK_COLS * SUM_ROWS)) * SUM_ROWS


def _flatten_for_chips(full, name):
    r, c = full.shape
    if name in ROW_SHARDED:
        return full.reshape(N_CHIPS, -1)
    return full.reshape(r, N_CHIPS, c // N_CHIPS).transpose(1, 0, 2).reshape(N_CHIPS, -1)


def _unflatten_from_chips(flat, name, shard_shape):
    r, c = shard_shape
    if name in ROW_SHARDED:
        return flat.reshape(N_CHIPS * r, c)
    return flat.reshape(N_CHIPS, r, c).transpose(1, 0, 2).reshape(r, N_CHIPS * c)


def _split_cols(w_full, H, D):
    W = H * HEAD_DIM
    sizes = (3 * W, H, 2 * W, 3 * W, H, H, W, 3 * D)
    offs = [0]
    for s in sizes:
        offs.append(offs[-1] + s)
    seg = [w_full[..., offs[i]:offs[i + 1]] for i in range(8)]
    main = jnp.concatenate([seg[0], seg[2], seg[3], seg[6], seg[7]], axis=-1)
    pad = jnp.zeros(w_full.shape[:-1] + (LANES - 3 * H,), w_full.dtype)
    tail = jnp.concatenate([seg[1], seg[4], seg[5], pad], axis=-1)
    return main, tail


def _join_cols(main, tail, H, D):
    W = H * HEAD_DIM
    return jnp.concatenate([main[..., :3 * W], tail[..., :H], main[..., 3 * W:5 * W], main[..., 5 * W:8 * W], tail[..., H:2 * H],
                            tail[..., 2 * H:3 * H], main[..., 8 * W:9 * W], main[..., 9 * W:]], axis=-1)


def _pad_cols(a, n):
    return jnp.pad(a, [(0, 0)] * (a.ndim - 1) + [(0, n - a.shape[-1])])


def kernel(x, w_in, b_in, sgu_ln_g, sgu_ln_b, sgu_w, sgu_b, gdn_conv_w, gdn_a_log, gdn_dt_bias, gdn_norm_g, w_proj_a, w_proj_b, w_proj_c, w_out, ln1_g, ln1_b, ffn_w_up, ffn_conv_w, ffn_conv_b, ffn_w_down, ln2_g, ln2_b, loss_target, m_w_in, m_b_in, m_sgu_ln_g, m_sgu_ln_b, m_sgu_w, m_sgu_b, m_gdn_conv_w, m_gdn_a_log, m_gdn_dt_bias, m_gdn_norm_g, m_w_proj_a, m_w_proj_b, m_w_proj_c, m_w_out, m_ln1_g, m_ln1_b, m_ffn_w_up, m_ffn_conv_w, m_ffn_conv_b, m_ffn_w_down, m_ln2_g, m_ln2_b, v_w_in, v_b_in, v_sgu_ln_g, v_sgu_ln_b, v_sgu_w, v_sgu_b, v_gdn_conv_w, v_gdn_a_log, v_gdn_dt_bias, v_gdn_norm_g, v_w_proj_a, v_w_proj_b, v_w_proj_c, v_w_out, v_ln1_g, v_ln1_b, v_ffn_w_up, v_ffn_conv_w, v_ffn_conv_b, v_ffn_w_down, v_ln2_g, v_ln2_b):
    P = dict(w_in=w_in, b_in=b_in, sgu_ln_g=sgu_ln_g, sgu_ln_b=sgu_ln_b, sgu_w=sgu_w, sgu_b=sgu_b, gdn_conv_w=gdn_conv_w,
             gdn_a_log=gdn_a_log, gdn_dt_bias=gdn_dt_bias, gdn_norm_g=gdn_norm_g, w_proj_a=w_proj_a, w_proj_b=w_proj_b,
             w_proj_c=w_proj_c, w_out=w_out, ln1_g=ln1_g, ln1_b=ln1_b, ffn_w_up=ffn_w_up, ffn_conv_w=ffn_conv_w,
             ffn_conv_b=ffn_conv_b, ffn_w_down=ffn_w_down, ln2_g=ln2_g, ln2_b=ln2_b)
    M1 = dict(w_in=m_w_in, b_in=m_b_in, sgu_ln_g=m_sgu_ln_g, sgu_ln_b=m_sgu_ln_b, sgu_w=m_sgu_w, sgu_b=m_sgu_b,
              gdn_conv_w=m_gdn_conv_w, gdn_a_log=m_gdn_a_log, gdn_dt_bias=m_gdn_dt_bias, gdn_norm_g=m_gdn_norm_g,
              w_proj_a=m_w_proj_a, w_proj_b=m_w_proj_b, w_proj_c=m_w_proj_c, w_out=m_w_out, ln1_g=m_ln1_g, ln1_b=m_ln1_b,
              ffn_w_up=m_ffn_w_up, ffn_conv_w=m_ffn_conv_w, ffn_conv_b=m_ffn_conv_b, ffn_w_down=m_ffn_w_down, ln2_g=m_ln2_g,
              ln2_b=m_ln2_b)
    M2 = dict(w_in=v_w_in, b_in=v_b_in, sgu_ln_g=v_sgu_ln_g, sgu_ln_b=v_sgu_ln_b, sgu_w=v_sgu_w, sgu_b=v_sgu_b,
              gdn_conv_w=v_gdn_conv_w, gdn_a_log=v_gdn_a_log, gdn_dt_bias=v_gdn_dt_bias, gdn_norm_g=v_gdn_norm_g,
              w_proj_a=v_w_proj_a, w_proj_b=v_w_proj_b, w_proj_c=v_w_proj_c, w_out=v_w_out, ln1_g=v_ln1_g, ln1_b=v_ln1_b,
              ffn_w_up=v_ffn_w_up, ffn_conv_w=v_ffn_conv_w, ffn_conv_b=v_ffn_conv_b, ffn_w_down=v_ffn_w_down, ln2_g=v_ln2_g,
              ln2_b=v_ln2_b)
    _, S, D = x.shape
    L = w_in.shape[0]
    N_IN = w_in.shape[2] * N_CHIPS
    H = (N_IN - 3 * D) // (9 * HEAD_DIM + 3)
    W = H * HEAD_DIM
    F = ffn_w_down.shape[1] * N_CHIPS
    Fp = -(-F // FF_ALIGN) * FF_ALIGN
    NM = 9 * W + 3 * D
    alpha = (2 * L) ** 0.25
    cx, cy, cc = lax.axis_index("x"), lax.axis_index("y"), lax.axis_index("c")
    chip = 2 * cx + cy

    shard_shapes = {n: P[n].shape[1:] for n in BIG}
    sizes = {n: shard_shapes[n][0] * shard_shapes[n][1] for n in BIG}
    total = sum(sizes.values())
    Rh = _pack_rows(total)
    padded = 2 * Rh * PACK_COLS
    full_w = []
    for l in range(L):
        flat = jnp.concatenate([P[n][l].astype(bf16).reshape(-1) for n in BIG] + [jnp.zeros((padded - total,), bf16)])
        got = _gather_weights(flat.reshape(2, Rh, PACK_COLS), "gather_weights").reshape(N_CHIPS, padded)
        off, d = 0, {}
        for n in BIG:
            d[n] = _unflatten_from_chips(got[:, off:off + sizes[n]], n, shard_shapes[n])
            off += sizes[n]
        full_w.append(d)
    gcw_cols, fcw_cols = gdn_conv_w.shape[2], ffn_conv_w.shape[2]
    only_south = (cc == 0).astype(f32)
    placed_g = lax.dynamic_update_slice(jnp.zeros((L, GDN_CONV, 3 * W), f32), gdn_conv_w * only_south, (0, 0, chip * gcw_cols))
    placed_f = lax.dynamic_update_slice(jnp.zeros((L, FFN_CONV, 2 * F), f32), ffn_conv_w * only_south, (0, 0, chip * fcw_cols))
    conv_all = _all_reduce8(jnp.concatenate([placed_g.reshape(-1), placed_f.reshape(-1)]), "conv_weights")
    gcw_full = conv_all[:L * GDN_CONV * 3 * W].reshape(L, GDN_CONV, 1, 3 * W)
    fcw_full = conv_all[L * GDN_CONV * 3 * W:].reshape(L, FFN_CONV, 1, 2 * F)

    saved = []
    h = x.reshape(S, D)
    for l in range(L):
        fw = full_w[l]
        w_main, w_tail = _split_cols(fw["w_in"], H, D)
        b_main, b_tail = _split_cols(b_in[l][None, :], H, D)
        wg = _pad_cols(fw["ffn_w_up"][:, :F], Fp)
        wv = _pad_cols(fw["ffn_w_up"][:, F:], Fp)
        wd = jnp.pad(fw["ffn_w_down"], ((0, Fp - F), (0, 0)))
        cwg, cwv = _pad_cols(fcw_full[l][..., :F], Fp), _pad_cols(fcw_full[l][..., F:], Fp)
        cbg, cbv = _pad_cols(ffn_conv_b[l][None, :F], Fp), _pad_cols(ffn_conv_b[l][None, F:], Fp)
        lw = dict(w_main=w_main, w_tail=w_tail, wa=fw["w_proj_a"], wb=fw["w_proj_b"], wc=fw["w_proj_c"], wo=fw["w_out"],
                  wg=wg, wv=wv, wd=wd, cwg=cwg, cwv=cwv, cbg=cbg, cbv=cbv, gcw=gcw_full[l],
                  sgu_ln_g=sgu_ln_g[l][None, :], sgu_ln_b=sgu_ln_b[l][None, :], sgu_w=sgu_w[l], sgu_b=sgu_b[l][:, :, None],
                  a_log=gdn_a_log[l].reshape(H, 1, 1), dt_bias=gdn_dt_bias[l].reshape(H, 1, 1),
                  norm_g=gdn_norm_g[l].reshape(1, 1, HEAD_DIM), ln1_g=ln1_g[l][None, :], ln1_b=ln1_b[l][None, :],
                  ln2_g=ln2_g[l][None, :], ln2_b=ln2_b[l][None, :])
        proj = _matmul(h, w_main, "nn", "proj_main", bias=b_main)
        tail = _matmul(h, w_tail, "nn", "proj_tail", bias=b_tail)
        csum = _fox_prep_fwd(tail, "fox_prep")
        c_col = csum[:, :H].T[:, :, None]
        c_row = csum[:, :H].T[:, None, :]
        y_a, lse = _fox_fwd(proj, c_col, c_row, H, "fox_fwd")
        y_b = _sgu_fwd(proj, lw["sgu_ln_g"], lw["sgu_ln_b"], lw["sgu_w"], lw["sgu_b"], 3 * W, 4 * W, "sgu_fwd")
        qkvc = _gdn_pre_fwd(proj, lw["gcw"], 5 * W, 2 * H, "gdn_pre")
        al = tail[:, H:2 * H].T[:, :, None]
        bl = tail[:, 2 * H:3 * H].T[:, :, None]
        y_c, states = _gdn_scan_fwd(qkvc, al, bl, proj, lw["a_log"], lw["dt_bias"], lw["norm_g"], 8 * W, "gdn_scan")
        pa = _matmul(y_a, lw["wa"], "nn", "branch_proj")
        pb = _matmul(y_b, lw["wb"], "nn", "branch_proj")
        pc = _matmul(y_c, lw["wc"], "nn", "branch_proj")
        merged = _merge_fwd(proj, pa, pb, pc, 9 * W, "merge")
        mix = _matmul(merged, lw["wo"], "nn", "out_proj")
        x1 = _ln_fwd(h, mix, lw["ln1_g"], lw["ln1_b"], alpha, "ln")
        upg = _matmul(x1, lw["wg"], "nn", "ffn_up")
        upv = _matmul(x1, lw["wv"], "nn", "ffn_up")
        act = _ffn_act_fwd(upg, upv, cwg, cwv, cbg, cbv, "ffn_act")
        ffn = _matmul(act, lw["wd"], "nn", "ffn_down")
        x2 = _ln_fwd(x1, ffn, lw["ln2_g"], lw["ln2_b"], alpha, "ln")
        saved.append(dict(lw=lw, h=h, proj=proj, tail=tail, c_col=c_col, c_row=c_row, y_a=y_a, lse=lse, y_b=y_b, qkvc=qkvc, al=al,
                          bl=bl, y_c=y_c, states=states, pa=pa, pb=pb, pc=pc, merged=merged, mix=mix, x1=x1, upg=upg, upv=upv,
                          act=act, ffn=ffn))
        h = x2

    loss_part, dh = _loss_head(h, loss_target.reshape(S, D), "loss_head")
    loss = lax.psum(loss_part[0, 0], ("x", "y", "c"))

    big_grads = [None] * L
    small_grads = [None] * L
    for l in reversed(range(L)):
        s = saved[l]
        lw = s["lw"]
        d_x1r, d_ffn, d_ln2g, d_ln2b = _ln_bwd(s["x1"], s["ffn"], lw["ln2_g"], lw["ln2_b"], dh, alpha, "ln_bwd")
        d_act = _matmul(d_ffn, lw["wd"], "nt", "ffn_down_dx")
        d_wd = _matmul(s["act"], d_ffn, "tn", "ffn_down_dw")
        dupg, dupv, dcwg, dcwv, dcbg, dcbv = _ffn_act_bwd(s["upg"], s["upv"], lw["cwg"], lw["cwv"], lw["cbg"], lw["cbv"], d_act,
                                                          "ffn_act_bwd")
        d_x1 = _matmul(dupg, lw["wg"], "nt", "ffn_up_dx", add=d_x1r)
        d_x1 = _matmul(dupv, lw["wv"], "nt", "ffn_up_dx", add=d_x1)
        d_wg = _matmul(s["x1"], dupg, "tn", "ffn_up_dw")
        d_wv = _matmul(s["x1"], dupv, "tn", "ffn_up_dw")
        d_hr, d_mix, d_ln1g, d_ln1b = _ln_bwd(s["h"], s["mix"], lw["ln1_g"], lw["ln1_b"], d_x1, alpha, "ln_bwd")
        d_merged = _matmul(d_mix, lw["wo"], "nt", "out_proj_dx")
        d_wo = _matmul(s["merged"], d_mix, "tn", "out_proj_dw")
        dg0, dg1, dg2, d_pa, d_pb, d_pc = _merge_bwd(s["proj"], s["pa"], s["pb"], s["pc"], d_merged, 9 * W, "merge_bwd")
        d_ya = _matmul(d_pa, lw["wa"], "nt", "branch_proj_dx")
        d_yb = _matmul(d_pb, lw["wb"], "nt", "branch_proj_dx")
        d_yc = _matmul(d_pc, lw["wc"], "nt", "branch_proj_dx")
        d_wa = _matmul(s["y_a"], d_pa, "tn", "branch_proj_dw")
        d_wb = _matmul(s["y_b"], d_pb, "tn", "branch_proj_dw")
        d_wc = _matmul(s["y_c"], d_pc, "tn", "branch_proj_dw")
        dqkvc, dal, dbl, dgate, d_alog, d_dt, d_ng = _gdn_scan_bwd(s["qkvc"], s["al"], s["bl"], s["proj"], lw["a_log"], lw["dt_bias"],
                                                                   lw["norm_g"], s["states"], d_yc, 8 * W, "gdn_scan_bwd")
        d_gqkv, d_gcw = _gdn_pre_bwd(s["proj"], lw["gcw"], dqkvc, 5 * W, 2 * H, "gdn_pre_bwd")
        d_u, d_v, d_slg, d_slb, d_sw, d_sb = _sgu_bwd(s["proj"], lw["sgu_ln_g"], lw["sgu_ln_b"], lw["sgu_w"], lw["sgu_b"], d_yb,
                                                      3 * W, 4 * W, "sgu_bwd")
        d_q, d_cq = _fox_bwd_q(s["proj"], s["c_col"], s["c_row"], s["y_a"], d_ya, s["lse"], H, "fox_bwd_q")
        d_k, d_v_att, d_c = _fox_bwd_kv(s["proj"], s["c_col"], s["c_row"], s["y_a"], d_ya, s["lse"], H, "fox_bwd_kv")
        d_f = _fox_prep_bwd(s["tail"], _pad_cols(d_cq[:, :, 0].T, LANES), _pad_cols(d_c[:, 0, :].T, LANES), "fox_prep_bwd")
        d_main = jnp.concatenate([d_q, d_k, d_v_att, d_u, d_v, d_gqkv, dgate, dg0, dg1, dg2], axis=1)
        d_tail = _pad_cols(jnp.concatenate([d_f[:, :H], dal[:, :, 0].T, dbl[:, :, 0].T], axis=1), LANES)
        d_wmain = _matmul(s["h"], d_main, "tn", "proj_main_dw")
        d_wtail = _matmul(s["h"], d_tail, "tn", "proj_tail_dw")
        d_bmain = _colsum(d_main, "proj_main_db")
        d_btail = _colsum(d_tail, "proj_tail_db")
        dh = _matmul(d_main, lw["w_main"], "nt", "proj_main_dx", add=d_hr)
        dh = _matmul(d_tail, lw["w_tail"], "nt", "proj_tail_dx", add=dh)
        big_grads[l] = dict(w_in=_join_cols(d_wmain, d_wtail, H, D), w_proj_a=d_wa, w_proj_b=d_wb, w_proj_c=d_wc, w_out=d_wo,
                            ffn_w_up=jnp.concatenate([d_wg[:, :F], d_wv[:, :F]], axis=1), ffn_w_down=d_wd[:F])
        small_grads[l] = dict(b_in=_join_cols(d_bmain, d_btail, H, D)[0], sgu_ln_g=d_slg[0], sgu_ln_b=d_slb[0], sgu_w=d_sw,
                              sgu_b=d_sb[:, :, 0], gdn_conv_w=d_gcw[:, 0, :], gdn_a_log=d_alog[:, 0, 0], gdn_dt_bias=d_dt[:, 0, 0],
                              gdn_norm_g=d_ng[0, 0], ln1_g=d_ln1g[0], ln1_b=d_ln1b[0],
                              ffn_conv_w=jnp.concatenate([dcwg[:, 0, :F], dcwv[:, 0, :F]], axis=1),
                              ffn_conv_b=jnp.concatenate([dcbg[0, :F], dcbv[0, :F]]), ln2_g=d_ln2g[0], ln2_b=d_ln2b[0])
    grad_x = dh.reshape(1, S, D)

    grads = {n: [None] * L for n in WEIGHTS}
    for l in range(L):
        flat = jnp.concatenate([_flatten_for_chips(big_grads[l][n], n) for n in BIG] + [jnp.zeros((N_CHIPS, padded - total), f32)],
                               axis=1)
        g = flat.reshape(N_CHIPS, 2, Rh, PACK_COLS).transpose(1, 0, 2, 3)
        mine = lax.dynamic_index_in_dim(g, cc, axis=0, keepdims=False)
        pair = _add2(mine, _pair_exchange(g, "grad_pair_exchange"), "grad_pair_sum")
        half = _sum_leading(_chip_exchange(pair, "grad_chip_exchange"), "grad_chip_sum")
        local = _pair_gather(half, "grad_pair_gather").reshape(-1)
        off = 0
        for n in BIG:
            grads[n][l] = local[off:off + sizes[n]].reshape(shard_shapes[n])
            off += sizes[n]
    small_shapes = {n: small_grads[0][n].shape for n in SMALL}
    small_flat = jnp.concatenate([small_grads[l][n].reshape(-1) for l in range(L) for n in SMALL])
    small_sum = _all_reduce8(small_flat, "small_grads")
    off = 0
    for l in range(L):
        for n in SMALL:
            size = math.prod(small_shapes[n])
            g = small_sum[off:off + size].reshape(small_shapes[n])
            off += size
            if n == "gdn_conv_w":
                g = lax.dynamic_slice_in_dim(g, chip * gcw_cols, gcw_cols, axis=1)
            elif n == "ffn_conv_w":
                g = lax.dynamic_slice_in_dim(g, chip * fcw_cols, fcw_cols, axis=1)
            grads[n][l] = g
    grads = {n: jnp.stack(grads[n]) for n in WEIGHTS}

    deltas, new_m, new_v = {}, {}, {}
    for n in WEIGHTS:
        deltas[n], new_m[n], new_v[n] = _adamw(P[n], grads[n], M1[n], M2[n], "adamw_" + n)
    return (loss, grad_x, *[grads[n] for n in WEIGHTS], *[deltas[n] for n in WEIGHTS], *[new_m[n] for n in WEIGHTS],
            *[new_v[n] for n in WEIGHTS])
```

```python
import functools
import math

import jax
import jax.numpy as jnp
from jax import lax
from jax.experimental import pallas as pl
from jax.experimental.pallas import tpu as pltpu

f32 = jnp.float32
bf16 = jnp.bfloat16
HIGHEST = lax.Precision.HIGHEST
MESH = pl.DeviceIdType.MESH

HEAD_DIM = 128
CHUNK = 64
SGU_SPAN = 128
GDN_CONV = 4
FFN_CONV = 3
N_CHIPS = 4
N_DEV = 8
LN_EPS = 1e-5
RMS_EPS = 1e-6
ADAM_LR = 0.001
ADAM_B1 = 0.9
ADAM_B2 = 0.999
ADAM_EPS = 1e-08
ADAM_WD = 0.01
ADAM_STEP = 10
NEG_BIG = -1e30
LANES = 128
FF_ALIGN = 512
SUM_ROWS = 256
ATTN_TILE = 1024
VMEM_MARGIN = 12 << 20
VMEM_MOST = 60 << 20

SMALL = ("b_in", "sgu_ln_g", "sgu_ln_b", "sgu_w", "sgu_b", "gdn_conv_w", "gdn_a_log", "gdn_dt_bias", "gdn_norm_g",
         "ln1_g", "ln1_b", "ffn_conv_w", "ffn_conv_b", "ln2_g", "ln2_b")
WEIGHTS = ("w_in", "b_in", "sgu_ln_g", "sgu_ln_b", "sgu_w", "sgu_b", "gdn_conv_w", "gdn_a_log", "gdn_dt_bias", "gdn_norm_g",
           "w_proj_a", "w_proj_b", "w_proj_c", "w_out", "ln1_g", "ln1_b", "ffn_w_up", "ffn_conv_w", "ffn_conv_b", "ffn_w_down",
           "ln2_g", "ln2_b")

ANY = pl.BlockSpec(memory_space=pl.ANY)


def _tile(dim, pref):
    t = pref
    while t > 128 and dim % t:
        t //= 2
    return min(t, dim) if dim % min(t, dim) == 0 else dim


def _params(*sem, vmem=None):
    if vmem is None:
        return pltpu.CompilerParams(dimension_semantics=sem)
    return pltpu.CompilerParams(dimension_semantics=sem, vmem_limit_bytes=min(vmem + VMEM_MARGIN, VMEM_MOST))


def _attn_vmem(T):
    return 8 * T * T * 4


_DIMS = {"nn": (((1,), (0,)), ((), ())), "nt": (((1,), (1,)), ((), ())), "tn": (((0,), (0,)), ((), ()))}


def _pick(dim, most):
    for unit in (256, LANES):
        for t in range(min(most, dim) // unit * unit, 0, -unit):
            if dim % t == 0:
                return t
    return dim


def _matmul(a, b, mode, name, bias=None, add=None):
    a, b = a.astype(bf16), b.astype(bf16)
    if mode == "nn":
        (M, K), (_, N) = a.shape, b.shape
    elif mode == "nt":
        (M, K), (N, _) = a.shape, b.shape
    else:
        (K, M), (_, N) = a.shape, b.shape
    has_bias, has_add = bias is not None, add is not None
    tm, tn, tk = _pick(M, 512 if has_add else 1024), _pick(N, 2816), _pick(K, 512)
    nk = K // tk
    vmem = 2 * (tm * tk * a.dtype.itemsize + tk * tn * b.dtype.itemsize + tm * tn * 4 * (2 if has_add else 1)) + tm * tn * 4

    def body(*refs):
        a_ref, b_ref = refs[0], refs[1]
        pos = 2
        bias_ref = add_ref = None
        if has_bias:
            bias_ref, pos = refs[pos], pos + 1
        if has_add:
            add_ref, pos = refs[pos], pos + 1
        o_ref, acc_ref = refs[pos], refs[pos + 1]
        k = pl.program_id(2)

        @pl.when(k == 0)
        def _():
            acc_ref[...] = jnp.zeros_like(acc_ref)

        acc_ref[...] += lax.dot_general(a_ref[...], b_ref[...], _DIMS[mode], preferred_element_type=f32)

        @pl.when(k == nk - 1)
        def _():
            r = acc_ref[...]
            if has_bias:
                r = r + bias_ref[...]
            if has_add:
                r = r + add_ref[...]
            o_ref[...] = r

    if mode == "nn":
        specs = [pl.BlockSpec((tm, tk), lambda i, j, k: (i, k)), pl.BlockSpec((tk, tn), lambda i, j, k: (k, j))]
    elif mode == "nt":
        specs = [pl.BlockSpec((tm, tk), lambda i, j, k: (i, k)), pl.BlockSpec((tn, tk), lambda i, j, k: (j, k))]
    else:
        specs = [pl.BlockSpec((tk, tm), lambda i, j, k: (k, i)), pl.BlockSpec((tk, tn), lambda i, j, k: (k, j))]
    ops = [a, b]
    if has_bias:
        specs.append(pl.BlockSpec((1, tn), lambda i, j, k: (0, j)))
        ops.append(bias)
    if has_add:
        specs.append(pl.BlockSpec((tm, tn), lambda i, j, k: (i, j)))
        ops.append(add)
    return pl.pallas_call(
        body, name=name, grid=(M // tm, N // tn, nk), in_specs=specs,
        out_specs=pl.BlockSpec((tm, tn), lambda i, j, k: (i, j)),
        out_shape=jax.ShapeDtypeStruct((M, N), f32), scratch_shapes=[pltpu.VMEM((tm, tn), f32)],
        compiler_params=pltpu.CompilerParams(dimension_semantics=("parallel", "parallel", "arbitrary"),
                                             vmem_limit_bytes=min(vmem + VMEM_MARGIN, VMEM_MOST)))(*ops)


def _colsum(a, name):
    S, N = a.shape
    tn = _tile(N, 512)

    def body(a_ref, o_ref):
        o_ref[...] = jnp.sum(a_ref[...], axis=0, keepdims=True)

    return pl.pallas_call(body, name=name, grid=(N // tn,), in_specs=[pl.BlockSpec((S, tn), lambda j: (0, j))],
                          out_specs=pl.BlockSpec((1, tn), lambda j: (0, j)), out_shape=jax.ShapeDtypeStruct((1, N), f32),
                          compiler_params=_params("parallel"))(a)


def _ln_fn(alpha, x, y, g, b):
    z = alpha * x + y
    mu = jnp.mean(z, axis=-1, keepdims=True)
    zc = z - mu
    var = jnp.mean(zc * zc, axis=-1, keepdims=True)
    return zc * lax.rsqrt(var + LN_EPS) * g + b


def _ln_fwd(x, y, g, b, alpha, name):
    S, D = x.shape
    tr = _tile(S, 256)

    def body(x_ref, y_ref, g_ref, b_ref, o_ref):
        o_ref[...] = _ln_fn(alpha, x_ref[...], y_ref[...], g_ref[...], b_ref[...])

    row = pl.BlockSpec((tr, D), lambda i: (i, 0))
    par = pl.BlockSpec((1, D), lambda i: (0, 0))
    return pl.pallas_call(body, name=name, grid=(S // tr,), in_specs=[row, row, par, par], out_specs=row,
                          out_shape=jax.ShapeDtypeStruct((S, D), f32), compiler_params=_params("parallel"))(x, y, g, b)


def _ln_bwd(x, y, g, b, dout, alpha, name):
    S, D = x.shape
    tr = _tile(S, 256)

    def body(x_ref, y_ref, g_ref, b_ref, d_ref, dx_ref, dy_ref, dg_ref, db_ref):
        _, vjp = jax.vjp(functools.partial(_ln_fn, alpha), x_ref[...], y_ref[...], g_ref[...], b_ref[...])
        dx, dy, dg, db = vjp(d_ref[...])
        dx_ref[...] = dx
        dy_ref[...] = dy

        @pl.when(pl.program_id(0) == 0)
        def _():
            dg_ref[...] = jnp.zeros_like(dg_ref)
            db_ref[...] = jnp.zeros_like(db_ref)

        dg_ref[...] += dg
        db_ref[...] += db

    row = pl.BlockSpec((tr, D), lambda i: (i, 0))
    par = pl.BlockSpec((1, D), lambda i: (0, 0))
    sd = jax.ShapeDtypeStruct
    return pl.pallas_call(body, name=name, grid=(S // tr,), in_specs=[row, row, par, par, row],
                          out_specs=[row, row, par, par],
                          out_shape=[sd((S, D), f32), sd((S, D), f32), sd((1, D), f32), sd((1, D), f32)],
                          compiler_params=_params("arbitrary"))(x, y, g, b, dout)


def _loss_head(y, t, name):
    S, D = y.shape
    tr = _tile(S, 256)

    def body(y_ref, t_ref, l_ref, d_ref):
        e = y_ref[...] - t_ref[...]
        d_ref[...] = e / D

        @pl.when(pl.program_id(0) == 0)
        def _():
            l_ref[...] = jnp.zeros_like(l_ref)

        l_ref[...] += 0.5 * jnp.sum(jnp.mean(e * e, axis=-1, keepdims=True))

    row = pl.BlockSpec((tr, D), lambda i: (i, 0))
    return pl.pallas_call(body, name=name, grid=(S // tr,), in_specs=[row, row],
                          out_specs=[pl.BlockSpec((8, LANES), lambda i: (0, 0)), row],
                          out_shape=[jax.ShapeDtypeStruct((8, LANES), f32), jax.ShapeDtypeStruct((S, D), f32)],
                          compiler_params=_params("arbitrary"))(y, t)


def _merge_fn(g0, g1, g2, pa, pb, pc):
    return jax.nn.sigmoid(g0) * pa + jax.nn.sigmoid(g1) * pb + jax.nn.sigmoid(g2) * pc


def _merge_specs(S, D, gate_off):
    tr = _tile(S, 512)
    tc = _tile(math.gcd(gate_off, D), 512)
    gates = [pl.BlockSpec((tr, tc), functools.partial(lambda k, i, j: (i, (gate_off + k * D) // tc + j), k)) for k in range(3)]
    tile = pl.BlockSpec((tr, tc), lambda i, j: (i, j))
    return tr, tc, gates, tile


def _merge_fwd(proj, pa, pb, pc, gate_off, name):
    S, D = pa.shape
    tr, tc, gates, tile = _merge_specs(S, D, gate_off)

    def body(g0, g1, g2, a, b, c, o_ref):
        o_ref[...] = _merge_fn(g0[...], g1[...], g2[...], a[...], b[...], c[...])

    return pl.pallas_call(body, name=name, grid=(S // tr, D // tc), in_specs=gates + [tile] * 3, out_specs=tile,
                          out_shape=jax.ShapeDtypeStruct((S, D), f32),
                          compiler_params=_params("parallel", "parallel"))(proj, proj, proj, pa, pb, pc)


def _merge_bwd(proj, pa, pb, pc, dm, gate_off, name):
    S, D = pa.shape
    tr, tc, gates, tile = _merge_specs(S, D, gate_off)

    def body(g0, g1, g2, a, b, c, d, dg0, dg1, dg2, da, db, dc):
        _, vjp = jax.vjp(_merge_fn, g0[...], g1[...], g2[...], a[...], b[...], c[...])
        for ref, val in zip((dg0, dg1, dg2, da, db, dc), vjp(d[...])):
            ref[...] = val

    sd = jax.ShapeDtypeStruct
    return pl.pallas_call(body, name=name, grid=(S // tr, D // tc), in_specs=gates + [tile] * 4,
                          out_specs=[tile] * 6, out_shape=[sd((S, D), f32)] * 6,
                          compiler_params=_params("parallel", "parallel"))(proj, proj, proj, pa, pb, pc, dm)


def _sgu_fn(nb, u, v, ln_g, ln_b, w_s, b_s):
    mu = jnp.mean(v, axis=-1, keepdims=True)
    vc = v - mu
    var = jnp.mean(vc * vc, axis=-1, keepdims=True)
    vn = vc * lax.rsqrt(var + LN_EPS) * ln_g + ln_b
    r = lax.broadcasted_iota(jnp.int32, (SGU_SPAN, SGU_SPAN), 0) // CHUNK
    c = lax.broadcasted_iota(jnp.int32, (SGU_SPAN, SGU_SPAN), 1) // CHUNK
    wm = jnp.where(r >= c, w_s, 0.0)
    vn3 = vn.reshape(nb, SGU_SPAN, HEAD_DIM)
    mixed = lax.dot_general(jnp.broadcast_to(wm, (nb, SGU_SPAN, SGU_SPAN)), vn3, (((2,), (1,)), ((0,), (0,))),
                            preferred_element_type=f32)
    mixed = mixed + b_s
    return u * mixed.reshape(nb * SGU_SPAN, HEAD_DIM)


def _sgu_specs(S, G, u_off, v_off):
    nb = max(1, min(8, S // SGU_SPAN))
    rows = nb * SGU_SPAN
    ub = pl.BlockSpec((rows, HEAD_DIM), lambda g, n: (n, u_off // HEAD_DIM + g))
    vb = pl.BlockSpec((rows, HEAD_DIM), lambda g, n: (n, v_off // HEAD_DIM + g))
    lnb = pl.BlockSpec((1, HEAD_DIM), lambda g, n: (0, g))
    wb = pl.BlockSpec((None, SGU_SPAN, SGU_SPAN), lambda g, n: (g, 0, 0))
    bb = pl.BlockSpec((None, SGU_SPAN, 1), lambda g, n: (g, 0, 0))
    return nb, rows, ub, vb, lnb, wb, bb


def _sgu_fwd(proj, ln_g, ln_b, w_s, b_s, u_off, v_off, name):
    S = proj.shape[0]
    G = w_s.shape[0]
    nb, rows, ub, vb, lnb, wb, bb = _sgu_specs(S, G, u_off, v_off)

    def body(u, v, lg, lb, w, b, o_ref):
        o_ref[...] = _sgu_fn(nb, u[...], v[...], lg[...], lb[...], w[...], b[...])

    return pl.pallas_call(body, name=name, grid=(G, S // rows), in_specs=[ub, vb, lnb, lnb, wb, bb],
                          out_specs=pl.BlockSpec((rows, HEAD_DIM), lambda g, n: (n, g)),
                          out_shape=jax.ShapeDtypeStruct((S, G * HEAD_DIM), f32),
                          compiler_params=_params("parallel", "parallel"))(proj, proj, ln_g, ln_b, w_s, b_s)


def _sgu_bwd(proj, ln_g, ln_b, w_s, b_s, dy, u_off, v_off, name):
    S = proj.shape[0]
    G = w_s.shape[0]
    nb, rows, ub, vb, lnb, wb, bb = _sgu_specs(S, G, u_off, v_off)

    def body(u, v, lg, lb, w, b, d, du, dv, dlg, dlb, dw, db):
        _, vjp = jax.vjp(functools.partial(_sgu_fn, nb), u[...], v[...], lg[...], lb[...], w[...], b[...])
        gu, gv, glg, glb, gw, gb = vjp(d[...])
        du[...] = gu
        dv[...] = gv

        @pl.when(pl.program_id(1) == 0)
        def _():
            for ref in (dlg, dlb, dw, db):
                ref[...] = jnp.zeros_like(ref)

        dlg[...] += glg
        dlb[...] += glb
        dw[...] += gw
        db[...] += gb

    tile = pl.BlockSpec((rows, HEAD_DIM), lambda g, n: (n, g))
    sd = jax.ShapeDtypeStruct
    W = G * HEAD_DIM
    return pl.pallas_call(body, name=name, grid=(G, S // rows), in_specs=[ub, vb, lnb, lnb, wb, bb, tile],
                          out_specs=[tile, tile, lnb, lnb, wb, bb],
                          out_shape=[sd((S, W), f32), sd((S, W), f32), sd((1, W), f32), sd((1, W), f32),
                                     sd((G, SGU_SPAN, SGU_SPAN), f32), sd((G, SGU_SPAN, 1), f32)],
                          compiler_params=_params("parallel", "arbitrary"))(proj, proj, ln_g, ln_b, w_s, b_s, dy)


def _shift_down(x, k):
    if k == 0:
        return x
    rows = lax.broadcasted_iota(jnp.int32, x.shape, 0)
    return jnp.where(rows >= k, pltpu.roll(x, k, 0), 0.0)


def _shift_up(x, k):
    if k == 0:
        return x
    n = x.shape[0]
    rows = lax.broadcasted_iota(jnp.int32, x.shape, 0)
    return jnp.where(rows < n - k, pltpu.roll(x, n - k, 0), 0.0)


def _conv(x, w_ref, width):
    out = w_ref[width - 1] * x
    for j in range(width - 1):
        out = out + w_ref[j] * _shift_down(x, width - 1 - j)
    return out


def _conv_bwd(x, dz, w_ref, dw_ref, width):
    dx = w_ref[width - 1] * dz
    dw_ref[width - 1] = jnp.sum(dz * x, axis=0, keepdims=True)
    for j in range(width - 1):
        k = width - 1 - j
        dx = dx + w_ref[j] * _shift_up(dz, k)
        dw_ref[j] = jnp.sum(dz * _shift_down(x, k), axis=0, keepdims=True)
    return dx


def _silu(z):
    return z * jax.nn.sigmoid(z)


def _dsilu(z):
    s = jax.nn.sigmoid(z)
    return s * (1.0 + z * (1.0 - s))


def _ffn_act_fwd(upg, upv, cwg, cwv, cbg, cbv, name):
    S, Fp = upg.shape

    def body(g_ref, v_ref, wg, wv, bg, bv, o_ref):
        hg = _conv(g_ref[...], wg, FFN_CONV) + bg[...]
        hv = _conv(v_ref[...], wv, FFN_CONV) + bv[...]
        o_ref[...] = _silu(hg) * hv

    col = pl.BlockSpec((S, LANES), lambda j: (0, j))
    wsp = pl.BlockSpec((FFN_CONV, 1, LANES), lambda j: (0, 0, j))
    bsp = pl.BlockSpec((1, LANES), lambda j: (0, j))
    return pl.pallas_call(body, name=name, grid=(Fp // LANES,), in_specs=[col, col, wsp, wsp, bsp, bsp], out_specs=col,
                          out_shape=jax.ShapeDtypeStruct((S, Fp), f32),
                          compiler_params=_params("parallel"))(upg, upv, cwg, cwv, cbg, cbv)


def _ffn_act_bwd(upg, upv, cwg, cwv, cbg, cbv, dact, name):
    S, Fp = upg.shape

    def body(g_ref, v_ref, wg, wv, bg, bv, d_ref, dg_ref, dv_ref, dwg, dwv, dbg, dbv):
        xg, xv, d = g_ref[...], v_ref[...], d_ref[...]
        hg = _conv(xg, wg, FFN_CONV) + bg[...]
        hv = _conv(xv, wv, FFN_CONV) + bv[...]
        dhg = d * hv * _dsilu(hg)
        dhv = d * _silu(hg)
        dbg[...] = jnp.sum(dhg, axis=0, keepdims=True)
        dbv[...] = jnp.sum(dhv, axis=0, keepdims=True)
        dg_ref[...] = _conv_bwd(xg, dhg, wg, dwg, FFN_CONV)
        dv_ref[...] = _conv_bwd(xv, dhv, wv, dwv, FFN_CONV)

    col = pl.BlockSpec((S, LANES), lambda j: (0, j))
    wsp = pl.BlockSpec((FFN_CONV, 1, LANES), lambda j: (0, 0, j))
    bsp = pl.BlockSpec((1, LANES), lambda j: (0, j))
    sd = jax.ShapeDtypeStruct
    return pl.pallas_call(body, name=name, grid=(Fp // LANES,), in_specs=[col, col, wsp, wsp, bsp, bsp, col],
                          out_specs=[col, col, wsp, wsp, bsp, bsp],
                          out_shape=[sd((S, Fp), f32), sd((S, Fp), f32), sd((FFN_CONV, 1, Fp), f32), sd((FFN_CONV, 1, Fp), f32),
                                     sd((1, Fp), f32), sd((1, Fp), f32)],
                          compiler_params=_params("parallel"))(upg, upv, cwg, cwv, cbg, cbv, dact)


def _gdn_pre_fwd(proj, cw, x_off, n_norm, name):
    S = proj.shape[0]
    C = cw.shape[2]

    def body(x_ref, w_ref, o_ref):
        s = _silu(_conv(x_ref[...], w_ref, GDN_CONV))
        r = lax.rsqrt(jnp.sum(s * s, axis=-1, keepdims=True) + RMS_EPS)
        o_ref[...] = jnp.where(pl.program_id(0) < n_norm, s * r, s)

    xs = pl.BlockSpec((S, LANES), lambda j: (0, x_off // LANES + j))
    col = pl.BlockSpec((S, LANES), lambda j: (0, j))
    wsp = pl.BlockSpec((GDN_CONV, 1, LANES), lambda j: (0, 0, j))
    return pl.pallas_call(body, name=name, grid=(C // LANES,), in_specs=[xs, wsp], out_specs=col,
                          out_shape=jax.ShapeDtypeStruct((S, C), f32), compiler_params=_params("parallel"))(proj, cw)


def _gdn_pre_bwd(proj, cw, dout, x_off, n_norm, name):
    S = proj.shape[0]
    C = cw.shape[2]

    def body(x_ref, w_ref, d_ref, dx_ref, dw_ref):
        x, d = x_ref[...], d_ref[...]
        z = _conv(x, w_ref, GDN_CONV)
        s = _silu(z)
        r = lax.rsqrt(jnp.sum(s * s, axis=-1, keepdims=True) + RMS_EPS)
        ds_norm = d * r - s * (r * r * r) * jnp.sum(d * s, axis=-1, keepdims=True)
        ds = jnp.where(pl.program_id(0) < n_norm, ds_norm, d)
        dz = ds * _dsilu(z)
        dx_ref[...] = _conv_bwd(x, dz, w_ref, dw_ref, GDN_CONV)

    xs = pl.BlockSpec((S, LANES), lambda j: (0, x_off // LANES + j))
    col = pl.BlockSpec((S, LANES), lambda j: (0, j))
    wsp = pl.BlockSpec((GDN_CONV, 1, LANES), lambda j: (0, 0, j))
    return pl.pallas_call(body, name=name, grid=(C // LANES,), in_specs=[xs, wsp, col], out_specs=[col, wsp],
                          out_shape=[jax.ShapeDtypeStruct((S, C), f32), jax.ShapeDtypeStruct((GDN_CONV, 1, C), f32)],
                          compiler_params=_params("parallel"))(proj, cw, dout)


def _bmm(a, b, prec=None):
    return lax.dot_general(a, b, (((2,), (1,)), ((0,), (0,))), precision=prec, preferred_element_type=f32)


def _bmm_nt(a, b, prec=None):
    return lax.dot_general(a, b, (((2,), (2,)), ((0,), (0,))), precision=prec, preferred_element_type=f32)


def _bmm_tn(a, b, prec=None):
    return lax.dot_general(a, b, (((1,), (1,)), ((0,), (0,))), precision=prec, preferred_element_type=f32)


def _softplus(x):
    return jnp.maximum(x, 0.0) + jnp.log1p(jnp.exp(-jnp.abs(x)))


def _gdn_chunk(q, k, v, al, bl, gate, a_log, dt_bias, norm_g, state):
    H, C, Dh = q.shape
    r = lax.broadcasted_iota(jnp.int32, (H, C, C), 1)
    c = lax.broadcasted_iota(jnp.int32, (H, C, C), 2)
    tril = r >= c
    strict = r > c
    lower = tril.astype(f32)
    upper = (r <= c).astype(f32)
    ones = jnp.ones((H, C, C), f32)
    g = -jnp.exp(a_log) * _softplus(al + dt_bias)
    beta = jax.nn.sigmoid(bl)
    g_lanes = jnp.broadcast_to(g, (H, C, Dh))
    g_sq = jnp.broadcast_to(g, (H, C, C))
    gc = _bmm(lower, g_lanes, HIGHEST)
    gc_i = _bmm(lower, g_sq, HIGHEST)
    gc_j = _bmm(ones, g_sq * upper, HIGHEST)
    decay = jnp.where(tril, jnp.exp(jnp.where(tril, gc_i - gc_j, 0.0)), 0.0)
    qs = q * (Dh ** -0.5)
    kb = k * beta
    a_kk = jnp.where(strict, _bmm_nt(kb, k) * decay, 0.0)
    rhs_u = v * beta
    rhs_w = kb * jnp.exp(gc)
    p = -a_kk
    inv = (r == c).astype(f32) + p
    for _ in range(int(math.log2(C)) - 1):
        p = _bmm(p, p, HIGHEST)
        inv = inv + _bmm(inv, p, HIGHEST)
    u = _bmm(inv, rhs_u, HIGHEST)
    w = _bmm(inv, rhs_w, HIGHEST)
    qk = jnp.where(tril, _bmm_nt(qs, k) * decay, 0.0)
    g_last = jnp.sum(g, axis=1, keepdims=True)
    k_dec = k * jnp.exp(g_last - gc)
    q_dec = qs * jnp.exp(gc)
    v_new = u - _bmm(w, state)
    o = _bmm(q_dec, state) + _bmm(qk, v_new)
    new_state = state * jnp.exp(g_last) + _bmm_tn(k_dec, v_new)
    y = o * lax.rsqrt(jnp.mean(o * o, axis=-1, keepdims=True) + RMS_EPS) * norm_g * _silu(gate)
    return y, new_state


def _heads(ref, off, H):
    return jnp.stack([ref[:, off + h * HEAD_DIM: off + (h + 1) * HEAD_DIM] for h in range(H)])


def _gdn_scan_fwd(qkvc, al, bl, proj, a_log, dt_bias, norm_g, gate_off, name):
    S = qkvc.shape[0]
    H = al.shape[0]
    W = H * HEAD_DIM
    n = S // CHUNK

    def body(x_ref, al_ref, bl_ref, gate_ref, alog_ref, dt_ref, ng_ref, y_ref, st_ref, state):
        @pl.when(pl.program_id(0) == 0)
        def _():
            state[...] = jnp.zeros_like(state)

        st_ref[...] = state[...]
        y, new = _gdn_chunk(_heads(x_ref, 0, H), _heads(x_ref, W, H), _heads(x_ref, 2 * W, H), al_ref[...], bl_ref[...],
                            _heads(gate_ref, 0, H), alog_ref[...], dt_ref[...], ng_ref[...], state[...])
        state[...] = new
        for h in range(H):
            y_ref[:, h * HEAD_DIM:(h + 1) * HEAD_DIM] = y[h]

    sd = jax.ShapeDtypeStruct
    col = pl.BlockSpec((H, CHUNK, 1), lambda i: (0, i, 0))
    par = pl.BlockSpec((H, 1, 1), lambda i: (0, 0, 0))
    return pl.pallas_call(
        body, name=name, grid=(n,),
        in_specs=[pl.BlockSpec((CHUNK, 3 * W), lambda i: (i, 0)), col, col,
                  pl.BlockSpec((CHUNK, W), lambda i: (i, gate_off // W)), par, par,
                  pl.BlockSpec((1, 1, HEAD_DIM), lambda i: (0, 0, 0))],
        out_specs=[pl.BlockSpec((CHUNK, W), lambda i: (i, 0)),
                   pl.BlockSpec((None, H, HEAD_DIM, HEAD_DIM), lambda i: (i, 0, 0, 0))],
        out_shape=[sd((S, W), f32), sd((n, H, HEAD_DIM, HEAD_DIM), f32)],
        scratch_shapes=[pltpu.VMEM((H, HEAD_DIM, HEAD_DIM), f32)],
        compiler_params=_params("arbitrary"))(qkvc, al, bl, proj, a_log, dt_bias, norm_g)


def _gdn_scan_bwd(qkvc, al, bl, proj, a_log, dt_bias, norm_g, states, dy, gate_off, name):
    S = qkvc.shape[0]
    H = al.shape[0]
    W = H * HEAD_DIM
    n = S // CHUNK

    def body(x_ref, al_ref, bl_ref, gate_ref, alog_ref, dt_ref, ng_ref, st_ref, dy_ref,
             dx_ref, dal_ref, dbl_ref, dgate_ref, dalog_ref, ddt_ref, dng_ref, dstate):
        @pl.when(pl.program_id(0) == 0)
        def _():
            dstate[...] = jnp.zeros_like(dstate)
            for ref in (dalog_ref, ddt_ref, dng_ref):
                ref[...] = jnp.zeros_like(ref)

        _, vjp = jax.vjp(_gdn_chunk, _heads(x_ref, 0, H), _heads(x_ref, W, H), _heads(x_ref, 2 * W, H), al_ref[...],
                         bl_ref[...], _heads(gate_ref, 0, H), alog_ref[...], dt_ref[...], ng_ref[...], st_ref[...])
        dq, dk, dv, dal, dbl, dgate, dalog, ddt, dng, dst = vjp((_heads(dy_ref, 0, H), dstate[...]))
        dstate[...] = dst
        for h in range(H):
            lo, hi = h * HEAD_DIM, (h + 1) * HEAD_DIM
            dx_ref[:, lo:hi] = dq[h]
            dx_ref[:, W + lo:W + hi] = dk[h]
            dx_ref[:, 2 * W + lo:2 * W + hi] = dv[h]
            dgate_ref[:, lo:hi] = dgate[h]
        dal_ref[...] = dal
        dbl_ref[...] = dbl
        dalog_ref[...] += dalog
        ddt_ref[...] += ddt
        dng_ref[...] += dng

    sd = jax.ShapeDtypeStruct
    rev = lambda i: n - 1 - i
    col = pl.BlockSpec((H, CHUNK, 1), lambda i: (0, rev(i), 0))
    par = pl.BlockSpec((H, 1, 1), lambda i: (0, 0, 0))
    ng = pl.BlockSpec((1, 1, HEAD_DIM), lambda i: (0, 0, 0))
    xs = pl.BlockSpec((CHUNK, 3 * W), lambda i: (rev(i), 0))
    ws = pl.BlockSpec((CHUNK, W), lambda i: (rev(i), 0))
    return pl.pallas_call(
        body, name=name, grid=(n,),
        in_specs=[xs, col, col, pl.BlockSpec((CHUNK, W), lambda i: (rev(i), gate_off // W)), par, par, ng,
                  pl.BlockSpec((None, H, HEAD_DIM, HEAD_DIM), lambda i: (rev(i), 0, 0, 0)), ws],
        out_specs=[xs, col, col, ws, par, par, ng],
        out_shape=[sd((S, 3 * W), f32), sd((H, S, 1), f32), sd((H, S, 1), f32), sd((S, W), f32), sd((H, 1, 1), f32),
                   sd((H, 1, 1), f32), sd((1, 1, HEAD_DIM), f32)],
        scratch_shapes=[pltpu.VMEM((H, HEAD_DIM, HEAD_DIM), f32)],
        compiler_params=_params("arbitrary"))(qkvc, al, bl, proj, a_log, dt_bias, norm_g, states, dy)


def _tri(n, upper):
    r = lax.broadcasted_iota(jnp.int32, (n, n), 0)
    c = lax.broadcasted_iota(jnp.int32, (n, n), 1)
    return (r <= c if upper else r >= c).astype(f32)


def _fox_prep_fwd(tail, name):
    S = tail.shape[0]
    tb = _tile(S, 512)

    def body(x_ref, o_ref, carry):
        @pl.when(pl.program_id(0) == 0)
        def _():
            carry[...] = jnp.zeros_like(carry)

        x = x_ref[...]
        lf = jnp.minimum(x, 0.0) - jnp.log1p(jnp.exp(-jnp.abs(x)))
        o_ref[...] = jnp.dot(_tri(tb, False), lf, precision=HIGHEST, preferred_element_type=f32) + carry[...]
        carry[...] += jnp.sum(lf, axis=0, keepdims=True)

    blk = pl.BlockSpec((tb, LANES), lambda i: (i, 0))
    return pl.pallas_call(body, name=name, grid=(S // tb,), in_specs=[blk], out_specs=blk,
                          out_shape=jax.ShapeDtypeStruct((S, LANES), f32), scratch_shapes=[pltpu.VMEM((1, LANES), f32)],
                          compiler_params=_params("arbitrary"))(tail)


def _fox_prep_bwd(tail, dc_q, dc_k, name):
    S = tail.shape[0]
    tb = _tile(S, 512)
    nb = S // tb

    def body(x_ref, dq_ref, d_ref, o_ref, carry):
        @pl.when(pl.program_id(0) == 0)
        def _():
            carry[...] = jnp.zeros_like(carry)

        d = d_ref[...] + dq_ref[...]
        dlf = jnp.dot(_tri(tb, True), d, precision=HIGHEST, preferred_element_type=f32) + carry[...]
        carry[...] += jnp.sum(d, axis=0, keepdims=True)
        o_ref[...] = dlf * jax.nn.sigmoid(-x_ref[...])

    blk = pl.BlockSpec((tb, LANES), lambda i: (nb - 1 - i, 0))
    return pl.pallas_call(body, name=name, grid=(nb,), in_specs=[blk, blk, blk], out_specs=blk,
                          out_shape=jax.ShapeDtypeStruct((S, LANES), f32), scratch_shapes=[pltpu.VMEM((1, LANES), f32)],
                          compiler_params=_params("arbitrary"))(tail, dc_q, dc_k)


def _dot_nt(a, b):
    return lax.dot_general(a.astype(bf16), b.astype(bf16), _DIMS["nt"], preferred_element_type=f32)


def _dot_tn(a, b):
    return lax.dot_general(a.astype(bf16), b.astype(bf16), _DIMS["tn"], preferred_element_type=f32)


def _dot_nn(a, b):
    return lax.dot_general(a.astype(bf16), b.astype(bf16), _DIMS["nn"], preferred_element_type=f32)


def _fox_logits(q, k, cc, cr, i, j, T):
    s = _dot_nt(q, k) * (HEAD_DIM ** -0.5) + cc - cr
    qpos = i * T + lax.broadcasted_iota(jnp.int32, (T, T), 0)
    kpos = j * T + lax.broadcasted_iota(jnp.int32, (T, T), 1)
    return jnp.where(qpos >= kpos, s, NEG_BIG)


def _fox_fwd(proj, c_col, c_row, H, name):
    S = proj.shape[0]
    T = _tile(S, ATTN_TILE)
    nt = S // T

    def body(q_ref, k_ref, v_ref, cc_ref, cr_ref, o_ref, lse_ref, m_s, l_s, acc_s):
        i, j = pl.program_id(1), pl.program_id(2)

        @pl.when(j == 0)
        def _():
            m_s[...] = jnp.full_like(m_s, NEG_BIG)
            l_s[...] = jnp.zeros_like(l_s)
            acc_s[...] = jnp.zeros_like(acc_s)

        @pl.when(j <= i)
        def _():
            s = _fox_logits(q_ref[...], k_ref[...], cc_ref[...], cr_ref[...], i, j, T)
            m_new = jnp.maximum(m_s[...], jnp.max(s, axis=-1, keepdims=True))
            p = jnp.exp(s - m_new)
            corr = jnp.exp(m_s[...] - m_new)
            l_s[...] = corr * l_s[...] + jnp.sum(p, axis=-1, keepdims=True)
            acc_s[...] = corr * acc_s[...] + _dot_nn(p, v_ref[...])
            m_s[...] = m_new

        @pl.when(j == i)
        def _():
            o_ref[...] = acc_s[...] / l_s[...]
            lse_ref[...] = m_s[...] + jnp.log(l_s[...])

    sd = jax.ShapeDtypeStruct
    return pl.pallas_call(
        body, name=name, grid=(H, nt, nt),
        in_specs=[pl.BlockSpec((T, HEAD_DIM), lambda h, i, j: (i, h)),
                  pl.BlockSpec((T, HEAD_DIM), lambda h, i, j: (jnp.minimum(j, i), H + h)),
                  pl.BlockSpec((T, HEAD_DIM), lambda h, i, j: (jnp.minimum(j, i), 2 * H + h)),
                  pl.BlockSpec((None, T, 1), lambda h, i, j: (h, i, 0)),
                  pl.BlockSpec((None, 1, T), lambda h, i, j: (h, 0, jnp.minimum(j, i)))],
        out_specs=[pl.BlockSpec((T, HEAD_DIM), lambda h, i, j: (i, h)), pl.BlockSpec((None, T, 1), lambda h, i, j: (h, i, 0))],
        out_shape=[sd((S, H * HEAD_DIM), f32), sd((H, S, 1), f32)],
        scratch_shapes=[pltpu.VMEM((T, 1), f32), pltpu.VMEM((T, 1), f32), pltpu.VMEM((T, HEAD_DIM), f32)],
        compiler_params=_params("parallel", "parallel", "arbitrary", vmem=_attn_vmem(T)))(proj, proj, proj, c_col, c_row)


def _fox_bwd_q(proj, c_col, c_row, o, do, lse, H, name):
    S = proj.shape[0]
    T = _tile(S, ATTN_TILE)
    nt = S // T

    def body(q_ref, k_ref, v_ref, cc_ref, cr_ref, o_ref, do_ref, lse_ref, dq_ref, dc_ref, acc_s, dc_s):
        i, j = pl.program_id(1), pl.program_id(2)

        @pl.when(j == 0)
        def _():
            acc_s[...] = jnp.zeros_like(acc_s)
            dc_s[...] = jnp.zeros_like(dc_s)

        @pl.when(j <= i)
        def _():
            s = _fox_logits(q_ref[...], k_ref[...], cc_ref[...], cr_ref[...], i, j, T)
            p = jnp.exp(s - lse_ref[...])
            do_ = do_ref[...]
            delta = jnp.sum(o_ref[...] * do_, axis=-1, keepdims=True)
            ds = p * (_dot_nt(do_, v_ref[...]) - delta)
            acc_s[...] += _dot_nn(ds, k_ref[...])
            dc_s[...] += jnp.sum(ds, axis=-1, keepdims=True)

        @pl.when(j == i)
        def _():
            dq_ref[...] = acc_s[...] * (HEAD_DIM ** -0.5)
            dc_ref[...] = dc_s[...]

    qb = pl.BlockSpec((T, HEAD_DIM), lambda h, i, j: (i, h))
    col = pl.BlockSpec((None, T, 1), lambda h, i, j: (h, i, 0))
    return pl.pallas_call(
        body, name=name, grid=(H, nt, nt),
        in_specs=[qb, pl.BlockSpec((T, HEAD_DIM), lambda h, i, j: (jnp.minimum(j, i), H + h)),
                  pl.BlockSpec((T, HEAD_DIM), lambda h, i, j: (jnp.minimum(j, i), 2 * H + h)),
                  col, pl.BlockSpec((None, 1, T), lambda h, i, j: (h, 0, jnp.minimum(j, i))), qb, qb, col],
        out_specs=[qb, col], out_shape=[jax.ShapeDtypeStruct((S, H * HEAD_DIM), f32), jax.ShapeDtypeStruct((H, S, 1), f32)],
        scratch_shapes=[pltpu.VMEM((T, HEAD_DIM), f32), pltpu.VMEM((T, 1), f32)],
        compiler_params=_params("parallel", "parallel", "arbitrary", vmem=_attn_vmem(T)))(proj, proj, proj, c_col, c_row, o, do, lse)


def _fox_bwd_kv(proj, c_col, c_row, o, do, lse, H, name):
    S = proj.shape[0]
    T = _tile(S, ATTN_TILE)
    nt = S // T

    def body(q_ref, k_ref, v_ref, cc_ref, cr_ref, o_ref, do_ref, lse_ref, dk_ref, dv_ref, dc_ref, dk_s, dv_s, dc_s):
        j, i = pl.program_id(1), pl.program_id(2)

        @pl.when(i == 0)
        def _():
            dk_s[...] = jnp.zeros_like(dk_s)
            dv_s[...] = jnp.zeros_like(dv_s)
            dc_s[...] = jnp.zeros_like(dc_s)

        @pl.when(i >= j)
        def _():
            s = _fox_logits(q_ref[...], k_ref[...], cc_ref[...], cr_ref[...], i, j, T)
            p = jnp.exp(s - lse_ref[...])
            do_ = do_ref[...]
            delta = jnp.sum(o_ref[...] * do_, axis=-1, keepdims=True)
            ds = p * (_dot_nt(do_, v_ref[...]) - delta)
            dv_s[...] += _dot_tn(p, do_)
            dk_s[...] += _dot_tn(ds, q_ref[...])
            dc_s[...] -= jnp.sum(ds, axis=0, keepdims=True)

        @pl.when(i == nt - 1)
        def _():
            dk_ref[...] = dk_s[...] * (HEAD_DIM ** -0.5)
            dv_ref[...] = dv_s[...]
            dc_ref[...] = dc_s[...]

    qb = pl.BlockSpec((T, HEAD_DIM), lambda h, j, i: (jnp.maximum(i, j), h))
    col = pl.BlockSpec((None, T, 1), lambda h, j, i: (h, jnp.maximum(i, j), 0))
    kb = pl.BlockSpec((T, HEAD_DIM), lambda h, j, i: (j, h))
    sd = jax.ShapeDtypeStruct
    return pl.pallas_call(
        body, name=name, grid=(H, nt, nt),
        in_specs=[qb, pl.BlockSpec((T, HEAD_DIM), lambda h, j, i: (j, H + h)),
                  pl.BlockSpec((T, HEAD_DIM), lambda h, j, i: (j, 2 * H + h)),
                  col, pl.BlockSpec((None, 1, T), lambda h, j, i: (h, 0, j)), qb, qb, col],
        out_specs=[kb, kb, pl.BlockSpec((None, 1, T), lambda h, j, i: (h, 0, j))],
        out_shape=[sd((S, H * HEAD_DIM), f32), sd((S, H * HEAD_DIM), f32), sd((H, 1, S), f32)],
        scratch_shapes=[pltpu.VMEM((T, HEAD_DIM), f32), pltpu.VMEM((T, HEAD_DIM), f32), pltpu.VMEM((1, T), f32)],
        compiler_params=_params("parallel", "parallel", "arbitrary", vmem=_attn_vmem(T)))(proj, proj, proj, c_col, c_row, o, do, lse)


def _adamw(w, g, m, v, name):
    shape = w.shape
    cols = shape[-1]
    rows = w.size // cols
    ops = [t.reshape(rows, cols) for t in (w, g, m, v)]
    tr = rows
    if rows % 8 == 0:
        tr = 8
        while tr * 2 <= rows and rows % (tr * 2) == 0 and tr * 2 * cols * 4 <= (1 << 20):
            tr *= 2

    def body(w_ref, g_ref, m_ref, v_ref, d_ref, mo_ref, vo_ref):
        g_ = g_ref[...]
        m_ = ADAM_B1 * m_ref[...] + (1.0 - ADAM_B1) * g_
        v_ = ADAM_B2 * v_ref[...] + (1.0 - ADAM_B2) * (g_ * g_)
        m_hat = m_ / (1.0 - ADAM_B1 ** ADAM_STEP)
        v_hat = v_ / (1.0 - ADAM_B2 ** ADAM_STEP)
        d_ref[...] = -ADAM_LR * (m_hat / (jnp.sqrt(v_hat) + ADAM_EPS) + ADAM_WD * w_ref[...])
        mo_ref[...] = m_
        vo_ref[...] = v_

    blk = pl.BlockSpec((tr, cols), lambda i: (i, 0))
    outs = pl.pallas_call(body, name=name, grid=(rows // tr,), in_specs=[blk] * 4, out_specs=[blk] * 3,
                          out_shape=[jax.ShapeDtypeStruct((rows, cols), f32)] * 3, compiler_params=_params("parallel"))(*ops)
    return [o.reshape(shape) for o in outs]


def _row_tile(rows, row_bytes, budget=2 << 20):
    best = None
    for t in range(8, rows + 1, 8):
        if rows % t == 0 and t * row_bytes <= budget:
            best = t
    return best or rows


def _sum_leading(a, name):
    n, R, C = a.shape
    tr = _row_tile(R, n * C * 4)

    def body(a_ref, o_ref):
        acc = a_ref[0]
        for k in range(1, n):
            acc = acc + a_ref[k]
        o_ref[...] = acc

    return pl.pallas_call(body, name=name, grid=(R // tr,), in_specs=[pl.BlockSpec((n, tr, C), lambda i: (0, i, 0))],
                          out_specs=pl.BlockSpec((tr, C), lambda i: (i, 0)), out_shape=jax.ShapeDtypeStruct((R, C), f32),
                          compiler_params=_params("parallel"))(a)


def _add2(a, b, name):
    n, R, C = a.shape
    tr = _row_tile(R, C * 4)

    def body(a_ref, b_ref, o_ref):
        o_ref[...] = a_ref[...] + b_ref[...]

    blk = pl.BlockSpec((None, tr, C), lambda k, i: (k, i, 0))
    return pl.pallas_call(body, name=name, grid=(n, R // tr), in_specs=[blk, blk], out_specs=blk,
                          out_shape=jax.ShapeDtypeStruct((n, R, C), f32), compiler_params=_params("parallel", "parallel"))(a, b)


def _place():
    x, y, c = lax.axis_index("x"), lax.axis_index("y"), lax.axis_index("c")
    chips = [(1 - x, y), (x, 1 - y), (1 - x, 1 - y)]
    return x, y, c, chips


def _gather_weights(ws, name):
    _, R, C = ws.shape

    def body(ws_ref, out_ref, send_sems, recv_sems, local_sem):
        x, y, c, chips = _place()
        sibling = (x, y, 1 - c)

        def half(px, py):
            return out_ref.at[2 * px + py, c]

        def copy(k, src, dst, to):
            return pltpu.make_async_remote_copy(src_ref=src, dst_ref=dst, send_sem=send_sems.at[k], recv_sem=recv_sems.at[k],
                                                device_id=to, device_id_type=MESH)

        mine = pltpu.make_async_copy(ws_ref, out_ref.at[2 * x + y], local_sem)
        mine.start()
        first = [copy(j, ws_ref.at[c], half(x, y), (*chip, c)) for j, chip in enumerate(chips)]
        for cp in first:
            cp.start()
        passed = [copy(3 + j, half(*chip), half(*chip), sibling) for j, chip in enumerate(chips)]
        for j, chip in enumerate(chips):
            copy(j, ws_ref.at[c], half(*chip), (*chip, c)).wait_recv()
            passed[j].start()
        for j, chip in enumerate(chips):
            pltpu.make_async_remote_copy(src_ref=half(*chip), dst_ref=out_ref.at[2 * chip[0] + chip[1], 1 - c],
                                         send_sem=send_sems.at[3 + j], recv_sem=recv_sems.at[3 + j], device_id=sibling,
                                         device_id_type=MESH).wait_recv()
        for cp in first + passed:
            cp.wait_send()
        mine.wait()

    return pl.pallas_call(body, name=name, out_shape=jax.ShapeDtypeStruct((N_CHIPS, 2, R, C), ws.dtype), in_specs=[ANY],
                          out_specs=ANY,
                          scratch_shapes=[pltpu.SemaphoreType.DMA((6,)), pltpu.SemaphoreType.DMA((6,)), pltpu.SemaphoreType.DMA(())])(ws)


def _pair_exchange(g, name):
    n, _, R, C = g.shape

    def body(g_ref, out_ref, send_sems, recv_sems):
        x, y, c, _ = _place()
        cps = [pltpu.make_async_remote_copy(src_ref=g_ref.at[s, 1 - c], dst_ref=out_ref.at[s], send_sem=send_sems.at[s],
                                            recv_sem=recv_sems.at[s], device_id=(x, y, 1 - c), device_id_type=MESH)
               for s in range(n)]
        for cp in cps:
            cp.start()
        for cp in cps:
            cp.wait()

    return pl.pallas_call(body, name=name, out_shape=jax.ShapeDtypeStruct((n, R, C), g.dtype), in_specs=[ANY], out_specs=ANY,
                          scratch_shapes=[pltpu.SemaphoreType.DMA((n,)), pltpu.SemaphoreType.DMA((n,))])(g)


def _chip_exchange(a, name):
    n, R, C = a.shape

    def body(a_ref, out_ref, send_sems, recv_sems, local_sem):
        x, y, c, chips = _place()
        me = 2 * x + y
        mine = pltpu.make_async_copy(a_ref.at[me], out_ref.at[me], local_sem)
        mine.start()
        sends = [pltpu.make_async_remote_copy(src_ref=a_ref.at[2 * chip[0] + chip[1]], dst_ref=out_ref.at[me],
                                              send_sem=send_sems.at[j], recv_sem=recv_sems.at[j], device_id=(*chip, c),
                                              device_id_type=MESH) for j, chip in enumerate(chips)]
        for cp in sends:
            cp.start()
        for j, chip in enumerate(chips):
            pltpu.make_async_remote_copy(src_ref=a_ref.at[me], dst_ref=out_ref.at[2 * chip[0] + chip[1]],
                                         send_sem=send_sems.at[j], recv_sem=recv_sems.at[j], device_id=(*chip, c),
                                         device_id_type=MESH).wait_recv()
        for cp in sends:
            cp.wait_send()
        mine.wait()

    return pl.pallas_call(body, name=name, out_shape=jax.ShapeDtypeStruct((n, R, C), a.dtype), in_specs=[ANY], out_specs=ANY,
                          scratch_shapes=[pltpu.SemaphoreType.DMA((3,)), pltpu.SemaphoreType.DMA((3,)), pltpu.SemaphoreType.DMA(())])(a)


def _pair_gather(r, name):
    R, C = r.shape

    def body(r_ref, out_ref, send_sem, recv_sem, local_sem):
        x, y, c, _ = _place()
        mine = pltpu.make_async_copy(r_ref, out_ref.at[c], local_sem)
        mine.start()
        cp = pltpu.make_async_remote_copy(src_ref=r_ref, dst_ref=out_ref.at[c], send_sem=send_sem, recv_sem=recv_sem,
                                          device_id=(x, y, 1 - c), device_id_type=MESH)
        cp.start()
        pltpu.make_async_remote_copy(src_ref=r_ref, dst_ref=out_ref.at[1 - c], send_sem=send_sem, recv_sem=recv_sem,
                                     device_id=(x, y, 1 - c), device_id_type=MESH).wait_recv()
        cp.wait_send()
        mine.wait()

    return pl.pallas_call(body, name=name, out_shape=jax.ShapeDtypeStruct((2, R, C), r.dtype), in_specs=[ANY], out_specs=ANY,
                          scratch_shapes=[pltpu.SemaphoreType.DMA(()), pltpu.SemaphoreType.DMA(()), pltpu.SemaphoreType.DMA(())])(r)


def _all_gather8(v, name):
    R, C = v.shape

    def body(v_ref, out_ref, send_sems, recv_sems, local_sem):
        x, y, c, _ = _place()
        me = 4 * x + 2 * y + c
        mine = pltpu.make_async_copy(v_ref, out_ref.at[me], local_sem)
        mine.start()
        peers = [(x ^ (k >> 2), y ^ ((k >> 1) & 1), c ^ (k & 1)) for k in range(1, N_DEV)]
        sends = [pltpu.make_async_remote_copy(src_ref=v_ref, dst_ref=out_ref.at[me], send_sem=send_sems.at[k], recv_sem=recv_sems.at[k],
                                              device_id=peer, device_id_type=MESH) for k, peer in enumerate(peers)]
        for cp in sends:
            cp.start()
        for k, (px, py, pc) in enumerate(peers):
            pltpu.make_async_remote_copy(src_ref=v_ref, dst_ref=out_ref.at[4 * px + 2 * py + pc], send_sem=send_sems.at[k],
                                         recv_sem=recv_sems.at[k], device_id=(px, py, pc), device_id_type=MESH).wait_recv()
        for cp in sends:
            cp.wait_send()
        mine.wait()

    return pl.pallas_call(body, name=name, out_shape=jax.ShapeDtypeStruct((N_DEV, R, C), v.dtype), in_specs=[ANY], out_specs=ANY,
                          scratch_shapes=[pltpu.SemaphoreType.DMA((7,)), pltpu.SemaphoreType.DMA((7,)), pltpu.SemaphoreType.DMA(())])(v)


def _all_reduce8(v, name):
    n = v.shape[0]
    rows = -(-n // (LANES * SUM_ROWS)) * SUM_ROWS
    padded = jnp.pad(v, (0, rows * LANES - n)).reshape(rows, LANES)
    return _sum_leading(_all_gather8(padded, name + "_gather"), name + "_sum").reshape(-1)[:n]


def _gather4(w, name):
    R, C = w.shape
    return _gather_weights(w.reshape(2, R // 2, C), name).reshape(N_CHIPS, R, C)


def _reduce_grads(g4, core, name):
    _, R, C = g4.shape
    g = g4.reshape(N_CHIPS, 2, R // 2, C)
    mine = lax.dynamic_index_in_dim(g, core, axis=1, keepdims=False)
    pair = _add2(mine, _pair_exchange(g, name + "_pair_exchange"), name + "_pair_sum")
    half = _sum_leading(_chip_exchange(pair, name + "_chip_exchange"), name + "_chip_sum")
    return _pair_gather(half, name + "_pair_gather").reshape(R, C)


def _segments(H, D):
    W = H * HEAD_DIM
    sizes = (3 * W, H, 2 * W, 3 * W, H, H, W, 3 * D)
    in_tail = (False, True, False, False, True, True, False, False)
    out, first, used = [], 0, [0, 0]
    for size, t in zip(sizes, in_tail):
        out.append((first, size, t, used[t]))
        first += size
        used[t] += size
    return out


def _main_tail_from_shards(g4, H, D):
    C = g4.shape[-1]
    parts = ([], [])
    for first, size, t, _ in _segments(H, D):
        for s in range(g4.shape[0]):
            a, b = max(first, s * C), min(first + size, (s + 1) * C)
            if a < b:
                parts[t].append(g4[s][..., a - s * C:b - s * C])
    parts[1].append(jnp.zeros(g4.shape[1:-1] + (LANES - 3 * H,), g4.dtype))
    return jnp.concatenate(parts[0], axis=-1), jnp.concatenate(parts[1], axis=-1)


def _shards_from_main_tail(main, tail, H, D):
    segs = _segments(H, D)
    C = sum(size for _, size, _, _ in segs) // N_CHIPS
    shards = []
    for s in range(N_CHIPS):
        pieces = []
        for first, size, t, there in segs:
            a, b = max(first, s * C), min(first + size, (s + 1) * C)
            if a < b:
                pieces.append((tail if t else main)[..., there + a - first:there + b - first])
        shards.append(jnp.concatenate(pieces, axis=-1))
    return jnp.stack(shards)


def _pad_cols(a, n):
    return jnp.pad(a, [(0, 0)] * (a.ndim - 1) + [(0, n - a.shape[-1])])


def kernel(x, w_in, b_in, sgu_ln_g, sgu_ln_b, sgu_w, sgu_b, gdn_conv_w, gdn_a_log, gdn_dt_bias, gdn_norm_g, w_proj_a, w_proj_b, w_proj_c, w_out, ln1_g, ln1_b, ffn_w_up, ffn_conv_w, ffn_conv_b, ffn_w_down, ln2_g, ln2_b, loss_target, m_w_in, m_b_in, m_sgu_ln_g, m_sgu_ln_b, m_sgu_w, m_sgu_b, m_gdn_conv_w, m_gdn_a_log, m_gdn_dt_bias, m_gdn_norm_g, m_w_proj_a, m_w_proj_b, m_w_proj_c, m_w_out, m_ln1_g, m_ln1_b, m_ffn_w_up, m_ffn_conv_w, m_ffn_conv_b, m_ffn_w_down, m_ln2_g, m_ln2_b, v_w_in, v_b_in, v_sgu_ln_g, v_sgu_ln_b, v_sgu_w, v_sgu_b, v_gdn_conv_w, v_gdn_a_log, v_gdn_dt_bias, v_gdn_norm_g, v_w_proj_a, v_w_proj_b, v_w_proj_c, v_w_out, v_ln1_g, v_ln1_b, v_ffn_w_up, v_ffn_conv_w, v_ffn_conv_b, v_ffn_w_down, v_ln2_g, v_ln2_b):
    P = dict(w_in=w_in, b_in=b_in, sgu_ln_g=sgu_ln_g, sgu_ln_b=sgu_ln_b, sgu_w=sgu_w, sgu_b=sgu_b, gdn_conv_w=gdn_conv_w,
             gdn_a_log=gdn_a_log, gdn_dt_bias=gdn_dt_bias, gdn_norm_g=gdn_norm_g, w_proj_a=w_proj_a, w_proj_b=w_proj_b,
             w_proj_c=w_proj_c, w_out=w_out, ln1_g=ln1_g, ln1_b=ln1_b, ffn_w_up=ffn_w_up, ffn_conv_w=ffn_conv_w,
             ffn_conv_b=ffn_conv_b, ffn_w_down=ffn_w_down, ln2_g=ln2_g, ln2_b=ln2_b)
    M1 = dict(w_in=m_w_in, b_in=m_b_in, sgu_ln_g=m_sgu_ln_g, sgu_ln_b=m_sgu_ln_b, sgu_w=m_sgu_w, sgu_b=m_sgu_b,
              gdn_conv_w=m_gdn_conv_w, gdn_a_log=m_gdn_a_log, gdn_dt_bias=m_gdn_dt_bias, gdn_norm_g=m_gdn_norm_g,
              w_proj_a=m_w_proj_a, w_proj_b=m_w_proj_b, w_proj_c=m_w_proj_c, w_out=m_w_out, ln1_g=m_ln1_g, ln1_b=m_ln1_b,
              ffn_w_up=m_ffn_w_up, ffn_conv_w=m_ffn_conv_w, ffn_conv_b=m_ffn_conv_b, ffn_w_down=m_ffn_w_down, ln2_g=m_ln2_g,
              ln2_b=m_ln2_b)
    M2 = dict(w_in=v_w_in, b_in=v_b_in, sgu_ln_g=v_sgu_ln_g, sgu_ln_b=v_sgu_ln_b, sgu_w=v_sgu_w, sgu_b=v_sgu_b,
              gdn_conv_w=v_gdn_conv_w, gdn_a_log=v_gdn_a_log, gdn_dt_bias=v_gdn_dt_bias, gdn_norm_g=v_gdn_norm_g,
              w_proj_a=v_w_proj_a, w_proj_b=v_w_proj_b, w_proj_c=v_w_proj_c, w_out=v_w_out, ln1_g=v_ln1_g, ln1_b=v_ln1_b,
              ffn_w_up=v_ffn_w_up, ffn_conv_w=v_ffn_conv_w, ffn_conv_b=v_ffn_conv_b, ffn_w_down=v_ffn_w_down, ln2_g=v_ln2_g,
              ln2_b=v_ln2_b)
    _, S, D = x.shape
    L = w_in.shape[0]
    N_IN = w_in.shape[2] * N_CHIPS
    H = (N_IN - 3 * D) // (9 * HEAD_DIM + 3)
    W = H * HEAD_DIM
    F = ffn_w_down.shape[1] * N_CHIPS
    Fp = -(-F // FF_ALIGN) * FF_ALIGN
    NM = 9 * W + 3 * D
    alpha = (2 * L) ** 0.25
    cx, cy, cc = lax.axis_index("x"), lax.axis_index("y"), lax.axis_index("c")
    chip = 2 * cx + cy

    full_w = []
    for l in range(L):
        g_in = _gather4(w_in[l].astype(bf16), "gather_w_in")
        g_proj = _gather4(jnp.concatenate([w_proj_a[l], w_proj_b[l], w_proj_c[l]], axis=0).astype(bf16), "gather_proj")
        g_rows = _gather4(jnp.concatenate([w_out[l], ffn_w_down[l]], axis=0).astype(bf16), "gather_rows")
        g_up = _gather4(ffn_w_up[l].astype(bf16), "gather_w_up")
        w_main, w_tail = _main_tail_from_shards(g_in, H, D)
        wa, wb, wc = [g_proj[:, k * W:(k + 1) * W].transpose(1, 0, 2).reshape(W, D) for k in range(3)]
        full_w.append(dict(w_main=w_main, w_tail=w_tail, wa=wa, wb=wb, wc=wc, wo=g_rows[:, :D // N_CHIPS].reshape(D, D),
                           wd=jnp.pad(g_rows[:, D // N_CHIPS:].reshape(F, D), ((0, Fp - F), (0, 0))),
                           wg=_pad_cols(jnp.concatenate([g_up[0], g_up[1]], axis=1), Fp),
                           wv=_pad_cols(jnp.concatenate([g_up[2], g_up[3]], axis=1), Fp)))
    gcw_cols, fcw_cols = gdn_conv_w.shape[2], ffn_conv_w.shape[2]
    only_south = (cc == 0).astype(f32)
    placed_g = lax.dynamic_update_slice(jnp.zeros((L, GDN_CONV, 3 * W), f32), gdn_conv_w * only_south, (0, 0, chip * gcw_cols))
    placed_f = lax.dynamic_update_slice(jnp.zeros((L, FFN_CONV, 2 * F), f32), ffn_conv_w * only_south, (0, 0, chip * fcw_cols))
    conv_all = _all_reduce8(jnp.concatenate([placed_g.reshape(-1), placed_f.reshape(-1)]), "conv_weights")
    gcw_full = conv_all[:L * GDN_CONV * 3 * W].reshape(L, GDN_CONV, 1, 3 * W)
    fcw_full = conv_all[L * GDN_CONV * 3 * W:].reshape(L, FFN_CONV, 1, 2 * F)

    saved = []
    h = x.reshape(S, D)
    for l in range(L):
        b_main, b_tail = _main_tail_from_shards(b_in[l][None, None, :], H, D)
        cwg, cwv = _pad_cols(fcw_full[l][..., :F], Fp), _pad_cols(fcw_full[l][..., F:], Fp)
        cbg, cbv = _pad_cols(ffn_conv_b[l][None, :F], Fp), _pad_cols(ffn_conv_b[l][None, F:], Fp)
        lw = dict(full_w[l], cwg=cwg, cwv=cwv, cbg=cbg, cbv=cbv, gcw=gcw_full[l],
                  sgu_ln_g=sgu_ln_g[l][None, :], sgu_ln_b=sgu_ln_b[l][None, :], sgu_w=sgu_w[l], sgu_b=sgu_b[l][:, :, None],
                  a_log=gdn_a_log[l].reshape(H, 1, 1), dt_bias=gdn_dt_bias[l].reshape(H, 1, 1),
                  norm_g=gdn_norm_g[l].reshape(1, 1, HEAD_DIM), ln1_g=ln1_g[l][None, :], ln1_b=ln1_b[l][None, :],
                  ln2_g=ln2_g[l][None, :], ln2_b=ln2_b[l][None, :])
        proj = _matmul(h, lw["w_main"], "nn", "proj_main", bias=b_main)
        tail = _matmul(h, lw["w_tail"], "nn", "proj_tail", bias=b_tail)
        csum = _fox_prep_fwd(tail, "fox_prep")
        c_col = csum[:, :H].T[:, :, None]
        c_row = csum[:, :H].T[:, None, :]
        y_a, lse = _fox_fwd(proj, c_col, c_row, H, "fox_fwd")
        y_b = _sgu_fwd(proj, lw["sgu_ln_g"], lw["sgu_ln_b"], lw["sgu_w"], lw["sgu_b"], 3 * W, 4 * W, "sgu_fwd")
        qkvc = _gdn_pre_fwd(proj, lw["gcw"], 5 * W, 2 * H, "gdn_pre")
        al = tail[:, H:2 * H].T[:, :, None]
        bl = tail[:, 2 * H:3 * H].T[:, :, None]
        y_c, states = _gdn_scan_fwd(qkvc, al, bl, proj, lw["a_log"], lw["dt_bias"], lw["norm_g"], 8 * W, "gdn_scan")
        pa = _matmul(y_a, lw["wa"], "nn", "branch_proj")
        pb = _matmul(y_b, lw["wb"], "nn", "branch_proj")
        pc = _matmul(y_c, lw["wc"], "nn", "branch_proj")
        merged = _merge_fwd(proj, pa, pb, pc, 9 * W, "merge")
        mix = _matmul(merged, lw["wo"], "nn", "out_proj")
        x1 = _ln_fwd(h, mix, lw["ln1_g"], lw["ln1_b"], alpha, "ln")
        upg = _matmul(x1, lw["wg"], "nn", "ffn_up")
        upv = _matmul(x1, lw["wv"], "nn", "ffn_up")
        act = _ffn_act_fwd(upg, upv, cwg, cwv, cbg, cbv, "ffn_act")
        ffn = _matmul(act, lw["wd"], "nn", "ffn_down")
        x2 = _ln_fwd(x1, ffn, lw["ln2_g"], lw["ln2_b"], alpha, "ln")
        saved.append(dict(lw=lw, h=h, proj=proj, tail=tail, c_col=c_col, c_row=c_row, y_a=y_a, lse=lse, y_b=y_b, qkvc=qkvc, al=al,
                          bl=bl, y_c=y_c, states=states, pa=pa, pb=pb, pc=pc, merged=merged, mix=mix, x1=x1, upg=upg, upv=upv,
                          act=act, ffn=ffn))
        h = x2

    loss_part, dh = _loss_head(h, loss_target.reshape(S, D), "loss_head")
    loss = lax.psum(loss_part[0, 0], ("x", "y", "c"))

    big_grads = [None] * L
    small_grads = [None] * L
    for l in reversed(range(L)):
        s = saved[l]
        lw = s["lw"]
        d_x1r, d_ffn, d_ln2g, d_ln2b = _ln_bwd(s["x1"], s["ffn"], lw["ln2_g"], lw["ln2_b"], dh, alpha, "ln_bwd")
        d_act = _matmul(d_ffn, lw["wd"], "nt", "ffn_down_dx")
        d_wd = _matmul(s["act"], d_ffn, "tn", "ffn_down_dw")
        dupg, dupv, dcwg, dcwv, dcbg, dcbv = _ffn_act_bwd(s["upg"], s["upv"], lw["cwg"], lw["cwv"], lw["cbg"], lw["cbv"], d_act,
                                                          "ffn_act_bwd")
        d_x1 = _matmul(dupg, lw["wg"], "nt", "ffn_up_dx", add=d_x1r)
        d_x1 = _matmul(dupv, lw["wv"], "nt", "ffn_up_dx", add=d_x1)
        d_wg = _matmul(s["x1"], dupg, "tn", "ffn_up_dw")
        d_wv = _matmul(s["x1"], dupv, "tn", "ffn_up_dw")
        d_hr, d_mix, d_ln1g, d_ln1b = _ln_bwd(s["h"], s["mix"], lw["ln1_g"], lw["ln1_b"], d_x1, alpha, "ln_bwd")
        d_merged = _matmul(d_mix, lw["wo"], "nt", "out_proj_dx")
        d_wo = _matmul(s["merged"], d_mix, "tn", "out_proj_dw")
        dg0, dg1, dg2, d_pa, d_pb, d_pc = _merge_bwd(s["proj"], s["pa"], s["pb"], s["pc"], d_merged, 9 * W, "merge_bwd")
        d_ya = _matmul(d_pa, lw["wa"], "nt", "branch_proj_dx")
        d_yb = _matmul(d_pb, lw["wb"], "nt", "branch_proj_dx")
        d_yc = _matmul(d_pc, lw["wc"], "nt", "branch_proj_dx")
        d_wa = _matmul(s["y_a"], d_pa, "tn", "branch_proj_dw")
        d_wb = _matmul(s["y_b"], d_pb, "tn", "branch_proj_dw")
        d_wc = _matmul(s["y_c"], d_pc, "tn", "branch_proj_dw")
        dqkvc, dal, dbl, dgate, d_alog, d_dt, d_ng = _gdn_scan_bwd(s["qkvc"], s["al"], s["bl"], s["proj"], lw["a_log"], lw["dt_bias"],
                                                                   lw["norm_g"], s["states"], d_yc, 8 * W, "gdn_scan_bwd")
        d_gqkv, d_gcw = _gdn_pre_bwd(s["proj"], lw["gcw"], dqkvc, 5 * W, 2 * H, "gdn_pre_bwd")
        d_u, d_v, d_slg, d_slb, d_sw, d_sb = _sgu_bwd(s["proj"], lw["sgu_ln_g"], lw["sgu_ln_b"], lw["sgu_w"], lw["sgu_b"], d_yb,
                                                      3 * W, 4 * W, "sgu_bwd")
        d_q, d_cq = _fox_bwd_q(s["proj"], s["c_col"], s["c_row"], s["y_a"], d_ya, s["lse"], H, "fox_bwd_q")
        d_k, d_v_att, d_c = _fox_bwd_kv(s["proj"], s["c_col"], s["c_row"], s["y_a"], d_ya, s["lse"], H, "fox_bwd_kv")
        d_f = _fox_prep_bwd(s["tail"], _pad_cols(d_cq[:, :, 0].T, LANES), _pad_cols(d_c[:, 0, :].T, LANES), "fox_prep_bwd")
        d_main = jnp.concatenate([d_q, d_k, d_v_att, d_u, d_v, d_gqkv, dgate, dg0, dg1, dg2], axis=1)
        d_tail = _pad_cols(jnp.concatenate([d_f[:, :H], dal[:, :, 0].T, dbl[:, :, 0].T], axis=1), LANES)
        d_wmain = _matmul(s["h"], d_main, "tn", "proj_main_dw")
        d_wtail = _matmul(s["h"], d_tail, "tn", "proj_tail_dw")
        d_bmain = _colsum(d_main, "proj_main_db")
        d_btail = _colsum(d_tail, "proj_tail_db")
        dh = _matmul(d_main, lw["w_main"], "nt", "proj_main_dx", add=d_hr)
        dh = _matmul(d_tail, lw["w_tail"], "nt", "proj_tail_dx", add=dh)
        by_cols = lambda g: g.reshape(g.shape[0], N_CHIPS, g.shape[1] // N_CHIPS).transpose(1, 0, 2)
        big_grads[l] = dict(
            w_in=_shards_from_main_tail(d_wmain, d_wtail, H, D),
            proj=jnp.concatenate([by_cols(d_wa), by_cols(d_wb), by_cols(d_wc)], axis=1),
            rows=jnp.concatenate([d_wo.reshape(N_CHIPS, D // N_CHIPS, D), d_wd[:F].reshape(N_CHIPS, F // N_CHIPS, D)], axis=1),
            w_up=jnp.stack([d_wg[:, :F // 2], d_wg[:, F // 2:F], d_wv[:, :F // 2], d_wv[:, F // 2:F]]))
        small_grads[l] = dict(b_in=_shards_from_main_tail(d_bmain, d_btail, H, D).reshape(-1), sgu_ln_g=d_slg[0], sgu_ln_b=d_slb[0], sgu_w=d_sw,
                              sgu_b=d_sb[:, :, 0], gdn_conv_w=d_gcw[:, 0, :], gdn_a_log=d_alog[:, 0, 0], gdn_dt_bias=d_dt[:, 0, 0],
                              gdn_norm_g=d_ng[0, 0], ln1_g=d_ln1g[0], ln1_b=d_ln1b[0],
                              ffn_conv_w=jnp.concatenate([dcwg[:, 0, :F], dcwv[:, 0, :F]], axis=1),
                              ffn_conv_b=jnp.concatenate([dcbg[0, :F], dcbv[0, :F]]), ln2_g=d_ln2g[0], ln2_b=d_ln2b[0])
    grad_x = dh.reshape(1, S, D)

    grads = {n: [None] * L for n in WEIGHTS}
    for l in range(L):
        grads["w_in"][l] = _reduce_grads(big_grads[l]["w_in"], cc, "grad_w_in")
        proj_sum = _reduce_grads(big_grads[l]["proj"], cc, "grad_proj")
        for k, n in enumerate(("w_proj_a", "w_proj_b", "w_proj_c")):
            grads[n][l] = proj_sum[k * W:(k + 1) * W]
        rows_sum = _reduce_grads(big_grads[l]["rows"], cc, "grad_rows")
        grads["w_out"][l] = rows_sum[:D // N_CHIPS]
        grads["ffn_w_down"][l] = rows_sum[D // N_CHIPS:]
        grads["ffn_w_up"][l] = _reduce_grads(big_grads[l]["w_up"], cc, "grad_w_up")
    small_shapes = {n: small_grads[0][n].shape for n in SMALL}
    small_flat = jnp.concatenate([small_grads[l][n].reshape(-1) for l in range(L) for n in SMALL])
    small_sum = _all_reduce8(small_flat, "small_grads")
    off = 0
    for l in range(L):
        for n in SMALL:
            size = math.prod(small_shapes[n])
            g = small_sum[off:off + size].reshape(small_shapes[n])
            off += size
            if n == "gdn_conv_w":
                g = lax.dynamic_slice_in_dim(g, chip * gcw_cols, gcw_cols, axis=1)
            elif n == "ffn_conv_w":
                g = lax.dynamic_slice_in_dim(g, chip * fcw_cols, fcw_cols, axis=1)
            grads[n][l] = g
    grads = {n: jnp.stack(grads[n]) for n in WEIGHTS}

    deltas, new_m, new_v = {}, {}, {}
    for n in WEIGHTS:
        deltas[n], new_m[n], new_v[n] = _adamw(P[n], grads[n], M1[n], M2[n], "adamw_" + n)
    return (loss, grad_x, *[grads[n] for n in WEIGHTS], *[deltas[n] for n in WEIGHTS], *[new_m[n] for n in WEIGHTS],
            *[new_v[n] for n in WEIGHTS])
```

```python
import functools
import math

import jax
import jax.numpy as jnp
from jax import lax
from jax.experimental import pallas as pl
from jax.experimental.pallas import tpu as pltpu

f32 = jnp.float32
bf16 = jnp.bfloat16
HIGHEST = lax.Precision.HIGHEST
MESH = pl.DeviceIdType.MESH

HEAD_DIM = 128
CHUNK = 64
SGU_SPAN = 128
GDN_CONV = 4
FFN_CONV = 3
N_CHIPS = 4
N_DEV = 8
LN_EPS = 1e-5
RMS_EPS = 1e-6
ADAM_LR = 0.001
ADAM_B1 = 0.9
ADAM_B2 = 0.999
ADAM_EPS = 1e-08
ADAM_WD = 0.01
ADAM_STEP = 10
NEG_BIG = -1e30
LANES = 128
FF_ALIGN = 512
SUM_ROWS = 256
ATTN_TILE = 1024
VMEM_MARGIN = 12 << 20
VMEM_MOST = 60 << 20

SMALL = ("b_in", "sgu_ln_g", "sgu_ln_b", "sgu_w", "sgu_b", "gdn_conv_w", "gdn_a_log", "gdn_dt_bias", "gdn_norm_g",
         "ln1_g", "ln1_b", "ffn_conv_w", "ffn_conv_b", "ln2_g", "ln2_b")
WEIGHTS = ("w_in", "b_in", "sgu_ln_g", "sgu_ln_b", "sgu_w", "sgu_b", "gdn_conv_w", "gdn_a_log", "gdn_dt_bias", "gdn_norm_g",
           "w_proj_a", "w_proj_b", "w_proj_c", "w_out", "ln1_g", "ln1_b", "ffn_w_up", "ffn_conv_w", "ffn_conv_b", "ffn_w_down",
           "ln2_g", "ln2_b")

ANY = pl.BlockSpec(memory_space=pl.ANY)


def _tile(dim, pref):
    t = pref
    while t > 128 and dim % t:
        t //= 2
    return min(t, dim) if dim % min(t, dim) == 0 else dim


def _params(*sem, vmem=None):
    if vmem is None:
        return pltpu.CompilerParams(dimension_semantics=sem)
    return pltpu.CompilerParams(dimension_semantics=sem, vmem_limit_bytes=min(vmem + VMEM_MARGIN, VMEM_MOST))


def _attn_vmem(T):
    return 8 * T * T * 4


_DIMS = {"nn": (((1,), (0,)), ((), ())), "nt": (((1,), (1,)), ((), ())), "tn": (((0,), (0,)), ((), ()))}


def _pick(dim, most):
    for unit in (256, LANES):
        for t in range(min(most, dim) // unit * unit, 0, -unit):
            if dim % t == 0:
                return t
    return dim


def _matmul(a, b, mode, name, bias=None, add=None):
    a, b = a.astype(bf16), b.astype(bf16)
    if mode == "nn":
        (M, K), (_, N) = a.shape, b.shape
    elif mode == "nt":
        (M, K), (N, _) = a.shape, b.shape
    else:
        (K, M), (_, N) = a.shape, b.shape
    has_bias, has_add = bias is not None, add is not None
    tm, tn, tk = _pick(M, 512 if has_add else 1024), _pick(N, 2816), _pick(K, 512)
    nk = K // tk
    vmem = 2 * (tm * tk * a.dtype.itemsize + tk * tn * b.dtype.itemsize + tm * tn * 4 * (2 if has_add else 1)) + tm * tn * 4

    def body(*refs):
        a_ref, b_ref = refs[0], refs[1]
        pos = 2
        bias_ref = add_ref = None
        if has_bias:
            bias_ref, pos = refs[pos], pos + 1
        if has_add:
            add_ref, pos = refs[pos], pos + 1
        o_ref, acc_ref = refs[pos], refs[pos + 1]
        k = pl.program_id(2)

        @pl.when(k == 0)
        def _():
            acc_ref[...] = jnp.zeros_like(acc_ref)

        acc_ref[...] += lax.dot_general(a_ref[...], b_ref[...], _DIMS[mode], preferred_element_type=f32)

        @pl.when(k == nk - 1)
        def _():
            r = acc_ref[...]
            if has_bias:
                r = r + bias_ref[...]
            if has_add:
                r = r + add_ref[...]
            o_ref[...] = r

    if mode == "nn":
        specs = [pl.BlockSpec((tm, tk), lambda i, j, k: (i, k)), pl.BlockSpec((tk, tn), lambda i, j, k: (k, j))]
    elif mode == "nt":
        specs = [pl.BlockSpec((tm, tk), lambda i, j, k: (i, k)), pl.BlockSpec((tn, tk), lambda i, j, k: (j, k))]
    else:
        specs = [pl.BlockSpec((tk, tm), lambda i, j, k: (k, i)), pl.BlockSpec((tk, tn), lambda i, j, k: (k, j))]
    ops = [a, b]
    if has_bias:
        specs.append(pl.BlockSpec((1, tn), lambda i, j, k: (0, j)))
        ops.append(bias)
    if has_add:
        specs.append(pl.BlockSpec((tm, tn), lambda i, j, k: (i, j)))
        ops.append(add)
    return pl.pallas_call(
        body, name=name, grid=(M // tm, N // tn, nk), in_specs=specs,
        out_specs=pl.BlockSpec((tm, tn), lambda i, j, k: (i, j)),
        out_shape=jax.ShapeDtypeStruct((M, N), f32), scratch_shapes=[pltpu.VMEM((tm, tn), f32)],
        compiler_params=pltpu.CompilerParams(dimension_semantics=("parallel", "parallel", "arbitrary"),
                                             vmem_limit_bytes=min(vmem + VMEM_MARGIN, VMEM_MOST)))(*ops)


def _colsum(a, name):
    S, N = a.shape
    tn = _tile(N, 512)

    def body(a_ref, o_ref):
        o_ref[...] = jnp.sum(a_ref[...], axis=0, keepdims=True)

    return pl.pallas_call(body, name=name, grid=(N // tn,), in_specs=[pl.BlockSpec((S, tn), lambda j: (0, j))],
                          out_specs=pl.BlockSpec((1, tn), lambda j: (0, j)), out_shape=jax.ShapeDtypeStruct((1, N), f32),
                          compiler_params=_params("parallel"))(a)


def _ln_fn(alpha, x, y, g, b):
    z = alpha * x + y
    mu = jnp.mean(z, axis=-1, keepdims=True)
    zc = z - mu
    var = jnp.mean(zc * zc, axis=-1, keepdims=True)
    return zc * lax.rsqrt(var + LN_EPS) * g + b


def _ln_fwd(x, y, g, b, alpha, name):
    S, D = x.shape
    tr = _tile(S, 256)

    def body(x_ref, y_ref, g_ref, b_ref, o_ref):
        o_ref[...] = _ln_fn(alpha, x_ref[...], y_ref[...], g_ref[...], b_ref[...])

    row = pl.BlockSpec((tr, D), lambda i: (i, 0))
    par = pl.BlockSpec((1, D), lambda i: (0, 0))
    return pl.pallas_call(body, name=name, grid=(S // tr,), in_specs=[row, row, par, par], out_specs=row,
                          out_shape=jax.ShapeDtypeStruct((S, D), f32), compiler_params=_params("parallel"))(x, y, g, b)


def _ln_bwd(x, y, g, b, dout, alpha, name):
    S, D = x.shape
    tr = _tile(S, 256)

    def body(x_ref, y_ref, g_ref, b_ref, d_ref, dx_ref, dy_ref, dg_ref, db_ref):
        _, vjp = jax.vjp(functools.partial(_ln_fn, alpha), x_ref[...], y_ref[...], g_ref[...], b_ref[...])
        dx, dy, dg, db = vjp(d_ref[...])
        dx_ref[...] = dx
        dy_ref[...] = dy

        @pl.when(pl.program_id(0) == 0)
        def _():
            dg_ref[...] = jnp.zeros_like(dg_ref)
            db_ref[...] = jnp.zeros_like(db_ref)

        dg_ref[...] += dg
        db_ref[...] += db

    row = pl.BlockSpec((tr, D), lambda i: (i, 0))
    par = pl.BlockSpec((1, D), lambda i: (0, 0))
    sd = jax.ShapeDtypeStruct
    return pl.pallas_call(body, name=name, grid=(S // tr,), in_specs=[row, row, par, par, row],
                          out_specs=[row, row, par, par],
                          out_shape=[sd((S, D), f32), sd((S, D), f32), sd((1, D), f32), sd((1, D), f32)],
                          compiler_params=_params("arbitrary"))(x, y, g, b, dout)


def _loss_head(y, t, name):
    S, D = y.shape
    tr = _tile(S, 256)

    def body(y_ref, t_ref, l_ref, d_ref):
        e = y_ref[...] - t_ref[...]
        d_ref[...] = e / D

        @pl.when(pl.program_id(0) == 0)
        def _():
            l_ref[...] = jnp.zeros_like(l_ref)

        l_ref[...] += 0.5 * jnp.sum(jnp.mean(e * e, axis=-1, keepdims=True))

    row = pl.BlockSpec((tr, D), lambda i: (i, 0))
    return pl.pallas_call(body, name=name, grid=(S // tr,), in_specs=[row, row],
                          out_specs=[pl.BlockSpec((8, LANES), lambda i: (0, 0)), row],
                          out_shape=[jax.ShapeDtypeStruct((8, LANES), f32), jax.ShapeDtypeStruct((S, D), f32)],
                          compiler_params=_params("arbitrary"))(y, t)


def _merge_fn(g0, g1, g2, pa, pb, pc):
    return jax.nn.sigmoid(g0) * pa + jax.nn.sigmoid(g1) * pb + jax.nn.sigmoid(g2) * pc


def _merge_specs(S, D, gate_off):
    tr = _tile(S, 512)
    tc = _tile(math.gcd(gate_off, D), 512)
    gates = [pl.BlockSpec((tr, tc), functools.partial(lambda k, i, j: (i, (gate_off + k * D) // tc + j), k)) for k in range(3)]
    tile = pl.BlockSpec((tr, tc), lambda i, j: (i, j))
    return tr, tc, gates, tile


def _merge_fwd(proj, pa, pb, pc, gate_off, name):
    S, D = pa.shape
    tr, tc, gates, tile = _merge_specs(S, D, gate_off)

    def body(g0, g1, g2, a, b, c, o_ref):
        o_ref[...] = _merge_fn(g0[...], g1[...], g2[...], a[...], b[...], c[...])

    return pl.pallas_call(body, name=name, grid=(S // tr, D // tc), in_specs=gates + [tile] * 3, out_specs=tile,
                          out_shape=jax.ShapeDtypeStruct((S, D), f32),
                          compiler_params=_params("parallel", "parallel"))(proj, proj, proj, pa, pb, pc)


def _merge_bwd(proj, pa, pb, pc, dm, gate_off, name):
    S, D = pa.shape
    tr, tc, gates, tile = _merge_specs(S, D, gate_off)

    def body(g0, g1, g2, a, b, c, d, dg0, dg1, dg2, da, db, dc):
        _, vjp = jax.vjp(_merge_fn, g0[...], g1[...], g2[...], a[...], b[...], c[...])
        for ref, val in zip((dg0, dg1, dg2, da, db, dc), vjp(d[...])):
            ref[...] = val

    sd = jax.ShapeDtypeStruct
    return pl.pallas_call(body, name=name, grid=(S // tr, D // tc), in_specs=gates + [tile] * 4,
                          out_specs=[tile] * 6, out_shape=[sd((S, D), f32)] * 6,
                          compiler_params=_params("parallel", "parallel"))(proj, proj, proj, pa, pb, pc, dm)


def _sgu_fn(nb, u, v, ln_g, ln_b, w_s, b_s):
    mu = jnp.mean(v, axis=-1, keepdims=True)
    vc = v - mu
    var = jnp.mean(vc * vc, axis=-1, keepdims=True)
    vn = vc * lax.rsqrt(var + LN_EPS) * ln_g + ln_b
    r = lax.broadcasted_iota(jnp.int32, (SGU_SPAN, SGU_SPAN), 0) // CHUNK
    c = lax.broadcasted_iota(jnp.int32, (SGU_SPAN, SGU_SPAN), 1) // CHUNK
    wm = jnp.where(r >= c, w_s, 0.0)
    vn3 = vn.reshape(nb, SGU_SPAN, HEAD_DIM)
    mixed = lax.dot_general(jnp.broadcast_to(wm, (nb, SGU_SPAN, SGU_SPAN)), vn3, (((2,), (1,)), ((0,), (0,))),
                            preferred_element_type=f32)
    mixed = mixed + b_s
    return u * mixed.reshape(nb * SGU_SPAN, HEAD_DIM)


def _sgu_specs(S, G, u_off, v_off):
    nb = max(1, min(8, S // SGU_SPAN))
    rows = nb * SGU_SPAN
    ub = pl.BlockSpec((rows, HEAD_DIM), lambda g, n: (n, u_off // HEAD_DIM + g))
    vb = pl.BlockSpec((rows, HEAD_DIM), lambda g, n: (n, v_off // HEAD_DIM + g))
    lnb = pl.BlockSpec((1, HEAD_DIM), lambda g, n: (0, g))
    wb = pl.BlockSpec((None, SGU_SPAN, SGU_SPAN), lambda g, n: (g, 0, 0))
    bb = pl.BlockSpec((None, SGU_SPAN, 1), lambda g, n: (g, 0, 0))
    return nb, rows, ub, vb, lnb, wb, bb


def _sgu_fwd(proj, ln_g, ln_b, w_s, b_s, u_off, v_off, name):
    S = proj.shape[0]
    G = w_s.shape[0]
    nb, rows, ub, vb, lnb, wb, bb = _sgu_specs(S, G, u_off, v_off)

    def body(u, v, lg, lb, w, b, o_ref):
        o_ref[...] = _sgu_fn(nb, u[...], v[...], lg[...], lb[...], w[...], b[...])

    return pl.pallas_call(body, name=name, grid=(G, S // rows), in_specs=[ub, vb, lnb, lnb, wb, bb],
                          out_specs=pl.BlockSpec((rows, HEAD_DIM), lambda g, n: (n, g)),
                          out_shape=jax.ShapeDtypeStruct((S, G * HEAD_DIM), f32),
                          compiler_params=_params("parallel", "parallel"))(proj, proj, ln_g, ln_b, w_s, b_s)


def _sgu_bwd(proj, ln_g, ln_b, w_s, b_s, dy, u_off, v_off, name):
    S = proj.shape[0]
    G = w_s.shape[0]
    nb, rows, ub, vb, lnb, wb, bb = _sgu_specs(S, G, u_off, v_off)

    def body(u, v, lg, lb, w, b, d, du, dv, dlg, dlb, dw, db):
        _, vjp = jax.vjp(functools.partial(_sgu_fn, nb), u[...], v[...], lg[...], lb[...], w[...], b[...])
        gu, gv, glg, glb, gw, gb = vjp(d[...])
        du[...] = gu
        dv[...] = gv

        @pl.when(pl.program_id(1) == 0)
        def _():
            for ref in (dlg, dlb, dw, db):
                ref[...] = jnp.zeros_like(ref)

        dlg[...] += glg
        dlb[...] += glb
        dw[...] += gw
        db[...] += gb

    tile = pl.BlockSpec((rows, HEAD_DIM), lambda g, n: (n, g))
    sd = jax.ShapeDtypeStruct
    W = G * HEAD_DIM
    return pl.pallas_call(body, name=name, grid=(G, S // rows), in_specs=[ub, vb, lnb, lnb, wb, bb, tile],
                          out_specs=[tile, tile, lnb, lnb, wb, bb],
                          out_shape=[sd((S, W), f32), sd((S, W), f32), sd((1, W), f32), sd((1, W), f32),
                                     sd((G, SGU_SPAN, SGU_SPAN), f32), sd((G, SGU_SPAN, 1), f32)],
                          compiler_params=_params("parallel", "arbitrary"))(proj, proj, ln_g, ln_b, w_s, b_s, dy)


def _shift_down(x, k):
    if k == 0:
        return x
    rows = lax.broadcasted_iota(jnp.int32, x.shape, 0)
    return jnp.where(rows >= k, pltpu.roll(x, k, 0), 0.0)


def _shift_up(x, k):
    if k == 0:
        return x
    n = x.shape[0]
    rows = lax.broadcasted_iota(jnp.int32, x.shape, 0)
    return jnp.where(rows < n - k, pltpu.roll(x, n - k, 0), 0.0)


def _conv(x, w_ref, width):
    out = w_ref[width - 1] * x
    for j in range(width - 1):
        out = out + w_ref[j] * _shift_down(x, width - 1 - j)
    return out


def _conv_bwd(x, dz, w_ref, dw_ref, width):
    dx = w_ref[width - 1] * dz
    dw_ref[width - 1] = jnp.sum(dz * x, axis=0, keepdims=True)
    for j in range(width - 1):
        k = width - 1 - j
        dx = dx + w_ref[j] * _shift_up(dz, k)
        dw_ref[j] = jnp.sum(dz * _shift_down(x, k), axis=0, keepdims=True)
    return dx


def _silu(z):
    return z * jax.nn.sigmoid(z)


def _dsilu(z):
    s = jax.nn.sigmoid(z)
    return s * (1.0 + z * (1.0 - s))


def _ffn_act_fwd(upg, upv, cwg, cwv, cbg, cbv, name):
    S, Fp = upg.shape

    def body(g_ref, v_ref, wg, wv, bg, bv, o_ref):
        hg = _conv(g_ref[...], wg, FFN_CONV) + bg[...]
        hv = _conv(v_ref[...], wv, FFN_CONV) + bv[...]
        o_ref[...] = _silu(hg) * hv

    col = pl.BlockSpec((S, LANES), lambda j: (0, j))
    wsp = pl.BlockSpec((FFN_CONV, 1, LANES), lambda j: (0, 0, j))
    bsp = pl.BlockSpec((1, LANES), lambda j: (0, j))
    return pl.pallas_call(body, name=name, grid=(Fp // LANES,), in_specs=[col, col, wsp, wsp, bsp, bsp], out_specs=col,
                          out_shape=jax.ShapeDtypeStruct((S, Fp), f32),
                          compiler_params=_params("parallel"))(upg, upv, cwg, cwv, cbg, cbv)


def _ffn_act_bwd(upg, upv, cwg, cwv, cbg, cbv, dact, name):
    S, Fp = upg.shape

    def body(g_ref, v_ref, wg, wv, bg, bv, d_ref, dg_ref, dv_ref, dwg, dwv, dbg, dbv):
        xg, xv, d = g_ref[...], v_ref[...], d_ref[...]
        hg = _conv(xg, wg, FFN_CONV) + bg[...]
        hv = _conv(xv, wv, FFN_CONV) + bv[...]
        dhg = d * hv * _dsilu(hg)
        dhv = d * _silu(hg)
        dbg[...] = jnp.sum(dhg, axis=0, keepdims=True)
        dbv[...] = jnp.sum(dhv, axis=0, keepdims=True)
        dg_ref[...] = _conv_bwd(xg, dhg, wg, dwg, FFN_CONV)
        dv_ref[...] = _conv_bwd(xv, dhv, wv, dwv, FFN_CONV)

    col = pl.BlockSpec((S, LANES), lambda j: (0, j))
    wsp = pl.BlockSpec((FFN_CONV, 1, LANES), lambda j: (0, 0, j))
    bsp = pl.BlockSpec((1, LANES), lambda j: (0, j))
    sd = jax.ShapeDtypeStruct
    return pl.pallas_call(body, name=name, grid=(Fp // LANES,), in_specs=[col, col, wsp, wsp, bsp, bsp, col],
                          out_specs=[col, col, wsp, wsp, bsp, bsp],
                          out_shape=[sd((S, Fp), f32), sd((S, Fp), f32), sd((FFN_CONV, 1, Fp), f32), sd((FFN_CONV, 1, Fp), f32),
                                     sd((1, Fp), f32), sd((1, Fp), f32)],
                          compiler_params=_params("parallel"))(upg, upv, cwg, cwv, cbg, cbv, dact)


def _gdn_pre_fwd(proj, cw, x_off, n_norm, name):
    S = proj.shape[0]
    C = cw.shape[2]

    def body(x_ref, w_ref, o_ref):
        s = _silu(_conv(x_ref[...], w_ref, GDN_CONV))
        r = lax.rsqrt(jnp.sum(s * s, axis=-1, keepdims=True) + RMS_EPS)
        o_ref[...] = jnp.where(pl.program_id(0) < n_norm, s * r, s)

    xs = pl.BlockSpec((S, LANES), lambda j: (0, x_off // LANES + j))
    col = pl.BlockSpec((S, LANES), lambda j: (0, j))
    wsp = pl.BlockSpec((GDN_CONV, 1, LANES), lambda j: (0, 0, j))
    return pl.pallas_call(body, name=name, grid=(C // LANES,), in_specs=[xs, wsp], out_specs=col,
                          out_shape=jax.ShapeDtypeStruct((S, C), f32), compiler_params=_params("parallel"))(proj, cw)


def _gdn_pre_bwd(proj, cw, dout, x_off, n_norm, name):
    S = proj.shape[0]
    C = cw.shape[2]

    def body(x_ref, w_ref, d_ref, dx_ref, dw_ref):
        x, d = x_ref[...], d_ref[...]
        z = _conv(x, w_ref, GDN_CONV)
        s = _silu(z)
        r = lax.rsqrt(jnp.sum(s * s, axis=-1, keepdims=True) + RMS_EPS)
        ds_norm = d * r - s * (r * r * r) * jnp.sum(d * s, axis=-1, keepdims=True)
        ds = jnp.where(pl.program_id(0) < n_norm, ds_norm, d)
        dz = ds * _dsilu(z)
        dx_ref[...] = _conv_bwd(x, dz, w_ref, dw_ref, GDN_CONV)

    xs = pl.BlockSpec((S, LANES), lambda j: (0, x_off // LANES + j))
    col = pl.BlockSpec((S, LANES), lambda j: (0, j))
    wsp = pl.BlockSpec((GDN_CONV, 1, LANES), lambda j: (0, 0, j))
    return pl.pallas_call(body, name=name, grid=(C // LANES,), in_specs=[xs, wsp, col], out_specs=[col, wsp],
                          out_shape=[jax.ShapeDtypeStruct((S, C), f32), jax.ShapeDtypeStruct((GDN_CONV, 1, C), f32)],
                          compiler_params=_params("parallel"))(proj, cw, dout)


def _bmm(a, b, prec=None):
    return lax.dot_general(a, b, (((2,), (1,)), ((0,), (0,))), precision=prec, preferred_element_type=f32)


def _bmm_nt(a, b, prec=None):
    return lax.dot_general(a, b, (((2,), (2,)), ((0,), (0,))), precision=prec, preferred_element_type=f32)


def _bmm_tn(a, b, prec=None):
    return lax.dot_general(a, b, (((1,), (1,)), ((0,), (0,))), precision=prec, preferred_element_type=f32)


def _softplus(x):
    return jnp.maximum(x, 0.0) + jnp.log1p(jnp.exp(-jnp.abs(x)))


def _gdn_chunk(q, k, v, al, bl, gate, a_log, dt_bias, norm_g, state):
    H, C, Dh = q.shape
    r = lax.broadcasted_iota(jnp.int32, (H, C, C), 1)
    c = lax.broadcasted_iota(jnp.int32, (H, C, C), 2)
    tril = r >= c
    strict = r > c
    lower = tril.astype(f32)
    upper = (r <= c).astype(f32)
    ones = jnp.ones((H, C, C), f32)
    g = -jnp.exp(a_log) * _softplus(al + dt_bias)
    beta = jax.nn.sigmoid(bl)
    g_lanes = jnp.broadcast_to(g, (H, C, Dh))
    g_sq = jnp.broadcast_to(g, (H, C, C))
    gc = _bmm(lower, g_lanes, HIGHEST)
    gc_i = _bmm(lower, g_sq, HIGHEST)
    gc_j = _bmm(ones, g_sq * upper, HIGHEST)
    decay = jnp.where(tril, jnp.exp(jnp.where(tril, gc_i - gc_j, 0.0)), 0.0)
    qs = q * (Dh ** -0.5)
    kb = k * beta
    a_kk = jnp.where(strict, _bmm_nt(kb, k) * decay, 0.0)
    rhs_u = v * beta
    rhs_w = kb * jnp.exp(gc)
    p = -a_kk
    inv = (r == c).astype(f32) + p
    for _ in range(int(math.log2(C)) - 1):
        p = _bmm(p, p, HIGHEST)
        inv = inv + _bmm(inv, p, HIGHEST)
    u = _bmm(inv, rhs_u, HIGHEST)
    w = _bmm(inv, rhs_w, HIGHEST)
    qk = jnp.where(tril, _bmm_nt(qs, k) * decay, 0.0)
    g_last = jnp.sum(g, axis=1, keepdims=True)
    k_dec = k * jnp.exp(g_last - gc)
    q_dec = qs * jnp.exp(gc)
    v_new = u - _bmm(w, state)
    o = _bmm(q_dec, state) + _bmm(qk, v_new)
    new_state = state * jnp.exp(g_last) + _bmm_tn(k_dec, v_new)
    y = o * lax.rsqrt(jnp.mean(o * o, axis=-1, keepdims=True) + RMS_EPS) * norm_g * _silu(gate)
    return y, new_state


def _heads(ref, off, H):
    return jnp.stack([ref[:, off + h * HEAD_DIM: off + (h + 1) * HEAD_DIM] for h in range(H)])


def _gdn_scan_fwd(qkvc, al, bl, proj, a_log, dt_bias, norm_g, gate_off, name):
    S = qkvc.shape[0]
    H = al.shape[0]
    W = H * HEAD_DIM
    n = S // CHUNK

    def body(x_ref, al_ref, bl_ref, gate_ref, alog_ref, dt_ref, ng_ref, y_ref, st_ref, state):
        @pl.when(pl.program_id(0) == 0)
        def _():
            state[...] = jnp.zeros_like(state)

        st_ref[...] = state[...]
        y, new = _gdn_chunk(_heads(x_ref, 0, H), _heads(x_ref, W, H), _heads(x_ref, 2 * W, H), al_ref[...], bl_ref[...],
                            _heads(gate_ref, 0, H), alog_ref[...], dt_ref[...], ng_ref[...], state[...])
        state[...] = new
        for h in range(H):
            y_ref[:, h * HEAD_DIM:(h + 1) * HEAD_DIM] = y[h]

    sd = jax.ShapeDtypeStruct
    col = pl.BlockSpec((H, CHUNK, 1), lambda i: (0, i, 0))
    par = pl.BlockSpec((H, 1, 1), lambda i: (0, 0, 0))
    return pl.pallas_call(
        body, name=name, grid=(n,),
        in_specs=[pl.BlockSpec((CHUNK, 3 * W), lambda i: (i, 0)), col, col,
                  pl.BlockSpec((CHUNK, W), lambda i: (i, gate_off // W)), par, par,
                  pl.BlockSpec((1, 1, HEAD_DIM), lambda i: (0, 0, 0))],
        out_specs=[pl.BlockSpec((CHUNK, W), lambda i: (i, 0)),
                   pl.BlockSpec((None, H, HEAD_DIM, HEAD_DIM), lambda i: (i, 0, 0, 0))],
        out_shape=[sd((S, W), f32), sd((n, H, HEAD_DIM, HEAD_DIM), f32)],
        scratch_shapes=[pltpu.VMEM((H, HEAD_DIM, HEAD_DIM), f32)],
        compiler_params=_params("arbitrary"))(qkvc, al, bl, proj, a_log, dt_bias, norm_g)


def _gdn_scan_bwd(qkvc, al, bl, proj, a_log, dt_bias, norm_g, states, dy, gate_off, name):
    S = qkvc.shape[0]
    H = al.shape[0]
    W = H * HEAD_DIM
    n = S // CHUNK

    def body(x_ref, al_ref, bl_ref, gate_ref, alog_ref, dt_ref, ng_ref, st_ref, dy_ref,
             dx_ref, dal_ref, dbl_ref, dgate_ref, dalog_ref, ddt_ref, dng_ref, dstate):
        @pl.when(pl.program_id(0) == 0)
        def _():
            dstate[...] = jnp.zeros_like(dstate)
            for ref in (dalog_ref, ddt_ref, dng_ref):
                ref[...] = jnp.zeros_like(ref)

        _, vjp = jax.vjp(_gdn_chunk, _heads(x_ref, 0, H), _heads(x_ref, W, H), _heads(x_ref, 2 * W, H), al_ref[...],
                         bl_ref[...], _heads(gate_ref, 0, H), alog_ref[...], dt_ref[...], ng_ref[...], st_ref[...])
        dq, dk, dv, dal, dbl, dgate, dalog, ddt, dng, dst = vjp((_heads(dy_ref, 0, H), dstate[...]))
        dstate[...] = dst
        for h in range(H):
            lo, hi = h * HEAD_DIM, (h + 1) * HEAD_DIM
            dx_ref[:, lo:hi] = dq[h]
            dx_ref[:, W + lo:W + hi] = dk[h]
            dx_ref[:, 2 * W + lo:2 * W + hi] = dv[h]
            dgate_ref[:, lo:hi] = dgate[h]
        dal_ref[...] = dal
        dbl_ref[...] = dbl
        dalog_ref[...] += dalog
        ddt_ref[...] += ddt
        dng_ref[...] += dng

    sd = jax.ShapeDtypeStruct
    rev = lambda i: n - 1 - i
    col = pl.BlockSpec((H, CHUNK, 1), lambda i: (0, rev(i), 0))
    par = pl.BlockSpec((H, 1, 1), lambda i: (0, 0, 0))
    ng = pl.BlockSpec((1, 1, HEAD_DIM), lambda i: (0, 0, 0))
    xs = pl.BlockSpec((CHUNK, 3 * W), lambda i: (rev(i), 0))
    ws = pl.BlockSpec((CHUNK, W), lambda i: (rev(i), 0))
    return pl.pallas_call(
        body, name=name, grid=(n,),
        in_specs=[xs, col, col, pl.BlockSpec((CHUNK, W), lambda i: (rev(i), gate_off // W)), par, par, ng,
                  pl.BlockSpec((None, H, HEAD_DIM, HEAD_DIM), lambda i: (rev(i), 0, 0, 0)), ws],
        out_specs=[xs, col, col, ws, par, par, ng],
        out_shape=[sd((S, 3 * W), f32), sd((H, S, 1), f32), sd((H, S, 1), f32), sd((S, W), f32), sd((H, 1, 1), f32),
                   sd((H, 1, 1), f32), sd((1, 1, HEAD_DIM), f32)],
        scratch_shapes=[pltpu.VMEM((H, HEAD_DIM, HEAD_DIM), f32)],
        compiler_params=_params("arbitrary"))(qkvc, al, bl, proj, a_log, dt_bias, norm_g, states, dy)


def _tri(n, upper):
    r = lax.broadcasted_iota(jnp.int32, (n, n), 0)
    c = lax.broadcasted_iota(jnp.int32, (n, n), 1)
    return (r <= c if upper else r >= c).astype(f32)


def _fox_prep_fwd(tail, name):
    S = tail.shape[0]
    tb = _tile(S, 512)

    def body(x_ref, o_ref, carry):
        @pl.when(pl.program_id(0) == 0)
        def _():
            carry[...] = jnp.zeros_like(carry)

        x = x_ref[...]
        lf = jnp.minimum(x, 0.0) - jnp.log1p(jnp.exp(-jnp.abs(x)))
        o_ref[...] = jnp.dot(_tri(tb, False), lf, precision=HIGHEST, preferred_element_type=f32) + carry[...]
        carry[...] += jnp.sum(lf, axis=0, keepdims=True)

    blk = pl.BlockSpec((tb, LANES), lambda i: (i, 0))
    return pl.pallas_call(body, name=name, grid=(S // tb,), in_specs=[blk], out_specs=blk,
                          out_shape=jax.ShapeDtypeStruct((S, LANES), f32), scratch_shapes=[pltpu.VMEM((1, LANES), f32)],
                          compiler_params=_params("arbitrary"))(tail)


def _fox_prep_bwd(tail, dc_q, dc_k, name):
    S = tail.shape[0]
    tb = _tile(S, 512)
    nb = S // tb

    def body(x_ref, dq_ref, d_ref, o_ref, carry):
        @pl.when(pl.program_id(0) == 0)
        def _():
            carry[...] = jnp.zeros_like(carry)

        d = d_ref[...] + dq_ref[...]
        dlf = jnp.dot(_tri(tb, True), d, precision=HIGHEST, preferred_element_type=f32) + carry[...]
        carry[...] += jnp.sum(d, axis=0, keepdims=True)
        o_ref[...] = dlf * jax.nn.sigmoid(-x_ref[...])

    blk = pl.BlockSpec((tb, LANES), lambda i: (nb - 1 - i, 0))
    return pl.pallas_call(body, name=name, grid=(nb,), in_specs=[blk, blk, blk], out_specs=blk,
                          out_shape=jax.ShapeDtypeStruct((S, LANES), f32), scratch_shapes=[pltpu.VMEM((1, LANES), f32)],
                          compiler_params=_params("arbitrary"))(tail, dc_q, dc_k)


def _dot_nt(a, b):
    return lax.dot_general(a.astype(bf16), b.astype(bf16), _DIMS["nt"], preferred_element_type=f32)


def _dot_tn(a, b):
    return lax.dot_general(a.astype(bf16), b.astype(bf16), _DIMS["tn"], preferred_element_type=f32)


def _dot_nn(a, b):
    return lax.dot_general(a.astype(bf16), b.astype(bf16), _DIMS["nn"], preferred_element_type=f32)


def _fox_logits(q, k, cc, cr, i, j, T):
    s = _dot_nt(q, k) * (HEAD_DIM ** -0.5) + cc - cr
    qpos = i * T + lax.broadcasted_iota(jnp.int32, (T, T), 0)
    kpos = j * T + lax.broadcasted_iota(jnp.int32, (T, T), 1)
    return jnp.where(qpos >= kpos, s, NEG_BIG)


def _fox_fwd(proj, c_col, c_row, H, name):
    S = proj.shape[0]
    T = _tile(S, ATTN_TILE)
    nt = S // T

    def body(q_ref, k_ref, v_ref, cc_ref, cr_ref, o_ref, lse_ref, m_s, l_s, acc_s):
        i, j = pl.program_id(1), pl.program_id(2)

        @pl.when(j == 0)
        def _():
            m_s[...] = jnp.full_like(m_s, NEG_BIG)
            l_s[...] = jnp.zeros_like(l_s)
            acc_s[...] = jnp.zeros_like(acc_s)

        @pl.when(j <= i)
        def _():
            s = _fox_logits(q_ref[...], k_ref[...], cc_ref[...], cr_ref[...], i, j, T)
            m_new = jnp.maximum(m_s[...], jnp.max(s, axis=-1, keepdims=True))
            p = jnp.exp(s - m_new)
            corr = jnp.exp(m_s[...] - m_new)
            l_s[...] = corr * l_s[...] + jnp.sum(p, axis=-1, keepdims=True)
            acc_s[...] = corr * acc_s[...] + _dot_nn(p, v_ref[...])
            m_s[...] = m_new

        @pl.when(j == i)
        def _():
            o_ref[...] = acc_s[...] / l_s[...]
            lse_ref[...] = m_s[...] + jnp.log(l_s[...])

    sd = jax.ShapeDtypeStruct
    return pl.pallas_call(
        body, name=name, grid=(H, nt, nt),
        in_specs=[pl.BlockSpec((T, HEAD_DIM), lambda h, i, j: (i, h)),
                  pl.BlockSpec((T, HEAD_DIM), lambda h, i, j: (jnp.minimum(j, i), H + h)),
                  pl.BlockSpec((T, HEAD_DIM), lambda h, i, j: (jnp.minimum(j, i), 2 * H + h)),
                  pl.BlockSpec((None, T, 1), lambda h, i, j: (h, i, 0)),
                  pl.BlockSpec((None, 1, T), lambda h, i, j: (h, 0, jnp.minimum(j, i)))],
        out_specs=[pl.BlockSpec((T, HEAD_DIM), lambda h, i, j: (i, h)), pl.BlockSpec((None, T, 1), lambda h, i, j: (h, i, 0))],
        out_shape=[sd((S, H * HEAD_DIM), f32), sd((H, S, 1), f32)],
        scratch_shapes=[pltpu.VMEM((T, 1), f32), pltpu.VMEM((T, 1), f32), pltpu.VMEM((T, HEAD_DIM), f32)],
        compiler_params=_params("parallel", "parallel", "arbitrary", vmem=_attn_vmem(T)))(proj, proj, proj, c_col, c_row)


def _fox_bwd_q(proj, c_col, c_row, o, do, lse, H, name):
    S = proj.shape[0]
    T = _tile(S, ATTN_TILE)
    nt = S // T

    def body(q_ref, k_ref, v_ref, cc_ref, cr_ref, o_ref, do_ref, lse_ref, dq_ref, dc_ref, acc_s, dc_s):
        i, j = pl.program_id(1), pl.program_id(2)

        @pl.when(j == 0)
        def _():
            acc_s[...] = jnp.zeros_like(acc_s)
            dc_s[...] = jnp.zeros_like(dc_s)

        @pl.when(j <= i)
        def _():
            s = _fox_logits(q_ref[...], k_ref[...], cc_ref[...], cr_ref[...], i, j, T)
            p = jnp.exp(s - lse_ref[...])
            do_ = do_ref[...]
            delta = jnp.sum(o_ref[...] * do_, axis=-1, keepdims=True)
            ds = p * (_dot_nt(do_, v_ref[...]) - delta)
            acc_s[...] += _dot_nn(ds, k_ref[...])
            dc_s[...] += jnp.sum(ds, axis=-1, keepdims=True)

        @pl.when(j == i)
        def _():
            dq_ref[...] = acc_s[...] * (HEAD_DIM ** -0.5)
            dc_ref[...] = dc_s[...]

    qb = pl.BlockSpec((T, HEAD_DIM), lambda h, i, j: (i, h))
    col = pl.BlockSpec((None, T, 1), lambda h, i, j: (h, i, 0))
    return pl.pallas_call(
        body, name=name, grid=(H, nt, nt),
        in_specs=[qb, pl.BlockSpec((T, HEAD_DIM), lambda h, i, j: (jnp.minimum(j, i), H + h)),
                  pl.BlockSpec((T, HEAD_DIM), lambda h, i, j: (jnp.minimum(j, i), 2 * H + h)),
                  col, pl.BlockSpec((None, 1, T), lambda h, i, j: (h, 0, jnp.minimum(j, i))), qb, qb, col],
        out_specs=[qb, col], out_shape=[jax.ShapeDtypeStruct((S, H * HEAD_DIM), f32), jax.ShapeDtypeStruct((H, S, 1), f32)],
        scratch_shapes=[pltpu.VMEM((T, HEAD_DIM), f32), pltpu.VMEM((T, 1), f32)],
        compiler_params=_params("parallel", "parallel", "arbitrary", vmem=_attn_vmem(T)))(proj, proj, proj, c_col, c_row, o, do, lse)


def _fox_bwd_kv(proj, c_col, c_row, o, do, lse, H, name):
    S = proj.shape[0]
    T = _tile(S, ATTN_TILE)
    nt = S // T

    def body(q_ref, k_ref, v_ref, cc_ref, cr_ref, o_ref, do_ref, lse_ref, dk_ref, dv_ref, dc_ref, dk_s, dv_s, dc_s):
        j, i = pl.program_id(1), pl.program_id(2)

        @pl.when(i == 0)
        def _():
            dk_s[...] = jnp.zeros_like(dk_s)
            dv_s[...] = jnp.zeros_like(dv_s)
            dc_s[...] = jnp.zeros_like(dc_s)

        @pl.when(i >= j)
        def _():
            s = _fox_logits(q_ref[...], k_ref[...], cc_ref[...], cr_ref[...], i, j, T)
            p = jnp.exp(s - lse_ref[...])
            do_ = do_ref[...]
            delta = jnp.sum(o_ref[...] * do_, axis=-1, keepdims=True)
            ds = p * (_dot_nt(do_, v_ref[...]) - delta)
            dv_s[...] += _dot_tn(p, do_)
            dk_s[...] += _dot_tn(ds, q_ref[...])
            dc_s[...] -= jnp.sum(ds, axis=0, keepdims=True)

        @pl.when(i == nt - 1)
        def _():
            dk_ref[...] = dk_s[...] * (HEAD_DIM ** -0.5)
            dv_ref[...] = dv_s[...]
            dc_ref[...] = dc_s[...]

    qb = pl.BlockSpec((T, HEAD_DIM), lambda h, j, i: (jnp.maximum(i, j), h))
    col = pl.BlockSpec((None, T, 1), lambda h, j, i: (h, jnp.maximum(i, j), 0))
    kb = pl.BlockSpec((T, HEAD_DIM), lambda h, j, i: (j, h))
    sd = jax.ShapeDtypeStruct
    return pl.pallas_call(
        body, name=name, grid=(H, nt, nt),
        in_specs=[qb, pl.BlockSpec((T, HEAD_DIM), lambda h, j, i: (j, H + h)),
                  pl.BlockSpec((T, HEAD_DIM), lambda h, j, i: (j, 2 * H + h)),
                  col, pl.BlockSpec((None, 1, T), lambda h, j, i: (h, 0, j)), qb, qb, col],
        out_specs=[kb, kb, pl.BlockSpec((None, 1, T), lambda h, j, i: (h, 0, j))],
        out_shape=[sd((S, H * HEAD_DIM), f32), sd((S, H * HEAD_DIM), f32), sd((H, 1, S), f32)],
        scratch_shapes=[pltpu.VMEM((T, HEAD_DIM), f32), pltpu.VMEM((T, HEAD_DIM), f32), pltpu.VMEM((1, T), f32)],
        compiler_params=_params("parallel", "parallel", "arbitrary", vmem=_attn_vmem(T)))(proj, proj, proj, c_col, c_row, o, do, lse)


def _adamw(w, g, m, v, name):
    shape = w.shape
    cols = shape[-1]
    rows = w.size // cols
    ops = [t.reshape(rows, cols) for t in (w, g, m, v)]
    tr = rows
    if rows % 8 == 0:
        tr = 8
        while tr * 2 <= rows and rows % (tr * 2) == 0 and tr * 2 * cols * 4 <= (1 << 20):
            tr *= 2

    def body(w_ref, g_ref, m_ref, v_ref, d_ref, mo_ref, vo_ref):
        g_ = g_ref[...]
        m_ = ADAM_B1 * m_ref[...] + (1.0 - ADAM_B1) * g_
        v_ = ADAM_B2 * v_ref[...] + (1.0 - ADAM_B2) * (g_ * g_)
        m_hat = m_ / (1.0 - ADAM_B1 ** ADAM_STEP)
        v_hat = v_ / (1.0 - ADAM_B2 ** ADAM_STEP)
        d_ref[...] = -ADAM_LR * (m_hat / (jnp.sqrt(v_hat) + ADAM_EPS) + ADAM_WD * w_ref[...])
        mo_ref[...] = m_
        vo_ref[...] = v_

    blk = pl.BlockSpec((tr, cols), lambda i: (i, 0))
    outs = pl.pallas_call(body, name=name, grid=(rows // tr,), in_specs=[blk] * 4, out_specs=[blk] * 3,
                          out_shape=[jax.ShapeDtypeStruct((rows, cols), f32)] * 3, compiler_params=_params("parallel"))(*ops)
    return [o.reshape(shape) for o in outs]


def _row_tile(rows, row_bytes, budget=2 << 20):
    best = None
    for t in range(16, rows + 1, 16):
        if rows % t == 0 and t * row_bytes <= budget:
            best = t
    return best or rows


def _sum_leading(a, name):
    n, R, C = a.shape
    tr = _row_tile(R, n * C * 4)

    def body(a_ref, o_ref):
        acc = a_ref[0].astype(f32)
        for k in range(1, n):
            acc = acc + a_ref[k].astype(f32)
        o_ref[...] = acc

    return pl.pallas_call(body, name=name, grid=(R // tr,), in_specs=[pl.BlockSpec((n, tr, C), lambda i: (0, i, 0))],
                          out_specs=pl.BlockSpec((tr, C), lambda i: (i, 0)), out_shape=jax.ShapeDtypeStruct((R, C), f32),
                          compiler_params=_params("parallel"))(a)


def _add2(a, b, dtype, name):
    n, R, C = a.shape
    tr = _row_tile(R, C * 4)

    def body(a_ref, b_ref, o_ref):
        o_ref[...] = (a_ref[...] + b_ref[...]).astype(dtype)

    blk = pl.BlockSpec((None, tr, C), lambda k, i: (k, i, 0))
    return pl.pallas_call(body, name=name, grid=(n, R // tr), in_specs=[blk, blk], out_specs=blk,
                          out_shape=jax.ShapeDtypeStruct((n, R, C), dtype), compiler_params=_params("parallel", "parallel"))(a, b)


def _place():
    x, y, c = lax.axis_index("x"), lax.axis_index("y"), lax.axis_index("c")
    chips = [(1 - x, y), (x, 1 - y), (1 - x, 1 - y)]
    return x, y, c, chips


def _gather_weights(ws, name):
    _, R, C = ws.shape

    def body(ws_ref, out_ref, send_sems, recv_sems):
        x, y, c, chips = _place()
        sibling = (x, y, 1 - c)

        def half(px, py):
            return out_ref.at[2 * px + py, c]

        def copy(k, src, dst, to):
            return pltpu.make_async_remote_copy(src_ref=src, dst_ref=dst, send_sem=send_sems.at[k], recv_sem=recv_sems.at[k],
                                                device_id=to, device_id_type=MESH)

        first = [copy(j, ws_ref.at[c], half(x, y), (*chip, c)) for j, chip in enumerate(chips)]
        for cp in first:
            cp.start()
        passed = [copy(3 + j, half(*chip), half(*chip), sibling) for j, chip in enumerate(chips)]
        for j, chip in enumerate(chips):
            copy(j, ws_ref.at[c], half(*chip), (*chip, c)).wait_recv()
            passed[j].start()
        for j, chip in enumerate(chips):
            pltpu.make_async_remote_copy(src_ref=half(*chip), dst_ref=out_ref.at[2 * chip[0] + chip[1], 1 - c],
                                         send_sem=send_sems.at[3 + j], recv_sem=recv_sems.at[3 + j], device_id=sibling,
                                         device_id_type=MESH).wait_recv()
        for cp in first + passed:
            cp.wait_send()

    return pl.pallas_call(body, name=name, out_shape=jax.ShapeDtypeStruct((N_CHIPS, 2, R, C), ws.dtype), in_specs=[ANY],
                          out_specs=ANY, scratch_shapes=[pltpu.SemaphoreType.DMA((6,)), pltpu.SemaphoreType.DMA((6,))])(ws)


def _pair_exchange(g, name):
    n, _, R, C = g.shape

    def body(g_ref, out_ref, send_sems, recv_sems):
        x, y, c, _ = _place()
        cps = [pltpu.make_async_remote_copy(src_ref=g_ref.at[s, 1 - c], dst_ref=out_ref.at[s], send_sem=send_sems.at[s],
                                            recv_sem=recv_sems.at[s], device_id=(x, y, 1 - c), device_id_type=MESH)
               for s in range(n)]
        for cp in cps:
            cp.start()
        for cp in cps:
            cp.wait()

    return pl.pallas_call(body, name=name, out_shape=jax.ShapeDtypeStruct((n, R, C), g.dtype), in_specs=[ANY], out_specs=ANY,
                          scratch_shapes=[pltpu.SemaphoreType.DMA((n,)), pltpu.SemaphoreType.DMA((n,))])(g)


def _chip_exchange(a, name):
    n, R, C = a.shape

    def body(a_ref, out_ref, send_sems, recv_sems):
        x, y, c, chips = _place()
        me = 2 * x + y
        sends = [pltpu.make_async_remote_copy(src_ref=a_ref.at[2 * chip[0] + chip[1]], dst_ref=out_ref.at[me],
                                              send_sem=send_sems.at[j], recv_sem=recv_sems.at[j], device_id=(*chip, c),
                                              device_id_type=MESH) for j, chip in enumerate(chips)]
        for cp in sends:
            cp.start()
        for j, chip in enumerate(chips):
            pltpu.make_async_remote_copy(src_ref=a_ref.at[me], dst_ref=out_ref.at[2 * chip[0] + chip[1]],
                                         send_sem=send_sems.at[j], recv_sem=recv_sems.at[j], device_id=(*chip, c),
                                         device_id_type=MESH).wait_recv()
        for cp in sends:
            cp.wait_send()

    return pl.pallas_call(body, name=name, out_shape=jax.ShapeDtypeStruct((n, R, C), a.dtype), in_specs=[ANY], out_specs=ANY,
                          scratch_shapes=[pltpu.SemaphoreType.DMA((3,)), pltpu.SemaphoreType.DMA((3,))])(a)


def _pair_swap(r, name):
    R, C = r.shape

    def body(r_ref, out_ref, send_sem, recv_sem):
        x, y, c, _ = _place()
        cp = pltpu.make_async_remote_copy(src_ref=r_ref, dst_ref=out_ref, send_sem=send_sem, recv_sem=recv_sem,
                                          device_id=(x, y, 1 - c), device_id_type=MESH)
        cp.start()
        cp.wait()

    return pl.pallas_call(body, name=name, out_shape=jax.ShapeDtypeStruct((R, C), r.dtype), in_specs=[ANY], out_specs=ANY,
                          scratch_shapes=[pltpu.SemaphoreType.DMA(()), pltpu.SemaphoreType.DMA(())])(r)


def _all_gather8(v, name):
    R, C = v.shape

    def body(v_ref, out_ref, send_sems, recv_sems):
        x, y, c, _ = _place()
        me = 4 * x + 2 * y + c
        peers = [(x ^ (k >> 2), y ^ ((k >> 1) & 1), c ^ (k & 1)) for k in range(1, N_DEV)]
        sends = [pltpu.make_async_remote_copy(src_ref=v_ref, dst_ref=out_ref.at[me], send_sem=send_sems.at[k], recv_sem=recv_sems.at[k],
                                              device_id=peer, device_id_type=MESH) for k, peer in enumerate(peers)]
        for cp in sends:
            cp.start()
        for k, (px, py, pc) in enumerate(peers):
            pltpu.make_async_remote_copy(src_ref=v_ref, dst_ref=out_ref.at[4 * px + 2 * py + pc], send_sem=send_sems.at[k],
                                         recv_sem=recv_sems.at[k], device_id=(px, py, pc), device_id_type=MESH).wait_recv()
        for cp in sends:
            cp.wait_send()

    return pl.pallas_call(body, name=name, out_shape=jax.ShapeDtypeStruct((N_DEV, R, C), v.dtype), in_specs=[ANY], out_specs=ANY,
                          scratch_shapes=[pltpu.SemaphoreType.DMA((7,)), pltpu.SemaphoreType.DMA((7,))])(v)


def _put(buf, block, index):
    return lax.dynamic_update_slice(buf, block[None], (index,) + (0,) * block.ndim)


def _all_reduce8(v, device, name):
    n = v.shape[0]
    rows = -(-n // (LANES * SUM_ROWS)) * SUM_ROWS
    padded = jnp.pad(v, (0, rows * LANES - n)).reshape(rows, LANES)
    return _sum_leading(_put(_all_gather8(padded, name + "_gather"), padded, device), name + "_sum").reshape(-1)[:n]


def _gather4(w, chip, name):
    R, C = w.shape
    halves = w.reshape(2, R // 2, C)
    return _put(_gather_weights(halves, name), halves, chip).reshape(N_CHIPS, R, C)


def _reduce_grads(g4, chip, core, name):
    _, R, C = g4.shape
    g = g4.reshape(N_CHIPS, 2, R // 2, C)
    mine = lax.dynamic_index_in_dim(g, core, axis=1, keepdims=False)
    pair = _add2(mine, _pair_exchange(g, name + "_pair_exchange"), bf16, name + "_pair_sum")
    own = lax.dynamic_index_in_dim(pair, chip, axis=0, keepdims=False)
    half = _sum_leading(_put(_chip_exchange(pair, name + "_chip_exchange"), own, chip), name + "_chip_sum")
    both = jnp.stack([half, _pair_swap(half, name + "_pair_swap")])
    return jnp.where(core == 0, both, both[::-1]).reshape(R, C)


def _segments(H, D):
    W = H * HEAD_DIM
    sizes = (3 * W, H, 2 * W, 3 * W, H, H, W, 3 * D)
    in_tail = (False, True, False, False, True, True, False, False)
    out, first, used = [], 0, [0, 0]
    for size, t in zip(sizes, in_tail):
        out.append((first, size, t, used[t]))
        first += size
        used[t] += size
    return out


def _main_tail_from_shards(g4, H, D):
    C = g4.shape[-1]
    parts = ([], [])
    for first, size, t, _ in _segments(H, D):
        for s in range(g4.shape[0]):
            a, b = max(first, s * C), min(first + size, (s + 1) * C)
            if a < b:
                parts[t].append(g4[s][..., a - s * C:b - s * C])
    parts[1].append(jnp.zeros(g4.shape[1:-1] + (LANES - 3 * H,), g4.dtype))
    return jnp.concatenate(parts[0], axis=-1), jnp.concatenate(parts[1], axis=-1)


def _shards_from_main_tail(main, tail, H, D):
    segs = _segments(H, D)
    C = sum(size for _, size, _, _ in segs) // N_CHIPS
    shards = []
    for s in range(N_CHIPS):
        pieces = []
        for first, size, t, there in segs:
            a, b = max(first, s * C), min(first + size, (s + 1) * C)
            if a < b:
                pieces.append((tail if t else main)[..., there + a - first:there + b - first])
        shards.append(jnp.concatenate(pieces, axis=-1))
    return jnp.stack(shards)


def _pad_cols(a, n):
    return jnp.pad(a, [(0, 0)] * (a.ndim - 1) + [(0, n - a.shape[-1])])


def kernel(x, w_in, b_in, sgu_ln_g, sgu_ln_b, sgu_w, sgu_b, gdn_conv_w, gdn_a_log, gdn_dt_bias, gdn_norm_g, w_proj_a, w_proj_b, w_proj_c, w_out, ln1_g, ln1_b, ffn_w_up, ffn_conv_w, ffn_conv_b, ffn_w_down, ln2_g, ln2_b, loss_target, m_w_in, m_b_in, m_sgu_ln_g, m_sgu_ln_b, m_sgu_w, m_sgu_b, m_gdn_conv_w, m_gdn_a_log, m_gdn_dt_bias, m_gdn_norm_g, m_w_proj_a, m_w_proj_b, m_w_proj_c, m_w_out, m_ln1_g, m_ln1_b, m_ffn_w_up, m_ffn_conv_w, m_ffn_conv_b, m_ffn_w_down, m_ln2_g, m_ln2_b, v_w_in, v_b_in, v_sgu_ln_g, v_sgu_ln_b, v_sgu_w, v_sgu_b, v_gdn_conv_w, v_gdn_a_log, v_gdn_dt_bias, v_gdn_norm_g, v_w_proj_a, v_w_proj_b, v_w_proj_c, v_w_out, v_ln1_g, v_ln1_b, v_ffn_w_up, v_ffn_conv_w, v_ffn_conv_b, v_ffn_w_down, v_ln2_g, v_ln2_b):
    P = dict(w_in=w_in, b_in=b_in, sgu_ln_g=sgu_ln_g, sgu_ln_b=sgu_ln_b, sgu_w=sgu_w, sgu_b=sgu_b, gdn_conv_w=gdn_conv_w,
             gdn_a_log=gdn_a_log, gdn_dt_bias=gdn_dt_bias, gdn_norm_g=gdn_norm_g, w_proj_a=w_proj_a, w_proj_b=w_proj_b,
             w_proj_c=w_proj_c, w_out=w_out, ln1_g=ln1_g, ln1_b=ln1_b, ffn_w_up=ffn_w_up, ffn_conv_w=ffn_conv_w,
             ffn_conv_b=ffn_conv_b, ffn_w_down=ffn_w_down, ln2_g=ln2_g, ln2_b=ln2_b)
    M1 = dict(w_in=m_w_in, b_in=m_b_in, sgu_ln_g=m_sgu_ln_g, sgu_ln_b=m_sgu_ln_b, sgu_w=m_sgu_w, sgu_b=m_sgu_b,
              gdn_conv_w=m_gdn_conv_w, gdn_a_log=m_gdn_a_log, gdn_dt_bias=m_gdn_dt_bias, gdn_norm_g=m_gdn_norm_g,
              w_proj_a=m_w_proj_a, w_proj_b=m_w_proj_b, w_proj_c=m_w_proj_c, w_out=m_w_out, ln1_g=m_ln1_g, ln1_b=m_ln1_b,
              ffn_w_up=m_ffn_w_up, ffn_conv_w=m_ffn_conv_w, ffn_conv_b=m_ffn_conv_b, ffn_w_down=m_ffn_w_down, ln2_g=m_ln2_g,
              ln2_b=m_ln2_b)
    M2 = dict(w_in=v_w_in, b_in=v_b_in, sgu_ln_g=v_sgu_ln_g, sgu_ln_b=v_sgu_ln_b, sgu_w=v_sgu_w, sgu_b=v_sgu_b,
              gdn_conv_w=v_gdn_conv_w, gdn_a_log=v_gdn_a_log, gdn_dt_bias=v_gdn_dt_bias, gdn_norm_g=v_gdn_norm_g,
              w_proj_a=v_w_proj_a, w_proj_b=v_w_proj_b, w_proj_c=v_w_proj_c, w_out=v_w_out, ln1_g=v_ln1_g, ln1_b=v_ln1_b,
              ffn_w_up=v_ffn_w_up, ffn_conv_w=v_ffn_conv_w, ffn_conv_b=v_ffn_conv_b, ffn_w_down=v_ffn_w_down, ln2_g=v_ln2_g,
              ln2_b=v_ln2_b)
    _, S, D = x.shape
    L = w_in.shape[0]
    N_IN = w_in.shape[2] * N_CHIPS
    H = (N_IN - 3 * D) // (9 * HEAD_DIM + 3)
    W = H * HEAD_DIM
    F = ffn_w_down.shape[1] * N_CHIPS
    Fp = -(-F // FF_ALIGN) * FF_ALIGN
    NM = 9 * W + 3 * D
    alpha = (2 * L) ** 0.25
    cx, cy, cc = lax.axis_index("x"), lax.axis_index("y"), lax.axis_index("c")
    chip = 2 * cx + cy

    full_w = []
    for l in range(L):
        g_in = _gather4(w_in[l].astype(bf16), chip, "gather_w_in")
        g_proj = _gather4(jnp.concatenate([w_proj_a[l], w_proj_b[l], w_proj_c[l]], axis=0).astype(bf16), chip, "gather_proj")
        g_rows = _gather4(jnp.concatenate([w_out[l], ffn_w_down[l]], axis=0).astype(bf16), chip, "gather_rows")
        g_up = _gather4(ffn_w_up[l].astype(bf16), chip, "gather_w_up")
        w_main, w_tail = _main_tail_from_shards(g_in, H, D)
        wa, wb, wc = [g_proj[:, k * W:(k + 1) * W].transpose(1, 0, 2).reshape(W, D) for k in range(3)]
        full_w.append(dict(w_main=w_main, w_tail=w_tail, wa=wa, wb=wb, wc=wc, wo=g_rows[:, :D // N_CHIPS].reshape(D, D),
                           wd=jnp.pad(g_rows[:, D // N_CHIPS:].reshape(F, D), ((0, Fp - F), (0, 0))),
                           wg=_pad_cols(jnp.concatenate([g_up[0], g_up[1]], axis=1), Fp),
                           wv=_pad_cols(jnp.concatenate([g_up[2], g_up[3]], axis=1), Fp)))
    gcw_cols, fcw_cols = gdn_conv_w.shape[2], ffn_conv_w.shape[2]
    only_south = (cc == 0).astype(f32)
    placed_g = lax.dynamic_update_slice(jnp.zeros((L, GDN_CONV, 3 * W), f32), gdn_conv_w * only_south, (0, 0, chip * gcw_cols))
    placed_f = lax.dynamic_update_slice(jnp.zeros((L, FFN_CONV, 2 * F), f32), ffn_conv_w * only_south, (0, 0, chip * fcw_cols))
    conv_all = _all_reduce8(jnp.concatenate([placed_g.reshape(-1), placed_f.reshape(-1)]), 2 * chip + cc, "conv_weights")
    gcw_full = conv_all[:L * GDN_CONV * 3 * W].reshape(L, GDN_CONV, 1, 3 * W)
    fcw_full = conv_all[L * GDN_CONV * 3 * W:].reshape(L, FFN_CONV, 1, 2 * F)

    saved = []
    h = x.reshape(S, D)
    for l in range(L):
        b_main, b_tail = _main_tail_from_shards(b_in[l][None, None, :], H, D)
        cwg, cwv = _pad_cols(fcw_full[l][..., :F], Fp), _pad_cols(fcw_full[l][..., F:], Fp)
        cbg, cbv = _pad_cols(ffn_conv_b[l][None, :F], Fp), _pad_cols(ffn_conv_b[l][None, F:], Fp)
        lw = dict(full_w[l], cwg=cwg, cwv=cwv, cbg=cbg, cbv=cbv, gcw=gcw_full[l],
                  sgu_ln_g=sgu_ln_g[l][None, :], sgu_ln_b=sgu_ln_b[l][None, :], sgu_w=sgu_w[l], sgu_b=sgu_b[l][:, :, None],
                  a_log=gdn_a_log[l].reshape(H, 1, 1), dt_bias=gdn_dt_bias[l].reshape(H, 1, 1),
                  norm_g=gdn_norm_g[l].reshape(1, 1, HEAD_DIM), ln1_g=ln1_g[l][None, :], ln1_b=ln1_b[l][None, :],
                  ln2_g=ln2_g[l][None, :], ln2_b=ln2_b[l][None, :])
        proj = _matmul(h, lw["w_main"], "nn", "proj_main", bias=b_main)
        tail = _matmul(h, lw["w_tail"], "nn", "proj_tail", bias=b_tail)
        csum = _fox_prep_fwd(tail, "fox_prep")
        c_col = csum[:, :H].T[:, :, None]
        c_row = csum[:, :H].T[:, None, :]
        y_a, lse = _fox_fwd(proj, c_col, c_row, H, "fox_fwd")
        y_b = _sgu_fwd(proj, lw["sgu_ln_g"], lw["sgu_ln_b"], lw["sgu_w"], lw["sgu_b"], 3 * W, 4 * W, "sgu_fwd")
        qkvc = _gdn_pre_fwd(proj, lw["gcw"], 5 * W, 2 * H, "gdn_pre")
        al = tail[:, H:2 * H].T[:, :, None]
        bl = tail[:, 2 * H:3 * H].T[:, :, None]
        y_c, states = _gdn_scan_fwd(qkvc, al, bl, proj, lw["a_log"], lw["dt_bias"], lw["norm_g"], 8 * W, "gdn_scan")
        pa = _matmul(y_a, lw["wa"], "nn", "branch_proj")
        pb = _matmul(y_b, lw["wb"], "nn", "branch_proj")
        pc = _matmul(y_c, lw["wc"], "nn", "branch_proj")
        merged = _merge_fwd(proj, pa, pb, pc, 9 * W, "merge")
        mix = _matmul(merged, lw["wo"], "nn", "out_proj")
        x1 = _ln_fwd(h, mix, lw["ln1_g"], lw["ln1_b"], alpha, "ln")
        upg = _matmul(x1, lw["wg"], "nn", "ffn_up")
        upv = _matmul(x1, lw["wv"], "nn", "ffn_up")
        act = _ffn_act_fwd(upg, upv, cwg, cwv, cbg, cbv, "ffn_act")
        ffn = _matmul(act, lw["wd"], "nn", "ffn_down")
        x2 = _ln_fwd(x1, ffn, lw["ln2_g"], lw["ln2_b"], alpha, "ln")
        saved.append(dict(lw=lw, h=h, proj=proj, tail=tail, c_col=c_col, c_row=c_row, y_a=y_a, lse=lse, y_b=y_b, qkvc=qkvc, al=al,
                          bl=bl, y_c=y_c, states=states, pa=pa, pb=pb, pc=pc, merged=merged, mix=mix, x1=x1, upg=upg, upv=upv,
                          act=act, ffn=ffn))
        h = x2

    loss_part, dh = _loss_head(h, loss_target.reshape(S, D), "loss_head")
    loss = lax.psum(loss_part[0, 0], ("x", "y", "c"))

    big_grads = [None] * L
    small_grads = [None] * L
    for l in reversed(range(L)):
        s = saved[l]
        lw = s["lw"]
        d_x1r, d_ffn, d_ln2g, d_ln2b = _ln_bwd(s["x1"], s["ffn"], lw["ln2_g"], lw["ln2_b"], dh, alpha, "ln_bwd")
        d_act = _matmul(d_ffn, lw["wd"], "nt", "ffn_down_dx")
        d_wd = _matmul(s["act"], d_ffn, "tn", "ffn_down_dw")
        dupg, dupv, dcwg, dcwv, dcbg, dcbv = _ffn_act_bwd(s["upg"], s["upv"], lw["cwg"], lw["cwv"], lw["cbg"], lw["cbv"], d_act,
                                                          "ffn_act_bwd")
        d_x1 = _matmul(dupg, lw["wg"], "nt", "ffn_up_dx", add=d_x1r)
        d_x1 = _matmul(dupv, lw["wv"], "nt", "ffn_up_dx", add=d_x1)
        d_wg = _matmul(s["x1"], dupg, "tn", "ffn_up_dw")
        d_wv = _matmul(s["x1"], dupv, "tn", "ffn_up_dw")
        d_hr, d_mix, d_ln1g, d_ln1b = _ln_bwd(s["h"], s["mix"], lw["ln1_g"], lw["ln1_b"], d_x1, alpha, "ln_bwd")
        d_merged = _matmul(d_mix, lw["wo"], "nt", "out_proj_dx")
        d_wo = _matmul(s["merged"], d_mix, "tn", "out_proj_dw")
        dg0, dg1, dg2, d_pa, d_pb, d_pc = _merge_bwd(s["proj"], s["pa"], s["pb"], s["pc"], d_merged, 9 * W, "merge_bwd")
        d_ya = _matmul(d_pa, lw["wa"], "nt", "branch_proj_dx")
        d_yb = _matmul(d_pb, lw["wb"], "nt", "branch_proj_dx")
        d_yc = _matmul(d_pc, lw["wc"], "nt", "branch_proj_dx")
        d_wa = _matmul(s["y_a"], d_pa, "tn", "branch_proj_dw")
        d_wb = _matmul(s["y_b"], d_pb, "tn", "branch_proj_dw")
        d_wc = _matmul(s["y_c"], d_pc, "tn", "branch_proj_dw")
        dqkvc, dal, dbl, dgate, d_alog, d_dt, d_ng = _gdn_scan_bwd(s["qkvc"], s["al"], s["bl"], s["proj"], lw["a_log"], lw["dt_bias"],
                                                                   lw["norm_g"], s["states"], d_yc, 8 * W, "gdn_scan_bwd")
        d_gqkv, d_gcw = _gdn_pre_bwd(s["proj"], lw["gcw"], dqkvc, 5 * W, 2 * H, "gdn_pre_bwd")
        d_u, d_v, d_slg, d_slb, d_sw, d_sb = _sgu_bwd(s["proj"], lw["sgu_ln_g"], lw["sgu_ln_b"], lw["sgu_w"], lw["sgu_b"], d_yb,
                                                      3 * W, 4 * W, "sgu_bwd")
        d_q, d_cq = _fox_bwd_q(s["proj"], s["c_col"], s["c_row"], s["y_a"], d_ya, s["lse"], H, "fox_bwd_q")
        d_k, d_v_att, d_c = _fox_bwd_kv(s["proj"], s["c_col"], s["c_row"], s["y_a"], d_ya, s["lse"], H, "fox_bwd_kv")
        d_f = _fox_prep_bwd(s["tail"], _pad_cols(d_cq[:, :, 0].T, LANES), _pad_cols(d_c[:, 0, :].T, LANES), "fox_prep_bwd")
        d_main = jnp.concatenate([d_q, d_k, d_v_att, d_u, d_v, d_gqkv, dgate, dg0, dg1, dg2], axis=1)
        d_tail = _pad_cols(jnp.concatenate([d_f[:, :H], dal[:, :, 0].T, dbl[:, :, 0].T], axis=1), LANES)
        d_wmain = _matmul(s["h"], d_main, "tn", "proj_main_dw")
        d_wtail = _matmul(s["h"], d_tail, "tn", "proj_tail_dw")
        d_bmain = _colsum(d_main, "proj_main_db")
        d_btail = _colsum(d_tail, "proj_tail_db")
        dh = _matmul(d_main, lw["w_main"], "nt", "proj_main_dx", add=d_hr)
        dh = _matmul(d_tail, lw["w_tail"], "nt", "proj_tail_dx", add=dh)
        by_cols = lambda g: g.reshape(g.shape[0], N_CHIPS, g.shape[1] // N_CHIPS).transpose(1, 0, 2)
        big_grads[l] = dict(
            w_in=_shards_from_main_tail(d_wmain, d_wtail, H, D),
            proj=jnp.concatenate([by_cols(d_wa), by_cols(d_wb), by_cols(d_wc)], axis=1),
            rows=jnp.concatenate([d_wo.reshape(N_CHIPS, D // N_CHIPS, D), d_wd[:F].reshape(N_CHIPS, F // N_CHIPS, D)], axis=1),
            w_up=jnp.stack([d_wg[:, :F // 2], d_wg[:, F // 2:F], d_wv[:, :F // 2], d_wv[:, F // 2:F]]))
        small_grads[l] = dict(b_in=_shards_from_main_tail(d_bmain, d_btail, H, D).reshape(-1), sgu_ln_g=d_slg[0], sgu_ln_b=d_slb[0], sgu_w=d_sw,
                              sgu_b=d_sb[:, :, 0], gdn_conv_w=d_gcw[:, 0, :], gdn_a_log=d_alog[:, 0, 0], gdn_dt_bias=d_dt[:, 0, 0],
                              gdn_norm_g=d_ng[0, 0], ln1_g=d_ln1g[0], ln1_b=d_ln1b[0],
                              ffn_conv_w=jnp.concatenate([dcwg[:, 0, :F], dcwv[:, 0, :F]], axis=1),
                              ffn_conv_b=jnp.concatenate([dcbg[0, :F], dcbv[0, :F]]), ln2_g=d_ln2g[0], ln2_b=d_ln2b[0])
    grad_x = dh.reshape(1, S, D)

    grads = {n: [None] * L for n in WEIGHTS}
    for l in range(L):
        grads["w_in"][l] = _reduce_grads(big_grads[l]["w_in"], chip, cc, "grad_w_in")
        proj_sum = _reduce_grads(big_grads[l]["proj"], chip, cc, "grad_proj")
        for k, n in enumerate(("w_proj_a", "w_proj_b", "w_proj_c")):
            grads[n][l] = proj_sum[k * W:(k + 1) * W]
        rows_sum = _reduce_grads(big_grads[l]["rows"], chip, cc, "grad_rows")
        grads["w_out"][l] = rows_sum[:D // N_CHIPS]
        grads["ffn_w_down"][l] = rows_sum[D // N_CHIPS:]
        grads["ffn_w_up"][l] = _reduce_grads(big_grads[l]["w_up"], chip, cc, "grad_w_up")
    small_shapes = {n: small_grads[0][n].shape for n in SMALL}
    small_flat = jnp.concatenate([small_grads[l][n].reshape(-1) for l in range(L) for n in SMALL])
    small_sum = _all_reduce8(small_flat, 2 * chip + cc, "small_grads")
    off = 0
    for l in range(L):
        for n in SMALL:
            size = math.prod(small_shapes[n])
            g = small_sum[off:off + size].reshape(small_shapes[n])
            off += size
            if n == "gdn_conv_w":
                g = lax.dynamic_slice_in_dim(g, chip * gcw_cols, gcw_cols, axis=1)
            elif n == "ffn_conv_w":
                g = lax.dynamic_slice_in_dim(g, chip * fcw_cols, fcw_cols, axis=1)
            grads[n][l] = g
    grads = {n: jnp.stack(grads[n]) for n in WEIGHTS}

    deltas, new_m, new_v = {}, {}, {}
    for n in WEIGHTS:
        deltas[n], new_m[n], new_v[n] = _adamw(P[n], grads[n], M1[n], M2[n], "adamw_" + n)
    return (loss, grad_x, *[grads[n] for n in WEIGHTS], *[deltas[n] for n in WEIGHTS], *[new_m[n] for n in WEIGHTS],
            *[new_v[n] for n in WEIGHTS])
```

```python
import functools
import math

import jax
import jax.numpy as jnp
from jax import lax
from jax.experimental import pallas as pl
from jax.experimental.pallas import tpu as pltpu

f32 = jnp.float32
bf16 = jnp.bfloat16
HIGHEST = lax.Precision.HIGHEST
MESH = pl.DeviceIdType.MESH

HEAD_DIM = 128
CHUNK = 64
SGU_SPAN = 128
GDN_CONV = 4
FFN_CONV = 3
N_CHIPS = 4
N_DEV = 8
LN_EPS = 1e-5
RMS_EPS = 1e-6
ADAM_LR = 0.001
ADAM_B1 = 0.9
ADAM_B2 = 0.999
ADAM_EPS = 1e-08
ADAM_WD = 0.01
ADAM_STEP = 10
NEG_BIG = -1e30
LANES = 128
FF_ALIGN = 512
SUM_ROWS = 256
ATTN_TILE = 1024
VMEM_MARGIN = 12 << 20
VMEM_MOST = 60 << 20

SHARDED = ("w_in", "proj", "rows", "w_up")
SMALL = ("b_in", "sgu_ln_g", "sgu_ln_b", "sgu_w", "sgu_b", "gdn_conv_w", "gdn_a_log", "gdn_dt_bias", "gdn_norm_g",
         "ln1_g", "ln1_b", "ffn_conv_w", "ffn_conv_b", "ln2_g", "ln2_b")
WEIGHTS = ("w_in", "b_in", "sgu_ln_g", "sgu_ln_b", "sgu_w", "sgu_b", "gdn_conv_w", "gdn_a_log", "gdn_dt_bias", "gdn_norm_g",
           "w_proj_a", "w_proj_b", "w_proj_c", "w_out", "ln1_g", "ln1_b", "ffn_w_up", "ffn_conv_w", "ffn_conv_b", "ffn_w_down",
           "ln2_g", "ln2_b")

ANY = pl.BlockSpec(memory_space=pl.ANY)


def _tile(dim, pref):
    t = pref
    while t > 128 and dim % t:
        t //= 2
    return min(t, dim) if dim % min(t, dim) == 0 else dim


def _params(*sem, vmem=None):
    if vmem is None:
        return pltpu.CompilerParams(dimension_semantics=sem)
    return pltpu.CompilerParams(dimension_semantics=sem, vmem_limit_bytes=min(vmem + VMEM_MARGIN, VMEM_MOST))


def _attn_vmem(T):
    return 8 * T * T * 4


_DIMS = {"nn": (((1,), (0,)), ((), ())), "nt": (((1,), (1,)), ((), ())), "tn": (((0,), (0,)), ((), ()))}


def _pick(dim, most):
    for unit in (256, LANES):
        for t in range(min(most, dim) // unit * unit, 0, -unit):
            if dim % t == 0:
                return t
    return dim


def _matmul(a, b, mode, name, bias=None, add=None):
    a, b = a.astype(bf16), b.astype(bf16)
    if mode == "nn":
        (M, K), (_, N) = a.shape, b.shape
    elif mode == "nt":
        (M, K), (N, _) = a.shape, b.shape
    else:
        (K, M), (_, N) = a.shape, b.shape
    has_bias, has_add = bias is not None, add is not None
    tm, tn, tk = _pick(M, 512 if has_add else 1024), _pick(N, 2816), _pick(K, 512)
    nk = K // tk
    vmem = 2 * (tm * tk * a.dtype.itemsize + tk * tn * b.dtype.itemsize + tm * tn * 4 * (2 if has_add else 1)) + tm * tn * 4

    def body(*refs):
        a_ref, b_ref = refs[0], refs[1]
        pos = 2
        bias_ref = add_ref = None
        if has_bias:
            bias_ref, pos = refs[pos], pos + 1
        if has_add:
            add_ref, pos = refs[pos], pos + 1
        o_ref, acc_ref = refs[pos], refs[pos + 1]
        k = pl.program_id(2)

        @pl.when(k == 0)
        def _():
            acc_ref[...] = jnp.zeros_like(acc_ref)

        acc_ref[...] += lax.dot_general(a_ref[...], b_ref[...], _DIMS[mode], preferred_element_type=f32)

        @pl.when(k == nk - 1)
        def _():
            r = acc_ref[...]
            if has_bias:
                r = r + bias_ref[...]
            if has_add:
                r = r + add_ref[...]
            o_ref[...] = r

    if mode == "nn":
        specs = [pl.BlockSpec((tm, tk), lambda i, j, k: (i, k)), pl.BlockSpec((tk, tn), lambda i, j, k: (k, j))]
    elif mode == "nt":
        specs = [pl.BlockSpec((tm, tk), lambda i, j, k: (i, k)), pl.BlockSpec((tn, tk), lambda i, j, k: (j, k))]
    else:
        specs = [pl.BlockSpec((tk, tm), lambda i, j, k: (k, i)), pl.BlockSpec((tk, tn), lambda i, j, k: (k, j))]
    ops = [a, b]
    if has_bias:
        specs.append(pl.BlockSpec((1, tn), lambda i, j, k: (0, j)))
        ops.append(bias)
    if has_add:
        specs.append(pl.BlockSpec((tm, tn), lambda i, j, k: (i, j)))
        ops.append(add)
    return pl.pallas_call(
        body, name=name, grid=(M // tm, N // tn, nk), in_specs=specs,
        out_specs=pl.BlockSpec((tm, tn), lambda i, j, k: (i, j)),
        out_shape=jax.ShapeDtypeStruct((M, N), f32), scratch_shapes=[pltpu.VMEM((tm, tn), f32)],
        compiler_params=pltpu.CompilerParams(dimension_semantics=("parallel", "parallel", "arbitrary"),
                                             vmem_limit_bytes=min(vmem + VMEM_MARGIN, VMEM_MOST)))(*ops)


def _colsum(a, name):
    S, N = a.shape
    tn = _tile(N, 512)

    def body(a_ref, o_ref):
        o_ref[...] = jnp.sum(a_ref[...], axis=0, keepdims=True)

    return pl.pallas_call(body, name=name, grid=(N // tn,), in_specs=[pl.BlockSpec((S, tn), lambda j: (0, j))],
                          out_specs=pl.BlockSpec((1, tn), lambda j: (0, j)), out_shape=jax.ShapeDtypeStruct((1, N), f32),
                          compiler_params=_params("parallel"))(a)


def _ln_fn(alpha, x, y, g, b):
    z = alpha * x + y
    mu = jnp.mean(z, axis=-1, keepdims=True)
    zc = z - mu
    var = jnp.mean(zc * zc, axis=-1, keepdims=True)
    return zc * lax.rsqrt(var + LN_EPS) * g + b


def _ln_fwd(x, y, g, b, alpha, name):
    S, D = x.shape
    tr = _tile(S, 256)

    def body(x_ref, y_ref, g_ref, b_ref, o_ref):
        o_ref[...] = _ln_fn(alpha, x_ref[...], y_ref[...], g_ref[...], b_ref[...])

    row = pl.BlockSpec((tr, D), lambda i: (i, 0))
    par = pl.BlockSpec((1, D), lambda i: (0, 0))
    return pl.pallas_call(body, name=name, grid=(S // tr,), in_specs=[row, row, par, par], out_specs=row,
                          out_shape=jax.ShapeDtypeStruct((S, D), f32), compiler_params=_params("parallel"))(x, y, g, b)


def _ln_bwd(x, y, g, b, dout, alpha, name):
    S, D = x.shape
    tr = _tile(S, 256)

    def body(x_ref, y_ref, g_ref, b_ref, d_ref, dx_ref, dy_ref, dg_ref, db_ref):
        _, vjp = jax.vjp(functools.partial(_ln_fn, alpha), x_ref[...], y_ref[...], g_ref[...], b_ref[...])
        dx, dy, dg, db = vjp(d_ref[...])
        dx_ref[...] = dx
        dy_ref[...] = dy

        @pl.when(pl.program_id(0) == 0)
        def _():
            dg_ref[...] = jnp.zeros_like(dg_ref)
            db_ref[...] = jnp.zeros_like(db_ref)

        dg_ref[...] += dg
        db_ref[...] += db

    row = pl.BlockSpec((tr, D), lambda i: (i, 0))
    par = pl.BlockSpec((1, D), lambda i: (0, 0))
    sd = jax.ShapeDtypeStruct
    return pl.pallas_call(body, name=name, grid=(S // tr,), in_specs=[row, row, par, par, row],
                          out_specs=[row, row, par, par],
                          out_shape=[sd((S, D), f32), sd((S, D), f32), sd((1, D), f32), sd((1, D), f32)],
                          compiler_params=_params("arbitrary"))(x, y, g, b, dout)


def _loss_head(y, t, name):
    S, D = y.shape
    tr = _tile(S, 256)

    def body(y_ref, t_ref, l_ref, d_ref):
        e = y_ref[...] - t_ref[...]
        d_ref[...] = e / D

        @pl.when(pl.program_id(0) == 0)
        def _():
            l_ref[...] = jnp.zeros_like(l_ref)

        l_ref[...] += 0.5 * jnp.sum(jnp.mean(e * e, axis=-1, keepdims=True))

    row = pl.BlockSpec((tr, D), lambda i: (i, 0))
    return pl.pallas_call(body, name=name, grid=(S // tr,), in_specs=[row, row],
                          out_specs=[pl.BlockSpec((8, LANES), lambda i: (0, 0)), row],
                          out_shape=[jax.ShapeDtypeStruct((8, LANES), f32), jax.ShapeDtypeStruct((S, D), f32)],
                          compiler_params=_params("arbitrary"))(y, t)


def _merge_fn(g0, g1, g2, pa, pb, pc):
    return jax.nn.sigmoid(g0) * pa + jax.nn.sigmoid(g1) * pb + jax.nn.sigmoid(g2) * pc


def _merge_specs(S, D, gate_off):
    tr = _tile(S, 512)
    tc = _tile(math.gcd(gate_off, D), 512)
    gates = [pl.BlockSpec((tr, tc), functools.partial(lambda k, i, j: (i, (gate_off + k * D) // tc + j), k)) for k in range(3)]
    tile = pl.BlockSpec((tr, tc), lambda i, j: (i, j))
    return tr, tc, gates, tile


def _merge_fwd(proj, pa, pb, pc, gate_off, name):
    S, D = pa.shape
    tr, tc, gates, tile = _merge_specs(S, D, gate_off)

    def body(g0, g1, g2, a, b, c, o_ref):
        o_ref[...] = _merge_fn(g0[...], g1[...], g2[...], a[...], b[...], c[...])

    return pl.pallas_call(body, name=name, grid=(S // tr, D // tc), in_specs=gates + [tile] * 3, out_specs=tile,
                          out_shape=jax.ShapeDtypeStruct((S, D), f32),
                          compiler_params=_params("parallel", "parallel"))(proj, proj, proj, pa, pb, pc)


def _merge_bwd(proj, pa, pb, pc, dm, gate_off, name):
    S, D = pa.shape
    tr, tc, gates, tile = _merge_specs(S, D, gate_off)

    def body(g0, g1, g2, a, b, c, d, dg0, dg1, dg2, da, db, dc):
        _, vjp = jax.vjp(_merge_fn, g0[...], g1[...], g2[...], a[...], b[...], c[...])
        for ref, val in zip((dg0, dg1, dg2, da, db, dc), vjp(d[...])):
            ref[...] = val

    sd = jax.ShapeDtypeStruct
    return pl.pallas_call(body, name=name, grid=(S // tr, D // tc), in_specs=gates + [tile] * 4,
                          out_specs=[tile] * 6, out_shape=[sd((S, D), f32)] * 6,
                          compiler_params=_params("parallel", "parallel"))(proj, proj, proj, pa, pb, pc, dm)


def _sgu_fn(nb, u, v, ln_g, ln_b, w_s, b_s):
    mu = jnp.mean(v, axis=-1, keepdims=True)
    vc = v - mu
    var = jnp.mean(vc * vc, axis=-1, keepdims=True)
    vn = vc * lax.rsqrt(var + LN_EPS) * ln_g + ln_b
    r = lax.broadcasted_iota(jnp.int32, (SGU_SPAN, SGU_SPAN), 0) // CHUNK
    c = lax.broadcasted_iota(jnp.int32, (SGU_SPAN, SGU_SPAN), 1) // CHUNK
    wm = jnp.where(r >= c, w_s, 0.0)
    vn3 = vn.reshape(nb, SGU_SPAN, HEAD_DIM)
    mixed = lax.dot_general(jnp.broadcast_to(wm, (nb, SGU_SPAN, SGU_SPAN)), vn3, (((2,), (1,)), ((0,), (0,))),
                            preferred_element_type=f32)
    mixed = mixed + b_s
    return u * mixed.reshape(nb * SGU_SPAN, HEAD_DIM)


def _sgu_specs(S, G, u_off, v_off):
    nb = max(1, min(8, S // SGU_SPAN))
    rows = nb * SGU_SPAN
    ub = pl.BlockSpec((rows, HEAD_DIM), lambda g, n: (n, u_off // HEAD_DIM + g))
    vb = pl.BlockSpec((rows, HEAD_DIM), lambda g, n: (n, v_off // HEAD_DIM + g))
    lnb = pl.BlockSpec((1, HEAD_DIM), lambda g, n: (0, g))
    wb = pl.BlockSpec((None, SGU_SPAN, SGU_SPAN), lambda g, n: (g, 0, 0))
    bb = pl.BlockSpec((None, SGU_SPAN, 1), lambda g, n: (g, 0, 0))
    return nb, rows, ub, vb, lnb, wb, bb


def _sgu_fwd(proj, ln_g, ln_b, w_s, b_s, u_off, v_off, name):
    S = proj.shape[0]
    G = w_s.shape[0]
    nb, rows, ub, vb, lnb, wb, bb = _sgu_specs(S, G, u_off, v_off)

    def body(u, v, lg, lb, w, b, o_ref):
        o_ref[...] = _sgu_fn(nb, u[...], v[...], lg[...], lb[...], w[...], b[...])

    return pl.pallas_call(body, name=name, grid=(G, S // rows), in_specs=[ub, vb, lnb, lnb, wb, bb],
                          out_specs=pl.BlockSpec((rows, HEAD_DIM), lambda g, n: (n, g)),
                          out_shape=jax.ShapeDtypeStruct((S, G * HEAD_DIM), f32),
                          compiler_params=_params("parallel", "parallel"))(proj, proj, ln_g, ln_b, w_s, b_s)


def _sgu_bwd(proj, ln_g, ln_b, w_s, b_s, dy, u_off, v_off, name):
    S = proj.shape[0]
    G = w_s.shape[0]
    nb, rows, ub, vb, lnb, wb, bb = _sgu_specs(S, G, u_off, v_off)

    def body(u, v, lg, lb, w, b, d, du, dv, dlg, dlb, dw, db):
        _, vjp = jax.vjp(functools.partial(_sgu_fn, nb), u[...], v[...], lg[...], lb[...], w[...], b[...])
        gu, gv, glg, glb, gw, gb = vjp(d[...])
        du[...] = gu
        dv[...] = gv

        @pl.when(pl.program_id(1) == 0)
        def _():
            for ref in (dlg, dlb, dw, db):
                ref[...] = jnp.zeros_like(ref)

        dlg[...] += glg
        dlb[...] += glb
        dw[...] += gw
        db[...] += gb

    tile = pl.BlockSpec((rows, HEAD_DIM), lambda g, n: (n, g))
    sd = jax.ShapeDtypeStruct
    W = G * HEAD_DIM
    return pl.pallas_call(body, name=name, grid=(G, S // rows), in_specs=[ub, vb, lnb, lnb, wb, bb, tile],
                          out_specs=[tile, tile, lnb, lnb, wb, bb],
                          out_shape=[sd((S, W), f32), sd((S, W), f32), sd((1, W), f32), sd((1, W), f32),
                                     sd((G, SGU_SPAN, SGU_SPAN), f32), sd((G, SGU_SPAN, 1), f32)],
                          compiler_params=_params("parallel", "arbitrary"))(proj, proj, ln_g, ln_b, w_s, b_s, dy)


def _shift_down(x, k):
    if k == 0:
        return x
    rows = lax.broadcasted_iota(jnp.int32, x.shape, 0)
    return jnp.where(rows >= k, pltpu.roll(x, k, 0), 0.0)


def _shift_up(x, k):
    if k == 0:
        return x
    n = x.shape[0]
    rows = lax.broadcasted_iota(jnp.int32, x.shape, 0)
    return jnp.where(rows < n - k, pltpu.roll(x, n - k, 0), 0.0)


def _conv(x, w_ref, width):
    out = w_ref[width - 1] * x
    for j in range(width - 1):
        out = out + w_ref[j] * _shift_down(x, width - 1 - j)
    return out


def _conv_bwd(x, dz, w_ref, dw_ref, width):
    dx = w_ref[width - 1] * dz
    dw_ref[width - 1] = jnp.sum(dz * x, axis=0, keepdims=True)
    for j in range(width - 1):
        k = width - 1 - j
        dx = dx + w_ref[j] * _shift_up(dz, k)
        dw_ref[j] = jnp.sum(dz * _shift_down(x, k), axis=0, keepdims=True)
    return dx


def _silu(z):
    return z * jax.nn.sigmoid(z)


def _dsilu(z):
    s = jax.nn.sigmoid(z)
    return s * (1.0 + z * (1.0 - s))


def _ffn_act_fwd(upg, upv, cwg, cwv, cbg, cbv, name):
    S, Fp = upg.shape

    def body(g_ref, v_ref, wg, wv, bg, bv, o_ref):
        hg = _conv(g_ref[...], wg, FFN_CONV) + bg[...]
        hv = _conv(v_ref[...], wv, FFN_CONV) + bv[...]
        o_ref[...] = _silu(hg) * hv

    col = pl.BlockSpec((S, LANES), lambda j: (0, j))
    wsp = pl.BlockSpec((FFN_CONV, 1, LANES), lambda j: (0, 0, j))
    bsp = pl.BlockSpec((1, LANES), lambda j: (0, j))
    return pl.pallas_call(body, name=name, grid=(Fp // LANES,), in_specs=[col, col, wsp, wsp, bsp, bsp], out_specs=col,
                          out_shape=jax.ShapeDtypeStruct((S, Fp), f32),
                          compiler_params=_params("parallel"))(upg, upv, cwg, cwv, cbg, cbv)


def _ffn_act_bwd(upg, upv, cwg, cwv, cbg, cbv, dact, name):
    S, Fp = upg.shape

    def body(g_ref, v_ref, wg, wv, bg, bv, d_ref, dg_ref, dv_ref, dwg, dwv, dbg, dbv):
        xg, xv, d = g_ref[...], v_ref[...], d_ref[...]
        hg = _conv(xg, wg, FFN_CONV) + bg[...]
        hv = _conv(xv, wv, FFN_CONV) + bv[...]
        dhg = d * hv * _dsilu(hg)
        dhv = d * _silu(hg)
        dbg[...] = jnp.sum(dhg, axis=0, keepdims=True)
        dbv[...] = jnp.sum(dhv, axis=0, keepdims=True)
        dg_ref[...] = _conv_bwd(xg, dhg, wg, dwg, FFN_CONV)
        dv_ref[...] = _conv_bwd(xv, dhv, wv, dwv, FFN_CONV)

    col = pl.BlockSpec((S, LANES), lambda j: (0, j))
    wsp = pl.BlockSpec((FFN_CONV, 1, LANES), lambda j: (0, 0, j))
    bsp = pl.BlockSpec((1, LANES), lambda j: (0, j))
    sd = jax.ShapeDtypeStruct
    return pl.pallas_call(body, name=name, grid=(Fp // LANES,), in_specs=[col, col, wsp, wsp, bsp, bsp, col],
                          out_specs=[col, col, wsp, wsp, bsp, bsp],
                          out_shape=[sd((S, Fp), f32), sd((S, Fp), f32), sd((FFN_CONV, 1, Fp), f32), sd((FFN_CONV, 1, Fp), f32),
                                     sd((1, Fp), f32), sd((1, Fp), f32)],
                          compiler_params=_params("parallel"))(upg, upv, cwg, cwv, cbg, cbv, dact)


def _gdn_pre_fwd(proj, cw, x_off, n_norm, name):
    S = proj.shape[0]
    C = cw.shape[2]

    def body(x_ref, w_ref, o_ref):
        s = _silu(_conv(x_ref[...], w_ref, GDN_CONV))
        r = lax.rsqrt(jnp.sum(s * s, axis=-1, keepdims=True) + RMS_EPS)
        o_ref[...] = jnp.where(pl.program_id(0) < n_norm, s * r, s)

    xs = pl.BlockSpec((S, LANES), lambda j: (0, x_off // LANES + j))
    col = pl.BlockSpec((S, LANES), lambda j: (0, j))
    wsp = pl.BlockSpec((GDN_CONV, 1, LANES), lambda j: (0, 0, j))
    return pl.pallas_call(body, name=name, grid=(C // LANES,), in_specs=[xs, wsp], out_specs=col,
                          out_shape=jax.ShapeDtypeStruct((S, C), f32), compiler_params=_params("parallel"))(proj, cw)


def _gdn_pre_bwd(proj, cw, dout, x_off, n_norm, name):
    S = proj.shape[0]
    C = cw.shape[2]

    def body(x_ref, w_ref, d_ref, dx_ref, dw_ref):
        x, d = x_ref[...], d_ref[...]
        z = _conv(x, w_ref, GDN_CONV)
        s = _silu(z)
        r = lax.rsqrt(jnp.sum(s * s, axis=-1, keepdims=True) + RMS_EPS)
        ds_norm = d * r - s * (r * r * r) * jnp.sum(d * s, axis=-1, keepdims=True)
        ds = jnp.where(pl.program_id(0) < n_norm, ds_norm, d)
        dz = ds * _dsilu(z)
        dx_ref[...] = _conv_bwd(x, dz, w_ref, dw_ref, GDN_CONV)

    xs = pl.BlockSpec((S, LANES), lambda j: (0, x_off // LANES + j))
    col = pl.BlockSpec((S, LANES), lambda j: (0, j))
    wsp = pl.BlockSpec((GDN_CONV, 1, LANES), lambda j: (0, 0, j))
    return pl.pallas_call(body, name=name, grid=(C // LANES,), in_specs=[xs, wsp, col], out_specs=[col, wsp],
                          out_shape=[jax.ShapeDtypeStruct((S, C), f32), jax.ShapeDtypeStruct((GDN_CONV, 1, C), f32)],
                          compiler_params=_params("parallel"))(proj, cw, dout)


def _bmm(a, b, prec=None):
    return lax.dot_general(a, b, (((2,), (1,)), ((0,), (0,))), precision=prec, preferred_element_type=f32)


def _bmm_nt(a, b, prec=None):
    return lax.dot_general(a, b, (((2,), (2,)), ((0,), (0,))), precision=prec, preferred_element_type=f32)


def _bmm_tn(a, b, prec=None):
    return lax.dot_general(a, b, (((1,), (1,)), ((0,), (0,))), precision=prec, preferred_element_type=f32)


def _softplus(x):
    return jnp.maximum(x, 0.0) + jnp.log1p(jnp.exp(-jnp.abs(x)))


@jax.custom_vjp
def _unit_lower_inverse(a):
    H, C, _ = a.shape
    r = lax.broadcasted_iota(jnp.int32, (H, C, C), 1)
    c = lax.broadcasted_iota(jnp.int32, (H, C, C), 2)
    p = -a
    inv = (r == c).astype(f32) + p
    for _ in range(int(math.log2(C)) - 1):
        p = _bmm(p, p, HIGHEST)
        inv = inv + _bmm(inv, p, HIGHEST)
    return inv


def _unit_lower_inverse_fwd(a):
    inv = _unit_lower_inverse(a)
    return inv, inv


def _unit_lower_inverse_bwd(inv, d_inv):
    return (-_bmm_nt(_bmm_tn(inv, d_inv, HIGHEST), inv, HIGHEST),)


_unit_lower_inverse.defvjp(_unit_lower_inverse_fwd, _unit_lower_inverse_bwd)


def _gdn_chunk(q, k, v, al, bl, gate, a_log, dt_bias, norm_g, state):
    H, C, Dh = q.shape
    r = lax.broadcasted_iota(jnp.int32, (H, C, C), 1)
    c = lax.broadcasted_iota(jnp.int32, (H, C, C), 2)
    tril = r >= c
    strict = r > c
    lower = tril.astype(f32)
    upper = (r <= c).astype(f32)
    ones = jnp.ones((H, C, C), f32)
    g = -jnp.exp(a_log) * _softplus(al + dt_bias)
    beta = jax.nn.sigmoid(bl)
    g_lanes = jnp.broadcast_to(g, (H, C, Dh))
    g_sq = jnp.broadcast_to(g, (H, C, C))
    gc = _bmm(lower, g_lanes, HIGHEST)
    gc_i = _bmm(lower, g_sq, HIGHEST)
    gc_j = _bmm(ones, g_sq * upper, HIGHEST)
    decay = jnp.where(tril, jnp.exp(jnp.where(tril, gc_i - gc_j, 0.0)), 0.0)
    qs = q * (Dh ** -0.5)
    kb = k * beta
    a_kk = jnp.where(strict, _bmm_nt(kb, k) * decay, 0.0)
    rhs_u = v * beta
    rhs_w = kb * jnp.exp(gc)
    inv = _unit_lower_inverse(a_kk)
    u = _bmm(inv, rhs_u, HIGHEST)
    w = _bmm(inv, rhs_w, HIGHEST)
    qk = jnp.where(tril, _bmm_nt(qs, k) * decay, 0.0)
    g_last = jnp.sum(g, axis=1, keepdims=True)
    k_dec = k * jnp.exp(g_last - gc)
    q_dec = qs * jnp.exp(gc)
    v_new = u - _bmm(w, state)
    o = _bmm(q_dec, state) + _bmm(qk, v_new)
    new_state = state * jnp.exp(g_last) + _bmm_tn(k_dec, v_new)
    y = o * lax.rsqrt(jnp.mean(o * o, axis=-1, keepdims=True) + RMS_EPS) * norm_g * _silu(gate)
    return y, new_state


def _heads(ref, off, H):
    return jnp.stack([ref[:, off + h * HEAD_DIM: off + (h + 1) * HEAD_DIM] for h in range(H)])


def _gdn_scan_fwd(qkvc, al, bl, proj, a_log, dt_bias, norm_g, gate_off, name):
    S = qkvc.shape[0]
    H = al.shape[0]
    W = H * HEAD_DIM
    n = S // CHUNK

    def body(x_ref, al_ref, bl_ref, gate_ref, alog_ref, dt_ref, ng_ref, y_ref, st_ref, state):
        @pl.when(pl.program_id(0) == 0)
        def _():
            state[...] = jnp.zeros_like(state)

        st_ref[...] = state[...]
        y, new = _gdn_chunk(_heads(x_ref, 0, H), _heads(x_ref, W, H), _heads(x_ref, 2 * W, H), al_ref[...], bl_ref[...],
                            _heads(gate_ref, 0, H), alog_ref[...], dt_ref[...], ng_ref[...], state[...])
        state[...] = new
        for h in range(H):
            y_ref[:, h * HEAD_DIM:(h + 1) * HEAD_DIM] = y[h]

    sd = jax.ShapeDtypeStruct
    col = pl.BlockSpec((H, CHUNK, 1), lambda i: (0, i, 0))
    par = pl.BlockSpec((H, 1, 1), lambda i: (0, 0, 0))
    return pl.pallas_call(
        body, name=name, grid=(n,),
        in_specs=[pl.BlockSpec((CHUNK, 3 * W), lambda i: (i, 0)), col, col,
                  pl.BlockSpec((CHUNK, W), lambda i: (i, gate_off // W)), par, par,
                  pl.BlockSpec((1, 1, HEAD_DIM), lambda i: (0, 0, 0))],
        out_specs=[pl.BlockSpec((CHUNK, W), lambda i: (i, 0)),
                   pl.BlockSpec((None, H, HEAD_DIM, HEAD_DIM), lambda i: (i, 0, 0, 0))],
        out_shape=[sd((S, W), f32), sd((n, H, HEAD_DIM, HEAD_DIM), f32)],
        scratch_shapes=[pltpu.VMEM((H, HEAD_DIM, HEAD_DIM), f32)],
        compiler_params=_params("arbitrary"))(qkvc, al, bl, proj, a_log, dt_bias, norm_g)


def _gdn_scan_bwd(qkvc, al, bl, proj, a_log, dt_bias, norm_g, states, dy, gate_off, name, carried=()):
    S = qkvc.shape[0]
    H = al.shape[0]
    W = H * HEAD_DIM
    n = S // CHUNK
    nc = len(carried)

    def body(*refs):
        x_ref, al_ref, bl_ref, gate_ref, alog_ref, dt_ref, ng_ref, st_ref, dy_ref = refs[:9]
        carried_refs = refs[9:9 + nc]
        dx_ref, dal_ref, dbl_ref, dgate_ref, dalog_ref, ddt_ref, dng_ref = refs[9 + nc:16 + nc]
        landed_refs = refs[16 + nc:16 + 2 * nc]
        dstate = refs[16 + 2 * nc]
        sems = refs[17 + 2 * nc:]

        @pl.when(pl.program_id(0) == 0)
        def _():
            dstate[...] = jnp.zeros_like(dstate)
            for ref in (dalog_ref, ddt_ref, dng_ref):
                ref[...] = jnp.zeros_like(ref)
            if nc:
                for cp in _chip_copies(carried_refs, landed_refs, *sems)[0]:
                    cp.start()

        _, vjp = jax.vjp(_gdn_chunk, _heads(x_ref, 0, H), _heads(x_ref, W, H), _heads(x_ref, 2 * W, H), al_ref[...],
                         bl_ref[...], _heads(gate_ref, 0, H), alog_ref[...], dt_ref[...], ng_ref[...], st_ref[...])
        dq, dk, dv, dal, dbl, dgate, dalog, ddt, dng, dst = vjp((_heads(dy_ref, 0, H), dstate[...]))
        dstate[...] = dst
        for h in range(H):
            lo, hi = h * HEAD_DIM, (h + 1) * HEAD_DIM
            dx_ref[:, lo:hi] = dq[h]
            dx_ref[:, W + lo:W + hi] = dk[h]
            dx_ref[:, 2 * W + lo:2 * W + hi] = dv[h]
            dgate_ref[:, lo:hi] = dgate[h]
        dal_ref[...] = dal
        dbl_ref[...] = dbl
        dalog_ref[...] += dalog
        ddt_ref[...] += ddt
        dng_ref[...] += dng

        if nc:
            @pl.when(pl.program_id(0) == n - 1)
            def _():
                sends, lands = _chip_copies(carried_refs, landed_refs, *sems)
                for cp in lands:
                    cp.wait_recv()
                for cp in sends:
                    cp.wait_send()

    sd = jax.ShapeDtypeStruct
    rev = lambda i: n - 1 - i
    col = pl.BlockSpec((H, CHUNK, 1), lambda i: (0, rev(i), 0))
    par = pl.BlockSpec((H, 1, 1), lambda i: (0, 0, 0))
    ng = pl.BlockSpec((1, 1, HEAD_DIM), lambda i: (0, 0, 0))
    xs = pl.BlockSpec((CHUNK, 3 * W), lambda i: (rev(i), 0))
    ws = pl.BlockSpec((CHUNK, W), lambda i: (rev(i), 0))
    return pl.pallas_call(
        body, name=name, grid=(n,),
        in_specs=[xs, col, col, pl.BlockSpec((CHUNK, W), lambda i: (rev(i), gate_off // W)), par, par, ng,
                  pl.BlockSpec((None, H, HEAD_DIM, HEAD_DIM), lambda i: (rev(i), 0, 0, 0)), ws] + [ANY] * nc,
        out_specs=[xs, col, col, ws, par, par, ng] + [ANY] * nc,
        out_shape=[sd((S, 3 * W), f32), sd((H, S, 1), f32), sd((H, S, 1), f32), sd((S, W), f32), sd((H, 1, 1), f32),
                   sd((H, 1, 1), f32), sd((1, 1, HEAD_DIM), f32)] + [sd(a.shape, a.dtype) for a in carried],
        scratch_shapes=[pltpu.VMEM((H, HEAD_DIM, HEAD_DIM), f32)] + (_chip_semaphores(nc) if nc else []),
        compiler_params=_params("arbitrary"))(qkvc, al, bl, proj, a_log, dt_bias, norm_g, states, dy, *carried)


def _tri(n, upper):
    r = lax.broadcasted_iota(jnp.int32, (n, n), 0)
    c = lax.broadcasted_iota(jnp.int32, (n, n), 1)
    return (r <= c if upper else r >= c).astype(f32)


def _fox_prep_fwd(tail, name):
    S = tail.shape[0]
    tb = _tile(S, 512)

    def body(x_ref, o_ref, carry):
        @pl.when(pl.program_id(0) == 0)
        def _():
            carry[...] = jnp.zeros_like(carry)

        x = x_ref[...]
        lf = jnp.minimum(x, 0.0) - jnp.log1p(jnp.exp(-jnp.abs(x)))
        o_ref[...] = jnp.dot(_tri(tb, False), lf, precision=HIGHEST, preferred_element_type=f32) + carry[...]
        carry[...] += jnp.sum(lf, axis=0, keepdims=True)

    blk = pl.BlockSpec((tb, LANES), lambda i: (i, 0))
    return pl.pallas_call(body, name=name, grid=(S // tb,), in_specs=[blk], out_specs=blk,
                          out_shape=jax.ShapeDtypeStruct((S, LANES), f32), scratch_shapes=[pltpu.VMEM((1, LANES), f32)],
                          compiler_params=_params("arbitrary"))(tail)


def _fox_prep_bwd(tail, dc_q, dc_k, name):
    S = tail.shape[0]
    tb = _tile(S, 512)
    nb = S // tb

    def body(x_ref, dq_ref, d_ref, o_ref, carry):
        @pl.when(pl.program_id(0) == 0)
        def _():
            carry[...] = jnp.zeros_like(carry)

        d = d_ref[...] + dq_ref[...]
        dlf = jnp.dot(_tri(tb, True), d, precision=HIGHEST, preferred_element_type=f32) + carry[...]
        carry[...] += jnp.sum(d, axis=0, keepdims=True)
        o_ref[...] = dlf * jax.nn.sigmoid(-x_ref[...])

    blk = pl.BlockSpec((tb, LANES), lambda i: (nb - 1 - i, 0))
    return pl.pallas_call(body, name=name, grid=(nb,), in_specs=[blk, blk, blk], out_specs=blk,
                          out_shape=jax.ShapeDtypeStruct((S, LANES), f32), scratch_shapes=[pltpu.VMEM((1, LANES), f32)],
                          compiler_params=_params("arbitrary"))(tail, dc_q, dc_k)


def _dot_nt(a, b):
    return lax.dot_general(a.astype(bf16), b.astype(bf16), _DIMS["nt"], preferred_element_type=f32)


def _dot_tn(a, b):
    return lax.dot_general(a.astype(bf16), b.astype(bf16), _DIMS["tn"], preferred_element_type=f32)


def _dot_nn(a, b):
    return lax.dot_general(a.astype(bf16), b.astype(bf16), _DIMS["nn"], preferred_element_type=f32)


def _fox_logits(q, k, cc, cr, i, j, T):
    s = _dot_nt(q, k) * (HEAD_DIM ** -0.5) + cc - cr
    qpos = i * T + lax.broadcasted_iota(jnp.int32, (T, T), 0)
    kpos = j * T + lax.broadcasted_iota(jnp.int32, (T, T), 1)
    return jnp.where(qpos >= kpos, s, NEG_BIG)


def _fox_fwd(proj, c_col, c_row, H, name):
    S = proj.shape[0]
    T = _tile(S, ATTN_TILE)
    nt = S // T

    def body(q_ref, k_ref, v_ref, cc_ref, cr_ref, o_ref, lse_ref, m_s, l_s, acc_s):
        i, j = pl.program_id(1), pl.program_id(2)

        @pl.when(j == 0)
        def _():
            m_s[...] = jnp.full_like(m_s, NEG_BIG)
            l_s[...] = jnp.zeros_like(l_s)
            acc_s[...] = jnp.zeros_like(acc_s)

        @pl.when(j <= i)
        def _():
            s = _fox_logits(q_ref[...], k_ref[...], cc_ref[...], cr_ref[...], i, j, T)
            m_new = jnp.maximum(m_s[...], jnp.max(s, axis=-1, keepdims=True))
            p = jnp.exp(s - m_new)
            corr = jnp.exp(m_s[...] - m_new)
            l_s[...] = corr * l_s[...] + jnp.sum(p, axis=-1, keepdims=True)
            acc_s[...] = corr * acc_s[...] + _dot_nn(p, v_ref[...])
            m_s[...] = m_new

        @pl.when(j == i)
        def _():
            o_ref[...] = acc_s[...] / l_s[...]
            lse_ref[...] = m_s[...] + jnp.log(l_s[...])

    sd = jax.ShapeDtypeStruct
    return pl.pallas_call(
        body, name=name, grid=(H, nt, nt),
        in_specs=[pl.BlockSpec((T, HEAD_DIM), lambda h, i, j: (i, h)),
                  pl.BlockSpec((T, HEAD_DIM), lambda h, i, j: (jnp.minimum(j, i), H + h)),
                  pl.BlockSpec((T, HEAD_DIM), lambda h, i, j: (jnp.minimum(j, i), 2 * H + h)),
                  pl.BlockSpec((None, T, 1), lambda h, i, j: (h, i, 0)),
                  pl.BlockSpec((None, 1, T), lambda h, i, j: (h, 0, jnp.minimum(j, i)))],
        out_specs=[pl.BlockSpec((T, HEAD_DIM), lambda h, i, j: (i, h)), pl.BlockSpec((None, T, 1), lambda h, i, j: (h, i, 0))],
        out_shape=[sd((S, H * HEAD_DIM), f32), sd((H, S, 1), f32)],
        scratch_shapes=[pltpu.VMEM((T, 1), f32), pltpu.VMEM((T, 1), f32), pltpu.VMEM((T, HEAD_DIM), f32)],
        compiler_params=_params("parallel", "parallel", "arbitrary", vmem=_attn_vmem(T)))(proj, proj, proj, c_col, c_row)


def _fox_bwd_q(proj, c_col, c_row, o, do, lse, H, name):
    S = proj.shape[0]
    T = _tile(S, ATTN_TILE)
    nt = S // T

    def body(q_ref, k_ref, v_ref, cc_ref, cr_ref, o_ref, do_ref, lse_ref, dq_ref, dc_ref, acc_s, dc_s):
        i, j = pl.program_id(1), pl.program_id(2)

        @pl.when(j == 0)
        def _():
            acc_s[...] = jnp.zeros_like(acc_s)
            dc_s[...] = jnp.zeros_like(dc_s)

        @pl.when(j <= i)
        def _():
            s = _fox_logits(q_ref[...], k_ref[...], cc_ref[...], cr_ref[...], i, j, T)
            p = jnp.exp(s - lse_ref[...])
            do_ = do_ref[...]
            delta = jnp.sum(o_ref[...] * do_, axis=-1, keepdims=True)
            ds = p * (_dot_nt(do_, v_ref[...]) - delta)
            acc_s[...] += _dot_nn(ds, k_ref[...])
            dc_s[...] += jnp.sum(ds, axis=-1, keepdims=True)

        @pl.when(j == i)
        def _():
            dq_ref[...] = acc_s[...] * (HEAD_DIM ** -0.5)
            dc_ref[...] = dc_s[...]

    qb = pl.BlockSpec((T, HEAD_DIM), lambda h, i, j: (i, h))
    col = pl.BlockSpec((None, T, 1), lambda h, i, j: (h, i, 0))
    return pl.pallas_call(
        body, name=name, grid=(H, nt, nt),
        in_specs=[qb, pl.BlockSpec((T, HEAD_DIM), lambda h, i, j: (jnp.minimum(j, i), H + h)),
                  pl.BlockSpec((T, HEAD_DIM), lambda h, i, j: (jnp.minimum(j, i), 2 * H + h)),
                  col, pl.BlockSpec((None, 1, T), lambda h, i, j: (h, 0, jnp.minimum(j, i))), qb, qb, col],
        out_specs=[qb, col], out_shape=[jax.ShapeDtypeStruct((S, H * HEAD_DIM), f32), jax.ShapeDtypeStruct((H, S, 1), f32)],
        scratch_shapes=[pltpu.VMEM((T, HEAD_DIM), f32), pltpu.VMEM((T, 1), f32)],
        compiler_params=_params("parallel", "parallel", "arbitrary", vmem=_attn_vmem(T)))(proj, proj, proj, c_col, c_row, o, do, lse)


def _fox_bwd_kv(proj, c_col, c_row, o, do, lse, H, name):
    S = proj.shape[0]
    T = _tile(S, ATTN_TILE)
    nt = S // T

    def body(q_ref, k_ref, v_ref, cc_ref, cr_ref, o_ref, do_ref, lse_ref, dk_ref, dv_ref, dc_ref, dk_s, dv_s, dc_s):
        j, i = pl.program_id(1), pl.program_id(2)

        @pl.when(i == 0)
        def _():
            dk_s[...] = jnp.zeros_like(dk_s)
            dv_s[...] = jnp.zeros_like(dv_s)
            dc_s[...] = jnp.zeros_like(dc_s)

        @pl.when(i >= j)
        def _():
            s = _fox_logits(q_ref[...], k_ref[...], cc_ref[...], cr_ref[...], i, j, T)
            p = jnp.exp(s - lse_ref[...])
            do_ = do_ref[...]
            delta = jnp.sum(o_ref[...] * do_, axis=-1, keepdims=True)
            ds = p * (_dot_nt(do_, v_ref[...]) - delta)
            dv_s[...] += _dot_tn(p, do_)
            dk_s[...] += _dot_tn(ds, q_ref[...])
            dc_s[...] -= jnp.sum(ds, axis=0, keepdims=True)

        @pl.when(i == nt - 1)
        def _():
            dk_ref[...] = dk_s[...] * (HEAD_DIM ** -0.5)
            dv_ref[...] = dv_s[...]
            dc_ref[...] = dc_s[...]

    qb = pl.BlockSpec((T, HEAD_DIM), lambda h, j, i: (jnp.maximum(i, j), h))
    col = pl.BlockSpec((None, T, 1), lambda h, j, i: (h, jnp.maximum(i, j), 0))
    kb = pl.BlockSpec((T, HEAD_DIM), lambda h, j, i: (j, h))
    sd = jax.ShapeDtypeStruct
    return pl.pallas_call(
        body, name=name, grid=(H, nt, nt),
        in_specs=[qb, pl.BlockSpec((T, HEAD_DIM), lambda h, j, i: (j, H + h)),
                  pl.BlockSpec((T, HEAD_DIM), lambda h, j, i: (j, 2 * H + h)),
                  col, pl.BlockSpec((None, 1, T), lambda h, j, i: (h, 0, j)), qb, qb, col],
        out_specs=[kb, kb, pl.BlockSpec((None, 1, T), lambda h, j, i: (h, 0, j))],
        out_shape=[sd((S, H * HEAD_DIM), f32), sd((S, H * HEAD_DIM), f32), sd((H, 1, S), f32)],
        scratch_shapes=[pltpu.VMEM((T, HEAD_DIM), f32), pltpu.VMEM((T, HEAD_DIM), f32), pltpu.VMEM((1, T), f32)],
        compiler_params=_params("parallel", "parallel", "arbitrary", vmem=_attn_vmem(T)))(proj, proj, proj, c_col, c_row, o, do, lse)


def _adamw(w, g, m, v, name):
    shape = w.shape
    cols = shape[-1]
    rows = w.size // cols
    ops = [t.reshape(rows, cols) for t in (w, g, m, v)]
    tr = rows
    if rows % 8 == 0:
        tr = 8
        while tr * 2 <= rows and rows % (tr * 2) == 0 and tr * 2 * cols * 4 <= (1 << 20):
            tr *= 2

    def body(w_ref, g_ref, m_ref, v_ref, d_ref, mo_ref, vo_ref):
        g_ = g_ref[...]
        m_ = ADAM_B1 * m_ref[...] + (1.0 - ADAM_B1) * g_
        v_ = ADAM_B2 * v_ref[...] + (1.0 - ADAM_B2) * (g_ * g_)
        m_hat = m_ / (1.0 - ADAM_B1 ** ADAM_STEP)
        v_hat = v_ / (1.0 - ADAM_B2 ** ADAM_STEP)
        d_ref[...] = -ADAM_LR * (m_hat / (jnp.sqrt(v_hat) + ADAM_EPS) + ADAM_WD * w_ref[...])
        mo_ref[...] = m_
        vo_ref[...] = v_

    blk = pl.BlockSpec((tr, cols), lambda i: (i, 0))
    outs = pl.pallas_call(body, name=name, grid=(rows // tr,), in_specs=[blk] * 4, out_specs=[blk] * 3,
                          out_shape=[jax.ShapeDtypeStruct((rows, cols), f32)] * 3, compiler_params=_params("parallel"))(*ops)
    return [o.reshape(shape) for o in outs]


def _row_tile(rows, row_bytes, budget=2 << 20):
    best = None
    for t in range(16, rows + 1, 16):
        if rows % t == 0 and t * row_bytes <= budget:
            best = t
    return best or rows


def _sum_leading(a, name):
    n, R, C = a.shape
    tr = _row_tile(R, n * C * 4)

    def body(a_ref, o_ref):
        acc = a_ref[0].astype(f32)
        for k in range(1, n):
            acc = acc + a_ref[k].astype(f32)
        o_ref[...] = acc

    return pl.pallas_call(body, name=name, grid=(R // tr,), in_specs=[pl.BlockSpec((n, tr, C), lambda i: (0, i, 0))],
                          out_specs=pl.BlockSpec((tr, C), lambda i: (i, 0)), out_shape=jax.ShapeDtypeStruct((R, C), f32),
                          compiler_params=_params("parallel"))(a)


def _add2(a, b, dtype, name):
    n, R, C = a.shape
    tr = _row_tile(R, C * 4)

    def body(a_ref, b_ref, o_ref):
        o_ref[...] = (a_ref[...] + b_ref[...]).astype(dtype)

    blk = pl.BlockSpec((None, tr, C), lambda k, i: (k, i, 0))
    return pl.pallas_call(body, name=name, grid=(n, R // tr), in_specs=[blk, blk], out_specs=blk,
                          out_shape=jax.ShapeDtypeStruct((n, R, C), dtype), compiler_params=_params("parallel", "parallel"))(a, b)


def _place():
    x, y, c = lax.axis_index("x"), lax.axis_index("y"), lax.axis_index("c")
    chips = [(1 - x, y), (x, 1 - y), (1 - x, 1 - y)]
    return x, y, c, chips


def _gather_weights(ws, name):
    _, R, C = ws.shape

    def body(ws_ref, out_ref, send_sems, recv_sems):
        x, y, c, chips = _place()
        sibling = (x, y, 1 - c)

        def half(px, py):
            return out_ref.at[2 * px + py, c]

        def copy(k, src, dst, to):
            return pltpu.make_async_remote_copy(src_ref=src, dst_ref=dst, send_sem=send_sems.at[k], recv_sem=recv_sems.at[k],
                                                device_id=to, device_id_type=MESH)

        first = [copy(j, ws_ref.at[c], half(x, y), (*chip, c)) for j, chip in enumerate(chips)]
        for cp in first:
            cp.start()
        passed = [copy(3 + j, half(*chip), half(*chip), sibling) for j, chip in enumerate(chips)]
        for j, chip in enumerate(chips):
            copy(j, ws_ref.at[c], half(*chip), (*chip, c)).wait_recv()
            passed[j].start()
        for j, chip in enumerate(chips):
            pltpu.make_async_remote_copy(src_ref=half(*chip), dst_ref=out_ref.at[2 * chip[0] + chip[1], 1 - c],
                                         send_sem=send_sems.at[3 + j], recv_sem=recv_sems.at[3 + j], device_id=sibling,
                                         device_id_type=MESH).wait_recv()
        for cp in first + passed:
            cp.wait_send()

    return pl.pallas_call(body, name=name, out_shape=jax.ShapeDtypeStruct((N_CHIPS, 2, R, C), ws.dtype), in_specs=[ANY],
                          out_specs=ANY, scratch_shapes=[pltpu.SemaphoreType.DMA((6,)), pltpu.SemaphoreType.DMA((6,))])(ws)


def _pair_exchange(g, name):
    n, _, R, C = g.shape

    def body(g_ref, out_ref, send_sems, recv_sems):
        x, y, c, _ = _place()
        cps = [pltpu.make_async_remote_copy(src_ref=g_ref.at[s, 1 - c], dst_ref=out_ref.at[s], send_sem=send_sems.at[s],
                                            recv_sem=recv_sems.at[s], device_id=(x, y, 1 - c), device_id_type=MESH)
               for s in range(n)]
        for cp in cps:
            cp.start()
        for cp in cps:
            cp.wait()

    return pl.pallas_call(body, name=name, out_shape=jax.ShapeDtypeStruct((n, R, C), g.dtype), in_specs=[ANY], out_specs=ANY,
                          scratch_shapes=[pltpu.SemaphoreType.DMA((n,)), pltpu.SemaphoreType.DMA((n,))])(g)


def _chip_copies(a_refs, out_refs, send_sems, recv_sems):
    x, y, c, chips = _place()
    me = 2 * x + y
    sends, lands = [], []
    for t, (a_ref, out_ref) in enumerate(zip(a_refs, out_refs)):
        for j, chip in enumerate(chips):
            them = 2 * chip[0] + chip[1]
            make = functools.partial(pltpu.make_async_remote_copy, send_sem=send_sems.at[3 * t + j], recv_sem=recv_sems.at[3 * t + j],
                                     device_id=(*chip, c), device_id_type=MESH)
            sends.append(make(src_ref=a_ref.at[them], dst_ref=out_ref.at[me]))
            lands.append(make(src_ref=a_ref.at[me], dst_ref=out_ref.at[them]))
    return sends, lands


def _chip_semaphores(n):
    return [pltpu.SemaphoreType.DMA((3 * n,)), pltpu.SemaphoreType.DMA((3 * n,))]


def _chip_exchange(arrays, name):
    n = len(arrays)

    def body(*refs):
        sends, lands = _chip_copies(refs[:n], refs[n:2 * n], refs[2 * n], refs[2 * n + 1])
        for cp in sends:
            cp.start()
        for cp in lands:
            cp.wait_recv()
        for cp in sends:
            cp.wait_send()

    return pl.pallas_call(body, name=name, out_shape=[jax.ShapeDtypeStruct(a.shape, a.dtype) for a in arrays], in_specs=[ANY] * n,
                          out_specs=[ANY] * n, scratch_shapes=_chip_semaphores(n))(*arrays)


def _pair_swap(r, name):
    R, C = r.shape

    def body(r_ref, out_ref, send_sem, recv_sem):
        x, y, c, _ = _place()
        cp = pltpu.make_async_remote_copy(src_ref=r_ref, dst_ref=out_ref, send_sem=send_sem, recv_sem=recv_sem,
                                          device_id=(x, y, 1 - c), device_id_type=MESH)
        cp.start()
        cp.wait()

    return pl.pallas_call(body, name=name, out_shape=jax.ShapeDtypeStruct((R, C), r.dtype), in_specs=[ANY], out_specs=ANY,
                          scratch_shapes=[pltpu.SemaphoreType.DMA(()), pltpu.SemaphoreType.DMA(())])(r)


def _all_gather8(v, name):
    R, C = v.shape

    def body(v_ref, out_ref, send_sems, recv_sems):
        x, y, c, _ = _place()
        me = 4 * x + 2 * y + c
        peers = [(x ^ (k >> 2), y ^ ((k >> 1) & 1), c ^ (k & 1)) for k in range(1, N_DEV)]
        sends = [pltpu.make_async_remote_copy(src_ref=v_ref, dst_ref=out_ref.at[me], send_sem=send_sems.at[k], recv_sem=recv_sems.at[k],
                                              device_id=peer, device_id_type=MESH) for k, peer in enumerate(peers)]
        for cp in sends:
            cp.start()
        for k, (px, py, pc) in enumerate(peers):
            pltpu.make_async_remote_copy(src_ref=v_ref, dst_ref=out_ref.at[4 * px + 2 * py + pc], send_sem=send_sems.at[k],
                                         recv_sem=recv_sems.at[k], device_id=(px, py, pc), device_id_type=MESH).wait_recv()
        for cp in sends:
            cp.wait_send()

    return pl.pallas_call(body, name=name, out_shape=jax.ShapeDtypeStruct((N_DEV, R, C), v.dtype), in_specs=[ANY], out_specs=ANY,
                          scratch_shapes=[pltpu.SemaphoreType.DMA((7,)), pltpu.SemaphoreType.DMA((7,))])(v)


def _put(buf, block, index):
    return lax.dynamic_update_slice(buf, block[None], (index,) + (0,) * block.ndim)


def _all_reduce8(v, device, name):
    n = v.shape[0]
    rows = -(-n // (LANES * SUM_ROWS)) * SUM_ROWS
    padded = jnp.pad(v, (0, rows * LANES - n)).reshape(rows, LANES)
    return _sum_leading(_put(_all_gather8(padded, name + "_gather"), padded, device), name + "_sum").reshape(-1)[:n]


def _gather4(w, chip, name):
    R, C = w.shape
    halves = w.reshape(2, R // 2, C)
    return _put(_gather_weights(halves, name), halves, chip).reshape(N_CHIPS, R, C)


def _pair_sums(g4, core, name):
    _, R, C = g4.shape
    g = g4.reshape(N_CHIPS, 2, R // 2, C)
    mine = lax.dynamic_index_in_dim(g, core, axis=1, keepdims=False)
    return _add2(mine, _pair_exchange(g, name + "_pair_exchange"), bf16, name + "_pair_sum")


def _finish_reduce(pair, landed, chip, core, name):
    _, R2, C = pair.shape
    own = lax.dynamic_index_in_dim(pair, chip, axis=0, keepdims=False)
    half = _sum_leading(_put(landed, own, chip), name + "_chip_sum")
    both = jnp.stack([half, _pair_swap(half, name + "_pair_swap")])
    return jnp.where(core == 0, both, both[::-1]).reshape(2 * R2, C)


def _segments(H, D):
    W = H * HEAD_DIM
    sizes = (3 * W, H, 2 * W, 3 * W, H, H, W, 3 * D)
    in_tail = (False, True, False, False, True, True, False, False)
    out, first, used = [], 0, [0, 0]
    for size, t in zip(sizes, in_tail):
        out.append((first, size, t, used[t]))
        first += size
        used[t] += size
    return out


def _main_tail_from_shards(g4, H, D):
    C = g4.shape[-1]
    parts = ([], [])
    for first, size, t, _ in _segments(H, D):
        for s in range(g4.shape[0]):
            a, b = max(first, s * C), min(first + size, (s + 1) * C)
            if a < b:
                parts[t].append(g4[s][..., a - s * C:b - s * C])
    parts[1].append(jnp.zeros(g4.shape[1:-1] + (LANES - 3 * H,), g4.dtype))
    return jnp.concatenate(parts[0], axis=-1), jnp.concatenate(parts[1], axis=-1)


def _shards_from_main_tail(main, tail, H, D):
    segs = _segments(H, D)
    C = sum(size for _, size, _, _ in segs) // N_CHIPS
    shards = []
    for s in range(N_CHIPS):
        pieces = []
        for first, size, t, there in segs:
            a, b = max(first, s * C), min(first + size, (s + 1) * C)
            if a < b:
                pieces.append((tail if t else main)[..., there + a - first:there + b - first])
        shards.append(jnp.concatenate(pieces, axis=-1))
    return jnp.stack(shards)


def _pad_cols(a, n):
    return jnp.pad(a, [(0, 0)] * (a.ndim - 1) + [(0, n - a.shape[-1])])


def kernel(x, w_in, b_in, sgu_ln_g, sgu_ln_b, sgu_w, sgu_b, gdn_conv_w, gdn_a_log, gdn_dt_bias, gdn_norm_g, w_proj_a, w_proj_b, w_proj_c, w_out, ln1_g, ln1_b, ffn_w_up, ffn_conv_w, ffn_conv_b, ffn_w_down, ln2_g, ln2_b, loss_target, m_w_in, m_b_in, m_sgu_ln_g, m_sgu_ln_b, m_sgu_w, m_sgu_b, m_gdn_conv_w, m_gdn_a_log, m_gdn_dt_bias, m_gdn_norm_g, m_w_proj_a, m_w_proj_b, m_w_proj_c, m_w_out, m_ln1_g, m_ln1_b, m_ffn_w_up, m_ffn_conv_w, m_ffn_conv_b, m_ffn_w_down, m_ln2_g, m_ln2_b, v_w_in, v_b_in, v_sgu_ln_g, v_sgu_ln_b, v_sgu_w, v_sgu_b, v_gdn_conv_w, v_gdn_a_log, v_gdn_dt_bias, v_gdn_norm_g, v_w_proj_a, v_w_proj_b, v_w_proj_c, v_w_out, v_ln1_g, v_ln1_b, v_ffn_w_up, v_ffn_conv_w, v_ffn_conv_b, v_ffn_w_down, v_ln2_g, v_ln2_b):
    P = dict(w_in=w_in, b_in=b_in, sgu_ln_g=sgu_ln_g, sgu_ln_b=sgu_ln_b, sgu_w=sgu_w, sgu_b=sgu_b, gdn_conv_w=gdn_conv_w,
             gdn_a_log=gdn_a_log, gdn_dt_bias=gdn_dt_bias, gdn_norm_g=gdn_norm_g, w_proj_a=w_proj_a, w_proj_b=w_proj_b,
             w_proj_c=w_proj_c, w_out=w_out, ln1_g=ln1_g, ln1_b=ln1_b, ffn_w_up=ffn_w_up, ffn_conv_w=ffn_conv_w,
             ffn_conv_b=ffn_conv_b, ffn_w_down=ffn_w_down, ln2_g=ln2_g, ln2_b=ln2_b)
    M1 = dict(w_in=m_w_in, b_in=m_b_in, sgu_ln_g=m_sgu_ln_g, sgu_ln_b=m_sgu_ln_b, sgu_w=m_sgu_w, sgu_b=m_sgu_b,
              gdn_conv_w=m_gdn_conv_w, gdn_a_log=m_gdn_a_log, gdn_dt_bias=m_gdn_dt_bias, gdn_norm_g=m_gdn_norm_g,
              w_proj_a=m_w_proj_a, w_proj_b=m_w_proj_b, w_proj_c=m_w_proj_c, w_out=m_w_out, ln1_g=m_ln1_g, ln1_b=m_ln1_b,
              ffn_w_up=m_ffn_w_up, ffn_conv_w=m_ffn_conv_w, ffn_conv_b=m_ffn_conv_b, ffn_w_down=m_ffn_w_down, ln2_g=m_ln2_g,
              ln2_b=m_ln2_b)
    M2 = dict(w_in=v_w_in, b_in=v_b_in, sgu_ln_g=v_sgu_ln_g, sgu_ln_b=v_sgu_ln_b, sgu_w=v_sgu_w, sgu_b=v_sgu_b,
              gdn_conv_w=v_gdn_conv_w, gdn_a_log=v_gdn_a_log, gdn_dt_bias=v_gdn_dt_bias, gdn_norm_g=v_gdn_norm_g,
              w_proj_a=v_w_proj_a, w_proj_b=v_w_proj_b, w_proj_c=v_w_proj_c, w_out=v_w_out, ln1_g=v_ln1_g, ln1_b=v_ln1_b,
              ffn_w_up=v_ffn_w_up, ffn_conv_w=v_ffn_conv_w, ffn_conv_b=v_ffn_conv_b, ffn_w_down=v_ffn_w_down, ln2_g=v_ln2_g,
              ln2_b=v_ln2_b)
    _, S, D = x.shape
    L = w_in.shape[0]
    N_IN = w_in.shape[2] * N_CHIPS
    H = (N_IN - 3 * D) // (9 * HEAD_DIM + 3)
    W = H * HEAD_DIM
    F = ffn_w_down.shape[1] * N_CHIPS
    Fp = -(-F // FF_ALIGN) * FF_ALIGN
    NM = 9 * W + 3 * D
    alpha = (2 * L) ** 0.25
    cx, cy, cc = lax.axis_index("x"), lax.axis_index("y"), lax.axis_index("c")
    chip = 2 * cx + cy

    full_w = []
    for l in range(L):
        g_in = _gather4(w_in[l].astype(bf16), chip, "gather_w_in")
        g_proj = _gather4(jnp.concatenate([w_proj_a[l], w_proj_b[l], w_proj_c[l]], axis=0).astype(bf16), chip, "gather_proj")
        g_rows = _gather4(jnp.concatenate([w_out[l], ffn_w_down[l]], axis=0).astype(bf16), chip, "gather_rows")
        g_up = _gather4(ffn_w_up[l].astype(bf16), chip, "gather_w_up")
        w_main, w_tail = _main_tail_from_shards(g_in, H, D)
        wa, wb, wc = [g_proj[:, k * W:(k + 1) * W].transpose(1, 0, 2).reshape(W, D) for k in range(3)]
        full_w.append(dict(w_main=w_main, w_tail=w_tail, wa=wa, wb=wb, wc=wc, wo=g_rows[:, :D // N_CHIPS].reshape(D, D),
                           wd=jnp.pad(g_rows[:, D // N_CHIPS:].reshape(F, D), ((0, Fp - F), (0, 0))),
                           wg=_pad_cols(jnp.concatenate([g_up[0], g_up[1]], axis=1), Fp),
                           wv=_pad_cols(jnp.concatenate([g_up[2], g_up[3]], axis=1), Fp)))
    gcw_cols, fcw_cols = gdn_conv_w.shape[2], ffn_conv_w.shape[2]
    only_south = (cc == 0).astype(f32)
    placed_g = lax.dynamic_update_slice(jnp.zeros((L, GDN_CONV, 3 * W), f32), gdn_conv_w * only_south, (0, 0, chip * gcw_cols))
    placed_f = lax.dynamic_update_slice(jnp.zeros((L, FFN_CONV, 2 * F), f32), ffn_conv_w * only_south, (0, 0, chip * fcw_cols))
    conv_all = _all_reduce8(jnp.concatenate([placed_g.reshape(-1), placed_f.reshape(-1)]), 2 * chip + cc, "conv_weights")
    gcw_full = conv_all[:L * GDN_CONV * 3 * W].reshape(L, GDN_CONV, 1, 3 * W)
    fcw_full = conv_all[L * GDN_CONV * 3 * W:].reshape(L, FFN_CONV, 1, 2 * F)

    saved = []
    h = x.reshape(S, D)
    for l in range(L):
        b_main, b_tail = _main_tail_from_shards(b_in[l][None, None, :], H, D)
        cwg, cwv = _pad_cols(fcw_full[l][..., :F], Fp), _pad_cols(fcw_full[l][..., F:], Fp)
        cbg, cbv = _pad_cols(ffn_conv_b[l][None, :F], Fp), _pad_cols(ffn_conv_b[l][None, F:], Fp)
        lw = dict(full_w[l], cwg=cwg, cwv=cwv, cbg=cbg, cbv=cbv, gcw=gcw_full[l],
                  sgu_ln_g=sgu_ln_g[l][None, :], sgu_ln_b=sgu_ln_b[l][None, :], sgu_w=sgu_w[l], sgu_b=sgu_b[l][:, :, None],
                  a_log=gdn_a_log[l].reshape(H, 1, 1), dt_bias=gdn_dt_bias[l].reshape(H, 1, 1),
                  norm_g=gdn_norm_g[l].reshape(1, 1, HEAD_DIM), ln1_g=ln1_g[l][None, :], ln1_b=ln1_b[l][None, :],
                  ln2_g=ln2_g[l][None, :], ln2_b=ln2_b[l][None, :])
        proj = _matmul(h, lw["w_main"], "nn", "proj_main", bias=b_main)
        tail = _matmul(h, lw["w_tail"], "nn", "proj_tail", bias=b_tail)
        csum = _fox_prep_fwd(tail, "fox_prep")
        c_col = csum[:, :H].T[:, :, None]
        c_row = csum[:, :H].T[:, None, :]
        y_a, lse = _fox_fwd(proj, c_col, c_row, H, "fox_fwd")
        y_b = _sgu_fwd(proj, lw["sgu_ln_g"], lw["sgu_ln_b"], lw["sgu_w"], lw["sgu_b"], 3 * W, 4 * W, "sgu_fwd")
        qkvc = _gdn_pre_fwd(proj, lw["gcw"], 5 * W, 2 * H, "gdn_pre")
        al = tail[:, H:2 * H].T[:, :, None]
        bl = tail[:, 2 * H:3 * H].T[:, :, None]
        y_c, states = _gdn_scan_fwd(qkvc, al, bl, proj, lw["a_log"], lw["dt_bias"], lw["norm_g"], 8 * W, "gdn_scan")
        pa = _matmul(y_a, lw["wa"], "nn", "branch_proj")
        pb = _matmul(y_b, lw["wb"], "nn", "branch_proj")
        pc = _matmul(y_c, lw["wc"], "nn", "branch_proj")
        merged = _merge_fwd(proj, pa, pb, pc, 9 * W, "merge")
        mix = _matmul(merged, lw["wo"], "nn", "out_proj")
        x1 = _ln_fwd(h, mix, lw["ln1_g"], lw["ln1_b"], alpha, "ln")
        upg = _matmul(x1, lw["wg"], "nn", "ffn_up")
        upv = _matmul(x1, lw["wv"], "nn", "ffn_up")
        act = _ffn_act_fwd(upg, upv, cwg, cwv, cbg, cbv, "ffn_act")
        ffn = _matmul(act, lw["wd"], "nn", "ffn_down")
        x2 = _ln_fwd(x1, ffn, lw["ln2_g"], lw["ln2_b"], alpha, "ln")
        saved.append(dict(lw=lw, h=h, proj=proj, tail=tail, c_col=c_col, c_row=c_row, y_a=y_a, lse=lse, y_b=y_b, qkvc=qkvc, al=al,
                          bl=bl, y_c=y_c, states=states, pa=pa, pb=pb, pc=pc, merged=merged, mix=mix, x1=x1, upg=upg, upv=upv,
                          act=act, ffn=ffn))
        h = x2

    loss_part, dh = _loss_head(h, loss_target.reshape(S, D), "loss_head")
    loss = lax.psum(loss_part[0, 0], ("x", "y", "c"))

    reduced = [None] * L
    small_grads = [None] * L
    pending = None
    for l in reversed(range(L)):
        s = saved[l]
        lw = s["lw"]
        d_x1r, d_ffn, d_ln2g, d_ln2b = _ln_bwd(s["x1"], s["ffn"], lw["ln2_g"], lw["ln2_b"], dh, alpha, "ln_bwd")
        d_act = _matmul(d_ffn, lw["wd"], "nt", "ffn_down_dx")
        d_wd = _matmul(s["act"], d_ffn, "tn", "ffn_down_dw")
        dupg, dupv, dcwg, dcwv, dcbg, dcbv = _ffn_act_bwd(s["upg"], s["upv"], lw["cwg"], lw["cwv"], lw["cbg"], lw["cbv"], d_act,
                                                          "ffn_act_bwd")
        d_x1 = _matmul(dupg, lw["wg"], "nt", "ffn_up_dx", add=d_x1r)
        d_x1 = _matmul(dupv, lw["wv"], "nt", "ffn_up_dx", add=d_x1)
        d_wg = _matmul(s["x1"], dupg, "tn", "ffn_up_dw")
        d_wv = _matmul(s["x1"], dupv, "tn", "ffn_up_dw")
        d_hr, d_mix, d_ln1g, d_ln1b = _ln_bwd(s["h"], s["mix"], lw["ln1_g"], lw["ln1_b"], d_x1, alpha, "ln_bwd")
        d_merged = _matmul(d_mix, lw["wo"], "nt", "out_proj_dx")
        d_wo = _matmul(s["merged"], d_mix, "tn", "out_proj_dw")
        dg0, dg1, dg2, d_pa, d_pb, d_pc = _merge_bwd(s["proj"], s["pa"], s["pb"], s["pc"], d_merged, 9 * W, "merge_bwd")
        d_ya = _matmul(d_pa, lw["wa"], "nt", "branch_proj_dx")
        d_yb = _matmul(d_pb, lw["wb"], "nt", "branch_proj_dx")
        d_yc = _matmul(d_pc, lw["wc"], "nt", "branch_proj_dx")
        d_wa = _matmul(s["y_a"], d_pa, "tn", "branch_proj_dw")
        d_wb = _matmul(s["y_b"], d_pb, "tn", "branch_proj_dw")
        d_wc = _matmul(s["y_c"], d_pc, "tn", "branch_proj_dw")
        carried = [pending[1][k] for k in SHARDED] if pending else []
        outs = _gdn_scan_bwd(s["qkvc"], s["al"], s["bl"], s["proj"], lw["a_log"], lw["dt_bias"], lw["norm_g"], s["states"], d_yc,
                             8 * W, "gdn_scan_bwd_carrying" if carried else "gdn_scan_bwd", carried=carried)
        dqkvc, dal, dbl, dgate, d_alog, d_dt, d_ng = outs[:7]
        if pending:
            reduced[pending[0]] = {k: _finish_reduce(pending[1][k], landed, chip, cc, "grad_" + k)
                                   for k, landed in zip(SHARDED, outs[7:])}
        d_gqkv, d_gcw = _gdn_pre_bwd(s["proj"], lw["gcw"], dqkvc, 5 * W, 2 * H, "gdn_pre_bwd")
        d_u, d_v, d_slg, d_slb, d_sw, d_sb = _sgu_bwd(s["proj"], lw["sgu_ln_g"], lw["sgu_ln_b"], lw["sgu_w"], lw["sgu_b"], d_yb,
                                                      3 * W, 4 * W, "sgu_bwd")
        d_q, d_cq = _fox_bwd_q(s["proj"], s["c_col"], s["c_row"], s["y_a"], d_ya, s["lse"], H, "fox_bwd_q")
        d_k, d_v_att, d_c = _fox_bwd_kv(s["proj"], s["c_col"], s["c_row"], s["y_a"], d_ya, s["lse"], H, "fox_bwd_kv")
        d_f = _fox_prep_bwd(s["tail"], _pad_cols(d_cq[:, :, 0].T, LANES), _pad_cols(d_c[:, 0, :].T, LANES), "fox_prep_bwd")
        d_main = jnp.concatenate([d_q, d_k, d_v_att, d_u, d_v, d_gqkv, dgate, dg0, dg1, dg2], axis=1)
        d_tail = _pad_cols(jnp.concatenate([d_f[:, :H], dal[:, :, 0].T, dbl[:, :, 0].T], axis=1), LANES)
        d_wmain = _matmul(s["h"], d_main, "tn", "proj_main_dw")
        d_wtail = _matmul(s["h"], d_tail, "tn", "proj_tail_dw")
        d_bmain = _colsum(d_main, "proj_main_db")
        d_btail = _colsum(d_tail, "proj_tail_db")
        dh = _matmul(d_main, lw["w_main"], "nt", "proj_main_dx", add=d_hr)
        dh = _matmul(d_tail, lw["w_tail"], "nt", "proj_tail_dx", add=dh)
        by_cols = lambda g: g.reshape(g.shape[0], N_CHIPS, g.shape[1] // N_CHIPS).transpose(1, 0, 2)
        big = dict(
            w_in=_shards_from_main_tail(d_wmain, d_wtail, H, D),
            proj=jnp.concatenate([by_cols(d_wa), by_cols(d_wb), by_cols(d_wc)], axis=1),
            rows=jnp.concatenate([d_wo.reshape(N_CHIPS, D // N_CHIPS, D), d_wd[:F].reshape(N_CHIPS, F // N_CHIPS, D)], axis=1),
            w_up=jnp.stack([d_wg[:, :F // 2], d_wg[:, F // 2:F], d_wv[:, :F // 2], d_wv[:, F // 2:F]]))
        pending = (l, {k: _pair_sums(big[k], cc, "grad_" + k) for k in SHARDED})
        small_grads[l] = dict(b_in=_shards_from_main_tail(d_bmain, d_btail, H, D).reshape(-1), sgu_ln_g=d_slg[0], sgu_ln_b=d_slb[0], sgu_w=d_sw,
                              sgu_b=d_sb[:, :, 0], gdn_conv_w=d_gcw[:, 0, :], gdn_a_log=d_alog[:, 0, 0], gdn_dt_bias=d_dt[:, 0, 0],
                              gdn_norm_g=d_ng[0, 0], ln1_g=d_ln1g[0], ln1_b=d_ln1b[0],
                              ffn_conv_w=jnp.concatenate([dcwg[:, 0, :F], dcwv[:, 0, :F]], axis=1),
                              ffn_conv_b=jnp.concatenate([dcbg[0, :F], dcbv[0, :F]]), ln2_g=d_ln2g[0], ln2_b=d_ln2b[0])
    grad_x = dh.reshape(1, S, D)

    landed = _chip_exchange([pending[1][k] for k in SHARDED], "grad_chip_exchange")
    reduced[pending[0]] = {k: _finish_reduce(pending[1][k], got, chip, cc, "grad_" + k) for k, got in zip(SHARDED, landed)}
    grads = {n: [None] * L for n in WEIGHTS}
    for l in range(L):
        grads["w_in"][l] = reduced[l]["w_in"]
        for k, n in enumerate(("w_proj_a", "w_proj_b", "w_proj_c")):
            grads[n][l] = reduced[l]["proj"][k * W:(k + 1) * W]
        grads["w_out"][l] = reduced[l]["rows"][:D // N_CHIPS]
        grads["ffn_w_down"][l] = reduced[l]["rows"][D // N_CHIPS:]
        grads["ffn_w_up"][l] = reduced[l]["w_up"]
    small_shapes = {n: small_grads[0][n].shape for n in SMALL}
    small_flat = jnp.concatenate([small_grads[l][n].reshape(-1) for l in range(L) for n in SMALL])
    small_sum = _all_reduce8(small_flat, 2 * chip + cc, "small_grads")
    off = 0
    for l in range(L):
        for n in SMALL:
            size = math.prod(small_shapes[n])
            g = small_sum[off:off + size].reshape(small_shapes[n])
            off += size
            if n == "gdn_conv_w":
                g = lax.dynamic_slice_in_dim(g, chip * gcw_cols, gcw_cols, axis=1)
            elif n == "ffn_conv_w":
                g = lax.dynamic_slice_in_dim(g, chip * fcw_cols, fcw_cols, axis=1)
            grads[n][l] = g
    grads = {n: jnp.stack(grads[n]) for n in WEIGHTS}

    deltas, new_m, new_v = {}, {}, {}
    for n in WEIGHTS:
        deltas[n], new_m[n], new_v[n] = _adamw(P[n], grads[n], M1[n], M2[n], "adamw_" + n)
    return (loss, grad_x, *[grads[n] for n in WEIGHTS], *[deltas[n] for n in WEIGHTS], *[new_m[n] for n in WEIGHTS],
            *[new_v[n] for n in WEIGHTS])
```

```python
import functools
import math
from typing import NamedTuple

import jax
import jax.numpy as jnp
from jax import lax
from jax.experimental import pallas as pl
from jax.experimental.pallas import tpu as pltpu

f32 = jnp.float32
bf16 = jnp.bfloat16
HIGHEST = lax.Precision.HIGHEST
MESH = pl.DeviceIdType.MESH

HEAD_DIM = 128
CHUNK = 64
SGU_SPAN = 128
GDN_CONV = 4
FFN_CONV = 3
N_CHIPS = 4
N_DEV = 8
LN_EPS = 1e-5
RMS_EPS = 1e-6
ADAM_LR = 0.001
ADAM_B1 = 0.9
ADAM_B2 = 0.999
ADAM_EPS = 1e-08
ADAM_WD = 0.01
ADAM_STEP = 10
NEG_BIG = -1e30
LANES = 128
FF_ALIGN = 512
SUM_ROWS = 256
ATTN_TILE = 1024
VMEM_MARGIN = 12 << 20
VMEM_MOST = 60 << 20

SHARDED = ("w_in", "proj", "rows", "w_up")
SMALL = ("b_in", "sgu_ln_g", "sgu_ln_b", "sgu_w", "sgu_b", "gdn_conv_w", "gdn_a_log", "gdn_dt_bias", "gdn_norm_g",
         "ln1_g", "ln1_b", "ffn_conv_w", "ffn_conv_b", "ln2_g", "ln2_b")
WEIGHTS = ("w_in", "b_in", "sgu_ln_g", "sgu_ln_b", "sgu_w", "sgu_b", "gdn_conv_w", "gdn_a_log", "gdn_dt_bias", "gdn_norm_g",
           "w_proj_a", "w_proj_b", "w_proj_c", "w_out", "ln1_g", "ln1_b", "ffn_w_up", "ffn_conv_w", "ffn_conv_b", "ffn_w_down",
           "ln2_g", "ln2_b")

ANY = pl.BlockSpec(memory_space=pl.ANY)


def _tile(dim, pref):
    t = pref
    while t > 128 and dim % t:
        t //= 2
    return min(t, dim) if dim % min(t, dim) == 0 else dim


def _params(*sem, vmem=None):
    if vmem is None:
        return pltpu.CompilerParams(dimension_semantics=sem)
    return pltpu.CompilerParams(dimension_semantics=sem, vmem_limit_bytes=min(vmem + VMEM_MARGIN, VMEM_MOST))


def _attn_vmem(T):
    return 8 * T * T * 4


_DIMS = {"nn": (((1,), (0,)), ((), ())), "nt": (((1,), (1,)), ((), ())), "tn": (((0,), (0,)), ((), ()))}


def _pick(dim, most):
    for unit in (256, LANES):
        for t in range(min(most, dim) // unit * unit, 0, -unit):
            if dim % t == 0:
                return t
    return dim


class _Cargo(NamedTuple):
    arrays: tuple
    landing: tuple
    copies: object
    n: int


def _cargo_specs(cargo):
    if cargo is None:
        return [], [], [], []
    sems = [pltpu.SemaphoreType.DMA((cargo.n,)), pltpu.SemaphoreType.DMA((cargo.n,))]
    return [ANY] * len(cargo.arrays), [ANY] * len(cargo.landing), list(cargo.landing), sems


def _split_refs(refs, n_in, n_out, cargo):
    na, nl, ns = (len(cargo.arrays), len(cargo.landing), 2) if cargo else (0, 0, 0)
    a, b, c = n_in + na, n_in + na + n_out, n_in + na + n_out + nl
    return refs[:n_in], refs[a:b], refs[c:len(refs) - ns], (refs[n_in:a], refs[b:c], refs[len(refs) - ns:])


def _cargo_start(cargo, first, hold):
    if cargo is not None:
        @pl.when(first)
        def _():
            for make in cargo.copies(hold[0], hold[1], *hold[2])[0]:
                make().start()


def _cargo_wait(cargo, last, hold):
    if cargo is not None:
        @pl.when(last)
        def _():
            sends, lands = cargo.copies(hold[0], hold[1], *hold[2])
            for make in lands:
                make().wait_recv()
            for make in sends:
                make().wait_send()


def _ship(cargo, name):
    in_specs, out_specs, out_shape, sems = _cargo_specs(cargo)

    def body(*refs):
        _, _, _, hold = _split_refs(refs, 0, 0, cargo)
        sends, lands = cargo.copies(hold[0], hold[1], *hold[2])
        started = [make() for make in sends]
        for cp in started:
            cp.start()
        for make in lands:
            make().wait_recv()
        for cp in started:
            cp.wait_send()

    return pl.pallas_call(body, name=name, out_shape=out_shape, in_specs=in_specs, out_specs=out_specs,
                          scratch_shapes=sems)(*cargo.arrays)


def _matmul(a, b, mode, name, bias=None, add=None, cargo=None):
    a, b = a.astype(bf16), b.astype(bf16)
    if mode == "nn":
        (M, K), (_, N) = a.shape, b.shape
    elif mode == "nt":
        (M, K), (N, _) = a.shape, b.shape
    else:
        (K, M), (_, N) = a.shape, b.shape
    has_bias, has_add = bias is not None, add is not None
    tm, tn, tk = _pick(M, 512 if has_add else 1024), _pick(N, 2816), _pick(K, 512)
    ni, nj, nk = M // tm, N // tn, K // tk
    vmem = 2 * (tm * tk * a.dtype.itemsize + tk * tn * b.dtype.itemsize + tm * tn * 4 * (2 if has_add else 1)) + tm * tn * 4

    def body(*refs):
        ins, (o_ref,), (acc_ref,), hold = _split_refs(refs, 2 + has_bias + has_add, 1, cargo)
        a_ref, b_ref = ins[0], ins[1]
        bias_ref = ins[2] if has_bias else None
        add_ref = ins[2 + has_bias] if has_add else None
        i, j, k = pl.program_id(0), pl.program_id(1), pl.program_id(2)
        _cargo_start(cargo, (i == 0) & (j == 0) & (k == 0), hold)

        @pl.when(k == 0)
        def _():
            acc_ref[...] = jnp.zeros_like(acc_ref)

        acc_ref[...] += lax.dot_general(a_ref[...], b_ref[...], _DIMS[mode], preferred_element_type=f32)

        @pl.when(k == nk - 1)
        def _():
            r = acc_ref[...]
            if has_bias:
                r = r + bias_ref[...]
            if has_add:
                r = r + add_ref[...]
            o_ref[...] = r

        _cargo_wait(cargo, (i == ni - 1) & (j == nj - 1) & (k == nk - 1), hold)

    if mode == "nn":
        specs = [pl.BlockSpec((tm, tk), lambda i, j, k: (i, k)), pl.BlockSpec((tk, tn), lambda i, j, k: (k, j))]
    elif mode == "nt":
        specs = [pl.BlockSpec((tm, tk), lambda i, j, k: (i, k)), pl.BlockSpec((tn, tk), lambda i, j, k: (j, k))]
    else:
        specs = [pl.BlockSpec((tk, tm), lambda i, j, k: (k, i)), pl.BlockSpec((tk, tn), lambda i, j, k: (k, j))]
    ops = [a, b]
    if has_bias:
        specs.append(pl.BlockSpec((1, tn), lambda i, j, k: (0, j)))
        ops.append(bias)
    if has_add:
        specs.append(pl.BlockSpec((tm, tn), lambda i, j, k: (i, j)))
        ops.append(add)
    c_in, c_out, c_shape, c_sems = _cargo_specs(cargo)
    outs = pl.pallas_call(
        body, name=name, grid=(ni, nj, nk), in_specs=specs + c_in,
        out_specs=[pl.BlockSpec((tm, tn), lambda i, j, k: (i, j))] + c_out,
        out_shape=[jax.ShapeDtypeStruct((M, N), f32)] + c_shape, scratch_shapes=[pltpu.VMEM((tm, tn), f32)] + c_sems,
        compiler_params=pltpu.CompilerParams(
            dimension_semantics=("arbitrary",) * 3 if cargo else ("parallel", "parallel", "arbitrary"),
            vmem_limit_bytes=min(vmem + VMEM_MARGIN, VMEM_MOST)))(*ops, *(cargo.arrays if cargo else ()))
    return outs if cargo else outs[0]


def _colsum(a, name):
    S, N = a.shape
    tn = _tile(N, 512)

    def body(a_ref, o_ref):
        o_ref[...] = jnp.sum(a_ref[...], axis=0, keepdims=True)

    return pl.pallas_call(body, name=name, grid=(N // tn,), in_specs=[pl.BlockSpec((S, tn), lambda j: (0, j))],
                          out_specs=pl.BlockSpec((1, tn), lambda j: (0, j)), out_shape=jax.ShapeDtypeStruct((1, N), f32),
                          compiler_params=_params("parallel"))(a)


def _ln_fn(alpha, x, y, g, b):
    z = alpha * x + y
    mu = jnp.mean(z, axis=-1, keepdims=True)
    zc = z - mu
    var = jnp.mean(zc * zc, axis=-1, keepdims=True)
    return zc * lax.rsqrt(var + LN_EPS) * g + b


def _ln_fwd(x, y, g, b, alpha, name):
    S, D = x.shape
    tr = _tile(S, 256)

    def body(x_ref, y_ref, g_ref, b_ref, o_ref):
        o_ref[...] = _ln_fn(alpha, x_ref[...], y_ref[...], g_ref[...], b_ref[...])

    row = pl.BlockSpec((tr, D), lambda i: (i, 0))
    par = pl.BlockSpec((1, D), lambda i: (0, 0))
    return pl.pallas_call(body, name=name, grid=(S // tr,), in_specs=[row, row, par, par], out_specs=row,
                          out_shape=jax.ShapeDtypeStruct((S, D), f32), compiler_params=_params("parallel"))(x, y, g, b)


def _ln_bwd(x, y, g, b, dout, alpha, name):
    S, D = x.shape
    tr = _tile(S, 256)

    def body(x_ref, y_ref, g_ref, b_ref, d_ref, dx_ref, dy_ref, dg_ref, db_ref):
        _, vjp = jax.vjp(functools.partial(_ln_fn, alpha), x_ref[...], y_ref[...], g_ref[...], b_ref[...])
        dx, dy, dg, db = vjp(d_ref[...])
        dx_ref[...] = dx
        dy_ref[...] = dy

        @pl.when(pl.program_id(0) == 0)
        def _():
            dg_ref[...] = jnp.zeros_like(dg_ref)
            db_ref[...] = jnp.zeros_like(db_ref)

        dg_ref[...] += dg
        db_ref[...] += db

    row = pl.BlockSpec((tr, D), lambda i: (i, 0))
    par = pl.BlockSpec((1, D), lambda i: (0, 0))
    sd = jax.ShapeDtypeStruct
    return pl.pallas_call(body, name=name, grid=(S // tr,), in_specs=[row, row, par, par, row],
                          out_specs=[row, row, par, par],
                          out_shape=[sd((S, D), f32), sd((S, D), f32), sd((1, D), f32), sd((1, D), f32)],
                          compiler_params=_params("arbitrary"))(x, y, g, b, dout)


def _loss_head(y, t, name):
    S, D = y.shape
    tr = _tile(S, 256)

    def body(y_ref, t_ref, l_ref, d_ref):
        e = y_ref[...] - t_ref[...]
        d_ref[...] = e / D

        @pl.when(pl.program_id(0) == 0)
        def _():
            l_ref[...] = jnp.zeros_like(l_ref)

        l_ref[...] += 0.5 * jnp.sum(jnp.mean(e * e, axis=-1, keepdims=True))

    row = pl.BlockSpec((tr, D), lambda i: (i, 0))
    return pl.pallas_call(body, name=name, grid=(S // tr,), in_specs=[row, row],
                          out_specs=[pl.BlockSpec((8, LANES), lambda i: (0, 0)), row],
                          out_shape=[jax.ShapeDtypeStruct((8, LANES), f32), jax.ShapeDtypeStruct((S, D), f32)],
                          compiler_params=_params("arbitrary"))(y, t)


def _merge_fn(g0, g1, g2, pa, pb, pc):
    return jax.nn.sigmoid(g0) * pa + jax.nn.sigmoid(g1) * pb + jax.nn.sigmoid(g2) * pc


def _merge_specs(S, D, gate_off):
    tr = _tile(S, 512)
    tc = _tile(math.gcd(gate_off, D), 512)
    gates = [pl.BlockSpec((tr, tc), functools.partial(lambda k, i, j: (i, (gate_off + k * D) // tc + j), k)) for k in range(3)]
    tile = pl.BlockSpec((tr, tc), lambda i, j: (i, j))
    return tr, tc, gates, tile


def _merge_fwd(proj, pa, pb, pc, gate_off, name):
    S, D = pa.shape
    tr, tc, gates, tile = _merge_specs(S, D, gate_off)

    def body(g0, g1, g2, a, b, c, o_ref):
        o_ref[...] = _merge_fn(g0[...], g1[...], g2[...], a[...], b[...], c[...])

    return pl.pallas_call(body, name=name, grid=(S // tr, D // tc), in_specs=gates + [tile] * 3, out_specs=tile,
                          out_shape=jax.ShapeDtypeStruct((S, D), f32),
                          compiler_params=_params("parallel", "parallel"))(proj, proj, proj, pa, pb, pc)


def _merge_bwd(proj, pa, pb, pc, dm, gate_off, name):
    S, D = pa.shape
    tr, tc, gates, tile = _merge_specs(S, D, gate_off)

    def body(g0, g1, g2, a, b, c, d, dg0, dg1, dg2, da, db, dc):
        _, vjp = jax.vjp(_merge_fn, g0[...], g1[...], g2[...], a[...], b[...], c[...])
        for ref, val in zip((dg0, dg1, dg2, da, db, dc), vjp(d[...])):
            ref[...] = val

    sd = jax.ShapeDtypeStruct
    return pl.pallas_call(body, name=name, grid=(S // tr, D // tc), in_specs=gates + [tile] * 4,
                          out_specs=[tile] * 6, out_shape=[sd((S, D), f32)] * 6,
                          compiler_params=_params("parallel", "parallel"))(proj, proj, proj, pa, pb, pc, dm)


def _sgu_fn(nb, u, v, ln_g, ln_b, w_s, b_s):
    mu = jnp.mean(v, axis=-1, keepdims=True)
    vc = v - mu
    var = jnp.mean(vc * vc, axis=-1, keepdims=True)
    vn = vc * lax.rsqrt(var + LN_EPS) * ln_g + ln_b
    r = lax.broadcasted_iota(jnp.int32, (SGU_SPAN, SGU_SPAN), 0) // CHUNK
    c = lax.broadcasted_iota(jnp.int32, (SGU_SPAN, SGU_SPAN), 1) // CHUNK
    wm = jnp.where(r >= c, w_s, 0.0)
    vn3 = vn.reshape(nb, SGU_SPAN, HEAD_DIM)
    mixed = lax.dot_general(jnp.broadcast_to(wm, (nb, SGU_SPAN, SGU_SPAN)), vn3, (((2,), (1,)), ((0,), (0,))),
                            preferred_element_type=f32)
    mixed = mixed + b_s
    return u * mixed.reshape(nb * SGU_SPAN, HEAD_DIM)


def _sgu_specs(S, G, u_off, v_off):
    nb = max(1, min(8, S // SGU_SPAN))
    rows = nb * SGU_SPAN
    ub = pl.BlockSpec((rows, HEAD_DIM), lambda g, n: (n, u_off // HEAD_DIM + g))
    vb = pl.BlockSpec((rows, HEAD_DIM), lambda g, n: (n, v_off // HEAD_DIM + g))
    lnb = pl.BlockSpec((1, HEAD_DIM), lambda g, n: (0, g))
    wb = pl.BlockSpec((None, SGU_SPAN, SGU_SPAN), lambda g, n: (g, 0, 0))
    bb = pl.BlockSpec((None, SGU_SPAN, 1), lambda g, n: (g, 0, 0))
    return nb, rows, ub, vb, lnb, wb, bb


def _sgu_fwd(proj, ln_g, ln_b, w_s, b_s, u_off, v_off, name):
    S = proj.shape[0]
    G = w_s.shape[0]
    nb, rows, ub, vb, lnb, wb, bb = _sgu_specs(S, G, u_off, v_off)

    def body(u, v, lg, lb, w, b, o_ref):
        o_ref[...] = _sgu_fn(nb, u[...], v[...], lg[...], lb[...], w[...], b[...])

    return pl.pallas_call(body, name=name, grid=(G, S // rows), in_specs=[ub, vb, lnb, lnb, wb, bb],
                          out_specs=pl.BlockSpec((rows, HEAD_DIM), lambda g, n: (n, g)),
                          out_shape=jax.ShapeDtypeStruct((S, G * HEAD_DIM), f32),
                          compiler_params=_params("parallel", "parallel"))(proj, proj, ln_g, ln_b, w_s, b_s)


def _sgu_bwd(proj, ln_g, ln_b, w_s, b_s, dy, u_off, v_off, name):
    S = proj.shape[0]
    G = w_s.shape[0]
    nb, rows, ub, vb, lnb, wb, bb = _sgu_specs(S, G, u_off, v_off)

    def body(u, v, lg, lb, w, b, d, du, dv, dlg, dlb, dw, db):
        _, vjp = jax.vjp(functools.partial(_sgu_fn, nb), u[...], v[...], lg[...], lb[...], w[...], b[...])
        gu, gv, glg, glb, gw, gb = vjp(d[...])
        du[...] = gu
        dv[...] = gv

        @pl.when(pl.program_id(1) == 0)
        def _():
            for ref in (dlg, dlb, dw, db):
                ref[...] = jnp.zeros_like(ref)

        dlg[...] += glg
        dlb[...] += glb
        dw[...] += gw
        db[...] += gb

    tile = pl.BlockSpec((rows, HEAD_DIM), lambda g, n: (n, g))
    sd = jax.ShapeDtypeStruct
    W = G * HEAD_DIM
    return pl.pallas_call(body, name=name, grid=(G, S // rows), in_specs=[ub, vb, lnb, lnb, wb, bb, tile],
                          out_specs=[tile, tile, lnb, lnb, wb, bb],
                          out_shape=[sd((S, W), f32), sd((S, W), f32), sd((1, W), f32), sd((1, W), f32),
                                     sd((G, SGU_SPAN, SGU_SPAN), f32), sd((G, SGU_SPAN, 1), f32)],
                          compiler_params=_params("parallel", "arbitrary"))(proj, proj, ln_g, ln_b, w_s, b_s, dy)


def _shift_down(x, k):
    if k == 0:
        return x
    rows = lax.broadcasted_iota(jnp.int32, x.shape, 0)
    return jnp.where(rows >= k, pltpu.roll(x, k, 0), 0.0)


def _shift_up(x, k):
    if k == 0:
        return x
    n = x.shape[0]
    rows = lax.broadcasted_iota(jnp.int32, x.shape, 0)
    return jnp.where(rows < n - k, pltpu.roll(x, n - k, 0), 0.0)


def _conv(x, w_ref, width):
    out = w_ref[width - 1] * x
    for j in range(width - 1):
        out = out + w_ref[j] * _shift_down(x, width - 1 - j)
    return out


def _conv_bwd(x, dz, w_ref, dw_ref, width):
    dx = w_ref[width - 1] * dz
    dw_ref[width - 1] = jnp.sum(dz * x, axis=0, keepdims=True)
    for j in range(width - 1):
        k = width - 1 - j
        dx = dx + w_ref[j] * _shift_up(dz, k)
        dw_ref[j] = jnp.sum(dz * _shift_down(x, k), axis=0, keepdims=True)
    return dx


def _silu(z):
    return z * jax.nn.sigmoid(z)


def _dsilu(z):
    s = jax.nn.sigmoid(z)
    return s * (1.0 + z * (1.0 - s))


def _ffn_act_fwd(upg, upv, cwg, cwv, cbg, cbv, name):
    S, Fp = upg.shape

    def body(g_ref, v_ref, wg, wv, bg, bv, o_ref):
        hg = _conv(g_ref[...], wg, FFN_CONV) + bg[...]
        hv = _conv(v_ref[...], wv, FFN_CONV) + bv[...]
        o_ref[...] = _silu(hg) * hv

    col = pl.BlockSpec((S, LANES), lambda j: (0, j))
    wsp = pl.BlockSpec((FFN_CONV, 1, LANES), lambda j: (0, 0, j))
    bsp = pl.BlockSpec((1, LANES), lambda j: (0, j))
    return pl.pallas_call(body, name=name, grid=(Fp // LANES,), in_specs=[col, col, wsp, wsp, bsp, bsp], out_specs=col,
                          out_shape=jax.ShapeDtypeStruct((S, Fp), f32),
                          compiler_params=_params("parallel"))(upg, upv, cwg, cwv, cbg, cbv)


def _ffn_act_bwd(upg, upv, cwg, cwv, cbg, cbv, dact, name):
    S, Fp = upg.shape

    def body(g_ref, v_ref, wg, wv, bg, bv, d_ref, dg_ref, dv_ref, dwg, dwv, dbg, dbv):
        xg, xv, d = g_ref[...], v_ref[...], d_ref[...]
        hg = _conv(xg, wg, FFN_CONV) + bg[...]
        hv = _conv(xv, wv, FFN_CONV) + bv[...]
        dhg = d * hv * _dsilu(hg)
        dhv = d * _silu(hg)
        dbg[...] = jnp.sum(dhg, axis=0, keepdims=True)
        dbv[...] = jnp.sum(dhv, axis=0, keepdims=True)
        dg_ref[...] = _conv_bwd(xg, dhg, wg, dwg, FFN_CONV)
        dv_ref[...] = _conv_bwd(xv, dhv, wv, dwv, FFN_CONV)

    col = pl.BlockSpec((S, LANES), lambda j: (0, j))
    wsp = pl.BlockSpec((FFN_CONV, 1, LANES), lambda j: (0, 0, j))
    bsp = pl.BlockSpec((1, LANES), lambda j: (0, j))
    sd = jax.ShapeDtypeStruct
    return pl.pallas_call(body, name=name, grid=(Fp // LANES,), in_specs=[col, col, wsp, wsp, bsp, bsp, col],
                          out_specs=[col, col, wsp, wsp, bsp, bsp],
                          out_shape=[sd((S, Fp), f32), sd((S, Fp), f32), sd((FFN_CONV, 1, Fp), f32), sd((FFN_CONV, 1, Fp), f32),
                                     sd((1, Fp), f32), sd((1, Fp), f32)],
                          compiler_params=_params("parallel"))(upg, upv, cwg, cwv, cbg, cbv, dact)


def _gdn_pre_fwd(proj, cw, x_off, n_norm, name):
    S = proj.shape[0]
    C = cw.shape[2]

    def body(x_ref, w_ref, o_ref):
        s = _silu(_conv(x_ref[...], w_ref, GDN_CONV))
        r = lax.rsqrt(jnp.sum(s * s, axis=-1, keepdims=True) + RMS_EPS)
        o_ref[...] = jnp.where(pl.program_id(0) < n_norm, s * r, s)

    xs = pl.BlockSpec((S, LANES), lambda j: (0, x_off // LANES + j))
    col = pl.BlockSpec((S, LANES), lambda j: (0, j))
    wsp = pl.BlockSpec((GDN_CONV, 1, LANES), lambda j: (0, 0, j))
    return pl.pallas_call(body, name=name, grid=(C // LANES,), in_specs=[xs, wsp], out_specs=col,
                          out_shape=jax.ShapeDtypeStruct((S, C), f32), compiler_params=_params("parallel"))(proj, cw)


def _gdn_pre_bwd(proj, cw, dout, x_off, n_norm, name):
    S = proj.shape[0]
    C = cw.shape[2]

    def body(x_ref, w_ref, d_ref, dx_ref, dw_ref):
        x, d = x_ref[...], d_ref[...]
        z = _conv(x, w_ref, GDN_CONV)
        s = _silu(z)
        r = lax.rsqrt(jnp.sum(s * s, axis=-1, keepdims=True) + RMS_EPS)
        ds_norm = d * r - s * (r * r * r) * jnp.sum(d * s, axis=-1, keepdims=True)
        ds = jnp.where(pl.program_id(0) < n_norm, ds_norm, d)
        dz = ds * _dsilu(z)
        dx_ref[...] = _conv_bwd(x, dz, w_ref, dw_ref, GDN_CONV)

    xs = pl.BlockSpec((S, LANES), lambda j: (0, x_off // LANES + j))
    col = pl.BlockSpec((S, LANES), lambda j: (0, j))
    wsp = pl.BlockSpec((GDN_CONV, 1, LANES), lambda j: (0, 0, j))
    return pl.pallas_call(body, name=name, grid=(C // LANES,), in_specs=[xs, wsp, col], out_specs=[col, wsp],
                          out_shape=[jax.ShapeDtypeStruct((S, C), f32), jax.ShapeDtypeStruct((GDN_CONV, 1, C), f32)],
                          compiler_params=_params("parallel"))(proj, cw, dout)


def _bmm(a, b, prec=None):
    return lax.dot_general(a, b, (((2,), (1,)), ((0,), (0,))), precision=prec, preferred_element_type=f32)


def _bmm_nt(a, b, prec=None):
    return lax.dot_general(a, b, (((2,), (2,)), ((0,), (0,))), precision=prec, preferred_element_type=f32)


def _bmm_tn(a, b, prec=None):
    return lax.dot_general(a, b, (((1,), (1,)), ((0,), (0,))), precision=prec, preferred_element_type=f32)


def _softplus(x):
    return jnp.maximum(x, 0.0) + jnp.log1p(jnp.exp(-jnp.abs(x)))


@jax.custom_vjp
def _unit_lower_inverse(a):
    H, C, _ = a.shape
    r = lax.broadcasted_iota(jnp.int32, (H, C, C), 1)
    c = lax.broadcasted_iota(jnp.int32, (H, C, C), 2)
    p = -a
    inv = (r == c).astype(f32) + p
    for _ in range(int(math.log2(C)) - 1):
        p = _bmm(p, p, HIGHEST)
        inv = inv + _bmm(inv, p, HIGHEST)
    return inv


def _unit_lower_inverse_fwd(a):
    inv = _unit_lower_inverse(a)
    return inv, inv


def _unit_lower_inverse_bwd(inv, d_inv):
    return (-_bmm_nt(_bmm_tn(inv, d_inv, HIGHEST), inv, HIGHEST),)


_unit_lower_inverse.defvjp(_unit_lower_inverse_fwd, _unit_lower_inverse_bwd)


def _gdn_chunk(q, k, v, al, bl, gate, a_log, dt_bias, norm_g, state):
    H, C, Dh = q.shape
    r = lax.broadcasted_iota(jnp.int32, (H, C, C), 1)
    c = lax.broadcasted_iota(jnp.int32, (H, C, C), 2)
    tril = r >= c
    strict = r > c
    lower = tril.astype(f32)
    upper = (r <= c).astype(f32)
    ones = jnp.ones((H, C, C), f32)
    g = -jnp.exp(a_log) * _softplus(al + dt_bias)
    beta = jax.nn.sigmoid(bl)
    g_lanes = jnp.broadcast_to(g, (H, C, Dh))
    g_sq = jnp.broadcast_to(g, (H, C, C))
    gc = _bmm(lower, g_lanes, HIGHEST)
    gc_i = _bmm(lower, g_sq, HIGHEST)
    gc_j = _bmm(ones, g_sq * upper, HIGHEST)
    decay = jnp.where(tril, jnp.exp(jnp.where(tril, gc_i - gc_j, 0.0)), 0.0)
    qs = q * (Dh ** -0.5)
    kb = k * beta
    a_kk = jnp.where(strict, _bmm_nt(kb, k) * decay, 0.0)
    rhs_u = v * beta
    rhs_w = kb * jnp.exp(gc)
    inv = _unit_lower_inverse(a_kk)
    u = _bmm(inv, rhs_u, HIGHEST)
    w = _bmm(inv, rhs_w, HIGHEST)
    qk = jnp.where(tril, _bmm_nt(qs, k) * decay, 0.0)
    g_last = jnp.sum(g, axis=1, keepdims=True)
    k_dec = k * jnp.exp(g_last - gc)
    q_dec = qs * jnp.exp(gc)
    v_new = u - _bmm(w, state)
    o = _bmm(q_dec, state) + _bmm(qk, v_new)
    new_state = state * jnp.exp(g_last) + _bmm_tn(k_dec, v_new)
    y = o * lax.rsqrt(jnp.mean(o * o, axis=-1, keepdims=True) + RMS_EPS) * norm_g * _silu(gate)
    return y, new_state


def _heads(ref, off, H):
    return jnp.stack([ref[:, off + h * HEAD_DIM: off + (h + 1) * HEAD_DIM] for h in range(H)])


def _gdn_scan_fwd(qkvc, al, bl, proj, a_log, dt_bias, norm_g, gate_off, name, cargo=None):
    S = qkvc.shape[0]
    H = al.shape[0]
    W = H * HEAD_DIM
    n = S // CHUNK

    def body(*refs):
        (x_ref, al_ref, bl_ref, gate_ref, alog_ref, dt_ref, ng_ref), (y_ref, st_ref), (state,), hold = _split_refs(refs, 7, 2, cargo)
        _cargo_start(cargo, pl.program_id(0) == 0, hold)

        @pl.when(pl.program_id(0) == 0)
        def _():
            state[...] = jnp.zeros_like(state)

        st_ref[...] = state[...]
        y, new = _gdn_chunk(_heads(x_ref, 0, H), _heads(x_ref, W, H), _heads(x_ref, 2 * W, H), al_ref[...], bl_ref[...],
                            _heads(gate_ref, 0, H), alog_ref[...], dt_ref[...], ng_ref[...], state[...])
        state[...] = new
        for h in range(H):
            y_ref[:, h * HEAD_DIM:(h + 1) * HEAD_DIM] = y[h]
        _cargo_wait(cargo, pl.program_id(0) == n - 1, hold)

    sd = jax.ShapeDtypeStruct
    col = pl.BlockSpec((H, CHUNK, 1), lambda i: (0, i, 0))
    par = pl.BlockSpec((H, 1, 1), lambda i: (0, 0, 0))
    c_in, c_out, c_shape, c_sems = _cargo_specs(cargo)
    return pl.pallas_call(
        body, name=name, grid=(n,),
        in_specs=[pl.BlockSpec((CHUNK, 3 * W), lambda i: (i, 0)), col, col,
                  pl.BlockSpec((CHUNK, W), lambda i: (i, gate_off // W)), par, par,
                  pl.BlockSpec((1, 1, HEAD_DIM), lambda i: (0, 0, 0))] + c_in,
        out_specs=[pl.BlockSpec((CHUNK, W), lambda i: (i, 0)),
                   pl.BlockSpec((None, H, HEAD_DIM, HEAD_DIM), lambda i: (i, 0, 0, 0))] + c_out,
        out_shape=[sd((S, W), f32), sd((n, H, HEAD_DIM, HEAD_DIM), f32)] + c_shape,
        scratch_shapes=[pltpu.VMEM((H, HEAD_DIM, HEAD_DIM), f32)] + c_sems,
        compiler_params=_params("arbitrary"))(qkvc, al, bl, proj, a_log, dt_bias, norm_g, *(cargo.arrays if cargo else ()))


def _gdn_scan_bwd(qkvc, al, bl, proj, a_log, dt_bias, norm_g, states, dy, gate_off, name, cargo=None):
    S = qkvc.shape[0]
    H = al.shape[0]
    W = H * HEAD_DIM
    n = S // CHUNK

    def body(*refs):
        ins, outs, (dstate,), hold = _split_refs(refs, 9, 7, cargo)
        x_ref, al_ref, bl_ref, gate_ref, alog_ref, dt_ref, ng_ref, st_ref, dy_ref = ins
        dx_ref, dal_ref, dbl_ref, dgate_ref, dalog_ref, ddt_ref, dng_ref = outs
        _cargo_start(cargo, pl.program_id(0) == 0, hold)

        @pl.when(pl.program_id(0) == 0)
        def _():
            dstate[...] = jnp.zeros_like(dstate)
            for ref in (dalog_ref, ddt_ref, dng_ref):
                ref[...] = jnp.zeros_like(ref)

        _, vjp = jax.vjp(_gdn_chunk, _heads(x_ref, 0, H), _heads(x_ref, W, H), _heads(x_ref, 2 * W, H), al_ref[...],
                         bl_ref[...], _heads(gate_ref, 0, H), alog_ref[...], dt_ref[...], ng_ref[...], st_ref[...])
        dq, dk, dv, dal, dbl, dgate, dalog, ddt, dng, dst = vjp((_heads(dy_ref, 0, H), dstate[...]))
        dstate[...] = dst
        for h in range(H):
            lo, hi = h * HEAD_DIM, (h + 1) * HEAD_DIM
            dx_ref[:, lo:hi] = dq[h]
            dx_ref[:, W + lo:W + hi] = dk[h]
            dx_ref[:, 2 * W + lo:2 * W + hi] = dv[h]
            dgate_ref[:, lo:hi] = dgate[h]
        dal_ref[...] = dal
        dbl_ref[...] = dbl
        dalog_ref[...] += dalog
        ddt_ref[...] += ddt
        dng_ref[...] += dng

        _cargo_wait(cargo, pl.program_id(0) == n - 1, hold)

    sd = jax.ShapeDtypeStruct
    c_in, c_out, c_shape, c_sems = _cargo_specs(cargo)
    rev = lambda i: n - 1 - i
    col = pl.BlockSpec((H, CHUNK, 1), lambda i: (0, rev(i), 0))
    par = pl.BlockSpec((H, 1, 1), lambda i: (0, 0, 0))
    ng = pl.BlockSpec((1, 1, HEAD_DIM), lambda i: (0, 0, 0))
    xs = pl.BlockSpec((CHUNK, 3 * W), lambda i: (rev(i), 0))
    ws = pl.BlockSpec((CHUNK, W), lambda i: (rev(i), 0))
    return pl.pallas_call(
        body, name=name, grid=(n,),
        in_specs=[xs, col, col, pl.BlockSpec((CHUNK, W), lambda i: (rev(i), gate_off // W)), par, par, ng,
                  pl.BlockSpec((None, H, HEAD_DIM, HEAD_DIM), lambda i: (rev(i), 0, 0, 0)), ws] + c_in,
        out_specs=[xs, col, col, ws, par, par, ng] + c_out,
        out_shape=[sd((S, 3 * W), f32), sd((H, S, 1), f32), sd((H, S, 1), f32), sd((S, W), f32), sd((H, 1, 1), f32),
                   sd((H, 1, 1), f32), sd((1, 1, HEAD_DIM), f32)] + c_shape,
        scratch_shapes=[pltpu.VMEM((H, HEAD_DIM, HEAD_DIM), f32)] + c_sems,
        compiler_params=_params("arbitrary"))(qkvc, al, bl, proj, a_log, dt_bias, norm_g, states, dy,
                                              *(cargo.arrays if cargo else ()))


def _tri(n, upper):
    r = lax.broadcasted_iota(jnp.int32, (n, n), 0)
    c = lax.broadcasted_iota(jnp.int32, (n, n), 1)
    return (r <= c if upper else r >= c).astype(f32)


def _fox_prep_fwd(tail, name):
    S = tail.shape[0]
    tb = _tile(S, 512)

    def body(x_ref, o_ref, carry):
        @pl.when(pl.program_id(0) == 0)
        def _():
            carry[...] = jnp.zeros_like(carry)

        x = x_ref[...]
        lf = jnp.minimum(x, 0.0) - jnp.log1p(jnp.exp(-jnp.abs(x)))
        o_ref[...] = jnp.dot(_tri(tb, False), lf, precision=HIGHEST, preferred_element_type=f32) + carry[...]
        carry[...] += jnp.sum(lf, axis=0, keepdims=True)

    blk = pl.BlockSpec((tb, LANES), lambda i: (i, 0))
    return pl.pallas_call(body, name=name, grid=(S // tb,), in_specs=[blk], out_specs=blk,
                          out_shape=jax.ShapeDtypeStruct((S, LANES), f32), scratch_shapes=[pltpu.VMEM((1, LANES), f32)],
                          compiler_params=_params("arbitrary"))(tail)


def _fox_prep_bwd(tail, dc_q, dc_k, name):
    S = tail.shape[0]
    tb = _tile(S, 512)
    nb = S // tb

    def body(x_ref, dq_ref, d_ref, o_ref, carry):
        @pl.when(pl.program_id(0) == 0)
        def _():
            carry[...] = jnp.zeros_like(carry)

        d = d_ref[...] + dq_ref[...]
        dlf = jnp.dot(_tri(tb, True), d, precision=HIGHEST, preferred_element_type=f32) + carry[...]
        carry[...] += jnp.sum(d, axis=0, keepdims=True)
        o_ref[...] = dlf * jax.nn.sigmoid(-x_ref[...])

    blk = pl.BlockSpec((tb, LANES), lambda i: (nb - 1 - i, 0))
    return pl.pallas_call(body, name=name, grid=(nb,), in_specs=[blk, blk, blk], out_specs=blk,
                          out_shape=jax.ShapeDtypeStruct((S, LANES), f32), scratch_shapes=[pltpu.VMEM((1, LANES), f32)],
                          compiler_params=_params("arbitrary"))(tail, dc_q, dc_k)


def _dot_nt(a, b):
    return lax.dot_general(a.astype(bf16), b.astype(bf16), _DIMS["nt"], preferred_element_type=f32)


def _dot_tn(a, b):
    return lax.dot_general(a.astype(bf16), b.astype(bf16), _DIMS["tn"], preferred_element_type=f32)


def _dot_nn(a, b):
    return lax.dot_general(a.astype(bf16), b.astype(bf16), _DIMS["nn"], preferred_element_type=f32)


def _fox_logits(q, k, cc, cr, i, j, T):
    s = _dot_nt(q, k) * (HEAD_DIM ** -0.5) + cc - cr
    qpos = i * T + lax.broadcasted_iota(jnp.int32, (T, T), 0)
    kpos = j * T + lax.broadcasted_iota(jnp.int32, (T, T), 1)
    return jnp.where(qpos >= kpos, s, NEG_BIG)


def _fox_fwd(proj, c_col, c_row, H, name, cargo=None):
    S = proj.shape[0]
    T = _tile(S, ATTN_TILE)
    nt = S // T

    def body(*refs):
        (q_ref, k_ref, v_ref, cc_ref, cr_ref), (o_ref, lse_ref), (m_s, l_s, acc_s), hold = _split_refs(refs, 5, 2, cargo)
        h, i, j = pl.program_id(0), pl.program_id(1), pl.program_id(2)
        _cargo_start(cargo, (h == 0) & (i == 0) & (j == 0), hold)

        @pl.when(j == 0)
        def _():
            m_s[...] = jnp.full_like(m_s, NEG_BIG)
            l_s[...] = jnp.zeros_like(l_s)
            acc_s[...] = jnp.zeros_like(acc_s)

        @pl.when(j <= i)
        def _():
            s = _fox_logits(q_ref[...], k_ref[...], cc_ref[...], cr_ref[...], i, j, T)
            m_new = jnp.maximum(m_s[...], jnp.max(s, axis=-1, keepdims=True))
            p = jnp.exp(s - m_new)
            corr = jnp.exp(m_s[...] - m_new)
            l_s[...] = corr * l_s[...] + jnp.sum(p, axis=-1, keepdims=True)
            acc_s[...] = corr * acc_s[...] + _dot_nn(p, v_ref[...])
            m_s[...] = m_new

        @pl.when(j == i)
        def _():
            o_ref[...] = acc_s[...] / l_s[...]
            lse_ref[...] = m_s[...] + jnp.log(l_s[...])

        _cargo_wait(cargo, (h == H - 1) & (i == nt - 1) & (j == nt - 1), hold)

    sd = jax.ShapeDtypeStruct
    c_in, c_out, c_shape, c_sems = _cargo_specs(cargo)
    sem = ("arbitrary",) * 3 if cargo else ("parallel", "parallel", "arbitrary")
    return pl.pallas_call(
        body, name=name, grid=(H, nt, nt),
        in_specs=[pl.BlockSpec((T, HEAD_DIM), lambda h, i, j: (i, h)),
                  pl.BlockSpec((T, HEAD_DIM), lambda h, i, j: (jnp.minimum(j, i), H + h)),
                  pl.BlockSpec((T, HEAD_DIM), lambda h, i, j: (jnp.minimum(j, i), 2 * H + h)),
                  pl.BlockSpec((None, T, 1), lambda h, i, j: (h, i, 0)),
                  pl.BlockSpec((None, 1, T), lambda h, i, j: (h, 0, jnp.minimum(j, i)))] + c_in,
        out_specs=[pl.BlockSpec((T, HEAD_DIM), lambda h, i, j: (i, h)), pl.BlockSpec((None, T, 1), lambda h, i, j: (h, i, 0))] + c_out,
        out_shape=[sd((S, H * HEAD_DIM), f32), sd((H, S, 1), f32)] + c_shape,
        scratch_shapes=[pltpu.VMEM((T, 1), f32), pltpu.VMEM((T, 1), f32), pltpu.VMEM((T, HEAD_DIM), f32)] + c_sems,
        compiler_params=_params(*sem, vmem=_attn_vmem(T)))(proj, proj, proj, c_col, c_row, *(cargo.arrays if cargo else ()))


def _fox_bwd_q(proj, c_col, c_row, o, do, lse, H, name):
    S = proj.shape[0]
    T = _tile(S, ATTN_TILE)
    nt = S // T

    def body(q_ref, k_ref, v_ref, cc_ref, cr_ref, o_ref, do_ref, lse_ref, dq_ref, dc_ref, acc_s, dc_s):
        i, j = pl.program_id(1), pl.program_id(2)

        @pl.when(j == 0)
        def _():
            acc_s[...] = jnp.zeros_like(acc_s)
            dc_s[...] = jnp.zeros_like(dc_s)

        @pl.when(j <= i)
        def _():
            s = _fox_logits(q_ref[...], k_ref[...], cc_ref[...], cr_ref[...], i, j, T)
            p = jnp.exp(s - lse_ref[...])
            do_ = do_ref[...]
            delta = jnp.sum(o_ref[...] * do_, axis=-1, keepdims=True)
            ds = p * (_dot_nt(do_, v_ref[...]) - delta)
            acc_s[...] += _dot_nn(ds, k_ref[...])
            dc_s[...] += jnp.sum(ds, axis=-1, keepdims=True)

        @pl.when(j == i)
        def _():
            dq_ref[...] = acc_s[...] * (HEAD_DIM ** -0.5)
            dc_ref[...] = dc_s[...]

    qb = pl.BlockSpec((T, HEAD_DIM), lambda h, i, j: (i, h))
    col = pl.BlockSpec((None, T, 1), lambda h, i, j: (h, i, 0))
    return pl.pallas_call(
        body, name=name, grid=(H, nt, nt),
        in_specs=[qb, pl.BlockSpec((T, HEAD_DIM), lambda h, i, j: (jnp.minimum(j, i), H + h)),
                  pl.BlockSpec((T, HEAD_DIM), lambda h, i, j: (jnp.minimum(j, i), 2 * H + h)),
                  col, pl.BlockSpec((None, 1, T), lambda h, i, j: (h, 0, jnp.minimum(j, i))), qb, qb, col],
        out_specs=[qb, col], out_shape=[jax.ShapeDtypeStruct((S, H * HEAD_DIM), f32), jax.ShapeDtypeStruct((H, S, 1), f32)],
        scratch_shapes=[pltpu.VMEM((T, HEAD_DIM), f32), pltpu.VMEM((T, 1), f32)],
        compiler_params=_params("parallel", "parallel", "arbitrary", vmem=_attn_vmem(T)))(proj, proj, proj, c_col, c_row, o, do, lse)


def _fox_bwd_kv(proj, c_col, c_row, o, do, lse, H, name):
    S = proj.shape[0]
    T = _tile(S, ATTN_TILE)
    nt = S // T

    def body(q_ref, k_ref, v_ref, cc_ref, cr_ref, o_ref, do_ref, lse_ref, dk_ref, dv_ref, dc_ref, dk_s, dv_s, dc_s):
        j, i = pl.program_id(1), pl.program_id(2)

        @pl.when(i == 0)
        def _():
            dk_s[...] = jnp.zeros_like(dk_s)
            dv_s[...] = jnp.zeros_like(dv_s)
            dc_s[...] = jnp.zeros_like(dc_s)

        @pl.when(i >= j)
        def _():
            s = _fox_logits(q_ref[...], k_ref[...], cc_ref[...], cr_ref[...], i, j, T)
            p = jnp.exp(s - lse_ref[...])
            do_ = do_ref[...]
            delta = jnp.sum(o_ref[...] * do_, axis=-1, keepdims=True)
            ds = p * (_dot_nt(do_, v_ref[...]) - delta)
            dv_s[...] += _dot_tn(p, do_)
            dk_s[...] += _dot_tn(ds, q_ref[...])
            dc_s[...] -= jnp.sum(ds, axis=0, keepdims=True)

        @pl.when(i == nt - 1)
        def _():
            dk_ref[...] = dk_s[...] * (HEAD_DIM ** -0.5)
            dv_ref[...] = dv_s[...]
            dc_ref[...] = dc_s[...]

    qb = pl.BlockSpec((T, HEAD_DIM), lambda h, j, i: (jnp.maximum(i, j), h))
    col = pl.BlockSpec((None, T, 1), lambda h, j, i: (h, jnp.maximum(i, j), 0))
    kb = pl.BlockSpec((T, HEAD_DIM), lambda h, j, i: (j, h))
    sd = jax.ShapeDtypeStruct
    return pl.pallas_call(
        body, name=name, grid=(H, nt, nt),
        in_specs=[qb, pl.BlockSpec((T, HEAD_DIM), lambda h, j, i: (j, H + h)),
                  pl.BlockSpec((T, HEAD_DIM), lambda h, j, i: (j, 2 * H + h)),
                  col, pl.BlockSpec((None, 1, T), lambda h, j, i: (h, 0, j)), qb, qb, col],
        out_specs=[kb, kb, pl.BlockSpec((None, 1, T), lambda h, j, i: (h, 0, j))],
        out_shape=[sd((S, H * HEAD_DIM), f32), sd((S, H * HEAD_DIM), f32), sd((H, 1, S), f32)],
        scratch_shapes=[pltpu.VMEM((T, HEAD_DIM), f32), pltpu.VMEM((T, HEAD_DIM), f32), pltpu.VMEM((1, T), f32)],
        compiler_params=_params("parallel", "parallel", "arbitrary", vmem=_attn_vmem(T)))(proj, proj, proj, c_col, c_row, o, do, lse)


def _adamw(w, g, m, v, name):
    shape = w.shape
    cols = shape[-1]
    rows = w.size // cols
    ops = [t.reshape(rows, cols) for t in (w, g, m, v)]
    tr = rows
    if rows % 8 == 0:
        tr = 8
        while tr * 2 <= rows and rows % (tr * 2) == 0 and tr * 2 * cols * 4 <= (1 << 20):
            tr *= 2

    def body(w_ref, g_ref, m_ref, v_ref, d_ref, mo_ref, vo_ref):
        g_ = g_ref[...]
        m_ = ADAM_B1 * m_ref[...] + (1.0 - ADAM_B1) * g_
        v_ = ADAM_B2 * v_ref[...] + (1.0 - ADAM_B2) * (g_ * g_)
        m_hat = m_ / (1.0 - ADAM_B1 ** ADAM_STEP)
        v_hat = v_ / (1.0 - ADAM_B2 ** ADAM_STEP)
        d_ref[...] = -ADAM_LR * (m_hat / (jnp.sqrt(v_hat) + ADAM_EPS) + ADAM_WD * w_ref[...])
        mo_ref[...] = m_
        vo_ref[...] = v_

    blk = pl.BlockSpec((tr, cols), lambda i: (i, 0))
    outs = pl.pallas_call(body, name=name, grid=(rows // tr,), in_specs=[blk] * 4, out_specs=[blk] * 3,
                          out_shape=[jax.ShapeDtypeStruct((rows, cols), f32)] * 3, compiler_params=_params("parallel"))(*ops)
    return [o.reshape(shape) for o in outs]


def _row_tile(rows, row_bytes, budget=2 << 20):
    best = None
    for t in range(16, rows + 1, 16):
        if rows % t == 0 and t * row_bytes <= budget:
            best = t
    return best or rows


def _sum_leading(a, name):
    n, R, C = a.shape
    tr = _row_tile(R, n * C * 4)

    def body(a_ref, o_ref):
        acc = a_ref[0].astype(f32)
        for k in range(1, n):
            acc = acc + a_ref[k].astype(f32)
        o_ref[...] = acc

    return pl.pallas_call(body, name=name, grid=(R // tr,), in_specs=[pl.BlockSpec((n, tr, C), lambda i: (0, i, 0))],
                          out_specs=pl.BlockSpec((tr, C), lambda i: (i, 0)), out_shape=jax.ShapeDtypeStruct((R, C), f32),
                          compiler_params=_params("parallel"))(a)


def _add2(a, b, dtype, name):
    n, R, C = a.shape
    tr = _row_tile(R, C * 4)

    def body(a_ref, b_ref, o_ref):
        o_ref[...] = (a_ref[...] + b_ref[...]).astype(dtype)

    blk = pl.BlockSpec((None, tr, C), lambda k, i: (k, i, 0))
    return pl.pallas_call(body, name=name, grid=(n, R // tr), in_specs=[blk, blk], out_specs=blk,
                          out_shape=jax.ShapeDtypeStruct((n, R, C), dtype), compiler_params=_params("parallel", "parallel"))(a, b)


def _place():
    x, y, c = lax.axis_index("x"), lax.axis_index("y"), lax.axis_index("c")
    chips = [(1 - x, y), (x, 1 - y), (1 - x, 1 - y)]
    return x, y, c, chips


def _gather_copies(ws_refs, out_refs, send_sems, recv_sems):
    x, y, c, chips = _place()
    sends, lands = [], []
    for t, (ws_ref, out_ref) in enumerate(zip(ws_refs, out_refs)):
        for j, chip in enumerate(chips):
            make = functools.partial(pltpu.make_async_remote_copy, src_ref=ws_ref.at[c], send_sem=send_sems.at[3 * t + j],
                                     recv_sem=recv_sems.at[3 * t + j], device_id=(*chip, c), device_id_type=MESH)
            sends.append(functools.partial(make, dst_ref=out_ref.at[2 * x + y, c]))
            lands.append(functools.partial(make, dst_ref=out_ref.at[2 * chip[0] + chip[1], c]))
    return sends, lands


def _gather_cargo(blocks):
    return _Cargo(tuple(blocks), tuple(jax.ShapeDtypeStruct((N_CHIPS,) + b.shape, b.dtype) for b in blocks), _gather_copies,
                  3 * len(blocks))


def _sibling_forward(gathered, name):
    n = len(gathered)

    def body(*refs):
        ins, outs, send_sems, recv_sems = refs[:n], refs[n:2 * n], refs[2 * n], refs[2 * n + 1]
        x, y, c, chips = _place()
        sends, lands = [], []
        for t in range(n):
            for j, chip in enumerate(chips):
                s = 2 * chip[0] + chip[1]
                make = functools.partial(pltpu.make_async_remote_copy, src_ref=ins[t].at[s, c], send_sem=send_sems.at[3 * t + j],
                                         recv_sem=recv_sems.at[3 * t + j], device_id=(x, y, 1 - c), device_id_type=MESH)
                sends.append(make(dst_ref=outs[t].at[s, c]))
                lands.append(make(dst_ref=outs[t].at[s, 1 - c]))
        for cp in sends:
            cp.start()
        for cp in lands:
            cp.wait_recv()
        for cp in sends:
            cp.wait_send()

    return pl.pallas_call(body, name=name, out_shape=[jax.ShapeDtypeStruct(g.shape, g.dtype) for g in gathered],
                          in_specs=[ANY] * n, out_specs=[ANY] * n, input_output_aliases={t: t for t in range(n)},
                          scratch_shapes=[pltpu.SemaphoreType.DMA((3 * n,)), pltpu.SemaphoreType.DMA((3 * n,))])(*gathered)


def _pair_exchange(g, name):
    n, _, R, C = g.shape

    def body(g_ref, out_ref, send_sems, recv_sems):
        x, y, c, _ = _place()
        cps = [pltpu.make_async_remote_copy(src_ref=g_ref.at[s, 1 - c], dst_ref=out_ref.at[s], send_sem=send_sems.at[s],
                                            recv_sem=recv_sems.at[s], device_id=(x, y, 1 - c), device_id_type=MESH)
               for s in range(n)]
        for cp in cps:
            cp.start()
        for cp in cps:
            cp.wait()

    return pl.pallas_call(body, name=name, out_shape=jax.ShapeDtypeStruct((n, R, C), g.dtype), in_specs=[ANY], out_specs=ANY,
                          scratch_shapes=[pltpu.SemaphoreType.DMA((n,)), pltpu.SemaphoreType.DMA((n,))])(g)


def _chip_copies(a_refs, out_refs, send_sems, recv_sems):
    x, y, c, chips = _place()
    me = 2 * x + y
    sends, lands = [], []
    for t, (a_ref, out_ref) in enumerate(zip(a_refs, out_refs)):
        for j, chip in enumerate(chips):
            them = 2 * chip[0] + chip[1]
            make = functools.partial(pltpu.make_async_remote_copy, send_sem=send_sems.at[3 * t + j], recv_sem=recv_sems.at[3 * t + j],
                                     device_id=(*chip, c), device_id_type=MESH)
            sends.append(functools.partial(make, src_ref=a_ref.at[them], dst_ref=out_ref.at[me]))
            lands.append(functools.partial(make, src_ref=a_ref.at[me], dst_ref=out_ref.at[them]))
    return sends, lands


def _chip_cargo(pairs):
    return _Cargo(tuple(pairs), tuple(jax.ShapeDtypeStruct(p.shape, p.dtype) for p in pairs), _chip_copies, 3 * len(pairs))


def _pair_swap(r, name):
    R, C = r.shape

    def body(r_ref, out_ref, send_sem, recv_sem):
        x, y, c, _ = _place()
        cp = pltpu.make_async_remote_copy(src_ref=r_ref, dst_ref=out_ref, send_sem=send_sem, recv_sem=recv_sem,
                                          device_id=(x, y, 1 - c), device_id_type=MESH)
        cp.start()
        cp.wait()

    return pl.pallas_call(body, name=name, out_shape=jax.ShapeDtypeStruct((R, C), r.dtype), in_specs=[ANY], out_specs=ANY,
                          scratch_shapes=[pltpu.SemaphoreType.DMA(()), pltpu.SemaphoreType.DMA(())])(r)


def _all_gather8(v, name):
    R, C = v.shape

    def body(v_ref, out_ref, send_sems, recv_sems):
        x, y, c, _ = _place()
        me = 4 * x + 2 * y + c
        peers = [(x ^ (k >> 2), y ^ ((k >> 1) & 1), c ^ (k & 1)) for k in range(1, N_DEV)]
        sends = [pltpu.make_async_remote_copy(src_ref=v_ref, dst_ref=out_ref.at[me], send_sem=send_sems.at[k], recv_sem=recv_sems.at[k],
                                              device_id=peer, device_id_type=MESH) for k, peer in enumerate(peers)]
        for cp in sends:
            cp.start()
        for k, (px, py, pc) in enumerate(peers):
            pltpu.make_async_remote_copy(src_ref=v_ref, dst_ref=out_ref.at[4 * px + 2 * py + pc], send_sem=send_sems.at[k],
                                         recv_sem=recv_sems.at[k], device_id=(px, py, pc), device_id_type=MESH).wait_recv()
        for cp in sends:
            cp.wait_send()

    return pl.pallas_call(body, name=name, out_shape=jax.ShapeDtypeStruct((N_DEV, R, C), v.dtype), in_specs=[ANY], out_specs=ANY,
                          scratch_shapes=[pltpu.SemaphoreType.DMA((7,)), pltpu.SemaphoreType.DMA((7,))])(v)


def _as_list(x):
    return list(x) if isinstance(x, (list, tuple)) else [x]


def _put(buf, block, index):
    return lax.dynamic_update_slice(buf, block[None], (index,) + (0,) * block.ndim)


def _all_reduce8(v, device, name):
    n = v.shape[0]
    rows = -(-n // (LANES * SUM_ROWS)) * SUM_ROWS
    padded = jnp.pad(v, (0, rows * LANES - n)).reshape(rows, LANES)
    return _sum_leading(_put(_all_gather8(padded, name + "_gather"), padded, device), name + "_sum").reshape(-1)[:n]


def _halves(w):
    R, C = w.shape
    return w.astype(bf16).reshape(2, R // 2, C)


def _finish_gather(blocks, landed, chip):
    full = _sibling_forward(landed, "gather_sibling_forward")
    return [_put(g, b, chip).reshape(N_CHIPS, 2 * b.shape[1], b.shape[2]) for g, b in zip(full, blocks)]


def _pair_sums(g4, core, name):
    _, R, C = g4.shape
    g = g4.reshape(N_CHIPS, 2, R // 2, C)
    mine = lax.dynamic_index_in_dim(g, core, axis=1, keepdims=False)
    return _add2(mine, _pair_exchange(g, name + "_pair_exchange"), bf16, name + "_pair_sum")


def _finish_reduce(pair, landed, chip, core, name):
    _, R2, C = pair.shape
    own = lax.dynamic_index_in_dim(pair, chip, axis=0, keepdims=False)
    half = _sum_leading(_put(landed, own, chip), name + "_chip_sum")
    both = jnp.stack([half, _pair_swap(half, name + "_pair_swap")])
    return jnp.where(core == 0, both, both[::-1]).reshape(2 * R2, C)


def _segments(H, D):
    W = H * HEAD_DIM
    sizes = (3 * W, H, 2 * W, 3 * W, H, H, W, 3 * D)
    in_tail = (False, True, False, False, True, True, False, False)
    out, first, used = [], 0, [0, 0]
    for size, t in zip(sizes, in_tail):
        out.append((first, size, t, used[t]))
        first += size
        used[t] += size
    return out


def _main_tail_from_shards(g4, H, D):
    C = g4.shape[-1]
    parts = ([], [])
    for first, size, t, _ in _segments(H, D):
        for s in range(g4.shape[0]):
            a, b = max(first, s * C), min(first + size, (s + 1) * C)
            if a < b:
                parts[t].append(g4[s][..., a - s * C:b - s * C])
    parts[1].append(jnp.zeros(g4.shape[1:-1] + (LANES - 3 * H,), g4.dtype))
    return jnp.concatenate(parts[0], axis=-1), jnp.concatenate(parts[1], axis=-1)


def _shards_from_main_tail(main, tail, H, D):
    segs = _segments(H, D)
    C = sum(size for _, size, _, _ in segs) // N_CHIPS
    shards = []
    for s in range(N_CHIPS):
        pieces = []
        for first, size, t, there in segs:
            a, b = max(first, s * C), min(first + size, (s + 1) * C)
            if a < b:
                pieces.append((tail if t else main)[..., there + a - first:there + b - first])
        shards.append(jnp.concatenate(pieces, axis=-1))
    return jnp.stack(shards)


def _pad_cols(a, n):
    return jnp.pad(a, [(0, 0)] * (a.ndim - 1) + [(0, n - a.shape[-1])])


def kernel(x, w_in, b_in, sgu_ln_g, sgu_ln_b, sgu_w, sgu_b, gdn_conv_w, gdn_a_log, gdn_dt_bias, gdn_norm_g, w_proj_a, w_proj_b, w_proj_c, w_out, ln1_g, ln1_b, ffn_w_up, ffn_conv_w, ffn_conv_b, ffn_w_down, ln2_g, ln2_b, loss_target, m_w_in, m_b_in, m_sgu_ln_g, m_sgu_ln_b, m_sgu_w, m_sgu_b, m_gdn_conv_w, m_gdn_a_log, m_gdn_dt_bias, m_gdn_norm_g, m_w_proj_a, m_w_proj_b, m_w_proj_c, m_w_out, m_ln1_g, m_ln1_b, m_ffn_w_up, m_ffn_conv_w, m_ffn_conv_b, m_ffn_w_down, m_ln2_g, m_ln2_b, v_w_in, v_b_in, v_sgu_ln_g, v_sgu_ln_b, v_sgu_w, v_sgu_b, v_gdn_conv_w, v_gdn_a_log, v_gdn_dt_bias, v_gdn_norm_g, v_w_proj_a, v_w_proj_b, v_w_proj_c, v_w_out, v_ln1_g, v_ln1_b, v_ffn_w_up, v_ffn_conv_w, v_ffn_conv_b, v_ffn_w_down, v_ln2_g, v_ln2_b):
    P = dict(w_in=w_in, b_in=b_in, sgu_ln_g=sgu_ln_g, sgu_ln_b=sgu_ln_b, sgu_w=sgu_w, sgu_b=sgu_b, gdn_conv_w=gdn_conv_w,
             gdn_a_log=gdn_a_log, gdn_dt_bias=gdn_dt_bias, gdn_norm_g=gdn_norm_g, w_proj_a=w_proj_a, w_proj_b=w_proj_b,
             w_proj_c=w_proj_c, w_out=w_out, ln1_g=ln1_g, ln1_b=ln1_b, ffn_w_up=ffn_w_up, ffn_conv_w=ffn_conv_w,
             ffn_conv_b=ffn_conv_b, ffn_w_down=ffn_w_down, ln2_g=ln2_g, ln2_b=ln2_b)
    M1 = dict(w_in=m_w_in, b_in=m_b_in, sgu_ln_g=m_sgu_ln_g, sgu_ln_b=m_sgu_ln_b, sgu_w=m_sgu_w, sgu_b=m_sgu_b,
              gdn_conv_w=m_gdn_conv_w, gdn_a_log=m_gdn_a_log, gdn_dt_bias=m_gdn_dt_bias, gdn_norm_g=m_gdn_norm_g,
              w_proj_a=m_w_proj_a, w_proj_b=m_w_proj_b, w_proj_c=m_w_proj_c, w_out=m_w_out, ln1_g=m_ln1_g, ln1_b=m_ln1_b,
              ffn_w_up=m_ffn_w_up, ffn_conv_w=m_ffn_conv_w, ffn_conv_b=m_ffn_conv_b, ffn_w_down=m_ffn_w_down, ln2_g=m_ln2_g,
              ln2_b=m_ln2_b)
    M2 = dict(w_in=v_w_in, b_in=v_b_in, sgu_ln_g=v_sgu_ln_g, sgu_ln_b=v_sgu_ln_b, sgu_w=v_sgu_w, sgu_b=v_sgu_b,
              gdn_conv_w=v_gdn_conv_w, gdn_a_log=v_gdn_a_log, gdn_dt_bias=v_gdn_dt_bias, gdn_norm_g=v_gdn_norm_g,
              w_proj_a=v_w_proj_a, w_proj_b=v_w_proj_b, w_proj_c=v_w_proj_c, w_out=v_w_out, ln1_g=v_ln1_g, ln1_b=v_ln1_b,
              ffn_w_up=v_ffn_w_up, ffn_conv_w=v_ffn_conv_w, ffn_conv_b=v_ffn_conv_b, ffn_w_down=v_ffn_w_down, ln2_g=v_ln2_g,
              ln2_b=v_ln2_b)
    _, S, D = x.shape
    L = w_in.shape[0]
    N_IN = w_in.shape[2] * N_CHIPS
    H = (N_IN - 3 * D) // (9 * HEAD_DIM + 3)
    W = H * HEAD_DIM
    F = ffn_w_down.shape[1] * N_CHIPS
    Fp = -(-F // FF_ALIGN) * FF_ALIGN
    NM = 9 * W + 3 * D
    alpha = (2 * L) ** 0.25
    cx, cy, cc = lax.axis_index("x"), lax.axis_index("y"), lax.axis_index("c")
    chip = 2 * cx + cy

    def blocks_of(l):
        return [_halves(w_in[l]), _halves(jnp.concatenate([w_proj_a[l], w_proj_b[l], w_proj_c[l]], axis=0)),
                _halves(jnp.concatenate([w_out[l], ffn_w_down[l]], axis=0)), _halves(ffn_w_up[l])]

    def full_weights(blocks, landed):
        g_in, g_proj, g_rows, g_up = _finish_gather(blocks, landed, chip)
        w_main, w_tail = _main_tail_from_shards(g_in, H, D)
        wa, wb, wc = [g_proj[:, k * W:(k + 1) * W].transpose(1, 0, 2).reshape(W, D) for k in range(3)]
        return dict(w_main=w_main, w_tail=w_tail, wa=wa, wb=wb, wc=wc, wo=g_rows[:, :D // N_CHIPS].reshape(D, D),
                    wd=jnp.pad(g_rows[:, D // N_CHIPS:].reshape(F, D), ((0, Fp - F), (0, 0))),
                    wg=_pad_cols(jnp.concatenate([g_up[0], g_up[1]], axis=1), Fp),
                    wv=_pad_cols(jnp.concatenate([g_up[2], g_up[3]], axis=1), Fp))

    blocks = blocks_of(0)
    weights = full_weights(blocks, _ship(_gather_cargo(blocks), "gather_first_layer"))
    gcw_cols, fcw_cols = gdn_conv_w.shape[2], ffn_conv_w.shape[2]
    only_south = (cc == 0).astype(f32)
    placed_g = lax.dynamic_update_slice(jnp.zeros((L, GDN_CONV, 3 * W), f32), gdn_conv_w * only_south, (0, 0, chip * gcw_cols))
    placed_f = lax.dynamic_update_slice(jnp.zeros((L, FFN_CONV, 2 * F), f32), ffn_conv_w * only_south, (0, 0, chip * fcw_cols))
    conv_all = _all_reduce8(jnp.concatenate([placed_g.reshape(-1), placed_f.reshape(-1)]), 2 * chip + cc, "conv_weights")
    gcw_full = conv_all[:L * GDN_CONV * 3 * W].reshape(L, GDN_CONV, 1, 3 * W)
    fcw_full = conv_all[L * GDN_CONV * 3 * W:].reshape(L, FFN_CONV, 1, 2 * F)

    saved = []
    h = x.reshape(S, D)
    for l in range(L):
        b_main, b_tail = _main_tail_from_shards(b_in[l][None, None, :], H, D)
        cwg, cwv = _pad_cols(fcw_full[l][..., :F], Fp), _pad_cols(fcw_full[l][..., F:], Fp)
        cbg, cbv = _pad_cols(ffn_conv_b[l][None, :F], Fp), _pad_cols(ffn_conv_b[l][None, F:], Fp)
        nxt = blocks_of(l + 1) if l + 1 < L else None
        ride = (lambda *idx: _gather_cargo([nxt[k] for k in idx])) if nxt else (lambda *idx: None)
        lw = dict(weights, cwg=cwg, cwv=cwv, cbg=cbg, cbv=cbv, gcw=gcw_full[l],
                  sgu_ln_g=sgu_ln_g[l][None, :], sgu_ln_b=sgu_ln_b[l][None, :], sgu_w=sgu_w[l], sgu_b=sgu_b[l][:, :, None],
                  a_log=gdn_a_log[l].reshape(H, 1, 1), dt_bias=gdn_dt_bias[l].reshape(H, 1, 1),
                  norm_g=gdn_norm_g[l].reshape(1, 1, HEAD_DIM), ln1_g=ln1_g[l][None, :], ln1_b=ln1_b[l][None, :],
                  ln2_g=ln2_g[l][None, :], ln2_b=ln2_b[l][None, :])
        tag = "_carrying" if nxt else ""
        proj, *land_in = _as_list(_matmul(h, lw["w_main"], "nn", "proj_main" + tag, bias=b_main, cargo=ride(0)))
        tail = _matmul(h, lw["w_tail"], "nn", "proj_tail", bias=b_tail)
        csum = _fox_prep_fwd(tail, "fox_prep")
        c_col = csum[:, :H].T[:, :, None]
        c_row = csum[:, :H].T[:, None, :]
        y_a, lse, *land_up = _fox_fwd(proj, c_col, c_row, H, "fox_fwd" + tag, cargo=ride(3))
        y_b = _sgu_fwd(proj, lw["sgu_ln_g"], lw["sgu_ln_b"], lw["sgu_w"], lw["sgu_b"], 3 * W, 4 * W, "sgu_fwd")
        qkvc = _gdn_pre_fwd(proj, lw["gcw"], 5 * W, 2 * H, "gdn_pre")
        al = tail[:, H:2 * H].T[:, :, None]
        bl = tail[:, 2 * H:3 * H].T[:, :, None]
        y_c, states, *land_rest = _gdn_scan_fwd(qkvc, al, bl, proj, lw["a_log"], lw["dt_bias"], lw["norm_g"], 8 * W, "gdn_scan" + tag,
                                                cargo=ride(1, 2))
        if nxt:
            weights = full_weights(nxt, land_in + [land_rest[0], land_rest[1]] + land_up)
        pa = _matmul(y_a, lw["wa"], "nn", "branch_proj")
        pb = _matmul(y_b, lw["wb"], "nn", "branch_proj")
        pc = _matmul(y_c, lw["wc"], "nn", "branch_proj")
        merged = _merge_fwd(proj, pa, pb, pc, 9 * W, "merge")
        mix = _matmul(merged, lw["wo"], "nn", "out_proj")
        x1 = _ln_fwd(h, mix, lw["ln1_g"], lw["ln1_b"], alpha, "ln")
        upg = _matmul(x1, lw["wg"], "nn", "ffn_up")
        upv = _matmul(x1, lw["wv"], "nn", "ffn_up")
        act = _ffn_act_fwd(upg, upv, cwg, cwv, cbg, cbv, "ffn_act")
        ffn = _matmul(act, lw["wd"], "nn", "ffn_down")
        x2 = _ln_fwd(x1, ffn, lw["ln2_g"], lw["ln2_b"], alpha, "ln")
        saved.append(dict(lw=lw, h=h, proj=proj, tail=tail, c_col=c_col, c_row=c_row, y_a=y_a, lse=lse, y_b=y_b, qkvc=qkvc, al=al,
                          bl=bl, y_c=y_c, states=states, pa=pa, pb=pb, pc=pc, merged=merged, mix=mix, x1=x1, upg=upg, upv=upv,
                          act=act, ffn=ffn))
        h = x2

    loss_part, dh = _loss_head(h, loss_target.reshape(S, D), "loss_head")
    loss = lax.psum(loss_part[0, 0], ("x", "y", "c"))

    reduced = [dict() for _ in range(L)]
    small_grads = [None] * L
    pending = None
    for l in reversed(range(L)):
        s = saved[l]
        lw = s["lw"]
        d_x1r, d_ffn, d_ln2g, d_ln2b = _ln_bwd(s["x1"], s["ffn"], lw["ln2_g"], lw["ln2_b"], dh, alpha, "ln_bwd")
        d_act = _matmul(d_ffn, lw["wd"], "nt", "ffn_down_dx")
        d_wd = _matmul(s["act"], d_ffn, "tn", "ffn_down_dw")
        dupg, dupv, dcwg, dcwv, dcbg, dcbv = _ffn_act_bwd(s["upg"], s["upv"], lw["cwg"], lw["cwv"], lw["cbg"], lw["cbv"], d_act,
                                                          "ffn_act_bwd")
        d_x1 = _matmul(dupg, lw["wg"], "nt", "ffn_up_dx", add=d_x1r)
        d_x1 = _matmul(dupv, lw["wv"], "nt", "ffn_up_dx", add=d_x1)
        d_wg = _matmul(s["x1"], dupg, "tn", "ffn_up_dw")
        d_wv = _matmul(s["x1"], dupv, "tn", "ffn_up_dw")
        d_hr, d_mix, d_ln1g, d_ln1b = _ln_bwd(s["h"], s["mix"], lw["ln1_g"], lw["ln1_b"], d_x1, alpha, "ln_bwd")
        d_merged = _matmul(d_mix, lw["wo"], "nt", "out_proj_dx")
        d_wo = _matmul(s["merged"], d_mix, "tn", "out_proj_dw")
        dg0, dg1, dg2, d_pa, d_pb, d_pc = _merge_bwd(s["proj"], s["pa"], s["pb"], s["pc"], d_merged, 9 * W, "merge_bwd")
        d_ya = _matmul(d_pa, lw["wa"], "nt", "branch_proj_dx")
        d_yb = _matmul(d_pb, lw["wb"], "nt", "branch_proj_dx")
        d_yc = _matmul(d_pc, lw["wc"], "nt", "branch_proj_dx")
        d_wa = _matmul(s["y_a"], d_pa, "tn", "branch_proj_dw")
        d_wb = _matmul(s["y_b"], d_pb, "tn", "branch_proj_dw")
        d_wc = _matmul(s["y_c"], d_pc, "tn", "branch_proj_dw")
        tag = "_carrying" if pending else ""
        ride = (lambda *keys: _chip_cargo([pending[1][k] for k in keys])) if pending else (lambda *keys: None)
        dqkvc, dal, dbl, dgate, d_alog, d_dt, d_ng, *landed = _gdn_scan_bwd(
            s["qkvc"], s["al"], s["bl"], s["proj"], lw["a_log"], lw["dt_bias"], lw["norm_g"], s["states"], d_yc, 8 * W,
            "gdn_scan_bwd" + tag, cargo=ride("w_in", "proj"))
        for k, got in zip(("w_in", "proj"), landed):
            reduced[pending[0]][k] = _finish_reduce(pending[1][k], got, chip, cc, "grad_" + k)
        d_gqkv, d_gcw = _gdn_pre_bwd(s["proj"], lw["gcw"], dqkvc, 5 * W, 2 * H, "gdn_pre_bwd")
        d_u, d_v, d_slg, d_slb, d_sw, d_sb = _sgu_bwd(s["proj"], lw["sgu_ln_g"], lw["sgu_ln_b"], lw["sgu_w"], lw["sgu_b"], d_yb,
                                                      3 * W, 4 * W, "sgu_bwd")
        d_q, d_cq = _fox_bwd_q(s["proj"], s["c_col"], s["c_row"], s["y_a"], d_ya, s["lse"], H, "fox_bwd_q")
        d_k, d_v_att, d_c = _fox_bwd_kv(s["proj"], s["c_col"], s["c_row"], s["y_a"], d_ya, s["lse"], H, "fox_bwd_kv")
        d_f = _fox_prep_bwd(s["tail"], _pad_cols(d_cq[:, :, 0].T, LANES), _pad_cols(d_c[:, 0, :].T, LANES), "fox_prep_bwd")
        d_main = jnp.concatenate([d_q, d_k, d_v_att, d_u, d_v, d_gqkv, dgate, dg0, dg1, dg2], axis=1)
        d_tail = _pad_cols(jnp.concatenate([d_f[:, :H], dal[:, :, 0].T, dbl[:, :, 0].T], axis=1), LANES)
        d_wmain = _matmul(s["h"], d_main, "tn", "proj_main_dw")
        d_wtail = _matmul(s["h"], d_tail, "tn", "proj_tail_dw")
        d_bmain = _colsum(d_main, "proj_main_db")
        d_btail = _colsum(d_tail, "proj_tail_db")
        dh, *landed = _as_list(_matmul(d_main, lw["w_main"], "nt", "proj_main_dx" + tag, add=d_hr, cargo=ride("rows", "w_up")))
        for k, got in zip(("rows", "w_up"), landed):
            reduced[pending[0]][k] = _finish_reduce(pending[1][k], got, chip, cc, "grad_" + k)
        dh = _matmul(d_tail, lw["w_tail"], "nt", "proj_tail_dx", add=dh)
        by_cols = lambda g: g.reshape(g.shape[0], N_CHIPS, g.shape[1] // N_CHIPS).transpose(1, 0, 2)
        big = dict(
            w_in=_shards_from_main_tail(d_wmain, d_wtail, H, D),
            proj=jnp.concatenate([by_cols(d_wa), by_cols(d_wb), by_cols(d_wc)], axis=1),
            rows=jnp.concatenate([d_wo.reshape(N_CHIPS, D // N_CHIPS, D), d_wd[:F].reshape(N_CHIPS, F // N_CHIPS, D)], axis=1),
            w_up=jnp.stack([d_wg[:, :F // 2], d_wg[:, F // 2:F], d_wv[:, :F // 2], d_wv[:, F // 2:F]]))
        pending = (l, {k: _pair_sums(big[k], cc, "grad_" + k) for k in SHARDED})
        small_grads[l] = dict(b_in=_shards_from_main_tail(d_bmain, d_btail, H, D).reshape(-1), sgu_ln_g=d_slg[0], sgu_ln_b=d_slb[0], sgu_w=d_sw,
                              sgu_b=d_sb[:, :, 0], gdn_conv_w=d_gcw[:, 0, :], gdn_a_log=d_alog[:, 0, 0], gdn_dt_bias=d_dt[:, 0, 0],
                              gdn_norm_g=d_ng[0, 0], ln1_g=d_ln1g[0], ln1_b=d_ln1b[0],
                              ffn_conv_w=jnp.concatenate([dcwg[:, 0, :F], dcwv[:, 0, :F]], axis=1),
                              ffn_conv_b=jnp.concatenate([dcbg[0, :F], dcbv[0, :F]]), ln2_g=d_ln2g[0], ln2_b=d_ln2b[0])
    grad_x = dh.reshape(1, S, D)

    landed = _ship(_chip_cargo([pending[1][k] for k in SHARDED]), "grad_chip_exchange")
    reduced[pending[0]] = {k: _finish_reduce(pending[1][k], got, chip, cc, "grad_" + k) for k, got in zip(SHARDED, landed)}
    grads = {n: [None] * L for n in WEIGHTS}
    for l in range(L):
        grads["w_in"][l] = reduced[l]["w_in"]
        for k, n in enumerate(("w_proj_a", "w_proj_b", "w_proj_c")):
            grads[n][l] = reduced[l]["proj"][k * W:(k + 1) * W]
        grads["w_out"][l] = reduced[l]["rows"][:D // N_CHIPS]
        grads["ffn_w_down"][l] = reduced[l]["rows"][D // N_CHIPS:]
        grads["ffn_w_up"][l] = reduced[l]["w_up"]
    small_shapes = {n: small_grads[0][n].shape for n in SMALL}
    small_flat = jnp.concatenate([small_grads[l][n].reshape(-1) for l in range(L) for n in SMALL])
    small_sum = _all_reduce8(small_flat, 2 * chip + cc, "small_grads")
    off = 0
    for l in range(L):
        for n in SMALL:
            size = math.prod(small_shapes[n])
            g = small_sum[off:off + size].reshape(small_shapes[n])
            off += size
            if n == "gdn_conv_w":
                g = lax.dynamic_slice_in_dim(g, chip * gcw_cols, gcw_cols, axis=1)
            elif n == "ffn_conv_w":
                g = lax.dynamic_slice_in_dim(g, chip * fcw_cols, fcw_cols, axis=1)
            grads[n][l] = g
    grads = {n: jnp.stack(grads[n]) for n in WEIGHTS}

    deltas, new_m, new_v = {}, {}, {}
    for n in WEIGHTS:
        deltas[n], new_m[n], new_v[n] = _adamw(P[n], grads[n], M1[n], M2[n], "adamw_" + n)
    return (loss, grad_x, *[grads[n] for n in WEIGHTS], *[deltas[n] for n in WEIGHTS], *[new_m[n] for n in WEIGHTS],
            *[new_v[n] for n in WEIGHTS])
```

```python
import functools
import math
from typing import NamedTuple

import jax
import jax.numpy as jnp
from jax import lax
from jax.experimental import pallas as pl
from jax.experimental.pallas import tpu as pltpu

f32 = jnp.float32
bf16 = jnp.bfloat16
HIGHEST = lax.Precision.HIGHEST
MESH = pl.DeviceIdType.MESH

HEAD_DIM = 128
CHUNK = 64
SGU_SPAN = 128
GDN_CONV = 4
FFN_CONV = 3
N_CHIPS = 4
N_DEV = 8
LN_EPS = 1e-5
RMS_EPS = 1e-6
ADAM_LR = 0.001
ADAM_B1 = 0.9
ADAM_B2 = 0.999
ADAM_EPS = 1e-08
ADAM_WD = 0.01
ADAM_STEP = 10
NEG_BIG = -1e30
LANES = 128
FF_ALIGN = 512
SUM_ROWS = 256
ATTN_TILE = 1024
VMEM_MARGIN = 12 << 20
VMEM_MOST = 60 << 20

SHARDED = ("w_in", "proj", "rows", "w_up")
SMALL = ("b_in", "sgu_ln_g", "sgu_ln_b", "sgu_w", "sgu_b", "gdn_conv_w", "gdn_a_log", "gdn_dt_bias", "gdn_norm_g",
         "ln1_g", "ln1_b", "ffn_conv_w", "ffn_conv_b", "ln2_g", "ln2_b")
WEIGHTS = ("w_in", "b_in", "sgu_ln_g", "sgu_ln_b", "sgu_w", "sgu_b", "gdn_conv_w", "gdn_a_log", "gdn_dt_bias", "gdn_norm_g",
           "w_proj_a", "w_proj_b", "w_proj_c", "w_out", "ln1_g", "ln1_b", "ffn_w_up", "ffn_conv_w", "ffn_conv_b", "ffn_w_down",
           "ln2_g", "ln2_b")

ANY = pl.BlockSpec(memory_space=pl.ANY)


def _tile(dim, pref):
    t = pref
    while t > 128 and dim % t:
        t //= 2
    return min(t, dim) if dim % min(t, dim) == 0 else dim


def _params(*sem, vmem=None):
    if vmem is None:
        return pltpu.CompilerParams(dimension_semantics=sem)
    return pltpu.CompilerParams(dimension_semantics=sem, vmem_limit_bytes=min(vmem + VMEM_MARGIN, VMEM_MOST))


def _attn_vmem(T):
    return 8 * T * T * 4


_DIMS = {"nn": (((1,), (0,)), ((), ())), "nt": (((1,), (1,)), ((), ())), "tn": (((0,), (0,)), ((), ()))}


def _pick(dim, most):
    for unit in (256, LANES):
        for t in range(min(most, dim) // unit * unit, 0, -unit):
            if dim % t == 0:
                return t
    return dim


class _Cargo(NamedTuple):
    arrays: tuple
    landing: tuple
    copies: object
    n: int


def _cargo_specs(cargo):
    if cargo is None:
        return [], [], [], []
    sems = [pltpu.SemaphoreType.DMA((cargo.n,)), pltpu.SemaphoreType.DMA((cargo.n,))]
    return [ANY] * len(cargo.arrays), [ANY] * len(cargo.landing), list(cargo.landing), sems


def _split_refs(refs, n_in, n_out, cargo):
    na, nl, ns = (len(cargo.arrays), len(cargo.landing), 2) if cargo else (0, 0, 0)
    a, b, c = n_in + na, n_in + na + n_out, n_in + na + n_out + nl
    return refs[:n_in], refs[a:b], refs[c:len(refs) - ns], (refs[n_in:a], refs[b:c], refs[len(refs) - ns:])


def _cargo_start(cargo, first, hold):
    if cargo is not None:
        @pl.when(first)
        def _():
            for make in cargo.copies(hold[0], hold[1], *hold[2])[0]:
                make().start()


def _cargo_wait(cargo, last, hold):
    if cargo is not None:
        @pl.when(last)
        def _():
            sends, lands = cargo.copies(hold[0], hold[1], *hold[2])
            for make in lands:
                make().wait_recv()
            for make in sends:
                make().wait_send()


def _ship(cargo, name):
    in_specs, out_specs, out_shape, sems = _cargo_specs(cargo)

    def body(*refs):
        _, _, _, hold = _split_refs(refs, 0, 0, cargo)
        sends, lands = cargo.copies(hold[0], hold[1], *hold[2])
        started = [make() for make in sends]
        for cp in started:
            cp.start()
        for make in lands:
            make().wait_recv()
        for cp in started:
            cp.wait_send()

    return pl.pallas_call(body, name=name, out_shape=out_shape, in_specs=in_specs, out_specs=out_specs,
                          scratch_shapes=sems)(*cargo.arrays)


def _matmul(a, b, mode, name, bias=None, add=None, cargo=None):
    a, b = a.astype(bf16), b.astype(bf16)
    if mode == "nn":
        (M, K), (_, N) = a.shape, b.shape
    elif mode == "nt":
        (M, K), (N, _) = a.shape, b.shape
    else:
        (K, M), (_, N) = a.shape, b.shape
    has_bias, has_add = bias is not None, add is not None
    tm, tn, tk = _pick(M, 512 if has_add else 1024), _pick(N, 2816), _pick(K, 512)
    ni, nj, nk = M // tm, N // tn, K // tk
    vmem = 2 * (tm * tk * a.dtype.itemsize + tk * tn * b.dtype.itemsize + tm * tn * 4 * (2 if has_add else 1)) + tm * tn * 4

    def body(*refs):
        ins, (o_ref,), (acc_ref,), hold = _split_refs(refs, 2 + has_bias + has_add, 1, cargo)
        a_ref, b_ref = ins[0], ins[1]
        bias_ref = ins[2] if has_bias else None
        add_ref = ins[2 + has_bias] if has_add else None
        i, j, k = pl.program_id(0), pl.program_id(1), pl.program_id(2)
        _cargo_start(cargo, (i == 0) & (j == 0) & (k == 0), hold)

        @pl.when(k == 0)
        def _():
            acc_ref[...] = jnp.zeros_like(acc_ref)

        acc_ref[...] += lax.dot_general(a_ref[...], b_ref[...], _DIMS[mode], preferred_element_type=f32)

        @pl.when(k == nk - 1)
        def _():
            r = acc_ref[...]
            if has_bias:
                r = r + bias_ref[...]
            if has_add:
                r = r + add_ref[...]
            o_ref[...] = r

        _cargo_wait(cargo, (i == ni - 1) & (j == nj - 1) & (k == nk - 1), hold)

    if mode == "nn":
        specs = [pl.BlockSpec((tm, tk), lambda i, j, k: (i, k)), pl.BlockSpec((tk, tn), lambda i, j, k: (k, j))]
    elif mode == "nt":
        specs = [pl.BlockSpec((tm, tk), lambda i, j, k: (i, k)), pl.BlockSpec((tn, tk), lambda i, j, k: (j, k))]
    else:
        specs = [pl.BlockSpec((tk, tm), lambda i, j, k: (k, i)), pl.BlockSpec((tk, tn), lambda i, j, k: (k, j))]
    ops = [a, b]
    if has_bias:
        specs.append(pl.BlockSpec((1, tn), lambda i, j, k: (0, j)))
        ops.append(bias)
    if has_add:
        specs.append(pl.BlockSpec((tm, tn), lambda i, j, k: (i, j)))
        ops.append(add)
    c_in, c_out, c_shape, c_sems = _cargo_specs(cargo)
    outs = pl.pallas_call(
        body, name=name, grid=(ni, nj, nk), in_specs=specs + c_in,
        out_specs=[pl.BlockSpec((tm, tn), lambda i, j, k: (i, j))] + c_out,
        out_shape=[jax.ShapeDtypeStruct((M, N), f32)] + c_shape, scratch_shapes=[pltpu.VMEM((tm, tn), f32)] + c_sems,
        compiler_params=pltpu.CompilerParams(
            dimension_semantics=("arbitrary",) * 3 if cargo else ("parallel", "parallel", "arbitrary"),
            vmem_limit_bytes=min(vmem + VMEM_MARGIN, VMEM_MOST)))(*ops, *(cargo.arrays if cargo else ()))
    return outs if cargo else outs[0]


def _colsum(a, name):
    S, N = a.shape
    tn = _tile(N, 512)

    def body(a_ref, o_ref):
        o_ref[...] = jnp.sum(a_ref[...], axis=0, keepdims=True)

    return pl.pallas_call(body, name=name, grid=(N // tn,), in_specs=[pl.BlockSpec((S, tn), lambda j: (0, j))],
                          out_specs=pl.BlockSpec((1, tn), lambda j: (0, j)), out_shape=jax.ShapeDtypeStruct((1, N), f32),
                          compiler_params=_params("parallel"))(a)


def _ln_fn(alpha, x, y, g, b):
    z = alpha * x + y
    mu = jnp.mean(z, axis=-1, keepdims=True)
    zc = z - mu
    var = jnp.mean(zc * zc, axis=-1, keepdims=True)
    return zc * lax.rsqrt(var + LN_EPS) * g + b


def _ln_fwd(x, y, g, b, alpha, name):
    S, D = x.shape
    tr = _tile(S, 256)

    def body(x_ref, y_ref, g_ref, b_ref, o_ref):
        o_ref[...] = _ln_fn(alpha, x_ref[...], y_ref[...], g_ref[...], b_ref[...])

    row = pl.BlockSpec((tr, D), lambda i: (i, 0))
    par = pl.BlockSpec((1, D), lambda i: (0, 0))
    return pl.pallas_call(body, name=name, grid=(S // tr,), in_specs=[row, row, par, par], out_specs=row,
                          out_shape=jax.ShapeDtypeStruct((S, D), f32), compiler_params=_params("parallel"))(x, y, g, b)


def _ln_bwd(x, y, g, b, dout, alpha, name):
    S, D = x.shape
    tr = _tile(S, 256)

    def body(x_ref, y_ref, g_ref, b_ref, d_ref, dx_ref, dy_ref, dg_ref, db_ref):
        _, vjp = jax.vjp(functools.partial(_ln_fn, alpha), x_ref[...], y_ref[...], g_ref[...], b_ref[...])
        dx, dy, dg, db = vjp(d_ref[...])
        dx_ref[...] = dx
        dy_ref[...] = dy.astype(dy_ref.dtype)

        @pl.when(pl.program_id(0) == 0)
        def _():
            dg_ref[...] = jnp.zeros_like(dg_ref)
            db_ref[...] = jnp.zeros_like(db_ref)

        dg_ref[...] += dg
        db_ref[...] += db

    row = pl.BlockSpec((tr, D), lambda i: (i, 0))
    par = pl.BlockSpec((1, D), lambda i: (0, 0))
    sd = jax.ShapeDtypeStruct
    return pl.pallas_call(body, name=name, grid=(S // tr,), in_specs=[row, row, par, par, row],
                          out_specs=[row, row, par, par],
                          out_shape=[sd((S, D), f32), sd((S, D), bf16), sd((1, D), f32), sd((1, D), f32)],
                          compiler_params=_params("arbitrary"))(x, y, g, b, dout)


def _loss_head(y, t, name):
    S, D = y.shape
    tr = _tile(S, 256)

    def body(y_ref, t_ref, l_ref, d_ref):
        e = y_ref[...] - t_ref[...]
        d_ref[...] = e / D

        @pl.when(pl.program_id(0) == 0)
        def _():
            l_ref[...] = jnp.zeros_like(l_ref)

        l_ref[...] += 0.5 * jnp.sum(jnp.mean(e * e, axis=-1, keepdims=True))

    row = pl.BlockSpec((tr, D), lambda i: (i, 0))
    return pl.pallas_call(body, name=name, grid=(S // tr,), in_specs=[row, row],
                          out_specs=[pl.BlockSpec((8, LANES), lambda i: (0, 0)), row],
                          out_shape=[jax.ShapeDtypeStruct((8, LANES), f32), jax.ShapeDtypeStruct((S, D), f32)],
                          compiler_params=_params("arbitrary"))(y, t)


def _merge_fn(g0, g1, g2, pa, pb, pc):
    return jax.nn.sigmoid(g0) * pa + jax.nn.sigmoid(g1) * pb + jax.nn.sigmoid(g2) * pc


def _merge_specs(S, D, gate_off):
    tr = _tile(S, 512)
    tc = _tile(math.gcd(gate_off, D), 512)
    gates = [pl.BlockSpec((tr, tc), functools.partial(lambda k, i, j: (i, (gate_off + k * D) // tc + j), k)) for k in range(3)]
    tile = pl.BlockSpec((tr, tc), lambda i, j: (i, j))
    return tr, tc, gates, tile


def _merge_fwd(proj, pa, pb, pc, gate_off, name):
    S, D = pa.shape
    tr, tc, gates, tile = _merge_specs(S, D, gate_off)

    def body(g0, g1, g2, a, b, c, o_ref):
        o_ref[...] = _merge_fn(g0[...], g1[...], g2[...], a[...], b[...], c[...]).astype(o_ref.dtype)

    return pl.pallas_call(body, name=name, grid=(S // tr, D // tc), in_specs=gates + [tile] * 3, out_specs=tile,
                          out_shape=jax.ShapeDtypeStruct((S, D), bf16),
                          compiler_params=_params("parallel", "parallel"))(proj, proj, proj, pa, pb, pc)


def _merge_bwd(proj, pa, pb, pc, dm, gate_off, name):
    S, D = pa.shape
    tr, tc, gates, tile = _merge_specs(S, D, gate_off)

    def body(g0, g1, g2, a, b, c, d, dg0, dg1, dg2, da, db, dc):
        _, vjp = jax.vjp(_merge_fn, g0[...], g1[...], g2[...], a[...], b[...], c[...])
        for ref, val in zip((dg0, dg1, dg2, da, db, dc), vjp(d[...])):
            ref[...] = val.astype(ref.dtype)

    sd = jax.ShapeDtypeStruct
    return pl.pallas_call(body, name=name, grid=(S // tr, D // tc), in_specs=gates + [tile] * 4,
                          out_specs=[tile] * 6, out_shape=[sd((S, D), f32)] * 3 + [sd((S, D), bf16)] * 3,
                          compiler_params=_params("parallel", "parallel"))(proj, proj, proj, pa, pb, pc, dm)


def _sgu_fn(nb, u, v, ln_g, ln_b, w_s, b_s):
    mu = jnp.mean(v, axis=-1, keepdims=True)
    vc = v - mu
    var = jnp.mean(vc * vc, axis=-1, keepdims=True)
    vn = vc * lax.rsqrt(var + LN_EPS) * ln_g + ln_b
    r = lax.broadcasted_iota(jnp.int32, (SGU_SPAN, SGU_SPAN), 0) // CHUNK
    c = lax.broadcasted_iota(jnp.int32, (SGU_SPAN, SGU_SPAN), 1) // CHUNK
    wm = jnp.where(r >= c, w_s, 0.0)
    vn3 = vn.reshape(nb, SGU_SPAN, HEAD_DIM)
    mixed = lax.dot_general(jnp.broadcast_to(wm, (nb, SGU_SPAN, SGU_SPAN)), vn3, (((2,), (1,)), ((0,), (0,))),
                            preferred_element_type=f32)
    mixed = mixed + b_s
    return u * mixed.reshape(nb * SGU_SPAN, HEAD_DIM)


def _sgu_specs(S, G, u_off, v_off):
    nb = max(1, min(8, S // SGU_SPAN))
    rows = nb * SGU_SPAN
    ub = pl.BlockSpec((rows, HEAD_DIM), lambda g, n: (n, u_off // HEAD_DIM + g))
    vb = pl.BlockSpec((rows, HEAD_DIM), lambda g, n: (n, v_off // HEAD_DIM + g))
    lnb = pl.BlockSpec((1, HEAD_DIM), lambda g, n: (0, g))
    wb = pl.BlockSpec((None, SGU_SPAN, SGU_SPAN), lambda g, n: (g, 0, 0))
    bb = pl.BlockSpec((None, SGU_SPAN, 1), lambda g, n: (g, 0, 0))
    return nb, rows, ub, vb, lnb, wb, bb


def _sgu_fwd(proj, ln_g, ln_b, w_s, b_s, u_off, v_off, name):
    S = proj.shape[0]
    G = w_s.shape[0]
    nb, rows, ub, vb, lnb, wb, bb = _sgu_specs(S, G, u_off, v_off)

    def body(u, v, lg, lb, w, b, o_ref):
        o_ref[...] = _sgu_fn(nb, u[...], v[...], lg[...], lb[...], w[...], b[...]).astype(o_ref.dtype)

    return pl.pallas_call(body, name=name, grid=(G, S // rows), in_specs=[ub, vb, lnb, lnb, wb, bb],
                          out_specs=pl.BlockSpec((rows, HEAD_DIM), lambda g, n: (n, g)),
                          out_shape=jax.ShapeDtypeStruct((S, G * HEAD_DIM), bf16),
                          compiler_params=_params("parallel", "parallel"))(proj, proj, ln_g, ln_b, w_s, b_s)


def _sgu_bwd(proj, ln_g, ln_b, w_s, b_s, dy, u_off, v_off, name):
    S = proj.shape[0]
    G = w_s.shape[0]
    nb, rows, ub, vb, lnb, wb, bb = _sgu_specs(S, G, u_off, v_off)

    def body(u, v, lg, lb, w, b, d, du, dv, dlg, dlb, dw, db):
        _, vjp = jax.vjp(functools.partial(_sgu_fn, nb), u[...], v[...], lg[...], lb[...], w[...], b[...])
        gu, gv, glg, glb, gw, gb = vjp(d[...])
        du[...] = gu
        dv[...] = gv

        @pl.when(pl.program_id(1) == 0)
        def _():
            for ref in (dlg, dlb, dw, db):
                ref[...] = jnp.zeros_like(ref)

        dlg[...] += glg
        dlb[...] += glb
        dw[...] += gw
        db[...] += gb

    tile = pl.BlockSpec((rows, HEAD_DIM), lambda g, n: (n, g))
    sd = jax.ShapeDtypeStruct
    W = G * HEAD_DIM
    return pl.pallas_call(body, name=name, grid=(G, S // rows), in_specs=[ub, vb, lnb, lnb, wb, bb, tile],
                          out_specs=[tile, tile, lnb, lnb, wb, bb],
                          out_shape=[sd((S, W), f32), sd((S, W), f32), sd((1, W), f32), sd((1, W), f32),
                                     sd((G, SGU_SPAN, SGU_SPAN), f32), sd((G, SGU_SPAN, 1), f32)],
                          compiler_params=_params("parallel", "arbitrary"))(proj, proj, ln_g, ln_b, w_s, b_s, dy)


def _shift_down(x, k):
    if k == 0:
        return x
    rows = lax.broadcasted_iota(jnp.int32, x.shape, 0)
    return jnp.where(rows >= k, pltpu.roll(x, k, 0), 0.0)


def _shift_up(x, k):
    if k == 0:
        return x
    n = x.shape[0]
    rows = lax.broadcasted_iota(jnp.int32, x.shape, 0)
    return jnp.where(rows < n - k, pltpu.roll(x, n - k, 0), 0.0)


def _conv(x, w_ref, width):
    out = w_ref[width - 1] * x
    for j in range(width - 1):
        out = out + w_ref[j] * _shift_down(x, width - 1 - j)
    return out


def _conv_bwd(x, dz, w_ref, dw_ref, width):
    dx = w_ref[width - 1] * dz
    dw_ref[width - 1] = jnp.sum(dz * x, axis=0, keepdims=True)
    for j in range(width - 1):
        k = width - 1 - j
        dx = dx + w_ref[j] * _shift_up(dz, k)
        dw_ref[j] = jnp.sum(dz * _shift_down(x, k), axis=0, keepdims=True)
    return dx


def _silu(z):
    return z * jax.nn.sigmoid(z)


def _dsilu(z):
    s = jax.nn.sigmoid(z)
    return s * (1.0 + z * (1.0 - s))


def _ffn_act_fwd(upg, upv, cwg, cwv, cbg, cbv, name):
    S, Fp = upg.shape

    def body(g_ref, v_ref, wg, wv, bg, bv, o_ref):
        hg = _conv(g_ref[...], wg, FFN_CONV) + bg[...]
        hv = _conv(v_ref[...], wv, FFN_CONV) + bv[...]
        o_ref[...] = (_silu(hg) * hv).astype(o_ref.dtype)

    col = pl.BlockSpec((S, LANES), lambda j: (0, j))
    wsp = pl.BlockSpec((FFN_CONV, 1, LANES), lambda j: (0, 0, j))
    bsp = pl.BlockSpec((1, LANES), lambda j: (0, j))
    return pl.pallas_call(body, name=name, grid=(Fp // LANES,), in_specs=[col, col, wsp, wsp, bsp, bsp], out_specs=col,
                          out_shape=jax.ShapeDtypeStruct((S, Fp), bf16),
                          compiler_params=_params("parallel"))(upg, upv, cwg, cwv, cbg, cbv)


def _ffn_act_bwd(upg, upv, cwg, cwv, cbg, cbv, dact, name):
    S, Fp = upg.shape

    def body(g_ref, v_ref, wg, wv, bg, bv, d_ref, dg_ref, dv_ref, dwg, dwv, dbg, dbv):
        xg, xv, d = g_ref[...], v_ref[...], d_ref[...]
        hg = _conv(xg, wg, FFN_CONV) + bg[...]
        hv = _conv(xv, wv, FFN_CONV) + bv[...]
        dhg = d * hv * _dsilu(hg)
        dhv = d * _silu(hg)
        dbg[...] = jnp.sum(dhg, axis=0, keepdims=True)
        dbv[...] = jnp.sum(dhv, axis=0, keepdims=True)
        dg_ref[...] = _conv_bwd(xg, dhg, wg, dwg, FFN_CONV).astype(dg_ref.dtype)
        dv_ref[...] = _conv_bwd(xv, dhv, wv, dwv, FFN_CONV).astype(dv_ref.dtype)

    col = pl.BlockSpec((S, LANES), lambda j: (0, j))
    wsp = pl.BlockSpec((FFN_CONV, 1, LANES), lambda j: (0, 0, j))
    bsp = pl.BlockSpec((1, LANES), lambda j: (0, j))
    sd = jax.ShapeDtypeStruct
    return pl.pallas_call(body, name=name, grid=(Fp // LANES,), in_specs=[col, col, wsp, wsp, bsp, bsp, col],
                          out_specs=[col, col, wsp, wsp, bsp, bsp],
                          out_shape=[sd((S, Fp), bf16), sd((S, Fp), bf16), sd((FFN_CONV, 1, Fp), f32), sd((FFN_CONV, 1, Fp), f32),
                                     sd((1, Fp), f32), sd((1, Fp), f32)],
                          compiler_params=_params("parallel"))(upg, upv, cwg, cwv, cbg, cbv, dact)


def _gdn_pre_fwd(proj, cw, x_off, n_norm, name):
    S = proj.shape[0]
    C = cw.shape[2]

    def body(x_ref, w_ref, o_ref):
        s = _silu(_conv(x_ref[...], w_ref, GDN_CONV))
        r = lax.rsqrt(jnp.sum(s * s, axis=-1, keepdims=True) + RMS_EPS)
        o_ref[...] = jnp.where(pl.program_id(0) < n_norm, s * r, s)

    xs = pl.BlockSpec((S, LANES), lambda j: (0, x_off // LANES + j))
    col = pl.BlockSpec((S, LANES), lambda j: (0, j))
    wsp = pl.BlockSpec((GDN_CONV, 1, LANES), lambda j: (0, 0, j))
    return pl.pallas_call(body, name=name, grid=(C // LANES,), in_specs=[xs, wsp], out_specs=col,
                          out_shape=jax.ShapeDtypeStruct((S, C), f32), compiler_params=_params("parallel"))(proj, cw)


def _gdn_pre_bwd(proj, cw, dout, x_off, n_norm, name):
    S = proj.shape[0]
    C = cw.shape[2]

    def body(x_ref, w_ref, d_ref, dx_ref, dw_ref):
        x, d = x_ref[...], d_ref[...]
        z = _conv(x, w_ref, GDN_CONV)
        s = _silu(z)
        r = lax.rsqrt(jnp.sum(s * s, axis=-1, keepdims=True) + RMS_EPS)
        ds_norm = d * r - s * (r * r * r) * jnp.sum(d * s, axis=-1, keepdims=True)
        ds = jnp.where(pl.program_id(0) < n_norm, ds_norm, d)
        dz = ds * _dsilu(z)
        dx_ref[...] = _conv_bwd(x, dz, w_ref, dw_ref, GDN_CONV)

    xs = pl.BlockSpec((S, LANES), lambda j: (0, x_off // LANES + j))
    col = pl.BlockSpec((S, LANES), lambda j: (0, j))
    wsp = pl.BlockSpec((GDN_CONV, 1, LANES), lambda j: (0, 0, j))
    return pl.pallas_call(body, name=name, grid=(C // LANES,), in_specs=[xs, wsp, col], out_specs=[col, wsp],
                          out_shape=[jax.ShapeDtypeStruct((S, C), f32), jax.ShapeDtypeStruct((GDN_CONV, 1, C), f32)],
                          compiler_params=_params("parallel"))(proj, cw, dout)


def _bmm(a, b, prec=None):
    return lax.dot_general(a, b, (((2,), (1,)), ((0,), (0,))), precision=prec, preferred_element_type=f32)


def _bmm_nt(a, b, prec=None):
    return lax.dot_general(a, b, (((2,), (2,)), ((0,), (0,))), precision=prec, preferred_element_type=f32)


def _bmm_tn(a, b, prec=None):
    return lax.dot_general(a, b, (((1,), (1,)), ((0,), (0,))), precision=prec, preferred_element_type=f32)


def _softplus(x):
    return jnp.maximum(x, 0.0) + jnp.log1p(jnp.exp(-jnp.abs(x)))


@jax.custom_vjp
def _unit_lower_inverse(a):
    H, C, _ = a.shape
    r = lax.broadcasted_iota(jnp.int32, (H, C, C), 1)
    c = lax.broadcasted_iota(jnp.int32, (H, C, C), 2)
    p = -a
    inv = (r == c).astype(f32) + p
    for _ in range(int(math.log2(C)) - 1):
        p = _bmm(p, p, HIGHEST)
        inv = inv + _bmm(inv, p, HIGHEST)
    return inv


def _unit_lower_inverse_fwd(a):
    inv = _unit_lower_inverse(a)
    return inv, inv


def _unit_lower_inverse_bwd(inv, d_inv):
    return (-_bmm_nt(_bmm_tn(inv, d_inv, HIGHEST), inv, HIGHEST),)


_unit_lower_inverse.defvjp(_unit_lower_inverse_fwd, _unit_lower_inverse_bwd)


def _gdn_chunk(q, k, v, al, bl, gate, a_log, dt_bias, norm_g, state):
    H, C, Dh = q.shape
    r = lax.broadcasted_iota(jnp.int32, (H, C, C), 1)
    c = lax.broadcasted_iota(jnp.int32, (H, C, C), 2)
    tril = r >= c
    strict = r > c
    lower = tril.astype(f32)
    upper = (r <= c).astype(f32)
    ones = jnp.ones((H, C, C), f32)
    g = -jnp.exp(a_log) * _softplus(al + dt_bias)
    beta = jax.nn.sigmoid(bl)
    g_lanes = jnp.broadcast_to(g, (H, C, Dh))
    g_sq = jnp.broadcast_to(g, (H, C, C))
    gc = _bmm(lower, g_lanes, HIGHEST)
    gc_i = _bmm(lower, g_sq, HIGHEST)
    gc_j = _bmm(ones, g_sq * upper, HIGHEST)
    decay = jnp.where(tril, jnp.exp(jnp.where(tril, gc_i - gc_j, 0.0)), 0.0)
    qs = q * (Dh ** -0.5)
    kb = k * beta
    a_kk = jnp.where(strict, _bmm_nt(kb, k) * decay, 0.0)
    rhs_u = v * beta
    rhs_w = kb * jnp.exp(gc)
    inv = _unit_lower_inverse(a_kk)
    u = _bmm(inv, rhs_u, HIGHEST)
    w = _bmm(inv, rhs_w, HIGHEST)
    qk = jnp.where(tril, _bmm_nt(qs, k) * decay, 0.0)
    g_last = jnp.sum(g, axis=1, keepdims=True)
    k_dec = k * jnp.exp(g_last - gc)
    q_dec = qs * jnp.exp(gc)
    v_new = u - _bmm(w, state)
    o = _bmm(q_dec, state) + _bmm(qk, v_new)
    new_state = state * jnp.exp(g_last) + _bmm_tn(k_dec, v_new)
    y = o * lax.rsqrt(jnp.mean(o * o, axis=-1, keepdims=True) + RMS_EPS) * norm_g * _silu(gate)
    return y, new_state


def _heads(ref, off, H):
    return jnp.stack([ref[:, off + h * HEAD_DIM: off + (h + 1) * HEAD_DIM] for h in range(H)])


def _gdn_scan_fwd(qkvc, al, bl, proj, a_log, dt_bias, norm_g, gate_off, name, cargo=None):
    S = qkvc.shape[0]
    H = al.shape[0]
    W = H * HEAD_DIM
    n = S // CHUNK

    def body(*refs):
        (x_ref, al_ref, bl_ref, gate_ref, alog_ref, dt_ref, ng_ref), (y_ref, st_ref), (state,), hold = _split_refs(refs, 7, 2, cargo)
        _cargo_start(cargo, pl.program_id(0) == 0, hold)

        @pl.when(pl.program_id(0) == 0)
        def _():
            state[...] = jnp.zeros_like(state)

        st_ref[...] = state[...]
        y, new = _gdn_chunk(_heads(x_ref, 0, H), _heads(x_ref, W, H), _heads(x_ref, 2 * W, H), al_ref[...], bl_ref[...],
                            _heads(gate_ref, 0, H), alog_ref[...], dt_ref[...], ng_ref[...], state[...])
        state[...] = new
        for h in range(H):
            y_ref[:, h * HEAD_DIM:(h + 1) * HEAD_DIM] = y[h].astype(y_ref.dtype)
        _cargo_wait(cargo, pl.program_id(0) == n - 1, hold)

    sd = jax.ShapeDtypeStruct
    col = pl.BlockSpec((H, CHUNK, 1), lambda i: (0, i, 0))
    par = pl.BlockSpec((H, 1, 1), lambda i: (0, 0, 0))
    c_in, c_out, c_shape, c_sems = _cargo_specs(cargo)
    return pl.pallas_call(
        body, name=name, grid=(n,),
        in_specs=[pl.BlockSpec((CHUNK, 3 * W), lambda i: (i, 0)), col, col,
                  pl.BlockSpec((CHUNK, W), lambda i: (i, gate_off // W)), par, par,
                  pl.BlockSpec((1, 1, HEAD_DIM), lambda i: (0, 0, 0))] + c_in,
        out_specs=[pl.BlockSpec((CHUNK, W), lambda i: (i, 0)),
                   pl.BlockSpec((None, H, HEAD_DIM, HEAD_DIM), lambda i: (i, 0, 0, 0))] + c_out,
        out_shape=[sd((S, W), bf16), sd((n, H, HEAD_DIM, HEAD_DIM), f32)] + c_shape,
        scratch_shapes=[pltpu.VMEM((H, HEAD_DIM, HEAD_DIM), f32)] + c_sems,
        compiler_params=_params("arbitrary"))(qkvc, al, bl, proj, a_log, dt_bias, norm_g, *(cargo.arrays if cargo else ()))


def _gdn_scan_bwd(qkvc, al, bl, proj, a_log, dt_bias, norm_g, states, dy, gate_off, name, cargo=None):
    S = qkvc.shape[0]
    H = al.shape[0]
    W = H * HEAD_DIM
    n = S // CHUNK

    def body(*refs):
        ins, outs, (dstate,), hold = _split_refs(refs, 9, 7, cargo)
        x_ref, al_ref, bl_ref, gate_ref, alog_ref, dt_ref, ng_ref, st_ref, dy_ref = ins
        dx_ref, dal_ref, dbl_ref, dgate_ref, dalog_ref, ddt_ref, dng_ref = outs
        _cargo_start(cargo, pl.program_id(0) == 0, hold)

        @pl.when(pl.program_id(0) == 0)
        def _():
            dstate[...] = jnp.zeros_like(dstate)
            for ref in (dalog_ref, ddt_ref, dng_ref):
                ref[...] = jnp.zeros_like(ref)

        _, vjp = jax.vjp(_gdn_chunk, _heads(x_ref, 0, H), _heads(x_ref, W, H), _heads(x_ref, 2 * W, H), al_ref[...],
                         bl_ref[...], _heads(gate_ref, 0, H), alog_ref[...], dt_ref[...], ng_ref[...], st_ref[...])
        dq, dk, dv, dal, dbl, dgate, dalog, ddt, dng, dst = vjp((_heads(dy_ref, 0, H), dstate[...]))
        dstate[...] = dst
        for h in range(H):
            lo, hi = h * HEAD_DIM, (h + 1) * HEAD_DIM
            dx_ref[:, lo:hi] = dq[h]
            dx_ref[:, W + lo:W + hi] = dk[h]
            dx_ref[:, 2 * W + lo:2 * W + hi] = dv[h]
            dgate_ref[:, lo:hi] = dgate[h]
        dal_ref[...] = dal
        dbl_ref[...] = dbl
        dalog_ref[...] += dalog
        ddt_ref[...] += ddt
        dng_ref[...] += dng

        _cargo_wait(cargo, pl.program_id(0) == n - 1, hold)

    sd = jax.ShapeDtypeStruct
    c_in, c_out, c_shape, c_sems = _cargo_specs(cargo)
    rev = lambda i: n - 1 - i
    col = pl.BlockSpec((H, CHUNK, 1), lambda i: (0, rev(i), 0))
    par = pl.BlockSpec((H, 1, 1), lambda i: (0, 0, 0))
    ng = pl.BlockSpec((1, 1, HEAD_DIM), lambda i: (0, 0, 0))
    xs = pl.BlockSpec((CHUNK, 3 * W), lambda i: (rev(i), 0))
    ws = pl.BlockSpec((CHUNK, W), lambda i: (rev(i), 0))
    return pl.pallas_call(
        body, name=name, grid=(n,),
        in_specs=[xs, col, col, pl.BlockSpec((CHUNK, W), lambda i: (rev(i), gate_off // W)), par, par, ng,
                  pl.BlockSpec((None, H, HEAD_DIM, HEAD_DIM), lambda i: (rev(i), 0, 0, 0)), ws] + c_in,
        out_specs=[xs, col, col, ws, par, par, ng] + c_out,
        out_shape=[sd((S, 3 * W), f32), sd((H, S, 1), f32), sd((H, S, 1), f32), sd((S, W), f32), sd((H, 1, 1), f32),
                   sd((H, 1, 1), f32), sd((1, 1, HEAD_DIM), f32)] + c_shape,
        scratch_shapes=[pltpu.VMEM((H, HEAD_DIM, HEAD_DIM), f32)] + c_sems,
        compiler_params=_params("arbitrary"))(qkvc, al, bl, proj, a_log, dt_bias, norm_g, states, dy,
                                              *(cargo.arrays if cargo else ()))


def _tri(n, upper):
    r = lax.broadcasted_iota(jnp.int32, (n, n), 0)
    c = lax.broadcasted_iota(jnp.int32, (n, n), 1)
    return (r <= c if upper else r >= c).astype(f32)


def _fox_prep_fwd(tail, name):
    S = tail.shape[0]
    tb = _tile(S, 512)

    def body(x_ref, o_ref, carry):
        @pl.when(pl.program_id(0) == 0)
        def _():
            carry[...] = jnp.zeros_like(carry)

        x = x_ref[...]
        lf = jnp.minimum(x, 0.0) - jnp.log1p(jnp.exp(-jnp.abs(x)))
        o_ref[...] = jnp.dot(_tri(tb, False), lf, precision=HIGHEST, preferred_element_type=f32) + carry[...]
        carry[...] += jnp.sum(lf, axis=0, keepdims=True)

    blk = pl.BlockSpec((tb, LANES), lambda i: (i, 0))
    return pl.pallas_call(body, name=name, grid=(S // tb,), in_specs=[blk], out_specs=blk,
                          out_shape=jax.ShapeDtypeStruct((S, LANES), f32), scratch_shapes=[pltpu.VMEM((1, LANES), f32)],
                          compiler_params=_params("arbitrary"))(tail)


def _fox_prep_bwd(tail, dc_q, dc_k, name):
    S = tail.shape[0]
    tb = _tile(S, 512)
    nb = S // tb

    def body(x_ref, dq_ref, d_ref, o_ref, carry):
        @pl.when(pl.program_id(0) == 0)
        def _():
            carry[...] = jnp.zeros_like(carry)

        d = d_ref[...] + dq_ref[...]
        dlf = jnp.dot(_tri(tb, True), d, precision=HIGHEST, preferred_element_type=f32) + carry[...]
        carry[...] += jnp.sum(d, axis=0, keepdims=True)
        o_ref[...] = dlf * jax.nn.sigmoid(-x_ref[...])

    blk = pl.BlockSpec((tb, LANES), lambda i: (nb - 1 - i, 0))
    return pl.pallas_call(body, name=name, grid=(nb,), in_specs=[blk, blk, blk], out_specs=blk,
                          out_shape=jax.ShapeDtypeStruct((S, LANES), f32), scratch_shapes=[pltpu.VMEM((1, LANES), f32)],
                          compiler_params=_params("arbitrary"))(tail, dc_q, dc_k)


def _dot_nt(a, b):
    return lax.dot_general(a.astype(bf16), b.astype(bf16), _DIMS["nt"], preferred_element_type=f32)


def _dot_tn(a, b):
    return lax.dot_general(a.astype(bf16), b.astype(bf16), _DIMS["tn"], preferred_element_type=f32)


def _dot_nn(a, b):
    return lax.dot_general(a.astype(bf16), b.astype(bf16), _DIMS["nn"], preferred_element_type=f32)


def _fox_logits(q, k, cc, cr, T, diagonal):
    s = _dot_nt(q * (HEAD_DIM ** -0.5), k) + cc - cr
    if not diagonal:
        return s
    return jnp.where(lax.broadcasted_iota(jnp.int32, (T, T), 0) >= lax.broadcasted_iota(jnp.int32, (T, T), 1), s, NEG_BIG)


def _fox_fwd(proj, c_col, c_row, H, name, cargo=None):
    S = proj.shape[0]
    T = _tile(S, ATTN_TILE)
    nt = S // T

    def body(*refs):
        (q_ref, k_ref, v_ref, cc_ref, cr_ref), (o_ref, lse_ref), (m_s, l_s, acc_s), hold = _split_refs(refs, 5, 2, cargo)
        h, i, j = pl.program_id(0), pl.program_id(1), pl.program_id(2)
        _cargo_start(cargo, (h == 0) & (i == 0) & (j == 0), hold)

        @pl.when(j == 0)
        def _():
            m_s[...] = jnp.full_like(m_s, NEG_BIG)
            l_s[...] = jnp.zeros_like(l_s)
            acc_s[...] = jnp.zeros_like(acc_s)

        def block(diagonal):
            s = _fox_logits(q_ref[...], k_ref[...], cc_ref[...], cr_ref[...], T, diagonal)
            m_new = jnp.maximum(m_s[...], jnp.max(s, axis=-1, keepdims=True))
            p = jnp.exp(s - m_new)
            corr = jnp.exp(m_s[...] - m_new)
            l_s[...] = corr * l_s[...] + jnp.sum(p, axis=-1, keepdims=True)
            acc_s[...] = corr * acc_s[...] + _dot_nn(p, v_ref[...])
            m_s[...] = m_new

        @pl.when(j < i)
        def _():
            block(False)

        @pl.when(j == i)
        def _():
            block(True)
            o_ref[...] = acc_s[...] / l_s[...]
            lse_ref[...] = m_s[...] + jnp.log(l_s[...])

        _cargo_wait(cargo, (h == H - 1) & (i == nt - 1) & (j == nt - 1), hold)

    sd = jax.ShapeDtypeStruct
    c_in, c_out, c_shape, c_sems = _cargo_specs(cargo)
    sem = ("arbitrary",) * 3 if cargo else ("parallel", "parallel", "arbitrary")
    return pl.pallas_call(
        body, name=name, grid=(H, nt, nt),
        in_specs=[pl.BlockSpec((T, HEAD_DIM), lambda h, i, j: (i, h)),
                  pl.BlockSpec((T, HEAD_DIM), lambda h, i, j: (jnp.minimum(j, i), H + h)),
                  pl.BlockSpec((T, HEAD_DIM), lambda h, i, j: (jnp.minimum(j, i), 2 * H + h)),
                  pl.BlockSpec((None, T, 1), lambda h, i, j: (h, i, 0)),
                  pl.BlockSpec((None, 1, T), lambda h, i, j: (h, 0, jnp.minimum(j, i)))] + c_in,
        out_specs=[pl.BlockSpec((T, HEAD_DIM), lambda h, i, j: (i, h)), pl.BlockSpec((None, T, 1), lambda h, i, j: (h, i, 0))] + c_out,
        out_shape=[sd((S, H * HEAD_DIM), f32), sd((H, S, 1), f32)] + c_shape,
        scratch_shapes=[pltpu.VMEM((T, 1), f32), pltpu.VMEM((T, 1), f32), pltpu.VMEM((T, HEAD_DIM), f32)] + c_sems,
        compiler_params=_params(*sem, vmem=_attn_vmem(T)))(proj, proj, proj, c_col, c_row, *(cargo.arrays if cargo else ()))


def _fox_bwd_q(proj, c_col, c_row, o, do, lse, H, name):
    S = proj.shape[0]
    T = _tile(S, ATTN_TILE)
    nt = S // T

    def body(q_ref, k_ref, v_ref, cc_ref, cr_ref, o_ref, do_ref, lse_ref, dq_ref, dc_ref, acc_s, dc_s):
        i, j = pl.program_id(1), pl.program_id(2)

        @pl.when(j == 0)
        def _():
            acc_s[...] = jnp.zeros_like(acc_s)
            dc_s[...] = jnp.zeros_like(dc_s)

        def block(diagonal):
            s = _fox_logits(q_ref[...], k_ref[...], cc_ref[...], cr_ref[...], T, diagonal)
            p = jnp.exp(s - lse_ref[...])
            do_ = do_ref[...]
            delta = jnp.sum(o_ref[...] * do_, axis=-1, keepdims=True)
            ds = p * (_dot_nt(do_, v_ref[...]) - delta)
            acc_s[...] += _dot_nn(ds, k_ref[...])
            dc_s[...] += jnp.sum(ds, axis=-1, keepdims=True)

        @pl.when(j < i)
        def _():
            block(False)

        @pl.when(j == i)
        def _():
            block(True)
            dq_ref[...] = acc_s[...] * (HEAD_DIM ** -0.5)
            dc_ref[...] = dc_s[...]

    qb = pl.BlockSpec((T, HEAD_DIM), lambda h, i, j: (i, h))
    col = pl.BlockSpec((None, T, 1), lambda h, i, j: (h, i, 0))
    return pl.pallas_call(
        body, name=name, grid=(H, nt, nt),
        in_specs=[qb, pl.BlockSpec((T, HEAD_DIM), lambda h, i, j: (jnp.minimum(j, i), H + h)),
                  pl.BlockSpec((T, HEAD_DIM), lambda h, i, j: (jnp.minimum(j, i), 2 * H + h)),
                  col, pl.BlockSpec((None, 1, T), lambda h, i, j: (h, 0, jnp.minimum(j, i))), qb, qb, col],
        out_specs=[qb, col], out_shape=[jax.ShapeDtypeStruct((S, H * HEAD_DIM), f32), jax.ShapeDtypeStruct((H, S, 1), f32)],
        scratch_shapes=[pltpu.VMEM((T, HEAD_DIM), f32), pltpu.VMEM((T, 1), f32)],
        compiler_params=_params("parallel", "parallel", "arbitrary", vmem=_attn_vmem(T)))(proj, proj, proj, c_col, c_row, o, do, lse)


def _fox_bwd_kv(proj, c_col, c_row, o, do, lse, H, name):
    S = proj.shape[0]
    T = _tile(S, ATTN_TILE)
    nt = S // T

    def body(q_ref, k_ref, v_ref, cc_ref, cr_ref, o_ref, do_ref, lse_ref, dk_ref, dv_ref, dc_ref, dk_s, dv_s, dc_s):
        j, i = pl.program_id(1), pl.program_id(2)

        @pl.when(i == 0)
        def _():
            dk_s[...] = jnp.zeros_like(dk_s)
            dv_s[...] = jnp.zeros_like(dv_s)
            dc_s[...] = jnp.zeros_like(dc_s)

        def block(diagonal):
            s = _fox_logits(q_ref[...], k_ref[...], cc_ref[...], cr_ref[...], T, diagonal)
            p = jnp.exp(s - lse_ref[...])
            do_ = do_ref[...]
            delta = jnp.sum(o_ref[...] * do_, axis=-1, keepdims=True)
            ds = p * (_dot_nt(do_, v_ref[...]) - delta)
            dv_s[...] += _dot_tn(p, do_)
            dk_s[...] += _dot_tn(ds, q_ref[...])
            dc_s[...] -= jnp.sum(ds, axis=0, keepdims=True)

        @pl.when(i > j)
        def _():
            block(False)

        @pl.when(i == j)
        def _():
            block(True)

        @pl.when(i == nt - 1)
        def _():
            dk_ref[...] = dk_s[...] * (HEAD_DIM ** -0.5)
            dv_ref[...] = dv_s[...]
            dc_ref[...] = dc_s[...]

    qb = pl.BlockSpec((T, HEAD_DIM), lambda h, j, i: (jnp.maximum(i, j), h))
    col = pl.BlockSpec((None, T, 1), lambda h, j, i: (h, jnp.maximum(i, j), 0))
    kb = pl.BlockSpec((T, HEAD_DIM), lambda h, j, i: (j, h))
    sd = jax.ShapeDtypeStruct
    return pl.pallas_call(
        body, name=name, grid=(H, nt, nt),
        in_specs=[qb, pl.BlockSpec((T, HEAD_DIM), lambda h, j, i: (j, H + h)),
                  pl.BlockSpec((T, HEAD_DIM), lambda h, j, i: (j, 2 * H + h)),
                  col, pl.BlockSpec((None, 1, T), lambda h, j, i: (h, 0, j)), qb, qb, col],
        out_specs=[kb, kb, pl.BlockSpec((None, 1, T), lambda h, j, i: (h, 0, j))],
        out_shape=[sd((S, H * HEAD_DIM), f32), sd((S, H * HEAD_DIM), f32), sd((H, 1, S), f32)],
        scratch_shapes=[pltpu.VMEM((T, HEAD_DIM), f32), pltpu.VMEM((T, HEAD_DIM), f32), pltpu.VMEM((1, T), f32)],
        compiler_params=_params("parallel", "parallel", "arbitrary", vmem=_attn_vmem(T)))(proj, proj, proj, c_col, c_row, o, do, lse)


def _adamw(w, g, m, v, name):
    shape = w.shape
    cols = shape[-1]
    rows = w.size // cols
    ops = [t.reshape(rows, cols) for t in (w, g, m, v)]
    tr = rows
    if rows % 8 == 0:
        tr = 8
        while tr * 2 <= rows and rows % (tr * 2) == 0 and tr * 2 * cols * 4 <= (1 << 20):
            tr *= 2

    def body(w_ref, g_ref, m_ref, v_ref, d_ref, mo_ref, vo_ref):
        g_ = g_ref[...]
        m_ = ADAM_B1 * m_ref[...] + (1.0 - ADAM_B1) * g_
        v_ = ADAM_B2 * v_ref[...] + (1.0 - ADAM_B2) * (g_ * g_)
        m_hat = m_ / (1.0 - ADAM_B1 ** ADAM_STEP)
        v_hat = v_ / (1.0 - ADAM_B2 ** ADAM_STEP)
        d_ref[...] = -ADAM_LR * (m_hat / (jnp.sqrt(v_hat) + ADAM_EPS) + ADAM_WD * w_ref[...])
        mo_ref[...] = m_
        vo_ref[...] = v_

    blk = pl.BlockSpec((tr, cols), lambda i: (i, 0))
    outs = pl.pallas_call(body, name=name, grid=(rows // tr,), in_specs=[blk] * 4, out_specs=[blk] * 3,
                          out_shape=[jax.ShapeDtypeStruct((rows, cols), f32)] * 3, compiler_params=_params("parallel"))(*ops)
    return [o.reshape(shape) for o in outs]


def _row_tile(rows, row_bytes, budget=2 << 20):
    best = None
    for t in range(16, rows + 1, 16):
        if rows % t == 0 and t * row_bytes <= budget:
            best = t
    return best or rows


def _sum_leading(a, name):
    n, R, C = a.shape
    tr = _row_tile(R, n * C * 4)

    def body(a_ref, o_ref):
        acc = a_ref[0].astype(f32)
        for k in range(1, n):
            acc = acc + a_ref[k].astype(f32)
        o_ref[...] = acc

    return pl.pallas_call(body, name=name, grid=(R // tr,), in_specs=[pl.BlockSpec((n, tr, C), lambda i: (0, i, 0))],
                          out_specs=pl.BlockSpec((tr, C), lambda i: (i, 0)), out_shape=jax.ShapeDtypeStruct((R, C), f32),
                          compiler_params=_params("parallel"))(a)


def _add2(a, b, dtype, name):
    n, R, C = a.shape
    rows = n * R
    tr = _row_tile(rows, C * 4)

    def body(a_ref, b_ref, o_ref):
        o_ref[...] = (a_ref[...] + b_ref[...]).astype(dtype)

    blk = pl.BlockSpec((tr, C), lambda i: (i, 0))
    out = pl.pallas_call(body, name=name, grid=(rows // tr,), in_specs=[blk, blk], out_specs=blk,
                         out_shape=jax.ShapeDtypeStruct((rows, C), dtype),
                         compiler_params=_params("parallel"))(a.reshape(rows, C), b.reshape(rows, C))
    return out.reshape(n, R, C)


def _place():
    x, y, c = lax.axis_index("x"), lax.axis_index("y"), lax.axis_index("c")
    chips = [(1 - x, y), (x, 1 - y), (1 - x, 1 - y)]
    return x, y, c, chips


def _gather_copies(ws_refs, out_refs, send_sems, recv_sems):
    x, y, c, chips = _place()
    sends, lands = [], []
    for t, (ws_ref, out_ref) in enumerate(zip(ws_refs, out_refs)):
        for j, chip in enumerate(chips):
            make = functools.partial(pltpu.make_async_remote_copy, src_ref=ws_ref.at[c], send_sem=send_sems.at[3 * t + j],
                                     recv_sem=recv_sems.at[3 * t + j], device_id=(*chip, c), device_id_type=MESH)
            sends.append(functools.partial(make, dst_ref=out_ref.at[2 * x + y, c]))
            lands.append(functools.partial(make, dst_ref=out_ref.at[2 * chip[0] + chip[1], c]))
    return sends, lands


def _gather_cargo(blocks):
    return _Cargo(tuple(blocks), tuple(jax.ShapeDtypeStruct((N_CHIPS,) + b.shape, b.dtype) for b in blocks), _gather_copies,
                  3 * len(blocks))


def _sibling_forward(gathered, name):
    n = len(gathered)

    def body(*refs):
        ins, outs, send_sems, recv_sems = refs[:n], refs[n:2 * n], refs[2 * n], refs[2 * n + 1]
        x, y, c, chips = _place()
        sends, lands = [], []
        for t in range(n):
            for j, chip in enumerate(chips):
                s = 2 * chip[0] + chip[1]
                make = functools.partial(pltpu.make_async_remote_copy, src_ref=ins[t].at[s, c], send_sem=send_sems.at[3 * t + j],
                                         recv_sem=recv_sems.at[3 * t + j], device_id=(x, y, 1 - c), device_id_type=MESH)
                sends.append(make(dst_ref=outs[t].at[s, c]))
                lands.append(make(dst_ref=outs[t].at[s, 1 - c]))
        for cp in sends:
            cp.start()
        for cp in lands:
            cp.wait_recv()
        for cp in sends:
            cp.wait_send()

    return pl.pallas_call(body, name=name, out_shape=[jax.ShapeDtypeStruct(g.shape, g.dtype) for g in gathered],
                          in_specs=[ANY] * n, out_specs=[ANY] * n, input_output_aliases={t: t for t in range(n)},
                          scratch_shapes=[pltpu.SemaphoreType.DMA((3 * n,)), pltpu.SemaphoreType.DMA((3 * n,))])(*gathered)


def _pair_exchange(g, name):
    n, _, R, C = g.shape

    def body(g_ref, out_ref, send_sems, recv_sems):
        x, y, c, _ = _place()
        cps = [pltpu.make_async_remote_copy(src_ref=g_ref.at[s, 1 - c], dst_ref=out_ref.at[s], send_sem=send_sems.at[s],
                                            recv_sem=recv_sems.at[s], device_id=(x, y, 1 - c), device_id_type=MESH)
               for s in range(n)]
        for cp in cps:
            cp.start()
        for cp in cps:
            cp.wait()

    return pl.pallas_call(body, name=name, out_shape=jax.ShapeDtypeStruct((n, R, C), g.dtype), in_specs=[ANY], out_specs=ANY,
                          scratch_shapes=[pltpu.SemaphoreType.DMA((n,)), pltpu.SemaphoreType.DMA((n,))])(g)


def _chip_copies(a_refs, out_refs, send_sems, recv_sems):
    x, y, c, chips = _place()
    me = 2 * x + y
    sends, lands = [], []
    for t, (a_ref, out_ref) in enumerate(zip(a_refs, out_refs)):
        for j, chip in enumerate(chips):
            them = 2 * chip[0] + chip[1]
            make = functools.partial(pltpu.make_async_remote_copy, send_sem=send_sems.at[3 * t + j], recv_sem=recv_sems.at[3 * t + j],
                                     device_id=(*chip, c), device_id_type=MESH)
            sends.append(functools.partial(make, src_ref=a_ref.at[them], dst_ref=out_ref.at[me]))
            lands.append(functools.partial(make, src_ref=a_ref.at[me], dst_ref=out_ref.at[them]))
    return sends, lands


def _chip_cargo(pairs):
    return _Cargo(tuple(pairs), tuple(jax.ShapeDtypeStruct(p.shape, p.dtype) for p in pairs), _chip_copies, 3 * len(pairs))


def _pair_swap(r, name):
    R, C = r.shape

    def body(r_ref, out_ref, send_sem, recv_sem):
        x, y, c, _ = _place()
        cp = pltpu.make_async_remote_copy(src_ref=r_ref, dst_ref=out_ref, send_sem=send_sem, recv_sem=recv_sem,
                                          device_id=(x, y, 1 - c), device_id_type=MESH)
        cp.start()
        cp.wait()

    return pl.pallas_call(body, name=name, out_shape=jax.ShapeDtypeStruct((R, C), r.dtype), in_specs=[ANY], out_specs=ANY,
                          scratch_shapes=[pltpu.SemaphoreType.DMA(()), pltpu.SemaphoreType.DMA(())])(r)


def _all_gather8(v, name):
    R, C = v.shape

    def body(v_ref, out_ref, send_sems, recv_sems):
        x, y, c, _ = _place()
        me = 4 * x + 2 * y + c
        peers = [(x ^ (k >> 2), y ^ ((k >> 1) & 1), c ^ (k & 1)) for k in range(1, N_DEV)]
        sends = [pltpu.make_async_remote_copy(src_ref=v_ref, dst_ref=out_ref.at[me], send_sem=send_sems.at[k], recv_sem=recv_sems.at[k],
                                              device_id=peer, device_id_type=MESH) for k, peer in enumerate(peers)]
        for cp in sends:
            cp.start()
        for k, (px, py, pc) in enumerate(peers):
            pltpu.make_async_remote_copy(src_ref=v_ref, dst_ref=out_ref.at[4 * px + 2 * py + pc], send_sem=send_sems.at[k],
                                         recv_sem=recv_sems.at[k], device_id=(px, py, pc), device_id_type=MESH).wait_recv()
        for cp in sends:
            cp.wait_send()

    return pl.pallas_call(body, name=name, out_shape=jax.ShapeDtypeStruct((N_DEV, R, C), v.dtype), in_specs=[ANY], out_specs=ANY,
                          scratch_shapes=[pltpu.SemaphoreType.DMA((7,)), pltpu.SemaphoreType.DMA((7,))])(v)


def _as_list(x):
    return list(x) if isinstance(x, (list, tuple)) else [x]


def _put(buf, block, index):
    return lax.dynamic_update_slice(buf, block[None], (index,) + (0,) * block.ndim)


def _all_reduce8(v, device, name):
    n = v.shape[0]
    rows = -(-n // (LANES * SUM_ROWS)) * SUM_ROWS
    padded = jnp.pad(v, (0, rows * LANES - n)).reshape(rows, LANES)
    return _sum_leading(_put(_all_gather8(padded, name + "_gather"), padded, device), name + "_sum").reshape(-1)[:n]


def _halves(w):
    R, C = w.shape
    return w.astype(bf16).reshape(2, R // 2, C)


def _finish_gather(blocks, landed, chip):
    full = _sibling_forward(landed, "gather_sibling_forward")
    return [_put(g, b, chip).reshape(N_CHIPS, 2 * b.shape[1], b.shape[2]) for g, b in zip(full, blocks)]


def _pair_sums(g4, core, name):
    _, R, C = g4.shape
    g = g4.reshape(N_CHIPS, 2, R // 2, C)
    mine = lax.dynamic_index_in_dim(g, core, axis=1, keepdims=False)
    return _add2(mine, _pair_exchange(g, name + "_pair_exchange"), bf16, name + "_pair_sum")


def _finish_reduce(pair, landed, chip, core, name):
    _, R2, C = pair.shape
    own = lax.dynamic_index_in_dim(pair, chip, axis=0, keepdims=False)
    half = _sum_leading(_put(landed, own, chip), name + "_chip_sum")
    both = jnp.stack([half, _pair_swap(half, name + "_pair_swap")])
    return jnp.where(core == 0, both, both[::-1]).reshape(2 * R2, C)


def _segments(H, D):
    W = H * HEAD_DIM
    sizes = (3 * W, H, 2 * W, 3 * W, H, H, W, 3 * D)
    in_tail = (False, True, False, False, True, True, False, False)
    out, first, used = [], 0, [0, 0]
    for size, t in zip(sizes, in_tail):
        out.append((first, size, t, used[t]))
        first += size
        used[t] += size
    return out


def _main_tail_from_shards(g4, H, D):
    C = g4.shape[-1]
    parts = ([], [])
    for first, size, t, _ in _segments(H, D):
        for s in range(g4.shape[0]):
            a, b = max(first, s * C), min(first + size, (s + 1) * C)
            if a < b:
                parts[t].append(g4[s][..., a - s * C:b - s * C])
    parts[1].append(jnp.zeros(g4.shape[1:-1] + (LANES - 3 * H,), g4.dtype))
    return jnp.concatenate(parts[0], axis=-1), jnp.concatenate(parts[1], axis=-1)


def _shards_from_main_tail(main, tail, H, D):
    segs = _segments(H, D)
    C = sum(size for _, size, _, _ in segs) // N_CHIPS
    shards = []
    for s in range(N_CHIPS):
        pieces = []
        for first, size, t, there in segs:
            a, b = max(first, s * C), min(first + size, (s + 1) * C)
            if a < b:
                pieces.append((tail if t else main)[..., there + a - first:there + b - first])
        shards.append(jnp.concatenate(pieces, axis=-1))
    return jnp.stack(shards)


def _pad_cols(a, n):
    return jnp.pad(a, [(0, 0)] * (a.ndim - 1) + [(0, n - a.shape[-1])])


def kernel(x, w_in, b_in, sgu_ln_g, sgu_ln_b, sgu_w, sgu_b, gdn_conv_w, gdn_a_log, gdn_dt_bias, gdn_norm_g, w_proj_a, w_proj_b, w_proj_c, w_out, ln1_g, ln1_b, ffn_w_up, ffn_conv_w, ffn_conv_b, ffn_w_down, ln2_g, ln2_b, loss_target, m_w_in, m_b_in, m_sgu_ln_g, m_sgu_ln_b, m_sgu_w, m_sgu_b, m_gdn_conv_w, m_gdn_a_log, m_gdn_dt_bias, m_gdn_norm_g, m_w_proj_a, m_w_proj_b, m_w_proj_c, m_w_out, m_ln1_g, m_ln1_b, m_ffn_w_up, m_ffn_conv_w, m_ffn_conv_b, m_ffn_w_down, m_ln2_g, m_ln2_b, v_w_in, v_b_in, v_sgu_ln_g, v_sgu_ln_b, v_sgu_w, v_sgu_b, v_gdn_conv_w, v_gdn_a_log, v_gdn_dt_bias, v_gdn_norm_g, v_w_proj_a, v_w_proj_b, v_w_proj_c, v_w_out, v_ln1_g, v_ln1_b, v_ffn_w_up, v_ffn_conv_w, v_ffn_conv_b, v_ffn_w_down, v_ln2_g, v_ln2_b):
    P = dict(w_in=w_in, b_in=b_in, sgu_ln_g=sgu_ln_g, sgu_ln_b=sgu_ln_b, sgu_w=sgu_w, sgu_b=sgu_b, gdn_conv_w=gdn_conv_w,
             gdn_a_log=gdn_a_log, gdn_dt_bias=gdn_dt_bias, gdn_norm_g=gdn_norm_g, w_proj_a=w_proj_a, w_proj_b=w_proj_b,
             w_proj_c=w_proj_c, w_out=w_out, ln1_g=ln1_g, ln1_b=ln1_b, ffn_w_up=ffn_w_up, ffn_conv_w=ffn_conv_w,
             ffn_conv_b=ffn_conv_b, ffn_w_down=ffn_w_down, ln2_g=ln2_g, ln2_b=ln2_b)
    M1 = dict(w_in=m_w_in, b_in=m_b_in, sgu_ln_g=m_sgu_ln_g, sgu_ln_b=m_sgu_ln_b, sgu_w=m_sgu_w, sgu_b=m_sgu_b,
              gdn_conv_w=m_gdn_conv_w, gdn_a_log=m_gdn_a_log, gdn_dt_bias=m_gdn_dt_bias, gdn_norm_g=m_gdn_norm_g,
              w_proj_a=m_w_proj_a, w_proj_b=m_w_proj_b, w_proj_c=m_w_proj_c, w_out=m_w_out, ln1_g=m_ln1_g, ln1_b=m_ln1_b,
              ffn_w_up=m_ffn_w_up, ffn_conv_w=m_ffn_conv_w, ffn_conv_b=m_ffn_conv_b, ffn_w_down=m_ffn_w_down, ln2_g=m_ln2_g,
              ln2_b=m_ln2_b)
    M2 = dict(w_in=v_w_in, b_in=v_b_in, sgu_ln_g=v_sgu_ln_g, sgu_ln_b=v_sgu_ln_b, sgu_w=v_sgu_w, sgu_b=v_sgu_b,
              gdn_conv_w=v_gdn_conv_w, gdn_a_log=v_gdn_a_log, gdn_dt_bias=v_gdn_dt_bias, gdn_norm_g=v_gdn_norm_g,
              w_proj_a=v_w_proj_a, w_proj_b=v_w_proj_b, w_proj_c=v_w_proj_c, w_out=v_w_out, ln1_g=v_ln1_g, ln1_b=v_ln1_b,
              ffn_w_up=v_ffn_w_up, ffn_conv_w=v_ffn_conv_w, ffn_conv_b=v_ffn_conv_b, ffn_w_down=v_ffn_w_down, ln2_g=v_ln2_g,
              ln2_b=v_ln2_b)
    _, S, D = x.shape
    L = w_in.shape[0]
    N_IN = w_in.shape[2] * N_CHIPS
    H = (N_IN - 3 * D) // (9 * HEAD_DIM + 3)
    W = H * HEAD_DIM
    F = ffn_w_down.shape[1] * N_CHIPS
    Fp = -(-F // FF_ALIGN) * FF_ALIGN
    NM = 9 * W + 3 * D
    alpha = (2 * L) ** 0.25
    cx, cy, cc = lax.axis_index("x"), lax.axis_index("y"), lax.axis_index("c")
    chip = 2 * cx + cy

    def blocks_of(l):
        return [_halves(w_in[l]), _halves(jnp.concatenate([w_proj_a[l], w_proj_b[l], w_proj_c[l]], axis=0)),
                _halves(jnp.concatenate([w_out[l], ffn_w_down[l]], axis=0)), _halves(ffn_w_up[l])]

    def full_weights(blocks, landed):
        g_in, g_proj, g_rows, g_up = _finish_gather(blocks, landed, chip)
        w_main, w_tail = _main_tail_from_shards(g_in, H, D)
        wa, wb, wc = [g_proj[:, k * W:(k + 1) * W].transpose(1, 0, 2).reshape(W, D) for k in range(3)]
        return dict(w_main=w_main, w_tail=w_tail, wa=wa, wb=wb, wc=wc, wo=g_rows[:, :D // N_CHIPS].reshape(D, D),
                    wd=jnp.pad(g_rows[:, D // N_CHIPS:].reshape(F, D), ((0, Fp - F), (0, 0))),
                    wg=_pad_cols(jnp.concatenate([g_up[0], g_up[1]], axis=1), Fp),
                    wv=_pad_cols(jnp.concatenate([g_up[2], g_up[3]], axis=1), Fp))

    blocks = blocks_of(0)
    weights = full_weights(blocks, _ship(_gather_cargo(blocks), "gather_first_layer"))
    gcw_cols, fcw_cols = gdn_conv_w.shape[2], ffn_conv_w.shape[2]
    only_south = (cc == 0).astype(f32)
    placed_g = lax.dynamic_update_slice(jnp.zeros((L, GDN_CONV, 3 * W), f32), gdn_conv_w * only_south, (0, 0, chip * gcw_cols))
    placed_f = lax.dynamic_update_slice(jnp.zeros((L, FFN_CONV, 2 * F), f32), ffn_conv_w * only_south, (0, 0, chip * fcw_cols))
    conv_all = _all_reduce8(jnp.concatenate([placed_g.reshape(-1), placed_f.reshape(-1)]), 2 * chip + cc, "conv_weights")
    gcw_full = conv_all[:L * GDN_CONV * 3 * W].reshape(L, GDN_CONV, 1, 3 * W)
    fcw_full = conv_all[L * GDN_CONV * 3 * W:].reshape(L, FFN_CONV, 1, 2 * F)

    saved = []
    h = x.reshape(S, D)
    for l in range(L):
        b_main, b_tail = _main_tail_from_shards(b_in[l][None, None, :], H, D)
        cwg, cwv = _pad_cols(fcw_full[l][..., :F], Fp), _pad_cols(fcw_full[l][..., F:], Fp)
        cbg, cbv = _pad_cols(ffn_conv_b[l][None, :F], Fp), _pad_cols(ffn_conv_b[l][None, F:], Fp)
        nxt = blocks_of(l + 1) if l + 1 < L else None
        ride = (lambda *idx: _gather_cargo([nxt[k] for k in idx])) if nxt else (lambda *idx: None)
        lw = dict(weights, cwg=cwg, cwv=cwv, cbg=cbg, cbv=cbv, gcw=gcw_full[l],
                  sgu_ln_g=sgu_ln_g[l][None, :], sgu_ln_b=sgu_ln_b[l][None, :], sgu_w=sgu_w[l], sgu_b=sgu_b[l][:, :, None],
                  a_log=gdn_a_log[l].reshape(H, 1, 1), dt_bias=gdn_dt_bias[l].reshape(H, 1, 1),
                  norm_g=gdn_norm_g[l].reshape(1, 1, HEAD_DIM), ln1_g=ln1_g[l][None, :], ln1_b=ln1_b[l][None, :],
                  ln2_g=ln2_g[l][None, :], ln2_b=ln2_b[l][None, :])
        tag = "_carrying" if nxt else ""
        proj, *land_in = _as_list(_matmul(h, lw["w_main"], "nn", "proj_main" + tag, bias=b_main, cargo=ride(0)))
        tail = _matmul(h, lw["w_tail"], "nn", "proj_tail", bias=b_tail)
        csum = _fox_prep_fwd(tail, "fox_prep")
        c_col = csum[:, :H].T[:, :, None]
        c_row = csum[:, :H].T[:, None, :]
        y_a, lse, *land_up = _fox_fwd(proj, c_col, c_row, H, "fox_fwd" + tag, cargo=ride(3))
        y_b = _sgu_fwd(proj, lw["sgu_ln_g"], lw["sgu_ln_b"], lw["sgu_w"], lw["sgu_b"], 3 * W, 4 * W, "sgu_fwd")
        qkvc = _gdn_pre_fwd(proj, lw["gcw"], 5 * W, 2 * H, "gdn_pre")
        al = tail[:, H:2 * H].T[:, :, None]
        bl = tail[:, 2 * H:3 * H].T[:, :, None]
        y_c, states, *land_rest = _gdn_scan_fwd(qkvc, al, bl, proj, lw["a_log"], lw["dt_bias"], lw["norm_g"], 8 * W, "gdn_scan" + tag,
                                                cargo=ride(1, 2))
        if nxt:
            weights = full_weights(nxt, land_in + [land_rest[0], land_rest[1]] + land_up)
        pa = _matmul(y_a, lw["wa"], "nn", "branch_proj")
        pb = _matmul(y_b, lw["wb"], "nn", "branch_proj")
        pc = _matmul(y_c, lw["wc"], "nn", "branch_proj")
        merged = _merge_fwd(proj, pa, pb, pc, 9 * W, "merge")
        mix = _matmul(merged, lw["wo"], "nn", "out_proj")
        x1 = _ln_fwd(h, mix, lw["ln1_g"], lw["ln1_b"], alpha, "ln")
        upg = _matmul(x1, lw["wg"], "nn", "ffn_up")
        upv = _matmul(x1, lw["wv"], "nn", "ffn_up")
        act = _ffn_act_fwd(upg, upv, cwg, cwv, cbg, cbv, "ffn_act")
        ffn = _matmul(act, lw["wd"], "nn", "ffn_down")
        x2 = _ln_fwd(x1, ffn, lw["ln2_g"], lw["ln2_b"], alpha, "ln")
        saved.append(dict(lw=lw, h=h, proj=proj, tail=tail, c_col=c_col, c_row=c_row, y_a=y_a, lse=lse, y_b=y_b, qkvc=qkvc, al=al,
                          bl=bl, y_c=y_c, states=states, pa=pa, pb=pb, pc=pc, merged=merged, mix=mix, x1=x1, upg=upg, upv=upv,
                          act=act, ffn=ffn))
        h = x2

    loss_part, dh = _loss_head(h, loss_target.reshape(S, D), "loss_head")
    loss = lax.psum(loss_part[0, 0], ("x", "y", "c"))

    reduced = [dict() for _ in range(L)]
    small_grads = [None] * L
    pending = None
    for l in reversed(range(L)):
        s = saved[l]
        lw = s["lw"]
        d_x1r, d_ffn, d_ln2g, d_ln2b = _ln_bwd(s["x1"], s["ffn"], lw["ln2_g"], lw["ln2_b"], dh, alpha, "ln_bwd")
        d_act = _matmul(d_ffn, lw["wd"], "nt", "ffn_down_dx")
        d_wd = _matmul(s["act"], d_ffn, "tn", "ffn_down_dw")
        dupg, dupv, dcwg, dcwv, dcbg, dcbv = _ffn_act_bwd(s["upg"], s["upv"], lw["cwg"], lw["cwv"], lw["cbg"], lw["cbv"], d_act,
                                                          "ffn_act_bwd")
        d_x1 = _matmul(dupg, lw["wg"], "nt", "ffn_up_dx", add=d_x1r)
        d_x1 = _matmul(dupv, lw["wv"], "nt", "ffn_up_dx", add=d_x1)
        d_wg = _matmul(s["x1"], dupg, "tn", "ffn_up_dw")
        d_wv = _matmul(s["x1"], dupv, "tn", "ffn_up_dw")
        d_hr, d_mix, d_ln1g, d_ln1b = _ln_bwd(s["h"], s["mix"], lw["ln1_g"], lw["ln1_b"], d_x1, alpha, "ln_bwd")
        d_merged = _matmul(d_mix, lw["wo"], "nt", "out_proj_dx")
        d_wo = _matmul(s["merged"], d_mix, "tn", "out_proj_dw")
        dg0, dg1, dg2, d_pa, d_pb, d_pc = _merge_bwd(s["proj"], s["pa"], s["pb"], s["pc"], d_merged, 9 * W, "merge_bwd")
        d_ya = _matmul(d_pa, lw["wa"], "nt", "branch_proj_dx")
        d_yb = _matmul(d_pb, lw["wb"], "nt", "branch_proj_dx")
        d_yc = _matmul(d_pc, lw["wc"], "nt", "branch_proj_dx")
        d_wa = _matmul(s["y_a"], d_pa, "tn", "branch_proj_dw")
        d_wb = _matmul(s["y_b"], d_pb, "tn", "branch_proj_dw")
        d_wc = _matmul(s["y_c"], d_pc, "tn", "branch_proj_dw")
        tag = "_carrying" if pending else ""
        ride = (lambda *keys: _chip_cargo([pending[1][k] for k in keys])) if pending else (lambda *keys: None)
        dqkvc, dal, dbl, dgate, d_alog, d_dt, d_ng, *landed = _gdn_scan_bwd(
            s["qkvc"], s["al"], s["bl"], s["proj"], lw["a_log"], lw["dt_bias"], lw["norm_g"], s["states"], d_yc, 8 * W,
            "gdn_scan_bwd" + tag, cargo=ride("w_in", "proj"))
        for k, got in zip(("w_in", "proj"), landed):
            reduced[pending[0]][k] = _finish_reduce(pending[1][k], got, chip, cc, "grad_" + k)
        d_gqkv, d_gcw = _gdn_pre_bwd(s["proj"], lw["gcw"], dqkvc, 5 * W, 2 * H, "gdn_pre_bwd")
        d_u, d_v, d_slg, d_slb, d_sw, d_sb = _sgu_bwd(s["proj"], lw["sgu_ln_g"], lw["sgu_ln_b"], lw["sgu_w"], lw["sgu_b"], d_yb,
                                                      3 * W, 4 * W, "sgu_bwd")
        d_q, d_cq = _fox_bwd_q(s["proj"], s["c_col"], s["c_row"], s["y_a"], d_ya, s["lse"], H, "fox_bwd_q")
        d_k, d_v_att, d_c = _fox_bwd_kv(s["proj"], s["c_col"], s["c_row"], s["y_a"], d_ya, s["lse"], H, "fox_bwd_kv")
        d_f = _fox_prep_bwd(s["tail"], _pad_cols(d_cq[:, :, 0].T, LANES), _pad_cols(d_c[:, 0, :].T, LANES), "fox_prep_bwd")
        d_main = jnp.concatenate([d_q, d_k, d_v_att, d_u, d_v, d_gqkv, dgate, dg0, dg1, dg2], axis=1)
        d_tail = _pad_cols(jnp.concatenate([d_f[:, :H], dal[:, :, 0].T, dbl[:, :, 0].T], axis=1), LANES)
        d_wmain = _matmul(s["h"], d_main, "tn", "proj_main_dw")
        d_wtail = _matmul(s["h"], d_tail, "tn", "proj_tail_dw")
        d_bmain = _colsum(d_main, "proj_main_db")
        d_btail = _colsum(d_tail, "proj_tail_db")
        dh, *landed = _as_list(_matmul(d_main, lw["w_main"], "nt", "proj_main_dx" + tag, add=d_hr, cargo=ride("rows", "w_up")))
        for k, got in zip(("rows", "w_up"), landed):
            reduced[pending[0]][k] = _finish_reduce(pending[1][k], got, chip, cc, "grad_" + k)
        dh = _matmul(d_tail, lw["w_tail"], "nt", "proj_tail_dx", add=dh)
        by_cols = lambda g: g.reshape(g.shape[0], N_CHIPS, g.shape[1] // N_CHIPS).transpose(1, 0, 2)
        big = dict(
            w_in=_shards_from_main_tail(d_wmain, d_wtail, H, D),
            proj=jnp.concatenate([by_cols(d_wa), by_cols(d_wb), by_cols(d_wc)], axis=1),
            rows=jnp.concatenate([d_wo.reshape(N_CHIPS, D // N_CHIPS, D), d_wd[:F].reshape(N_CHIPS, F // N_CHIPS, D)], axis=1),
            w_up=jnp.stack([d_wg[:, :F // 2], d_wg[:, F // 2:F], d_wv[:, :F // 2], d_wv[:, F // 2:F]]))
        pending = (l, {k: _pair_sums(big[k], cc, "grad_" + k) for k in SHARDED})
        small_grads[l] = dict(b_in=_shards_from_main_tail(d_bmain, d_btail, H, D).reshape(-1), sgu_ln_g=d_slg[0], sgu_ln_b=d_slb[0], sgu_w=d_sw,
                              sgu_b=d_sb[:, :, 0], gdn_conv_w=d_gcw[:, 0, :], gdn_a_log=d_alog[:, 0, 0], gdn_dt_bias=d_dt[:, 0, 0],
                              gdn_norm_g=d_ng[0, 0], ln1_g=d_ln1g[0], ln1_b=d_ln1b[0],
                              ffn_conv_w=jnp.concatenate([dcwg[:, 0, :F], dcwv[:, 0, :F]], axis=1),
                              ffn_conv_b=jnp.concatenate([dcbg[0, :F], dcbv[0, :F]]), ln2_g=d_ln2g[0], ln2_b=d_ln2b[0])
    grad_x = dh.reshape(1, S, D)

    landed = _ship(_chip_cargo([pending[1][k] for k in SHARDED]), "grad_chip_exchange")
    reduced[pending[0]] = {k: _finish_reduce(pending[1][k], got, chip, cc, "grad_" + k) for k, got in zip(SHARDED, landed)}
    grads = {n: [None] * L for n in WEIGHTS}
    for l in range(L):
        grads["w_in"][l] = reduced[l]["w_in"]
        for k, n in enumerate(("w_proj_a", "w_proj_b", "w_proj_c")):
            grads[n][l] = reduced[l]["proj"][k * W:(k + 1) * W]
        grads["w_out"][l] = reduced[l]["rows"][:D // N_CHIPS]
        grads["ffn_w_down"][l] = reduced[l]["rows"][D // N_CHIPS:]
        grads["ffn_w_up"][l] = reduced[l]["w_up"]
    small_shapes = {n: small_grads[0][n].shape for n in SMALL}
    small_flat = jnp.concatenate([small_grads[l][n].reshape(-1) for l in range(L) for n in SMALL])
    small_sum = _all_reduce8(small_flat, 2 * chip + cc, "small_grads")
    off = 0
    for l in range(L):
        for n in SMALL:
            size = math.prod(small_shapes[n])
            g = small_sum[off:off + size].reshape(small_shapes[n])
            off += size
            if n == "gdn_conv_w":
                g = lax.dynamic_slice_in_dim(g, chip * gcw_cols, gcw_cols, axis=1)
            elif n == "ffn_conv_w":
                g = lax.dynamic_slice_in_dim(g, chip * fcw_cols, fcw_cols, axis=1)
            grads[n][l] = g
    grads = {n: jnp.stack(grads[n]) for n in WEIGHTS}

    deltas, new_m, new_v = {}, {}, {}
    for n in WEIGHTS:
        deltas[n], new_m[n], new_v[n] = _adamw(P[n], grads[n], M1[n], M2[n], "adamw_" + n)
    return (loss, grad_x, *[grads[n] for n in WEIGHTS], *[deltas[n] for n in WEIGHTS], *[new_m[n] for n in WEIGHTS],
            *[new_v[n] for n in WEIGHTS])
```

```python
import functools
import math
from typing import NamedTuple

import jax
import jax.numpy as jnp
from jax import lax
from jax.experimental import pallas as pl
from jax.experimental.pallas import tpu as pltpu

f32 = jnp.float32
bf16 = jnp.bfloat16
HIGHEST = lax.Precision.HIGHEST
MESH = pl.DeviceIdType.MESH

HEAD_DIM = 128
CHUNK = 64
SGU_SPAN = 128
GDN_CONV = 4
FFN_CONV = 3
N_CHIPS = 4
N_DEV = 8
LN_EPS = 1e-5
RMS_EPS = 1e-6
ADAM_LR = 0.001
ADAM_B1 = 0.9
ADAM_B2 = 0.999
ADAM_EPS = 1e-08
ADAM_WD = 0.01
ADAM_STEP = 10
NEG_BIG = -1e30
LANES = 128
FF_ALIGN = 512
SUM_ROWS = 256
ATTN_TILE = 1024
VMEM_MARGIN = 12 << 20
VMEM_MOST = 60 << 20

SHARDED = ("w_in", "proj", "rows", "w_up")
SMALL = ("b_in", "sgu_ln_g", "sgu_ln_b", "sgu_w", "sgu_b", "gdn_conv_w", "gdn_a_log", "gdn_dt_bias", "gdn_norm_g",
         "ln1_g", "ln1_b", "ffn_conv_w", "ffn_conv_b", "ln2_g", "ln2_b")
WEIGHTS = ("w_in", "b_in", "sgu_ln_g", "sgu_ln_b", "sgu_w", "sgu_b", "gdn_conv_w", "gdn_a_log", "gdn_dt_bias", "gdn_norm_g",
           "w_proj_a", "w_proj_b", "w_proj_c", "w_out", "ln1_g", "ln1_b", "ffn_w_up", "ffn_conv_w", "ffn_conv_b", "ffn_w_down",
           "ln2_g", "ln2_b")

ANY = pl.BlockSpec(memory_space=pl.ANY)


def _tile(dim, pref):
    t = pref
    while t > 128 and dim % t:
        t //= 2
    return min(t, dim) if dim % min(t, dim) == 0 else dim


def _params(*sem, vmem=None):
    if vmem is None:
        return pltpu.CompilerParams(dimension_semantics=sem)
    return pltpu.CompilerParams(dimension_semantics=sem, vmem_limit_bytes=min(vmem + VMEM_MARGIN, VMEM_MOST))


def _attn_vmem(T):
    return 8 * T * T * 4


_DIMS = {"nn": (((1,), (0,)), ((), ())), "nt": (((1,), (1,)), ((), ())), "tn": (((0,), (0,)), ((), ()))}


def _pick(dim, most):
    for unit in (256, LANES):
        for t in range(min(most, dim) // unit * unit, 0, -unit):
            if dim % t == 0:
                return t
    return dim


class _Cargo(NamedTuple):
    arrays: tuple
    landing: tuple
    copies: object
    n: int


def _cargo_specs(cargo):
    if cargo is None:
        return [], [], [], []
    sems = [pltpu.SemaphoreType.DMA((cargo.n,)), pltpu.SemaphoreType.DMA((cargo.n,))]
    return [ANY] * len(cargo.arrays), [ANY] * len(cargo.landing), list(cargo.landing), sems


def _split_refs(refs, n_in, n_out, cargo):
    na, nl, ns = (len(cargo.arrays), len(cargo.landing), 2) if cargo else (0, 0, 0)
    a, b, c = n_in + na, n_in + na + n_out, n_in + na + n_out + nl
    return refs[:n_in], refs[a:b], refs[c:len(refs) - ns], (refs[n_in:a], refs[b:c], refs[len(refs) - ns:])


def _cargo_start(cargo, first, hold):
    if cargo is not None:
        @pl.when(first)
        def _():
            for make in cargo.copies(hold[0], hold[1], *hold[2])[0]:
                make().start()


def _cargo_wait(cargo, last, hold):
    if cargo is not None:
        @pl.when(last)
        def _():
            sends, lands = cargo.copies(hold[0], hold[1], *hold[2])
            for make in lands:
                make().wait_recv()
            for make in sends:
                make().wait_send()


def _ship(cargo, name):
    in_specs, out_specs, out_shape, sems = _cargo_specs(cargo)

    def body(*refs):
        _, _, _, hold = _split_refs(refs, 0, 0, cargo)
        sends, lands = cargo.copies(hold[0], hold[1], *hold[2])
        started = [make() for make in sends]
        for cp in started:
            cp.start()
        for make in lands:
            make().wait_recv()
        for cp in started:
            cp.wait_send()

    return pl.pallas_call(body, name=name, out_shape=out_shape, in_specs=in_specs, out_specs=out_specs,
                          scratch_shapes=sems)(*cargo.arrays)


def _matmul(a, b, mode, name, bias=None, add=None, cargo=None):
    a, b = a.astype(bf16), b.astype(bf16)
    if mode == "nn":
        (M, K), (_, N) = a.shape, b.shape
    elif mode == "nt":
        (M, K), (N, _) = a.shape, b.shape
    else:
        (K, M), (_, N) = a.shape, b.shape
    has_bias, has_add = bias is not None, add is not None
    tn, tk = _pick(N, 2816), _pick(K, 512)

    def need(tm):
        return 2 * (tm * tk * a.dtype.itemsize + tk * tn * b.dtype.itemsize + tm * tn * 4 * (2 if has_add else 1)) + tm * tn * 4

    tm = _pick(M, 1024)
    if need(tm) + VMEM_MARGIN > VMEM_MOST:
        tm = _pick(M, 512)
    vmem = need(tm)
    ni, nj, nk = M // tm, N // tn, K // tk

    def body(*refs):
        ins, (o_ref,), (acc_ref,), hold = _split_refs(refs, 2 + has_bias + has_add, 1, cargo)
        a_ref, b_ref = ins[0], ins[1]
        bias_ref = ins[2] if has_bias else None
        add_ref = ins[2 + has_bias] if has_add else None
        i, j, k = pl.program_id(0), pl.program_id(1), pl.program_id(2)
        _cargo_start(cargo, (i == 0) & (j == 0) & (k == 0), hold)

        @pl.when(k == 0)
        def _():
            acc_ref[...] = jnp.zeros_like(acc_ref)

        acc_ref[...] += lax.dot_general(a_ref[...], b_ref[...], _DIMS[mode], preferred_element_type=f32)

        @pl.when(k == nk - 1)
        def _():
            r = acc_ref[...]
            if has_bias:
                r = r + bias_ref[...]
            if has_add:
                r = r + add_ref[...]
            o_ref[...] = r

        _cargo_wait(cargo, (i == ni - 1) & (j == nj - 1) & (k == nk - 1), hold)

    if mode == "nn":
        specs = [pl.BlockSpec((tm, tk), lambda i, j, k: (i, k)), pl.BlockSpec((tk, tn), lambda i, j, k: (k, j))]
    elif mode == "nt":
        specs = [pl.BlockSpec((tm, tk), lambda i, j, k: (i, k)), pl.BlockSpec((tn, tk), lambda i, j, k: (j, k))]
    else:
        specs = [pl.BlockSpec((tk, tm), lambda i, j, k: (k, i)), pl.BlockSpec((tk, tn), lambda i, j, k: (k, j))]
    ops = [a, b]
    if has_bias:
        specs.append(pl.BlockSpec((1, tn), lambda i, j, k: (0, j)))
        ops.append(bias)
    if has_add:
        specs.append(pl.BlockSpec((tm, tn), lambda i, j, k: (i, j)))
        ops.append(add)
    c_in, c_out, c_shape, c_sems = _cargo_specs(cargo)
    outs = pl.pallas_call(
        body, name=name, grid=(ni, nj, nk), in_specs=specs + c_in,
        out_specs=[pl.BlockSpec((tm, tn), lambda i, j, k: (i, j))] + c_out,
        out_shape=[jax.ShapeDtypeStruct((M, N), f32)] + c_shape, scratch_shapes=[pltpu.VMEM((tm, tn), f32)] + c_sems,
        compiler_params=pltpu.CompilerParams(
            dimension_semantics=("arbitrary",) * 3 if cargo else ("parallel", "parallel", "arbitrary"),
            vmem_limit_bytes=min(vmem + VMEM_MARGIN, VMEM_MOST)))(*ops, *(cargo.arrays if cargo else ()))
    return outs if cargo else outs[0]


def _colsum(a, name):
    S, N = a.shape
    tn = _tile(N, 512)

    def body(a_ref, o_ref):
        o_ref[...] = jnp.sum(a_ref[...], axis=0, keepdims=True)

    return pl.pallas_call(body, name=name, grid=(N // tn,), in_specs=[pl.BlockSpec((S, tn), lambda j: (0, j))],
                          out_specs=pl.BlockSpec((1, tn), lambda j: (0, j)), out_shape=jax.ShapeDtypeStruct((1, N), f32),
                          compiler_params=_params("parallel"))(a)


def _ln_fn(alpha, x, y, g, b):
    z = alpha * x + y
    mu = jnp.mean(z, axis=-1, keepdims=True)
    zc = z - mu
    var = jnp.mean(zc * zc, axis=-1, keepdims=True)
    return zc * lax.rsqrt(var + LN_EPS) * g + b


def _ln_fwd(x, y, g, b, alpha, name):
    S, D = x.shape
    tr = _tile(S, 256)

    def body(x_ref, y_ref, g_ref, b_ref, o_ref):
        o_ref[...] = _ln_fn(alpha, x_ref[...], y_ref[...], g_ref[...], b_ref[...])

    row = pl.BlockSpec((tr, D), lambda i: (i, 0))
    par = pl.BlockSpec((1, D), lambda i: (0, 0))
    return pl.pallas_call(body, name=name, grid=(S // tr,), in_specs=[row, row, par, par], out_specs=row,
                          out_shape=jax.ShapeDtypeStruct((S, D), f32), compiler_params=_params("parallel"))(x, y, g, b)


def _ln_bwd(x, y, g, b, dout, alpha, name):
    S, D = x.shape
    tr = _tile(S, 256)

    def body(x_ref, y_ref, g_ref, b_ref, d_ref, dx_ref, dy_ref, dg_ref, db_ref):
        _, vjp = jax.vjp(functools.partial(_ln_fn, alpha), x_ref[...], y_ref[...], g_ref[...], b_ref[...])
        dx, dy, dg, db = vjp(d_ref[...])
        dx_ref[...] = dx
        dy_ref[...] = dy.astype(dy_ref.dtype)

        @pl.when(pl.program_id(0) == 0)
        def _():
            dg_ref[...] = jnp.zeros_like(dg_ref)
            db_ref[...] = jnp.zeros_like(db_ref)

        dg_ref[...] += dg
        db_ref[...] += db

    row = pl.BlockSpec((tr, D), lambda i: (i, 0))
    par = pl.BlockSpec((1, D), lambda i: (0, 0))
    sd = jax.ShapeDtypeStruct
    return pl.pallas_call(body, name=name, grid=(S // tr,), in_specs=[row, row, par, par, row],
                          out_specs=[row, row, par, par],
                          out_shape=[sd((S, D), f32), sd((S, D), bf16), sd((1, D), f32), sd((1, D), f32)],
                          compiler_params=_params("arbitrary"))(x, y, g, b, dout)


def _loss_head(y, t, name):
    S, D = y.shape
    tr = _tile(S, 256)

    def body(y_ref, t_ref, l_ref, d_ref):
        e = y_ref[...] - t_ref[...]
        d_ref[...] = e / D

        @pl.when(pl.program_id(0) == 0)
        def _():
            l_ref[...] = jnp.zeros_like(l_ref)

        l_ref[...] += 0.5 * jnp.sum(jnp.mean(e * e, axis=-1, keepdims=True))

    row = pl.BlockSpec((tr, D), lambda i: (i, 0))
    return pl.pallas_call(body, name=name, grid=(S // tr,), in_specs=[row, row],
                          out_specs=[pl.BlockSpec((8, LANES), lambda i: (0, 0)), row],
                          out_shape=[jax.ShapeDtypeStruct((8, LANES), f32), jax.ShapeDtypeStruct((S, D), f32)],
                          compiler_params=_params("arbitrary"))(y, t)


def _merge_fn(g0, g1, g2, pa, pb, pc):
    return jax.nn.sigmoid(g0) * pa + jax.nn.sigmoid(g1) * pb + jax.nn.sigmoid(g2) * pc


def _merge_specs(S, D, gate_off):
    tr = _tile(S, 512)
    tc = _tile(math.gcd(gate_off, D), 512)
    gates = [pl.BlockSpec((tr, tc), functools.partial(lambda k, i, j: (i, (gate_off + k * D) // tc + j), k)) for k in range(3)]
    tile = pl.BlockSpec((tr, tc), lambda i, j: (i, j))
    return tr, tc, gates, tile


def _merge_fwd(proj, pa, pb, pc, gate_off, name):
    S, D = pa.shape
    tr, tc, gates, tile = _merge_specs(S, D, gate_off)

    def body(g0, g1, g2, a, b, c, o_ref):
        o_ref[...] = _merge_fn(g0[...], g1[...], g2[...], a[...], b[...], c[...]).astype(o_ref.dtype)

    return pl.pallas_call(body, name=name, grid=(S // tr, D // tc), in_specs=gates + [tile] * 3, out_specs=tile,
                          out_shape=jax.ShapeDtypeStruct((S, D), bf16),
                          compiler_params=_params("parallel", "parallel"))(proj, proj, proj, pa, pb, pc)


def _merge_bwd(proj, pa, pb, pc, dm, gate_off, name):
    S, D = pa.shape
    tr, tc, gates, tile = _merge_specs(S, D, gate_off)

    def body(g0, g1, g2, a, b, c, d, dg0, dg1, dg2, da, db, dc):
        _, vjp = jax.vjp(_merge_fn, g0[...], g1[...], g2[...], a[...], b[...], c[...])
        for ref, val in zip((dg0, dg1, dg2, da, db, dc), vjp(d[...])):
            ref[...] = val.astype(ref.dtype)

    sd = jax.ShapeDtypeStruct
    return pl.pallas_call(body, name=name, grid=(S // tr, D // tc), in_specs=gates + [tile] * 4,
                          out_specs=[tile] * 6, out_shape=[sd((S, D), f32)] * 3 + [sd((S, D), bf16)] * 3,
                          compiler_params=_params("parallel", "parallel"))(proj, proj, proj, pa, pb, pc, dm)


def _sgu_fn(nb, u, v, ln_g, ln_b, w_s, b_s):
    mu = jnp.mean(v, axis=-1, keepdims=True)
    vc = v - mu
    var = jnp.mean(vc * vc, axis=-1, keepdims=True)
    vn = vc * lax.rsqrt(var + LN_EPS) * ln_g + ln_b
    r = lax.broadcasted_iota(jnp.int32, (SGU_SPAN, SGU_SPAN), 0) // CHUNK
    c = lax.broadcasted_iota(jnp.int32, (SGU_SPAN, SGU_SPAN), 1) // CHUNK
    wm = jnp.where(r >= c, w_s, 0.0)
    vn3 = vn.reshape(nb, SGU_SPAN, HEAD_DIM)
    mixed = lax.dot_general(jnp.broadcast_to(wm, (nb, SGU_SPAN, SGU_SPAN)), vn3, (((2,), (1,)), ((0,), (0,))),
                            preferred_element_type=f32)
    mixed = mixed + b_s
    return u * mixed.reshape(nb * SGU_SPAN, HEAD_DIM)


def _sgu_specs(S, G, u_off, v_off):
    nb = max(1, min(8, S // SGU_SPAN))
    rows = nb * SGU_SPAN
    ub = pl.BlockSpec((rows, HEAD_DIM), lambda g, n: (n, u_off // HEAD_DIM + g))
    vb = pl.BlockSpec((rows, HEAD_DIM), lambda g, n: (n, v_off // HEAD_DIM + g))
    lnb = pl.BlockSpec((1, HEAD_DIM), lambda g, n: (0, g))
    wb = pl.BlockSpec((None, SGU_SPAN, SGU_SPAN), lambda g, n: (g, 0, 0))
    bb = pl.BlockSpec((None, SGU_SPAN, 1), lambda g, n: (g, 0, 0))
    return nb, rows, ub, vb, lnb, wb, bb


def _sgu_fwd(proj, ln_g, ln_b, w_s, b_s, u_off, v_off, name):
    S = proj.shape[0]
    G = w_s.shape[0]
    nb, rows, ub, vb, lnb, wb, bb = _sgu_specs(S, G, u_off, v_off)

    def body(u, v, lg, lb, w, b, o_ref):
        o_ref[...] = _sgu_fn(nb, u[...], v[...], lg[...], lb[...], w[...], b[...]).astype(o_ref.dtype)

    return pl.pallas_call(body, name=name, grid=(G, S // rows), in_specs=[ub, vb, lnb, lnb, wb, bb],
                          out_specs=pl.BlockSpec((rows, HEAD_DIM), lambda g, n: (n, g)),
                          out_shape=jax.ShapeDtypeStruct((S, G * HEAD_DIM), bf16),
                          compiler_params=_params("parallel", "parallel"))(proj, proj, ln_g, ln_b, w_s, b_s)


def _sgu_bwd(proj, ln_g, ln_b, w_s, b_s, dy, u_off, v_off, name):
    S = proj.shape[0]
    G = w_s.shape[0]
    nb, rows, ub, vb, lnb, wb, bb = _sgu_specs(S, G, u_off, v_off)

    def body(u, v, lg, lb, w, b, d, du, dv, dlg, dlb, dw, db):
        _, vjp = jax.vjp(functools.partial(_sgu_fn, nb), u[...], v[...], lg[...], lb[...], w[...], b[...])
        gu, gv, glg, glb, gw, gb = vjp(d[...])
        du[...] = gu
        dv[...] = gv

        @pl.when(pl.program_id(1) == 0)
        def _():
            for ref in (dlg, dlb, dw, db):
                ref[...] = jnp.zeros_like(ref)

        dlg[...] += glg
        dlb[...] += glb
        dw[...] += gw
        db[...] += gb

    tile = pl.BlockSpec((rows, HEAD_DIM), lambda g, n: (n, g))
    sd = jax.ShapeDtypeStruct
    W = G * HEAD_DIM
    return pl.pallas_call(body, name=name, grid=(G, S // rows), in_specs=[ub, vb, lnb, lnb, wb, bb, tile],
                          out_specs=[tile, tile, lnb, lnb, wb, bb],
                          out_shape=[sd((S, W), f32), sd((S, W), f32), sd((1, W), f32), sd((1, W), f32),
                                     sd((G, SGU_SPAN, SGU_SPAN), f32), sd((G, SGU_SPAN, 1), f32)],
                          compiler_params=_params("parallel", "arbitrary"))(proj, proj, ln_g, ln_b, w_s, b_s, dy)


def _shift_down(x, k):
    if k == 0:
        return x
    rows = lax.broadcasted_iota(jnp.int32, x.shape, 0)
    return jnp.where(rows >= k, pltpu.roll(x, k, 0), 0.0)


def _shift_up(x, k):
    if k == 0:
        return x
    n = x.shape[0]
    rows = lax.broadcasted_iota(jnp.int32, x.shape, 0)
    return jnp.where(rows < n - k, pltpu.roll(x, n - k, 0), 0.0)


def _conv(x, w_ref, width):
    out = w_ref[width - 1] * x
    for j in range(width - 1):
        out = out + w_ref[j] * _shift_down(x, width - 1 - j)
    return out


def _conv_bwd(x, dz, w_ref, dw_ref, width):
    dx = w_ref[width - 1] * dz
    dw_ref[width - 1] = jnp.sum(dz * x, axis=0, keepdims=True)
    for j in range(width - 1):
        k = width - 1 - j
        dx = dx + w_ref[j] * _shift_up(dz, k)
        dw_ref[j] = jnp.sum(dz * _shift_down(x, k), axis=0, keepdims=True)
    return dx


def _silu(z):
    return z * jax.nn.sigmoid(z)


def _silu_and_slope(z):
    s = jax.nn.sigmoid(z)
    return z * s, s * (1.0 + z * (1.0 - s))


def _ffn_act_fwd(upg, upv, cwg, cwv, cbg, cbv, name):
    S, Fp = upg.shape

    def body(g_ref, v_ref, wg, wv, bg, bv, o_ref):
        hg = _conv(g_ref[...], wg, FFN_CONV) + bg[...]
        hv = _conv(v_ref[...], wv, FFN_CONV) + bv[...]
        o_ref[...] = (_silu(hg) * hv).astype(o_ref.dtype)

    col = pl.BlockSpec((S, LANES), lambda j: (0, j))
    wsp = pl.BlockSpec((FFN_CONV, 1, LANES), lambda j: (0, 0, j))
    bsp = pl.BlockSpec((1, LANES), lambda j: (0, j))
    return pl.pallas_call(body, name=name, grid=(Fp // LANES,), in_specs=[col, col, wsp, wsp, bsp, bsp], out_specs=col,
                          out_shape=jax.ShapeDtypeStruct((S, Fp), bf16),
                          compiler_params=_params("parallel"))(upg, upv, cwg, cwv, cbg, cbv)


def _ffn_act_bwd(upg, upv, cwg, cwv, cbg, cbv, dact, name):
    S, Fp = upg.shape

    def body(g_ref, v_ref, wg, wv, bg, bv, d_ref, dg_ref, dv_ref, dwg, dwv, dbg, dbv):
        xg, xv, d = g_ref[...], v_ref[...], d_ref[...]
        hg = _conv(xg, wg, FFN_CONV) + bg[...]
        hv = _conv(xv, wv, FFN_CONV) + bv[...]
        act_g, slope_g = _silu_and_slope(hg)
        dhg = d * hv * slope_g
        dhv = d * act_g
        dbg[...] = jnp.sum(dhg, axis=0, keepdims=True)
        dbv[...] = jnp.sum(dhv, axis=0, keepdims=True)
        dg_ref[...] = _conv_bwd(xg, dhg, wg, dwg, FFN_CONV).astype(dg_ref.dtype)
        dv_ref[...] = _conv_bwd(xv, dhv, wv, dwv, FFN_CONV).astype(dv_ref.dtype)

    col = pl.BlockSpec((S, LANES), lambda j: (0, j))
    wsp = pl.BlockSpec((FFN_CONV, 1, LANES), lambda j: (0, 0, j))
    bsp = pl.BlockSpec((1, LANES), lambda j: (0, j))
    sd = jax.ShapeDtypeStruct
    return pl.pallas_call(body, name=name, grid=(Fp // LANES,), in_specs=[col, col, wsp, wsp, bsp, bsp, col],
                          out_specs=[col, col, wsp, wsp, bsp, bsp],
                          out_shape=[sd((S, Fp), bf16), sd((S, Fp), bf16), sd((FFN_CONV, 1, Fp), f32), sd((FFN_CONV, 1, Fp), f32),
                                     sd((1, Fp), f32), sd((1, Fp), f32)],
                          compiler_params=_params("parallel"))(upg, upv, cwg, cwv, cbg, cbv, dact)


def _gdn_pre_fwd(proj, cw, x_off, n_norm, name):
    S = proj.shape[0]
    C = cw.shape[2]

    def body(x_ref, w_ref, o_ref):
        s = _silu(_conv(x_ref[...], w_ref, GDN_CONV))
        r = lax.rsqrt(jnp.sum(s * s, axis=-1, keepdims=True) + RMS_EPS)
        o_ref[...] = jnp.where(pl.program_id(0) < n_norm, s * r, s)

    xs = pl.BlockSpec((S, LANES), lambda j: (0, x_off // LANES + j))
    col = pl.BlockSpec((S, LANES), lambda j: (0, j))
    wsp = pl.BlockSpec((GDN_CONV, 1, LANES), lambda j: (0, 0, j))
    return pl.pallas_call(body, name=name, grid=(C // LANES,), in_specs=[xs, wsp], out_specs=col,
                          out_shape=jax.ShapeDtypeStruct((S, C), f32), compiler_params=_params("parallel"))(proj, cw)


def _gdn_pre_bwd(proj, cw, dout, x_off, n_norm, name):
    S = proj.shape[0]
    C = cw.shape[2]

    def body(x_ref, w_ref, d_ref, dx_ref, dw_ref):
        x, d = x_ref[...], d_ref[...]
        z = _conv(x, w_ref, GDN_CONV)
        s, slope = _silu_and_slope(z)
        r = lax.rsqrt(jnp.sum(s * s, axis=-1, keepdims=True) + RMS_EPS)
        ds_norm = d * r - s * (r * r * r) * jnp.sum(d * s, axis=-1, keepdims=True)
        ds = jnp.where(pl.program_id(0) < n_norm, ds_norm, d)
        dz = ds * slope
        dx_ref[...] = _conv_bwd(x, dz, w_ref, dw_ref, GDN_CONV)

    xs = pl.BlockSpec((S, LANES), lambda j: (0, x_off // LANES + j))
    col = pl.BlockSpec((S, LANES), lambda j: (0, j))
    wsp = pl.BlockSpec((GDN_CONV, 1, LANES), lambda j: (0, 0, j))
    return pl.pallas_call(body, name=name, grid=(C // LANES,), in_specs=[xs, wsp, col], out_specs=[col, wsp],
                          out_shape=[jax.ShapeDtypeStruct((S, C), f32), jax.ShapeDtypeStruct((GDN_CONV, 1, C), f32)],
                          compiler_params=_params("parallel"))(proj, cw, dout)


def _bmm(a, b, prec=None):
    return lax.dot_general(a, b, (((2,), (1,)), ((0,), (0,))), precision=prec, preferred_element_type=f32)


def _bmm_nt(a, b, prec=None):
    return lax.dot_general(a, b, (((2,), (2,)), ((0,), (0,))), precision=prec, preferred_element_type=f32)


def _bmm_tn(a, b, prec=None):
    return lax.dot_general(a, b, (((1,), (1,)), ((0,), (0,))), precision=prec, preferred_element_type=f32)


def _softplus(x):
    return jnp.maximum(x, 0.0) + jnp.log1p(jnp.exp(-jnp.abs(x)))


@jax.custom_vjp
def _unit_lower_inverse(a):
    H, C, _ = a.shape
    r = lax.broadcasted_iota(jnp.int32, (H, C, C), 1)
    c = lax.broadcasted_iota(jnp.int32, (H, C, C), 2)
    p = -a
    inv = (r == c).astype(f32) + p
    for _ in range(int(math.log2(C)) - 1):
        p = _bmm(p, p, HIGHEST)
        inv = inv + _bmm(inv, p, HIGHEST)
    return inv


def _unit_lower_inverse_fwd(a):
    inv = _unit_lower_inverse(a)
    return inv, inv


def _unit_lower_inverse_bwd(inv, d_inv):
    return (-_bmm_nt(_bmm_tn(inv, d_inv, HIGHEST), inv, HIGHEST),)


_unit_lower_inverse.defvjp(_unit_lower_inverse_fwd, _unit_lower_inverse_bwd)


def _gdn_chunk(q, k, v, al, bl, gate, a_log, dt_bias, norm_g, state):
    H, C, Dh = q.shape
    r = lax.broadcasted_iota(jnp.int32, (H, C, C), 1)
    c = lax.broadcasted_iota(jnp.int32, (H, C, C), 2)
    tril = r >= c
    strict = r > c
    lower = tril.astype(f32)
    upper = (r <= c).astype(f32)
    ones = jnp.ones((H, C, C), f32)
    g = -jnp.exp(a_log) * _softplus(al + dt_bias)
    beta = jax.nn.sigmoid(bl)
    g_lanes = jnp.broadcast_to(g, (H, C, Dh))
    g_sq = jnp.broadcast_to(g, (H, C, C))
    gc = _bmm(lower, g_lanes, HIGHEST)
    gc_i = _bmm(lower, g_sq, HIGHEST)
    gc_j = _bmm(ones, g_sq * upper, HIGHEST)
    decay = jnp.where(tril, jnp.exp(jnp.where(tril, gc_i - gc_j, 0.0)), 0.0)
    qs = q * (Dh ** -0.5)
    kb = k * beta
    a_kk = jnp.where(strict, _bmm_nt(kb, k) * decay, 0.0)
    rhs_u = v * beta
    rhs_w = kb * jnp.exp(gc)
    inv = _unit_lower_inverse(a_kk)
    u = _bmm(inv, rhs_u, HIGHEST)
    w = _bmm(inv, rhs_w, HIGHEST)
    qk = jnp.where(tril, _bmm_nt(qs, k) * decay, 0.0)
    g_last = jnp.sum(g, axis=1, keepdims=True)
    k_dec = k * jnp.exp(g_last - gc)
    q_dec = qs * jnp.exp(gc)
    v_new = u - _bmm(w, state)
    o = _bmm(q_dec, state) + _bmm(qk, v_new)
    new_state = state * jnp.exp(g_last) + _bmm_tn(k_dec, v_new)
    y = o * lax.rsqrt(jnp.mean(o * o, axis=-1, keepdims=True) + RMS_EPS) * norm_g * _silu(gate)
    return y, new_state


def _heads(ref, off, H):
    return jnp.stack([ref[:, off + h * HEAD_DIM: off + (h + 1) * HEAD_DIM] for h in range(H)])


def _gdn_scan_fwd(qkvc, al, bl, proj, a_log, dt_bias, norm_g, gate_off, name, cargo=None):
    S = qkvc.shape[0]
    H = al.shape[0]
    W = H * HEAD_DIM
    n = S // CHUNK

    def body(*refs):
        (x_ref, al_ref, bl_ref, gate_ref, alog_ref, dt_ref, ng_ref), (y_ref, st_ref), (state,), hold = _split_refs(refs, 7, 2, cargo)
        _cargo_start(cargo, pl.program_id(0) == 0, hold)

        @pl.when(pl.program_id(0) == 0)
        def _():
            state[...] = jnp.zeros_like(state)

        st_ref[...] = state[...]
        y, new = _gdn_chunk(_heads(x_ref, 0, H), _heads(x_ref, W, H), _heads(x_ref, 2 * W, H), al_ref[...], bl_ref[...],
                            _heads(gate_ref, 0, H), alog_ref[...], dt_ref[...], ng_ref[...], state[...])
        state[...] = new
        for h in range(H):
            y_ref[:, h * HEAD_DIM:(h + 1) * HEAD_DIM] = y[h].astype(y_ref.dtype)
        _cargo_wait(cargo, pl.program_id(0) == n - 1, hold)

    sd = jax.ShapeDtypeStruct
    col = pl.BlockSpec((H, CHUNK, 1), lambda i: (0, i, 0))
    par = pl.BlockSpec((H, 1, 1), lambda i: (0, 0, 0))
    c_in, c_out, c_shape, c_sems = _cargo_specs(cargo)
    return pl.pallas_call(
        body, name=name, grid=(n,),
        in_specs=[pl.BlockSpec((CHUNK, 3 * W), lambda i: (i, 0)), col, col,
                  pl.BlockSpec((CHUNK, W), lambda i: (i, gate_off // W)), par, par,
                  pl.BlockSpec((1, 1, HEAD_DIM), lambda i: (0, 0, 0))] + c_in,
        out_specs=[pl.BlockSpec((CHUNK, W), lambda i: (i, 0)),
                   pl.BlockSpec((None, H, HEAD_DIM, HEAD_DIM), lambda i: (i, 0, 0, 0))] + c_out,
        out_shape=[sd((S, W), bf16), sd((n, H, HEAD_DIM, HEAD_DIM), f32)] + c_shape,
        scratch_shapes=[pltpu.VMEM((H, HEAD_DIM, HEAD_DIM), f32)] + c_sems,
        compiler_params=_params("arbitrary"))(qkvc, al, bl, proj, a_log, dt_bias, norm_g, *(cargo.arrays if cargo else ()))


def _gdn_scan_bwd(qkvc, al, bl, proj, a_log, dt_bias, norm_g, states, dy, gate_off, name, cargo=None):
    S = qkvc.shape[0]
    H = al.shape[0]
    W = H * HEAD_DIM
    n = S // CHUNK

    def body(*refs):
        ins, outs, (dstate,), hold = _split_refs(refs, 9, 7, cargo)
        x_ref, al_ref, bl_ref, gate_ref, alog_ref, dt_ref, ng_ref, st_ref, dy_ref = ins
        dx_ref, dal_ref, dbl_ref, dgate_ref, dalog_ref, ddt_ref, dng_ref = outs
        _cargo_start(cargo, pl.program_id(0) == 0, hold)

        @pl.when(pl.program_id(0) == 0)
        def _():
            dstate[...] = jnp.zeros_like(dstate)
            for ref in (dalog_ref, ddt_ref, dng_ref):
                ref[...] = jnp.zeros_like(ref)

        _, vjp = jax.vjp(_gdn_chunk, _heads(x_ref, 0, H), _heads(x_ref, W, H), _heads(x_ref, 2 * W, H), al_ref[...],
                         bl_ref[...], _heads(gate_ref, 0, H), alog_ref[...], dt_ref[...], ng_ref[...], st_ref[...])
        dq, dk, dv, dal, dbl, dgate, dalog, ddt, dng, dst = vjp((_heads(dy_ref, 0, H), dstate[...]))
        dstate[...] = dst
        for h in range(H):
            lo, hi = h * HEAD_DIM, (h + 1) * HEAD_DIM
            dx_ref[:, lo:hi] = dq[h]
            dx_ref[:, W + lo:W + hi] = dk[h]
            dx_ref[:, 2 * W + lo:2 * W + hi] = dv[h]
            dgate_ref[:, lo:hi] = dgate[h]
        dal_ref[...] = dal
        dbl_ref[...] = dbl
        dalog_ref[...] += dalog
        ddt_ref[...] += ddt
        dng_ref[...] += dng

        _cargo_wait(cargo, pl.program_id(0) == n - 1, hold)

    sd = jax.ShapeDtypeStruct
    c_in, c_out, c_shape, c_sems = _cargo_specs(cargo)
    rev = lambda i: n - 1 - i
    col = pl.BlockSpec((H, CHUNK, 1), lambda i: (0, rev(i), 0))
    par = pl.BlockSpec((H, 1, 1), lambda i: (0, 0, 0))
    ng = pl.BlockSpec((1, 1, HEAD_DIM), lambda i: (0, 0, 0))
    xs = pl.BlockSpec((CHUNK, 3 * W), lambda i: (rev(i), 0))
    ws = pl.BlockSpec((CHUNK, W), lambda i: (rev(i), 0))
    return pl.pallas_call(
        body, name=name, grid=(n,),
        in_specs=[xs, col, col, pl.BlockSpec((CHUNK, W), lambda i: (rev(i), gate_off // W)), par, par, ng,
                  pl.BlockSpec((None, H, HEAD_DIM, HEAD_DIM), lambda i: (rev(i), 0, 0, 0)), ws] + c_in,
        out_specs=[xs, col, col, ws, par, par, ng] + c_out,
        out_shape=[sd((S, 3 * W), f32), sd((H, S, 1), f32), sd((H, S, 1), f32), sd((S, W), f32), sd((H, 1, 1), f32),
                   sd((H, 1, 1), f32), sd((1, 1, HEAD_DIM), f32)] + c_shape,
        scratch_shapes=[pltpu.VMEM((H, HEAD_DIM, HEAD_DIM), f32)] + c_sems,
        compiler_params=_params("arbitrary"))(qkvc, al, bl, proj, a_log, dt_bias, norm_g, states, dy,
                                              *(cargo.arrays if cargo else ()))


def _tri(n, upper):
    r = lax.broadcasted_iota(jnp.int32, (n, n), 0)
    c = lax.broadcasted_iota(jnp.int32, (n, n), 1)
    return (r <= c if upper else r >= c).astype(f32)


def _fox_prep_fwd(tail, name):
    S = tail.shape[0]
    tb = _tile(S, 512)

    def body(x_ref, o_ref, carry):
        @pl.when(pl.program_id(0) == 0)
        def _():
            carry[...] = jnp.zeros_like(carry)

        x = x_ref[...]
        lf = jnp.minimum(x, 0.0) - jnp.log1p(jnp.exp(-jnp.abs(x)))
        o_ref[...] = jnp.dot(_tri(tb, False), lf, precision=HIGHEST, preferred_element_type=f32) + carry[...]
        carry[...] += jnp.sum(lf, axis=0, keepdims=True)

    blk = pl.BlockSpec((tb, LANES), lambda i: (i, 0))
    return pl.pallas_call(body, name=name, grid=(S // tb,), in_specs=[blk], out_specs=blk,
                          out_shape=jax.ShapeDtypeStruct((S, LANES), f32), scratch_shapes=[pltpu.VMEM((1, LANES), f32)],
                          compiler_params=_params("arbitrary"))(tail)


def _fox_prep_bwd(tail, dc_q, dc_k, name):
    S = tail.shape[0]
    tb = _tile(S, 512)
    nb = S // tb

    def body(x_ref, dq_ref, d_ref, o_ref, carry):
        @pl.when(pl.program_id(0) == 0)
        def _():
            carry[...] = jnp.zeros_like(carry)

        d = d_ref[...] + dq_ref[...]
        dlf = jnp.dot(_tri(tb, True), d, precision=HIGHEST, preferred_element_type=f32) + carry[...]
        carry[...] += jnp.sum(d, axis=0, keepdims=True)
        o_ref[...] = dlf * jax.nn.sigmoid(-x_ref[...])

    blk = pl.BlockSpec((tb, LANES), lambda i: (nb - 1 - i, 0))
    return pl.pallas_call(body, name=name, grid=(nb,), in_specs=[blk, blk, blk], out_specs=blk,
                          out_shape=jax.ShapeDtypeStruct((S, LANES), f32), scratch_shapes=[pltpu.VMEM((1, LANES), f32)],
                          compiler_params=_params("arbitrary"))(tail, dc_q, dc_k)


def _dot_nt(a, b):
    return lax.dot_general(a.astype(bf16), b.astype(bf16), _DIMS["nt"], preferred_element_type=f32)


def _dot_tn(a, b):
    return lax.dot_general(a.astype(bf16), b.astype(bf16), _DIMS["tn"], preferred_element_type=f32)


def _dot_nn(a, b):
    return lax.dot_general(a.astype(bf16), b.astype(bf16), _DIMS["nn"], preferred_element_type=f32)


def _fox_logits(q, k, cc, cr, T, diagonal):
    s = _dot_nt(q * (HEAD_DIM ** -0.5), k) + cc - cr
    if not diagonal:
        return s
    return jnp.where(lax.broadcasted_iota(jnp.int32, (T, T), 0) >= lax.broadcasted_iota(jnp.int32, (T, T), 1), s, NEG_BIG)


def _fox_fwd(proj, c_col, c_row, H, name, cargo=None):
    S = proj.shape[0]
    T = _tile(S, ATTN_TILE)
    nt = S // T

    def body(*refs):
        (q_ref, k_ref, v_ref, cc_ref, cr_ref), (o_ref, lse_ref), (m_s, l_s, acc_s), hold = _split_refs(refs, 5, 2, cargo)
        h, i, j = pl.program_id(0), pl.program_id(1), pl.program_id(2)
        _cargo_start(cargo, (h == 0) & (i == 0) & (j == 0), hold)

        @pl.when(j == 0)
        def _():
            m_s[...] = jnp.full_like(m_s, NEG_BIG)
            l_s[...] = jnp.zeros_like(l_s)
            acc_s[...] = jnp.zeros_like(acc_s)

        def block(diagonal):
            s = _fox_logits(q_ref[...], k_ref[...], cc_ref[...], cr_ref[...], T, diagonal)
            m_new = jnp.maximum(m_s[...], jnp.max(s, axis=-1, keepdims=True))
            p = jnp.exp(s - m_new)
            corr = jnp.exp(m_s[...] - m_new)
            l_s[...] = corr * l_s[...] + jnp.sum(p, axis=-1, keepdims=True)
            acc_s[...] = corr * acc_s[...] + _dot_nn(p, v_ref[...])
            m_s[...] = m_new

        @pl.when(j < i)
        def _():
            block(False)

        @pl.when(j == i)
        def _():
            block(True)
            o_ref[...] = acc_s[...] / l_s[...]
            lse_ref[...] = m_s[...] + jnp.log(l_s[...])

        _cargo_wait(cargo, (h == H - 1) & (i == nt - 1) & (j == nt - 1), hold)

    sd = jax.ShapeDtypeStruct
    c_in, c_out, c_shape, c_sems = _cargo_specs(cargo)
    sem = ("arbitrary",) * 3 if cargo else ("parallel", "parallel", "arbitrary")
    return pl.pallas_call(
        body, name=name, grid=(H, nt, nt),
        in_specs=[pl.BlockSpec((T, HEAD_DIM), lambda h, i, j: (i, h)),
                  pl.BlockSpec((T, HEAD_DIM), lambda h, i, j: (jnp.minimum(j, i), H + h)),
                  pl.BlockSpec((T, HEAD_DIM), lambda h, i, j: (jnp.minimum(j, i), 2 * H + h)),
                  pl.BlockSpec((None, T, 1), lambda h, i, j: (h, i, 0)),
                  pl.BlockSpec((None, 1, T), lambda h, i, j: (h, 0, jnp.minimum(j, i)))] + c_in,
        out_specs=[pl.BlockSpec((T, HEAD_DIM), lambda h, i, j: (i, h)), pl.BlockSpec((None, T, 1), lambda h, i, j: (h, i, 0))] + c_out,
        out_shape=[sd((S, H * HEAD_DIM), f32), sd((H, S, 1), f32)] + c_shape,
        scratch_shapes=[pltpu.VMEM((T, 1), f32), pltpu.VMEM((T, 1), f32), pltpu.VMEM((T, HEAD_DIM), f32)] + c_sems,
        compiler_params=_params(*sem, vmem=_attn_vmem(T)))(proj, proj, proj, c_col, c_row, *(cargo.arrays if cargo else ()))


def _fox_bwd_q(proj, c_col, c_row, o, do, lse, H, name):
    S = proj.shape[0]
    T = _tile(S, ATTN_TILE)
    nt = S // T

    def body(q_ref, k_ref, v_ref, cc_ref, cr_ref, o_ref, do_ref, lse_ref, dq_ref, dc_ref, acc_s, dc_s):
        i, j = pl.program_id(1), pl.program_id(2)

        @pl.when(j == 0)
        def _():
            acc_s[...] = jnp.zeros_like(acc_s)
            dc_s[...] = jnp.zeros_like(dc_s)

        def block(diagonal):
            s = _fox_logits(q_ref[...], k_ref[...], cc_ref[...], cr_ref[...], T, diagonal)
            p = jnp.exp(s - lse_ref[...])
            do_ = do_ref[...]
            delta = jnp.sum(o_ref[...] * do_, axis=-1, keepdims=True)
            ds = p * (_dot_nt(do_, v_ref[...]) - delta)
            acc_s[...] += _dot_nn(ds, k_ref[...])
            dc_s[...] += jnp.sum(ds, axis=-1, keepdims=True)

        @pl.when(j < i)
        def _():
            block(False)

        @pl.when(j == i)
        def _():
            block(True)
            dq_ref[...] = acc_s[...] * (HEAD_DIM ** -0.5)
            dc_ref[...] = dc_s[...]

    qb = pl.BlockSpec((T, HEAD_DIM), lambda h, i, j: (i, h))
    col = pl.BlockSpec((None, T, 1), lambda h, i, j: (h, i, 0))
    return pl.pallas_call(
        body, name=name, grid=(H, nt, nt),
        in_specs=[qb, pl.BlockSpec((T, HEAD_DIM), lambda h, i, j: (jnp.minimum(j, i), H + h)),
                  pl.BlockSpec((T, HEAD_DIM), lambda h, i, j: (jnp.minimum(j, i), 2 * H + h)),
                  col, pl.BlockSpec((None, 1, T), lambda h, i, j: (h, 0, jnp.minimum(j, i))), qb, qb, col],
        out_specs=[qb, col], out_shape=[jax.ShapeDtypeStruct((S, H * HEAD_DIM), f32), jax.ShapeDtypeStruct((H, S, 1), f32)],
        scratch_shapes=[pltpu.VMEM((T, HEAD_DIM), f32), pltpu.VMEM((T, 1), f32)],
        compiler_params=_params("parallel", "parallel", "arbitrary", vmem=_attn_vmem(T)))(proj, proj, proj, c_col, c_row, o, do, lse)


def _fox_bwd_kv(proj, c_col, c_row, o, do, lse, H, name):
    S = proj.shape[0]
    T = _tile(S, ATTN_TILE)
    nt = S // T

    def body(q_ref, k_ref, v_ref, cc_ref, cr_ref, o_ref, do_ref, lse_ref, dk_ref, dv_ref, dc_ref, dk_s, dv_s, dc_s):
        j, i = pl.program_id(1), pl.program_id(2)

        @pl.when(i == 0)
        def _():
            dk_s[...] = jnp.zeros_like(dk_s)
            dv_s[...] = jnp.zeros_like(dv_s)
            dc_s[...] = jnp.zeros_like(dc_s)

        def block(diagonal):
            s = _fox_logits(q_ref[...], k_ref[...], cc_ref[...], cr_ref[...], T, diagonal)
            p = jnp.exp(s - lse_ref[...])
            do_ = do_ref[...]
            delta = jnp.sum(o_ref[...] * do_, axis=-1, keepdims=True)
            ds = p * (_dot_nt(do_, v_ref[...]) - delta)
            dv_s[...] += _dot_tn(p, do_)
            dk_s[...] += _dot_tn(ds, q_ref[...])
            dc_s[...] -= jnp.sum(ds, axis=0, keepdims=True)

        @pl.when(i > j)
        def _():
            block(False)

        @pl.when(i == j)
        def _():
            block(True)

        @pl.when(i == nt - 1)
        def _():
            dk_ref[...] = dk_s[...] * (HEAD_DIM ** -0.5)
            dv_ref[...] = dv_s[...]
            dc_ref[...] = dc_s[...]

    qb = pl.BlockSpec((T, HEAD_DIM), lambda h, j, i: (jnp.maximum(i, j), h))
    col = pl.BlockSpec((None, T, 1), lambda h, j, i: (h, jnp.maximum(i, j), 0))
    kb = pl.BlockSpec((T, HEAD_DIM), lambda h, j, i: (j, h))
    sd = jax.ShapeDtypeStruct
    return pl.pallas_call(
        body, name=name, grid=(H, nt, nt),
        in_specs=[qb, pl.BlockSpec((T, HEAD_DIM), lambda h, j, i: (j, H + h)),
                  pl.BlockSpec((T, HEAD_DIM), lambda h, j, i: (j, 2 * H + h)),
                  col, pl.BlockSpec((None, 1, T), lambda h, j, i: (h, 0, j)), qb, qb, col],
        out_specs=[kb, kb, pl.BlockSpec((None, 1, T), lambda h, j, i: (h, 0, j))],
        out_shape=[sd((S, H * HEAD_DIM), f32), sd((S, H * HEAD_DIM), f32), sd((H, 1, S), f32)],
        scratch_shapes=[pltpu.VMEM((T, HEAD_DIM), f32), pltpu.VMEM((T, HEAD_DIM), f32), pltpu.VMEM((1, T), f32)],
        compiler_params=_params("parallel", "parallel", "arbitrary", vmem=_attn_vmem(T)))(proj, proj, proj, c_col, c_row, o, do, lse)


def _adamw(w, g, m, v, name):
    shape = w.shape
    cols = shape[-1]
    rows = w.size // cols
    ops = [t.reshape(rows, cols) for t in (w, g, m, v)]
    tr = rows
    if rows % 8 == 0:
        tr = 8
        while tr * 2 <= rows and rows % (tr * 2) == 0 and tr * 2 * cols * 4 <= (1 << 20):
            tr *= 2

    def body(w_ref, g_ref, m_ref, v_ref, d_ref, mo_ref, vo_ref):
        g_ = g_ref[...]
        m_ = ADAM_B1 * m_ref[...] + (1.0 - ADAM_B1) * g_
        v_ = ADAM_B2 * v_ref[...] + (1.0 - ADAM_B2) * (g_ * g_)
        m_hat = m_ / (1.0 - ADAM_B1 ** ADAM_STEP)
        v_hat = v_ / (1.0 - ADAM_B2 ** ADAM_STEP)
        d_ref[...] = -ADAM_LR * (m_hat / (jnp.sqrt(v_hat) + ADAM_EPS) + ADAM_WD * w_ref[...])
        mo_ref[...] = m_
        vo_ref[...] = v_

    blk = pl.BlockSpec((tr, cols), lambda i: (i, 0))
    outs = pl.pallas_call(body, name=name, grid=(rows // tr,), in_specs=[blk] * 4, out_specs=[blk] * 3,
                          out_shape=[jax.ShapeDtypeStruct((rows, cols), f32)] * 3, compiler_params=_params("parallel"))(*ops)
    return [o.reshape(shape) for o in outs]


def _row_tile(rows, row_bytes, budget=2 << 20):
    best = None
    for t in range(16, rows + 1, 16):
        if rows % t == 0 and t * row_bytes <= budget:
            best = t
    return best or rows


def _sum_leading(a, name):
    n, R, C = a.shape
    tr = _row_tile(R, n * C * 4)

    def body(a_ref, o_ref):
        acc = a_ref[0].astype(f32)
        for k in range(1, n):
            acc = acc + a_ref[k].astype(f32)
        o_ref[...] = acc

    return pl.pallas_call(body, name=name, grid=(R // tr,), in_specs=[pl.BlockSpec((n, tr, C), lambda i: (0, i, 0))],
                          out_specs=pl.BlockSpec((tr, C), lambda i: (i, 0)), out_shape=jax.ShapeDtypeStruct((R, C), f32),
                          compiler_params=_params("parallel"))(a)


def _add2(a, b, dtype, name):
    n, R, C = a.shape
    rows = n * R
    tr = _row_tile(rows, C * 4)

    def body(a_ref, b_ref, o_ref):
        o_ref[...] = (a_ref[...] + b_ref[...]).astype(dtype)

    blk = pl.BlockSpec((tr, C), lambda i: (i, 0))
    out = pl.pallas_call(body, name=name, grid=(rows // tr,), in_specs=[blk, blk], out_specs=blk,
                         out_shape=jax.ShapeDtypeStruct((rows, C), dtype),
                         compiler_params=_params("parallel"))(a.reshape(rows, C), b.reshape(rows, C))
    return out.reshape(n, R, C)


def _place():
    x, y, c = lax.axis_index("x"), lax.axis_index("y"), lax.axis_index("c")
    chips = [(1 - x, y), (x, 1 - y), (1 - x, 1 - y)]
    return x, y, c, chips


def _gather_copies(ws_refs, out_refs, send_sems, recv_sems):
    x, y, c, chips = _place()
    sends, lands = [], []
    for t, (ws_ref, out_ref) in enumerate(zip(ws_refs, out_refs)):
        for j, chip in enumerate(chips):
            make = functools.partial(pltpu.make_async_remote_copy, src_ref=ws_ref.at[c], send_sem=send_sems.at[3 * t + j],
                                     recv_sem=recv_sems.at[3 * t + j], device_id=(*chip, c), device_id_type=MESH)
            sends.append(functools.partial(make, dst_ref=out_ref.at[2 * x + y, c]))
            lands.append(functools.partial(make, dst_ref=out_ref.at[2 * chip[0] + chip[1], c]))
    return sends, lands


def _gather_cargo(blocks):
    return _Cargo(tuple(blocks), tuple(jax.ShapeDtypeStruct((N_CHIPS,) + b.shape, b.dtype) for b in blocks), _gather_copies,
                  3 * len(blocks))


def _sibling_forward(gathered, blocks, name):
    n = len(gathered)

    def body(*refs):
        ins, own, outs, send_sems, recv_sems = refs[:n], refs[n:2 * n], refs[2 * n:3 * n], refs[3 * n], refs[3 * n + 1]
        x, y, c, chips = _place()
        sends, lands = [], []
        for t in range(n):
            for j, chip in enumerate(chips):
                s = 2 * chip[0] + chip[1]
                make = functools.partial(pltpu.make_async_remote_copy, src_ref=ins[t].at[s, c], send_sem=send_sems.at[4 * t + j],
                                         recv_sem=recv_sems.at[4 * t + j], device_id=(x, y, 1 - c), device_id_type=MESH)
                sends.append(make(dst_ref=outs[t].at[s, c]))
                lands.append(make(dst_ref=outs[t].at[s, 1 - c]))
            make = functools.partial(pltpu.make_async_remote_copy, src_ref=own[t], dst_ref=outs[t].at[2 * x + y],
                                     send_sem=send_sems.at[4 * t + 3], recv_sem=recv_sems.at[4 * t + 3], device_id=(x, y, 1 - c),
                                     device_id_type=MESH)
            sends.append(make())
            lands.append(make())
        for cp in sends:
            cp.start()
        for cp in lands:
            cp.wait_recv()
        for cp in sends:
            cp.wait_send()

    return pl.pallas_call(body, name=name, out_shape=[jax.ShapeDtypeStruct(g.shape, g.dtype) for g in gathered],
                          in_specs=[ANY] * (2 * n), out_specs=[ANY] * n, input_output_aliases={t: t for t in range(n)},
                          scratch_shapes=[pltpu.SemaphoreType.DMA((4 * n,)), pltpu.SemaphoreType.DMA((4 * n,))])(*gathered, *blocks)


def _pair_exchange(g, name):
    n, _, R, C = g.shape

    def body(g_ref, out_ref, send_sems, recv_sems):
        x, y, c, _ = _place()
        cps = [pltpu.make_async_remote_copy(src_ref=g_ref.at[s, 1 - c], dst_ref=out_ref.at[s], send_sem=send_sems.at[s],
                                            recv_sem=recv_sems.at[s], device_id=(x, y, 1 - c), device_id_type=MESH)
               for s in range(n)]
        for cp in cps:
            cp.start()
        for cp in cps:
            cp.wait()

    return pl.pallas_call(body, name=name, out_shape=jax.ShapeDtypeStruct((n, R, C), g.dtype), in_specs=[ANY], out_specs=ANY,
                          scratch_shapes=[pltpu.SemaphoreType.DMA((n,)), pltpu.SemaphoreType.DMA((n,))])(g)


def _chip_copies(a_refs, out_refs, send_sems, recv_sems):
    x, y, c, chips = _place()
    me = 2 * x + y
    sends, lands = [], []
    for t, (a_ref, out_ref) in enumerate(zip(a_refs, out_refs)):
        for j, chip in enumerate(chips):
            them = 2 * chip[0] + chip[1]
            make = functools.partial(pltpu.make_async_remote_copy, send_sem=send_sems.at[3 * t + j], recv_sem=recv_sems.at[3 * t + j],
                                     device_id=(*chip, c), device_id_type=MESH)
            sends.append(functools.partial(make, src_ref=a_ref.at[them], dst_ref=out_ref.at[me]))
            lands.append(functools.partial(make, src_ref=a_ref.at[me], dst_ref=out_ref.at[them]))
    return sends, lands


def _chip_cargo(pairs):
    return _Cargo(tuple(pairs), tuple(jax.ShapeDtypeStruct(p.shape, p.dtype) for p in pairs), _chip_copies, 3 * len(pairs))


def _pair_swap(r, name):
    R, C = r.shape

    def body(r_ref, out_ref, send_sem, recv_sem):
        x, y, c, _ = _place()
        cp = pltpu.make_async_remote_copy(src_ref=r_ref, dst_ref=out_ref, send_sem=send_sem, recv_sem=recv_sem,
                                          device_id=(x, y, 1 - c), device_id_type=MESH)
        cp.start()
        cp.wait()

    return pl.pallas_call(body, name=name, out_shape=jax.ShapeDtypeStruct((R, C), r.dtype), in_specs=[ANY], out_specs=ANY,
                          scratch_shapes=[pltpu.SemaphoreType.DMA(()), pltpu.SemaphoreType.DMA(())])(r)


def _all_gather8(v, name):
    R, C = v.shape

    def body(v_ref, out_ref, send_sems, recv_sems):
        x, y, c, _ = _place()
        me = 4 * x + 2 * y + c
        peers = [(x ^ (k >> 2), y ^ ((k >> 1) & 1), c ^ (k & 1)) for k in range(1, N_DEV)]
        sends = [pltpu.make_async_remote_copy(src_ref=v_ref, dst_ref=out_ref.at[me], send_sem=send_sems.at[k], recv_sem=recv_sems.at[k],
                                              device_id=peer, device_id_type=MESH) for k, peer in enumerate(peers)]
        for cp in sends:
            cp.start()
        for k, (px, py, pc) in enumerate(peers):
            pltpu.make_async_remote_copy(src_ref=v_ref, dst_ref=out_ref.at[4 * px + 2 * py + pc], send_sem=send_sems.at[k],
                                         recv_sem=recv_sems.at[k], device_id=(px, py, pc), device_id_type=MESH).wait_recv()
        for cp in sends:
            cp.wait_send()

    return pl.pallas_call(body, name=name, out_shape=jax.ShapeDtypeStruct((N_DEV, R, C), v.dtype), in_specs=[ANY], out_specs=ANY,
                          scratch_shapes=[pltpu.SemaphoreType.DMA((7,)), pltpu.SemaphoreType.DMA((7,))])(v)


def _as_list(x):
    return list(x) if isinstance(x, (list, tuple)) else [x]


def _put(buf, block, index):
    return lax.dynamic_update_slice(buf, block[None], (index,) + (0,) * block.ndim)


def _all_reduce8(v, device, name):
    n = v.shape[0]
    rows = -(-n // (LANES * SUM_ROWS)) * SUM_ROWS
    padded = jnp.pad(v, (0, rows * LANES - n)).reshape(rows, LANES)
    return _sum_leading(_put(_all_gather8(padded, name + "_gather"), padded, device), name + "_sum").reshape(-1)[:n]


def _halves(w):
    R, C = w.shape
    return w.astype(bf16).reshape(2, R // 2, C)


def _finish_gather(blocks, landed):
    full = _sibling_forward(landed, blocks, "gather_sibling_forward")
    return [g.reshape(N_CHIPS, 2 * b.shape[1], b.shape[2]) for g, b in zip(full, blocks)]


def _pair_sums(g4, core, name):
    _, R, C = g4.shape
    g = g4.reshape(N_CHIPS, 2, R // 2, C)
    mine = lax.dynamic_index_in_dim(g, core, axis=1, keepdims=False)
    return _add2(mine, _pair_exchange(g, name + "_pair_exchange"), bf16, name + "_pair_sum")


def _finish_reduce(pair, landed, chip, core, name):
    _, R2, C = pair.shape
    own = lax.dynamic_index_in_dim(pair, chip, axis=0, keepdims=False)
    half = _sum_leading(_put(landed, own, chip), name + "_chip_sum")
    both = jnp.stack([half, _pair_swap(half, name + "_pair_swap")])
    return jnp.where(core == 0, both, both[::-1]).reshape(2 * R2, C)


def _segments(H, D):
    W = H * HEAD_DIM
    sizes = (3 * W, H, 2 * W, 3 * W, H, H, W, 3 * D)
    in_tail = (False, True, False, False, True, True, False, False)
    out, first, used = [], 0, [0, 0]
    for size, t in zip(sizes, in_tail):
        out.append((first, size, t, used[t]))
        first += size
        used[t] += size
    return out


def _main_tail_from_shards(g4, H, D):
    C = g4.shape[-1]
    parts = ([], [])
    for first, size, t, _ in _segments(H, D):
        for s in range(g4.shape[0]):
            a, b = max(first, s * C), min(first + size, (s + 1) * C)
            if a < b:
                parts[t].append(g4[s][..., a - s * C:b - s * C])
    parts[1].append(jnp.zeros(g4.shape[1:-1] + (LANES - 3 * H,), g4.dtype))
    return jnp.concatenate(parts[0], axis=-1), jnp.concatenate(parts[1], axis=-1)


def _shards_from_main_tail(main, tail, H, D):
    segs = _segments(H, D)
    C = sum(size for _, size, _, _ in segs) // N_CHIPS
    shards = []
    for s in range(N_CHIPS):
        pieces = []
        for first, size, t, there in segs:
            a, b = max(first, s * C), min(first + size, (s + 1) * C)
            if a < b:
                pieces.append((tail if t else main)[..., there + a - first:there + b - first])
        shards.append(jnp.concatenate(pieces, axis=-1))
    return jnp.stack(shards)


def _pad_cols(a, n):
    return jnp.pad(a, [(0, 0)] * (a.ndim - 1) + [(0, n - a.shape[-1])])


def kernel(x, w_in, b_in, sgu_ln_g, sgu_ln_b, sgu_w, sgu_b, gdn_conv_w, gdn_a_log, gdn_dt_bias, gdn_norm_g, w_proj_a, w_proj_b, w_proj_c, w_out, ln1_g, ln1_b, ffn_w_up, ffn_conv_w, ffn_conv_b, ffn_w_down, ln2_g, ln2_b, loss_target, m_w_in, m_b_in, m_sgu_ln_g, m_sgu_ln_b, m_sgu_w, m_sgu_b, m_gdn_conv_w, m_gdn_a_log, m_gdn_dt_bias, m_gdn_norm_g, m_w_proj_a, m_w_proj_b, m_w_proj_c, m_w_out, m_ln1_g, m_ln1_b, m_ffn_w_up, m_ffn_conv_w, m_ffn_conv_b, m_ffn_w_down, m_ln2_g, m_ln2_b, v_w_in, v_b_in, v_sgu_ln_g, v_sgu_ln_b, v_sgu_w, v_sgu_b, v_gdn_conv_w, v_gdn_a_log, v_gdn_dt_bias, v_gdn_norm_g, v_w_proj_a, v_w_proj_b, v_w_proj_c, v_w_out, v_ln1_g, v_ln1_b, v_ffn_w_up, v_ffn_conv_w, v_ffn_conv_b, v_ffn_w_down, v_ln2_g, v_ln2_b):
    P = dict(w_in=w_in, b_in=b_in, sgu_ln_g=sgu_ln_g, sgu_ln_b=sgu_ln_b, sgu_w=sgu_w, sgu_b=sgu_b, gdn_conv_w=gdn_conv_w,
             gdn_a_log=gdn_a_log, gdn_dt_bias=gdn_dt_bias, gdn_norm_g=gdn_norm_g, w_proj_a=w_proj_a, w_proj_b=w_proj_b,
             w_proj_c=w_proj_c, w_out=w_out, ln1_g=ln1_g, ln1_b=ln1_b, ffn_w_up=ffn_w_up, ffn_conv_w=ffn_conv_w,
             ffn_conv_b=ffn_conv_b, ffn_w_down=ffn_w_down, ln2_g=ln2_g, ln2_b=ln2_b)
    M1 = dict(w_in=m_w_in, b_in=m_b_in, sgu_ln_g=m_sgu_ln_g, sgu_ln_b=m_sgu_ln_b, sgu_w=m_sgu_w, sgu_b=m_sgu_b,
              gdn_conv_w=m_gdn_conv_w, gdn_a_log=m_gdn_a_log, gdn_dt_bias=m_gdn_dt_bias, gdn_norm_g=m_gdn_norm_g,
              w_proj_a=m_w_proj_a, w_proj_b=m_w_proj_b, w_proj_c=m_w_proj_c, w_out=m_w_out, ln1_g=m_ln1_g, ln1_b=m_ln1_b,
              ffn_w_up=m_ffn_w_up, ffn_conv_w=m_ffn_conv_w, ffn_conv_b=m_ffn_conv_b, ffn_w_down=m_ffn_w_down, ln2_g=m_ln2_g,
              ln2_b=m_ln2_b)
    M2 = dict(w_in=v_w_in, b_in=v_b_in, sgu_ln_g=v_sgu_ln_g, sgu_ln_b=v_sgu_ln_b, sgu_w=v_sgu_w, sgu_b=v_sgu_b,
              gdn_conv_w=v_gdn_conv_w, gdn_a_log=v_gdn_a_log, gdn_dt_bias=v_gdn_dt_bias, gdn_norm_g=v_gdn_norm_g,
              w_proj_a=v_w_proj_a, w_proj_b=v_w_proj_b, w_proj_c=v_w_proj_c, w_out=v_w_out, ln1_g=v_ln1_g, ln1_b=v_ln1_b,
              ffn_w_up=v_ffn_w_up, ffn_conv_w=v_ffn_conv_w, ffn_conv_b=v_ffn_conv_b, ffn_w_down=v_ffn_w_down, ln2_g=v_ln2_g,
              ln2_b=v_ln2_b)
    _, S, D = x.shape
    L = w_in.shape[0]
    N_IN = w_in.shape[2] * N_CHIPS
    H = (N_IN - 3 * D) // (9 * HEAD_DIM + 3)
    W = H * HEAD_DIM
    F = ffn_w_down.shape[1] * N_CHIPS
    Fp = -(-F // FF_ALIGN) * FF_ALIGN
    NM = 9 * W + 3 * D
    alpha = (2 * L) ** 0.25
    cx, cy, cc = lax.axis_index("x"), lax.axis_index("y"), lax.axis_index("c")
    chip = 2 * cx + cy

    def blocks_of(l):
        return [_halves(w_in[l]), _halves(jnp.concatenate([w_proj_a[l], w_proj_b[l], w_proj_c[l]], axis=0)),
                _halves(jnp.concatenate([w_out[l], ffn_w_down[l]], axis=0)), _halves(ffn_w_up[l])]

    def full_weights(blocks, landed):
        g_in, g_proj, g_rows, g_up = _finish_gather(blocks, landed)
        w_main, w_tail = _main_tail_from_shards(g_in, H, D)
        wa, wb, wc = [g_proj[:, k * W:(k + 1) * W].transpose(1, 0, 2).reshape(W, D) for k in range(3)]
        return dict(w_main=w_main, w_tail=w_tail, wa=wa, wb=wb, wc=wc, wo=g_rows[:, :D // N_CHIPS].reshape(D, D),
                    wd=jnp.pad(g_rows[:, D // N_CHIPS:].reshape(F, D), ((0, Fp - F), (0, 0))),
                    wg=_pad_cols(jnp.concatenate([g_up[0], g_up[1]], axis=1), Fp),
                    wv=_pad_cols(jnp.concatenate([g_up[2], g_up[3]], axis=1), Fp))

    blocks = blocks_of(0)
    weights = full_weights(blocks, _ship(_gather_cargo(blocks), "gather_first_layer"))
    gcw_cols, fcw_cols = gdn_conv_w.shape[2], ffn_conv_w.shape[2]
    only_south = (cc == 0).astype(f32)
    placed_g = lax.dynamic_update_slice(jnp.zeros((L, GDN_CONV, 3 * W), f32), gdn_conv_w * only_south, (0, 0, chip * gcw_cols))
    placed_f = lax.dynamic_update_slice(jnp.zeros((L, FFN_CONV, 2 * F), f32), ffn_conv_w * only_south, (0, 0, chip * fcw_cols))
    conv_all = _all_reduce8(jnp.concatenate([placed_g.reshape(-1), placed_f.reshape(-1)]), 2 * chip + cc, "conv_weights")
    gcw_full = conv_all[:L * GDN_CONV * 3 * W].reshape(L, GDN_CONV, 1, 3 * W)
    fcw_full = conv_all[L * GDN_CONV * 3 * W:].reshape(L, FFN_CONV, 1, 2 * F)

    saved = []
    h = x.reshape(S, D)
    for l in range(L):
        b_main, b_tail = _main_tail_from_shards(b_in[l][None, None, :], H, D)
        cwg, cwv = _pad_cols(fcw_full[l][..., :F], Fp), _pad_cols(fcw_full[l][..., F:], Fp)
        cbg, cbv = _pad_cols(ffn_conv_b[l][None, :F], Fp), _pad_cols(ffn_conv_b[l][None, F:], Fp)
        nxt = blocks_of(l + 1) if l + 1 < L else None
        ride = (lambda *idx: _gather_cargo([nxt[k] for k in idx])) if nxt else (lambda *idx: None)
        lw = dict(weights, cwg=cwg, cwv=cwv, cbg=cbg, cbv=cbv, gcw=gcw_full[l],
                  sgu_ln_g=sgu_ln_g[l][None, :], sgu_ln_b=sgu_ln_b[l][None, :], sgu_w=sgu_w[l], sgu_b=sgu_b[l][:, :, None],
                  a_log=gdn_a_log[l].reshape(H, 1, 1), dt_bias=gdn_dt_bias[l].reshape(H, 1, 1),
                  norm_g=gdn_norm_g[l].reshape(1, 1, HEAD_DIM), ln1_g=ln1_g[l][None, :], ln1_b=ln1_b[l][None, :],
                  ln2_g=ln2_g[l][None, :], ln2_b=ln2_b[l][None, :])
        tag = "_carrying" if nxt else ""
        proj, *land_in = _as_list(_matmul(h, lw["w_main"], "nn", "proj_main" + tag, bias=b_main, cargo=ride(0)))
        tail = _matmul(h, lw["w_tail"], "nn", "proj_tail", bias=b_tail)
        csum = _fox_prep_fwd(tail, "fox_prep")
        c_col = csum[:, :H].T[:, :, None]
        c_row = csum[:, :H].T[:, None, :]
        y_a, lse, *land_up = _fox_fwd(proj, c_col, c_row, H, "fox_fwd" + tag, cargo=ride(3))
        y_b = _sgu_fwd(proj, lw["sgu_ln_g"], lw["sgu_ln_b"], lw["sgu_w"], lw["sgu_b"], 3 * W, 4 * W, "sgu_fwd")
        qkvc = _gdn_pre_fwd(proj, lw["gcw"], 5 * W, 2 * H, "gdn_pre")
        al = tail[:, H:2 * H].T[:, :, None]
        bl = tail[:, 2 * H:3 * H].T[:, :, None]
        y_c, states, *land_rest = _gdn_scan_fwd(qkvc, al, bl, proj, lw["a_log"], lw["dt_bias"], lw["norm_g"], 8 * W, "gdn_scan" + tag,
                                                cargo=ride(1, 2))
        if nxt:
            weights = full_weights(nxt, land_in + [land_rest[0], land_rest[1]] + land_up)
        pa = _matmul(y_a, lw["wa"], "nn", "branch_proj")
        pb = _matmul(y_b, lw["wb"], "nn", "branch_proj")
        pc = _matmul(y_c, lw["wc"], "nn", "branch_proj")
        merged = _merge_fwd(proj, pa, pb, pc, 9 * W, "merge")
        mix = _matmul(merged, lw["wo"], "nn", "out_proj")
        x1 = _ln_fwd(h, mix, lw["ln1_g"], lw["ln1_b"], alpha, "ln")
        upg = _matmul(x1, lw["wg"], "nn", "ffn_up")
        upv = _matmul(x1, lw["wv"], "nn", "ffn_up")
        act = _ffn_act_fwd(upg, upv, cwg, cwv, cbg, cbv, "ffn_act")
        ffn = _matmul(act, lw["wd"], "nn", "ffn_down")
        x2 = _ln_fwd(x1, ffn, lw["ln2_g"], lw["ln2_b"], alpha, "ln")
        saved.append(dict(lw=lw, h=h, proj=proj, tail=tail, c_col=c_col, c_row=c_row, y_a=y_a, lse=lse, y_b=y_b, qkvc=qkvc, al=al,
                          bl=bl, y_c=y_c, states=states, pa=pa, pb=pb, pc=pc, merged=merged, mix=mix, x1=x1, upg=upg, upv=upv,
                          act=act, ffn=ffn))
        h = x2

    loss_part, dh = _loss_head(h, loss_target.reshape(S, D), "loss_head")
    loss = lax.psum(loss_part[0, 0], ("x", "y", "c"))

    reduced = [dict() for _ in range(L)]
    small_grads = [None] * L
    pending = None
    for l in reversed(range(L)):
        s = saved[l]
        lw = s["lw"]
        d_x1r, d_ffn, d_ln2g, d_ln2b = _ln_bwd(s["x1"], s["ffn"], lw["ln2_g"], lw["ln2_b"], dh, alpha, "ln_bwd")
        d_act = _matmul(d_ffn, lw["wd"], "nt", "ffn_down_dx")
        d_wd = _matmul(s["act"], d_ffn, "tn", "ffn_down_dw")
        dupg, dupv, dcwg, dcwv, dcbg, dcbv = _ffn_act_bwd(s["upg"], s["upv"], lw["cwg"], lw["cwv"], lw["cbg"], lw["cbv"], d_act,
                                                          "ffn_act_bwd")
        d_x1 = _matmul(dupg, lw["wg"], "nt", "ffn_up_dx", add=d_x1r)
        d_x1 = _matmul(dupv, lw["wv"], "nt", "ffn_up_dx", add=d_x1)
        d_wg = _matmul(s["x1"], dupg, "tn", "ffn_up_dw")
        d_wv = _matmul(s["x1"], dupv, "tn", "ffn_up_dw")
        d_hr, d_mix, d_ln1g, d_ln1b = _ln_bwd(s["h"], s["mix"], lw["ln1_g"], lw["ln1_b"], d_x1, alpha, "ln_bwd")
        d_merged = _matmul(d_mix, lw["wo"], "nt", "out_proj_dx")
        d_wo = _matmul(s["merged"], d_mix, "tn", "out_proj_dw")
        dg0, dg1, dg2, d_pa, d_pb, d_pc = _merge_bwd(s["proj"], s["pa"], s["pb"], s["pc"], d_merged, 9 * W, "merge_bwd")
        d_ya = _matmul(d_pa, lw["wa"], "nt", "branch_proj_dx")
        d_yb = _matmul(d_pb, lw["wb"], "nt", "branch_proj_dx")
        d_yc = _matmul(d_pc, lw["wc"], "nt", "branch_proj_dx")
        d_wa = _matmul(s["y_a"], d_pa, "tn", "branch_proj_dw")
        d_wb = _matmul(s["y_b"], d_pb, "tn", "branch_proj_dw")
        d_wc = _matmul(s["y_c"], d_pc, "tn", "branch_proj_dw")
        tag = "_carrying" if pending else ""
        ride = (lambda *keys: _chip_cargo([pending[1][k] for k in keys])) if pending else (lambda *keys: None)
        dqkvc, dal, dbl, dgate, d_alog, d_dt, d_ng, *landed = _gdn_scan_bwd(
            s["qkvc"], s["al"], s["bl"], s["proj"], lw["a_log"], lw["dt_bias"], lw["norm_g"], s["states"], d_yc, 8 * W,
            "gdn_scan_bwd" + tag, cargo=ride("w_in", "proj"))
        for k, got in zip(("w_in", "proj"), landed):
            reduced[pending[0]][k] = _finish_reduce(pending[1][k], got, chip, cc, "grad_" + k)
        d_gqkv, d_gcw = _gdn_pre_bwd(s["proj"], lw["gcw"], dqkvc, 5 * W, 2 * H, "gdn_pre_bwd")
        d_u, d_v, d_slg, d_slb, d_sw, d_sb = _sgu_bwd(s["proj"], lw["sgu_ln_g"], lw["sgu_ln_b"], lw["sgu_w"], lw["sgu_b"], d_yb,
                                                      3 * W, 4 * W, "sgu_bwd")
        d_q, d_cq = _fox_bwd_q(s["proj"], s["c_col"], s["c_row"], s["y_a"], d_ya, s["lse"], H, "fox_bwd_q")
        d_k, d_v_att, d_c = _fox_bwd_kv(s["proj"], s["c_col"], s["c_row"], s["y_a"], d_ya, s["lse"], H, "fox_bwd_kv")
        d_f = _fox_prep_bwd(s["tail"], _pad_cols(d_cq[:, :, 0].T, LANES), _pad_cols(d_c[:, 0, :].T, LANES), "fox_prep_bwd")
        d_main = jnp.concatenate([d_q, d_k, d_v_att, d_u, d_v, d_gqkv, dgate, dg0, dg1, dg2], axis=1)
        d_tail = _pad_cols(jnp.concatenate([d_f[:, :H], dal[:, :, 0].T, dbl[:, :, 0].T], axis=1), LANES)
        d_wmain = _matmul(s["h"], d_main, "tn", "proj_main_dw")
        d_wtail = _matmul(s["h"], d_tail, "tn", "proj_tail_dw")
        d_bmain = _colsum(d_main, "proj_main_db")
        d_btail = _colsum(d_tail, "proj_tail_db")
        dh, *landed = _as_list(_matmul(d_main, lw["w_main"], "nt", "proj_main_dx" + tag, add=d_hr, cargo=ride("rows", "w_up")))
        for k, got in zip(("rows", "w_up"), landed):
            reduced[pending[0]][k] = _finish_reduce(pending[1][k], got, chip, cc, "grad_" + k)
        dh = _matmul(d_tail, lw["w_tail"], "nt", "proj_tail_dx", add=dh)
        by_cols = lambda g: g.reshape(g.shape[0], N_CHIPS, g.shape[1] // N_CHIPS).transpose(1, 0, 2)
        big = dict(
            w_in=_shards_from_main_tail(d_wmain, d_wtail, H, D),
            proj=jnp.concatenate([by_cols(d_wa), by_cols(d_wb), by_cols(d_wc)], axis=1),
            rows=jnp.concatenate([d_wo.reshape(N_CHIPS, D // N_CHIPS, D), d_wd[:F].reshape(N_CHIPS, F // N_CHIPS, D)], axis=1),
            w_up=jnp.stack([d_wg[:, :F // 2], d_wg[:, F // 2:F], d_wv[:, :F // 2], d_wv[:, F // 2:F]]))
        pending = (l, {k: _pair_sums(big[k], cc, "grad_" + k) for k in SHARDED})
        small_grads[l] = dict(b_in=_shards_from_main_tail(d_bmain, d_btail, H, D).reshape(-1), sgu_ln_g=d_slg[0], sgu_ln_b=d_slb[0], sgu_w=d_sw,
                              sgu_b=d_sb[:, :, 0], gdn_conv_w=d_gcw[:, 0, :], gdn_a_log=d_alog[:, 0, 0], gdn_dt_bias=d_dt[:, 0, 0],
                              gdn_norm_g=d_ng[0, 0], ln1_g=d_ln1g[0], ln1_b=d_ln1b[0],
                              ffn_conv_w=jnp.concatenate([dcwg[:, 0, :F], dcwv[:, 0, :F]], axis=1),
                              ffn_conv_b=jnp.concatenate([dcbg[0, :F], dcbv[0, :F]]), ln2_g=d_ln2g[0], ln2_b=d_ln2b[0])
    grad_x = dh.reshape(1, S, D)

    landed = _ship(_chip_cargo([pending[1][k] for k in SHARDED]), "grad_chip_exchange")
    reduced[pending[0]] = {k: _finish_reduce(pending[1][k], got, chip, cc, "grad_" + k) for k, got in zip(SHARDED, landed)}
    grads = {n: [None] * L for n in WEIGHTS}
    for l in range(L):
        grads["w_in"][l] = reduced[l]["w_in"]
        for k, n in enumerate(("w_proj_a", "w_proj_b", "w_proj_c")):
            grads[n][l] = reduced[l]["proj"][k * W:(k + 1) * W]
        grads["w_out"][l] = reduced[l]["rows"][:D // N_CHIPS]
        grads["ffn_w_down"][l] = reduced[l]["rows"][D // N_CHIPS:]
        grads["ffn_w_up"][l] = reduced[l]["w_up"]
    small_shapes = {n: small_grads[0][n].shape for n in SMALL}
    small_flat = jnp.concatenate([small_grads[l][n].reshape(-1) for l in range(L) for n in SMALL])
    small_sum = _all_reduce8(small_flat, 2 * chip + cc, "small_grads")
    off = 0
    for l in range(L):
        for n in SMALL:
            size = math.prod(small_shapes[n])
            g = small_sum[off:off + size].reshape(small_shapes[n])
            off += size
            if n == "gdn_conv_w":
                g = lax.dynamic_slice_in_dim(g, chip * gcw_cols, gcw_cols, axis=1)
            elif n == "ffn_conv_w":
                g = lax.dynamic_slice_in_dim(g, chip * fcw_cols, fcw_cols, axis=1)
            grads[n][l] = g
    grads = {n: jnp.stack(grads[n]) for n in WEIGHTS}

    deltas, new_m, new_v = {}, {}, {}
    for n in WEIGHTS:
        deltas[n], new_m[n], new_v[n] = _adamw(P[n], grads[n], M1[n], M2[n], "adamw_" + n)
    return (loss, grad_x, *[grads[n] for n in WEIGHTS], *[deltas[n] for n in WEIGHTS], *[new_m[n] for n in WEIGHTS],
            *[new_v[n] for n in WEIGHTS])
```

```python
import functools
import math
from typing import NamedTuple

import jax
import jax.numpy as jnp
from jax import lax
from jax.experimental import pallas as pl
from jax.experimental.pallas import tpu as pltpu

f32 = jnp.float32
bf16 = jnp.bfloat16
HIGHEST = lax.Precision.HIGHEST
MESH = pl.DeviceIdType.MESH

HEAD_DIM = 128
CHUNK = 64
SGU_SPAN = 128
GDN_CONV = 4
FFN_CONV = 3
N_CHIPS = 4
N_DEV = 8
LN_EPS = 1e-5
RMS_EPS = 1e-6
ADAM_LR = 0.001
ADAM_B1 = 0.9
ADAM_B2 = 0.999
ADAM_EPS = 1e-08
ADAM_WD = 0.01
ADAM_STEP = 10
NEG_BIG = -1e30
LANES = 128
FF_ALIGN = 512
SUM_ROWS = 256
ATTN_TILE = 1024
VMEM_MARGIN = 12 << 20
VMEM_MOST = 60 << 20

SHARDED = ("w_in", "proj", "rows", "w_up")
SMALL = ("b_in", "sgu_ln_g", "sgu_ln_b", "sgu_w", "sgu_b", "gdn_conv_w", "gdn_a_log", "gdn_dt_bias", "gdn_norm_g",
         "ln1_g", "ln1_b", "ffn_conv_w", "ffn_conv_b", "ln2_g", "ln2_b")
WEIGHTS = ("w_in", "b_in", "sgu_ln_g", "sgu_ln_b", "sgu_w", "sgu_b", "gdn_conv_w", "gdn_a_log", "gdn_dt_bias", "gdn_norm_g",
           "w_proj_a", "w_proj_b", "w_proj_c", "w_out", "ln1_g", "ln1_b", "ffn_w_up", "ffn_conv_w", "ffn_conv_b", "ffn_w_down",
           "ln2_g", "ln2_b")

ANY = pl.BlockSpec(memory_space=pl.ANY)


def _tile(dim, pref):
    t = pref
    while t > 128 and dim % t:
        t //= 2
    return min(t, dim) if dim % min(t, dim) == 0 else dim


def _params(*sem, vmem=None):
    if vmem is None:
        return pltpu.CompilerParams(dimension_semantics=sem)
    return pltpu.CompilerParams(dimension_semantics=sem, vmem_limit_bytes=min(vmem + VMEM_MARGIN, VMEM_MOST))


def _attn_vmem(T):
    return 8 * T * T * 4


_DIMS = {"nn": (((1,), (0,)), ((), ())), "nt": (((1,), (1,)), ((), ())), "tn": (((0,), (0,)), ((), ()))}


def _pick(dim, most):
    for unit in (256, LANES):
        for t in range(min(most, dim) // unit * unit, 0, -unit):
            if dim % t == 0:
                return t
    return dim


class _Cargo(NamedTuple):
    arrays: tuple
    landing: tuple
    copies: object
    n: int


def _cargo_specs(cargo):
    if cargo is None:
        return [], [], [], []
    sems = [pltpu.SemaphoreType.DMA((cargo.n,)), pltpu.SemaphoreType.DMA((cargo.n,))]
    return [ANY] * len(cargo.arrays), [ANY] * len(cargo.landing), list(cargo.landing), sems


def _split_refs(refs, n_in, n_out, cargo):
    na, nl, ns = (len(cargo.arrays), len(cargo.landing), 2) if cargo else (0, 0, 0)
    a, b, c = n_in + na, n_in + na + n_out, n_in + na + n_out + nl
    return refs[:n_in], refs[a:b], refs[c:len(refs) - ns], (refs[n_in:a], refs[b:c], refs[len(refs) - ns:])


def _cargo_start(cargo, first, hold):
    if cargo is not None:
        @pl.when(first)
        def _():
            for make in cargo.copies(hold[0], hold[1], *hold[2])[0]:
                make().start()


def _cargo_wait(cargo, last, hold):
    if cargo is not None:
        @pl.when(last)
        def _():
            sends, lands = cargo.copies(hold[0], hold[1], *hold[2])
            for make in lands:
                make().wait_recv()
            for make in sends:
                make().wait_send()


def _ship(cargo, name):
    in_specs, out_specs, out_shape, sems = _cargo_specs(cargo)

    def body(*refs):
        _, _, _, hold = _split_refs(refs, 0, 0, cargo)
        sends, lands = cargo.copies(hold[0], hold[1], *hold[2])
        started = [make() for make in sends]
        for cp in started:
            cp.start()
        for make in lands:
            make().wait_recv()
        for cp in started:
            cp.wait_send()

    return pl.pallas_call(body, name=name, out_shape=out_shape, in_specs=in_specs, out_specs=out_specs,
                          scratch_shapes=sems)(*cargo.arrays)


def _matmul(a, b, mode, name, bias=None, add=None, cargo=None):
    a, b = a.astype(bf16), b.astype(bf16)
    if mode == "nn":
        (M, K), (_, N) = a.shape, b.shape
    elif mode == "nt":
        (M, K), (N, _) = a.shape, b.shape
    else:
        (K, M), (_, N) = a.shape, b.shape
    has_bias, has_add = bias is not None, add is not None
    tn, tk = _pick(N, 2816), _pick(K, 512)

    def need(tm):
        return 2 * (tm * tk * a.dtype.itemsize + tk * tn * b.dtype.itemsize + tm * tn * 4 * (2 if has_add else 1)) + tm * tn * 4

    tm = _pick(M, 1024)
    if need(tm) + VMEM_MARGIN > VMEM_MOST:
        tm = _pick(M, 512)
    vmem = need(tm)
    ni, nj, nk = M // tm, N // tn, K // tk

    def body(*refs):
        ins, (o_ref,), (acc_ref,), hold = _split_refs(refs, 2 + has_bias + has_add, 1, cargo)
        a_ref, b_ref = ins[0], ins[1]
        bias_ref = ins[2] if has_bias else None
        add_ref = ins[2 + has_bias] if has_add else None
        i, j, k = pl.program_id(0), pl.program_id(1), pl.program_id(2)
        _cargo_start(cargo, (i == 0) & (j == 0) & (k == 0), hold)

        @pl.when(k == 0)
        def _():
            acc_ref[...] = jnp.zeros_like(acc_ref)

        acc_ref[...] += lax.dot_general(a_ref[...], b_ref[...], _DIMS[mode], preferred_element_type=f32)

        @pl.when(k == nk - 1)
        def _():
            r = acc_ref[...]
            if has_bias:
                r = r + bias_ref[...]
            if has_add:
                r = r + add_ref[...]
            o_ref[...] = r

        _cargo_wait(cargo, (i == ni - 1) & (j == nj - 1) & (k == nk - 1), hold)

    if mode == "nn":
        specs = [pl.BlockSpec((tm, tk), lambda i, j, k: (i, k)), pl.BlockSpec((tk, tn), lambda i, j, k: (k, j))]
    elif mode == "nt":
        specs = [pl.BlockSpec((tm, tk), lambda i, j, k: (i, k)), pl.BlockSpec((tn, tk), lambda i, j, k: (j, k))]
    else:
        specs = [pl.BlockSpec((tk, tm), lambda i, j, k: (k, i)), pl.BlockSpec((tk, tn), lambda i, j, k: (k, j))]
    ops = [a, b]
    if has_bias:
        specs.append(pl.BlockSpec((1, tn), lambda i, j, k: (0, j)))
        ops.append(bias)
    if has_add:
        specs.append(pl.BlockSpec((tm, tn), lambda i, j, k: (i, j)))
        ops.append(add)
    c_in, c_out, c_shape, c_sems = _cargo_specs(cargo)
    outs = pl.pallas_call(
        body, name=name, grid=(ni, nj, nk), in_specs=specs + c_in,
        out_specs=[pl.BlockSpec((tm, tn), lambda i, j, k: (i, j))] + c_out,
        out_shape=[jax.ShapeDtypeStruct((M, N), f32)] + c_shape, scratch_shapes=[pltpu.VMEM((tm, tn), f32)] + c_sems,
        compiler_params=pltpu.CompilerParams(
            dimension_semantics=("arbitrary",) * 3 if cargo else ("parallel", "parallel", "arbitrary"),
            vmem_limit_bytes=min(vmem + VMEM_MARGIN, VMEM_MOST)))(*ops, *(cargo.arrays if cargo else ()))
    return outs if cargo else outs[0]


def _colsum(a, name):
    S, N = a.shape
    tn = _tile(N, 512)

    def body(a_ref, o_ref):
        o_ref[...] = jnp.sum(a_ref[...], axis=0, keepdims=True)

    return pl.pallas_call(body, name=name, grid=(N // tn,), in_specs=[pl.BlockSpec((S, tn), lambda j: (0, j))],
                          out_specs=pl.BlockSpec((1, tn), lambda j: (0, j)), out_shape=jax.ShapeDtypeStruct((1, N), f32),
                          compiler_params=_params("parallel"))(a)


def _ln_fn(alpha, x, y, g, b):
    z = alpha * x + y
    mu = jnp.mean(z, axis=-1, keepdims=True)
    zc = z - mu
    var = jnp.mean(zc * zc, axis=-1, keepdims=True)
    return zc * lax.rsqrt(var + LN_EPS) * g + b


def _ln_fwd(x, y, g, b, alpha, name):
    S, D = x.shape
    tr = _tile(S, 256)

    def body(x_ref, y_ref, g_ref, b_ref, o_ref):
        o_ref[...] = _ln_fn(alpha, x_ref[...], y_ref[...], g_ref[...], b_ref[...])

    row = pl.BlockSpec((tr, D), lambda i: (i, 0))
    par = pl.BlockSpec((1, D), lambda i: (0, 0))
    return pl.pallas_call(body, name=name, grid=(S // tr,), in_specs=[row, row, par, par], out_specs=row,
                          out_shape=jax.ShapeDtypeStruct((S, D), f32), compiler_params=_params("parallel"))(x, y, g, b)


def _ln_bwd(x, y, g, b, dout, alpha, name):
    S, D = x.shape
    tr = _tile(S, 256)

    def body(x_ref, y_ref, g_ref, b_ref, d_ref, dx_ref, dy_ref, dg_ref, db_ref):
        _, vjp = jax.vjp(functools.partial(_ln_fn, alpha), x_ref[...], y_ref[...], g_ref[...], b_ref[...])
        dx, dy, dg, db = vjp(d_ref[...])
        dx_ref[...] = dx
        dy_ref[...] = dy.astype(dy_ref.dtype)

        @pl.when(pl.program_id(0) == 0)
        def _():
            dg_ref[...] = jnp.zeros_like(dg_ref)
            db_ref[...] = jnp.zeros_like(db_ref)

        dg_ref[...] += dg
        db_ref[...] += db

    row = pl.BlockSpec((tr, D), lambda i: (i, 0))
    par = pl.BlockSpec((1, D), lambda i: (0, 0))
    sd = jax.ShapeDtypeStruct
    return pl.pallas_call(body, name=name, grid=(S // tr,), in_specs=[row, row, par, par, row],
                          out_specs=[row, row, par, par],
                          out_shape=[sd((S, D), f32), sd((S, D), bf16), sd((1, D), f32), sd((1, D), f32)],
                          compiler_params=_params("arbitrary"))(x, y, g, b, dout)


def _loss_head(y, t, name):
    S, D = y.shape
    tr = _tile(S, 256)

    def body(y_ref, t_ref, l_ref, d_ref):
        e = y_ref[...] - t_ref[...]
        d_ref[...] = e / D

        @pl.when(pl.program_id(0) == 0)
        def _():
            l_ref[...] = jnp.zeros_like(l_ref)

        l_ref[...] += 0.5 * jnp.sum(jnp.mean(e * e, axis=-1, keepdims=True))

    row = pl.BlockSpec((tr, D), lambda i: (i, 0))
    return pl.pallas_call(body, name=name, grid=(S // tr,), in_specs=[row, row],
                          out_specs=[pl.BlockSpec((8, LANES), lambda i: (0, 0)), row],
                          out_shape=[jax.ShapeDtypeStruct((8, LANES), f32), jax.ShapeDtypeStruct((S, D), f32)],
                          compiler_params=_params("arbitrary"))(y, t)


def _merge_fn(g0, g1, g2, pa, pb, pc):
    return jax.nn.sigmoid(g0) * pa + jax.nn.sigmoid(g1) * pb + jax.nn.sigmoid(g2) * pc


def _merge_specs(S, D, gate_off):
    tr = _tile(S, 512)
    tc = _tile(math.gcd(gate_off, D), 512)
    gates = [pl.BlockSpec((tr, tc), functools.partial(lambda k, i, j: (i, (gate_off + k * D) // tc + j), k)) for k in range(3)]
    tile = pl.BlockSpec((tr, tc), lambda i, j: (i, j))
    return tr, tc, gates, tile


def _merge_fwd(proj, pa, pb, pc, gate_off, name):
    S, D = pa.shape
    tr, tc, gates, tile = _merge_specs(S, D, gate_off)

    def body(g0, g1, g2, a, b, c, o_ref):
        o_ref[...] = _merge_fn(g0[...], g1[...], g2[...], a[...], b[...], c[...]).astype(o_ref.dtype)

    return pl.pallas_call(body, name=name, grid=(S // tr, D // tc), in_specs=gates + [tile] * 3, out_specs=tile,
                          out_shape=jax.ShapeDtypeStruct((S, D), bf16),
                          compiler_params=_params("parallel", "parallel"))(proj, proj, proj, pa, pb, pc)


def _merge_bwd(proj, pa, pb, pc, dm, gate_off, name):
    S, D = pa.shape
    tr, tc, gates, tile = _merge_specs(S, D, gate_off)

    def body(g0, g1, g2, a, b, c, d, dg0, dg1, dg2, da, db, dc):
        _, vjp = jax.vjp(_merge_fn, g0[...], g1[...], g2[...], a[...], b[...], c[...])
        for ref, val in zip((dg0, dg1, dg2, da, db, dc), vjp(d[...])):
            ref[...] = val.astype(ref.dtype)

    sd = jax.ShapeDtypeStruct
    return pl.pallas_call(body, name=name, grid=(S // tr, D // tc), in_specs=gates + [tile] * 4,
                          out_specs=[tile] * 6, out_shape=[sd((S, D), f32)] * 3 + [sd((S, D), bf16)] * 3,
                          compiler_params=_params("parallel", "parallel"))(proj, proj, proj, pa, pb, pc, dm)


def _sgu_fn(nb, u, v, ln_g, ln_b, w_s, b_s):
    mu = jnp.mean(v, axis=-1, keepdims=True)
    vc = v - mu
    var = jnp.mean(vc * vc, axis=-1, keepdims=True)
    vn = vc * lax.rsqrt(var + LN_EPS) * ln_g + ln_b
    r = lax.broadcasted_iota(jnp.int32, (SGU_SPAN, SGU_SPAN), 0) // CHUNK
    c = lax.broadcasted_iota(jnp.int32, (SGU_SPAN, SGU_SPAN), 1) // CHUNK
    wm = jnp.where(r >= c, w_s, 0.0)
    vn3 = vn.reshape(nb, SGU_SPAN, HEAD_DIM)
    mixed = lax.dot_general(jnp.broadcast_to(wm, (nb, SGU_SPAN, SGU_SPAN)), vn3, (((2,), (1,)), ((0,), (0,))),
                            preferred_element_type=f32)
    mixed = mixed + b_s
    return u * mixed.reshape(nb * SGU_SPAN, HEAD_DIM)


def _sgu_specs(S, G, u_off, v_off):
    nb = max(1, min(8, S // SGU_SPAN))
    rows = nb * SGU_SPAN
    ub = pl.BlockSpec((rows, HEAD_DIM), lambda g, n: (n, u_off // HEAD_DIM + g))
    vb = pl.BlockSpec((rows, HEAD_DIM), lambda g, n: (n, v_off // HEAD_DIM + g))
    lnb = pl.BlockSpec((1, HEAD_DIM), lambda g, n: (0, g))
    wb = pl.BlockSpec((None, SGU_SPAN, SGU_SPAN), lambda g, n: (g, 0, 0))
    bb = pl.BlockSpec((None, SGU_SPAN, 1), lambda g, n: (g, 0, 0))
    return nb, rows, ub, vb, lnb, wb, bb


def _sgu_fwd(proj, ln_g, ln_b, w_s, b_s, u_off, v_off, name):
    S = proj.shape[0]
    G = w_s.shape[0]
    nb, rows, ub, vb, lnb, wb, bb = _sgu_specs(S, G, u_off, v_off)

    def body(u, v, lg, lb, w, b, o_ref):
        o_ref[...] = _sgu_fn(nb, u[...], v[...], lg[...], lb[...], w[...], b[...]).astype(o_ref.dtype)

    return pl.pallas_call(body, name=name, grid=(G, S // rows), in_specs=[ub, vb, lnb, lnb, wb, bb],
                          out_specs=pl.BlockSpec((rows, HEAD_DIM), lambda g, n: (n, g)),
                          out_shape=jax.ShapeDtypeStruct((S, G * HEAD_DIM), bf16),
                          compiler_params=_params("parallel", "parallel"))(proj, proj, ln_g, ln_b, w_s, b_s)


def _sgu_bwd(proj, ln_g, ln_b, w_s, b_s, dy, u_off, v_off, name):
    S = proj.shape[0]
    G = w_s.shape[0]
    nb, rows, ub, vb, lnb, wb, bb = _sgu_specs(S, G, u_off, v_off)

    def body(u, v, lg, lb, w, b, d, du, dv, dlg, dlb, dw, db):
        _, vjp = jax.vjp(functools.partial(_sgu_fn, nb), u[...], v[...], lg[...], lb[...], w[...], b[...])
        gu, gv, glg, glb, gw, gb = vjp(d[...])
        du[...] = gu
        dv[...] = gv

        @pl.when(pl.program_id(1) == 0)
        def _():
            for ref in (dlg, dlb, dw, db):
                ref[...] = jnp.zeros_like(ref)

        dlg[...] += glg
        dlb[...] += glb
        dw[...] += gw
        db[...] += gb

    tile = pl.BlockSpec((rows, HEAD_DIM), lambda g, n: (n, g))
    sd = jax.ShapeDtypeStruct
    W = G * HEAD_DIM
    return pl.pallas_call(body, name=name, grid=(G, S // rows), in_specs=[ub, vb, lnb, lnb, wb, bb, tile],
                          out_specs=[tile, tile, lnb, lnb, wb, bb],
                          out_shape=[sd((S, W), f32), sd((S, W), f32), sd((1, W), f32), sd((1, W), f32),
                                     sd((G, SGU_SPAN, SGU_SPAN), f32), sd((G, SGU_SPAN, 1), f32)],
                          compiler_params=_params("parallel", "arbitrary"))(proj, proj, ln_g, ln_b, w_s, b_s, dy)


def _shift_down(x, k):
    if k == 0:
        return x
    rows = lax.broadcasted_iota(jnp.int32, x.shape, 0)
    return jnp.where(rows >= k, pltpu.roll(x, k, 0), 0.0)


def _shift_up(x, k):
    if k == 0:
        return x
    n = x.shape[0]
    rows = lax.broadcasted_iota(jnp.int32, x.shape, 0)
    return jnp.where(rows < n - k, pltpu.roll(x, n - k, 0), 0.0)


def _conv(x, w_ref, width):
    out = w_ref[width - 1] * x
    for j in range(width - 1):
        out = out + w_ref[j] * _shift_down(x, width - 1 - j)
    return out


def _conv_bwd(x, dz, w_ref, dw_ref, width):
    dx = w_ref[width - 1] * dz
    dw_ref[width - 1] = jnp.sum(dz * x, axis=0, keepdims=True)
    for j in range(width - 1):
        k = width - 1 - j
        dx = dx + w_ref[j] * _shift_up(dz, k)
        dw_ref[j] = jnp.sum(dz * _shift_down(x, k), axis=0, keepdims=True)
    return dx


def _silu(z):
    return z * jax.nn.sigmoid(z)


def _silu_and_slope(z):
    s = jax.nn.sigmoid(z)
    return z * s, s * (1.0 + z * (1.0 - s))


def _ffn_act_fwd(upg, upv, cwg, cwv, cbg, cbv, name):
    S, Fp = upg.shape

    def body(g_ref, v_ref, wg, wv, bg, bv, o_ref):
        hg = _conv(g_ref[...], wg, FFN_CONV) + bg[...]
        hv = _conv(v_ref[...], wv, FFN_CONV) + bv[...]
        o_ref[...] = (_silu(hg) * hv).astype(o_ref.dtype)

    col = pl.BlockSpec((S, LANES), lambda j: (0, j))
    wsp = pl.BlockSpec((FFN_CONV, 1, LANES), lambda j: (0, 0, j))
    bsp = pl.BlockSpec((1, LANES), lambda j: (0, j))
    return pl.pallas_call(body, name=name, grid=(Fp // LANES,), in_specs=[col, col, wsp, wsp, bsp, bsp], out_specs=col,
                          out_shape=jax.ShapeDtypeStruct((S, Fp), bf16),
                          compiler_params=_params("parallel"))(upg, upv, cwg, cwv, cbg, cbv)


def _ffn_act_bwd(upg, upv, cwg, cwv, cbg, cbv, dact, name):
    S, Fp = upg.shape

    def body(g_ref, v_ref, wg, wv, bg, bv, d_ref, dg_ref, dv_ref, dwg, dwv, dbg, dbv):
        xg, xv, d = g_ref[...], v_ref[...], d_ref[...]
        hg = _conv(xg, wg, FFN_CONV) + bg[...]
        hv = _conv(xv, wv, FFN_CONV) + bv[...]
        act_g, slope_g = _silu_and_slope(hg)
        dhg = d * hv * slope_g
        dhv = d * act_g
        dbg[...] = jnp.sum(dhg, axis=0, keepdims=True)
        dbv[...] = jnp.sum(dhv, axis=0, keepdims=True)
        dg_ref[...] = _conv_bwd(xg, dhg, wg, dwg, FFN_CONV).astype(dg_ref.dtype)
        dv_ref[...] = _conv_bwd(xv, dhv, wv, dwv, FFN_CONV).astype(dv_ref.dtype)

    col = pl.BlockSpec((S, LANES), lambda j: (0, j))
    wsp = pl.BlockSpec((FFN_CONV, 1, LANES), lambda j: (0, 0, j))
    bsp = pl.BlockSpec((1, LANES), lambda j: (0, j))
    sd = jax.ShapeDtypeStruct
    return pl.pallas_call(body, name=name, grid=(Fp // LANES,), in_specs=[col, col, wsp, wsp, bsp, bsp, col],
                          out_specs=[col, col, wsp, wsp, bsp, bsp],
                          out_shape=[sd((S, Fp), bf16), sd((S, Fp), bf16), sd((FFN_CONV, 1, Fp), f32), sd((FFN_CONV, 1, Fp), f32),
                                     sd((1, Fp), f32), sd((1, Fp), f32)],
                          compiler_params=_params("parallel"))(upg, upv, cwg, cwv, cbg, cbv, dact)


def _gdn_pre_fwd(proj, cw, x_off, n_norm, name):
    S = proj.shape[0]
    C = cw.shape[2]

    def body(x_ref, w_ref, o_ref):
        s = _silu(_conv(x_ref[...], w_ref, GDN_CONV))
        r = lax.rsqrt(jnp.sum(s * s, axis=-1, keepdims=True) + RMS_EPS)
        o_ref[...] = jnp.where(pl.program_id(0) < n_norm, s * r, s)

    xs = pl.BlockSpec((S, LANES), lambda j: (0, x_off // LANES + j))
    col = pl.BlockSpec((S, LANES), lambda j: (0, j))
    wsp = pl.BlockSpec((GDN_CONV, 1, LANES), lambda j: (0, 0, j))
    return pl.pallas_call(body, name=name, grid=(C // LANES,), in_specs=[xs, wsp], out_specs=col,
                          out_shape=jax.ShapeDtypeStruct((S, C), f32), compiler_params=_params("parallel"))(proj, cw)


def _gdn_pre_bwd(proj, cw, dout, x_off, n_norm, name):
    S = proj.shape[0]
    C = cw.shape[2]

    def body(x_ref, w_ref, d_ref, dx_ref, dw_ref):
        x, d = x_ref[...], d_ref[...]
        z = _conv(x, w_ref, GDN_CONV)
        s, slope = _silu_and_slope(z)
        r = lax.rsqrt(jnp.sum(s * s, axis=-1, keepdims=True) + RMS_EPS)
        ds_norm = d * r - s * (r * r * r) * jnp.sum(d * s, axis=-1, keepdims=True)
        ds = jnp.where(pl.program_id(0) < n_norm, ds_norm, d)
        dz = ds * slope
        dx_ref[...] = _conv_bwd(x, dz, w_ref, dw_ref, GDN_CONV)

    xs = pl.BlockSpec((S, LANES), lambda j: (0, x_off // LANES + j))
    col = pl.BlockSpec((S, LANES), lambda j: (0, j))
    wsp = pl.BlockSpec((GDN_CONV, 1, LANES), lambda j: (0, 0, j))
    return pl.pallas_call(body, name=name, grid=(C // LANES,), in_specs=[xs, wsp, col], out_specs=[col, wsp],
                          out_shape=[jax.ShapeDtypeStruct((S, C), f32), jax.ShapeDtypeStruct((GDN_CONV, 1, C), f32)],
                          compiler_params=_params("parallel"))(proj, cw, dout)


def _bmm(a, b, prec=None):
    return lax.dot_general(a, b, (((2,), (1,)), ((0,), (0,))), precision=prec, preferred_element_type=f32)


def _bmm_nt(a, b, prec=None):
    return lax.dot_general(a, b, (((2,), (2,)), ((0,), (0,))), precision=prec, preferred_element_type=f32)


def _bmm_tn(a, b, prec=None):
    return lax.dot_general(a, b, (((1,), (1,)), ((0,), (0,))), precision=prec, preferred_element_type=f32)


def _softplus(x):
    return jnp.maximum(x, 0.0) + jnp.log1p(jnp.exp(-jnp.abs(x)))


@jax.custom_vjp
def _unit_lower_inverse(a):
    H, C, _ = a.shape
    r = lax.broadcasted_iota(jnp.int32, (H, C, C), 1)
    c = lax.broadcasted_iota(jnp.int32, (H, C, C), 2)
    p = -a
    inv = (r == c).astype(f32) + p
    for _ in range(int(math.log2(C)) - 1):
        p = _bmm(p, p, HIGHEST)
        inv = inv + _bmm(inv, p, HIGHEST)
    return inv


def _unit_lower_inverse_fwd(a):
    inv = _unit_lower_inverse(a)
    return inv, inv


def _unit_lower_inverse_bwd(inv, d_inv):
    return (-_bmm_nt(_bmm_tn(inv, d_inv, HIGHEST), inv, HIGHEST),)


_unit_lower_inverse.defvjp(_unit_lower_inverse_fwd, _unit_lower_inverse_bwd)


def _gdn_chunk(q, k, v, al, bl, gate, a_log, dt_bias, norm_g, state):
    H, C, Dh = q.shape
    r = lax.broadcasted_iota(jnp.int32, (H, C, C), 1)
    c = lax.broadcasted_iota(jnp.int32, (H, C, C), 2)
    tril = r >= c
    strict = r > c
    lower = tril.astype(f32)
    upper = (r <= c).astype(f32)
    ones = jnp.ones((H, C, C), f32)
    g = -jnp.exp(a_log) * _softplus(al + dt_bias)
    beta = jax.nn.sigmoid(bl)
    g_lanes = jnp.broadcast_to(g, (H, C, Dh))
    g_sq = jnp.broadcast_to(g, (H, C, C))
    gc = _bmm(lower, g_lanes, HIGHEST)
    gc_i = _bmm(lower, g_sq, HIGHEST)
    gc_j = _bmm(ones, g_sq * upper, HIGHEST)
    decay = jnp.where(tril, jnp.exp(jnp.where(tril, gc_i - gc_j, 0.0)), 0.0)
    qs = q * (Dh ** -0.5)
    kb = k * beta
    a_kk = jnp.where(strict, _bmm_nt(kb, k) * decay, 0.0)
    rhs_u = v * beta
    rhs_w = kb * jnp.exp(gc)
    inv = _unit_lower_inverse(a_kk)
    u = _bmm(inv, rhs_u, HIGHEST)
    w = _bmm(inv, rhs_w, HIGHEST)
    qk = jnp.where(tril, _bmm_nt(qs, k) * decay, 0.0)
    g_last = jnp.sum(g, axis=1, keepdims=True)
    k_dec = k * jnp.exp(g_last - gc)
    q_dec = qs * jnp.exp(gc)
    v_new = u - _bmm(w, state)
    o = _bmm(q_dec, state) + _bmm(qk, v_new)
    new_state = state * jnp.exp(g_last) + _bmm_tn(k_dec, v_new)
    y = o * lax.rsqrt(jnp.mean(o * o, axis=-1, keepdims=True) + RMS_EPS) * norm_g * _silu(gate)
    return y, new_state


def _heads(ref, off, H):
    return jnp.stack([ref[:, off + h * HEAD_DIM: off + (h + 1) * HEAD_DIM] for h in range(H)])


def _gdn_scan_fwd(qkvc, al, bl, proj, a_log, dt_bias, norm_g, gate_off, name, cargo=None):
    S = qkvc.shape[0]
    H = al.shape[0]
    W = H * HEAD_DIM
    n = S // CHUNK

    def body(*refs):
        (x_ref, al_ref, bl_ref, gate_ref, alog_ref, dt_ref, ng_ref), (y_ref, st_ref), (state,), hold = _split_refs(refs, 7, 2, cargo)
        _cargo_start(cargo, pl.program_id(0) == 0, hold)

        @pl.when(pl.program_id(0) == 0)
        def _():
            state[...] = jnp.zeros_like(state)

        st_ref[...] = state[...]
        y, new = _gdn_chunk(_heads(x_ref, 0, H), _heads(x_ref, W, H), _heads(x_ref, 2 * W, H), al_ref[...], bl_ref[...],
                            _heads(gate_ref, 0, H), alog_ref[...], dt_ref[...], ng_ref[...], state[...])
        state[...] = new
        for h in range(H):
            y_ref[:, h * HEAD_DIM:(h + 1) * HEAD_DIM] = y[h].astype(y_ref.dtype)
        _cargo_wait(cargo, pl.program_id(0) == n - 1, hold)

    sd = jax.ShapeDtypeStruct
    col = pl.BlockSpec((H, CHUNK, 1), lambda i: (0, i, 0))
    par = pl.BlockSpec((H, 1, 1), lambda i: (0, 0, 0))
    c_in, c_out, c_shape, c_sems = _cargo_specs(cargo)
    return pl.pallas_call(
        body, name=name, grid=(n,),
        in_specs=[pl.BlockSpec((CHUNK, 3 * W), lambda i: (i, 0)), col, col,
                  pl.BlockSpec((CHUNK, W), lambda i: (i, gate_off // W)), par, par,
                  pl.BlockSpec((1, 1, HEAD_DIM), lambda i: (0, 0, 0))] + c_in,
        out_specs=[pl.BlockSpec((CHUNK, W), lambda i: (i, 0)),
                   pl.BlockSpec((None, H, HEAD_DIM, HEAD_DIM), lambda i: (i, 0, 0, 0))] + c_out,
        out_shape=[sd((S, W), bf16), sd((n, H, HEAD_DIM, HEAD_DIM), f32)] + c_shape,
        scratch_shapes=[pltpu.VMEM((H, HEAD_DIM, HEAD_DIM), f32)] + c_sems,
        compiler_params=_params("arbitrary"))(qkvc, al, bl, proj, a_log, dt_bias, norm_g, *(cargo.arrays if cargo else ()))


def _gdn_scan_bwd(qkvc, al, bl, proj, a_log, dt_bias, norm_g, states, dy, gate_off, name, cargo=None):
    S = qkvc.shape[0]
    H = al.shape[0]
    W = H * HEAD_DIM
    n = S // CHUNK

    def body(*refs):
        ins, outs, (dstate,), hold = _split_refs(refs, 9, 7, cargo)
        x_ref, al_ref, bl_ref, gate_ref, alog_ref, dt_ref, ng_ref, st_ref, dy_ref = ins
        dx_ref, dal_ref, dbl_ref, dgate_ref, dalog_ref, ddt_ref, dng_ref = outs
        _cargo_start(cargo, pl.program_id(0) == 0, hold)

        @pl.when(pl.program_id(0) == 0)
        def _():
            dstate[...] = jnp.zeros_like(dstate)
            for ref in (dalog_ref, ddt_ref, dng_ref):
                ref[...] = jnp.zeros_like(ref)

        _, vjp = jax.vjp(_gdn_chunk, _heads(x_ref, 0, H), _heads(x_ref, W, H), _heads(x_ref, 2 * W, H), al_ref[...],
                         bl_ref[...], _heads(gate_ref, 0, H), alog_ref[...], dt_ref[...], ng_ref[...], st_ref[...])
        dq, dk, dv, dal, dbl, dgate, dalog, ddt, dng, dst = vjp((_heads(dy_ref, 0, H), dstate[...]))
        dstate[...] = dst
        for h in range(H):
            lo, hi = h * HEAD_DIM, (h + 1) * HEAD_DIM
            dx_ref[:, lo:hi] = dq[h]
            dx_ref[:, W + lo:W + hi] = dk[h]
            dx_ref[:, 2 * W + lo:2 * W + hi] = dv[h]
            dgate_ref[:, lo:hi] = dgate[h]
        dal_ref[...] = dal
        dbl_ref[...] = dbl
        dalog_ref[...] += dalog
        ddt_ref[...] += ddt
        dng_ref[...] += dng

        _cargo_wait(cargo, pl.program_id(0) == n - 1, hold)

    sd = jax.ShapeDtypeStruct
    c_in, c_out, c_shape, c_sems = _cargo_specs(cargo)
    rev = lambda i: n - 1 - i
    col = pl.BlockSpec((H, CHUNK, 1), lambda i: (0, rev(i), 0))
    par = pl.BlockSpec((H, 1, 1), lambda i: (0, 0, 0))
    ng = pl.BlockSpec((1, 1, HEAD_DIM), lambda i: (0, 0, 0))
    xs = pl.BlockSpec((CHUNK, 3 * W), lambda i: (rev(i), 0))
    ws = pl.BlockSpec((CHUNK, W), lambda i: (rev(i), 0))
    return pl.pallas_call(
        body, name=name, grid=(n,),
        in_specs=[xs, col, col, pl.BlockSpec((CHUNK, W), lambda i: (rev(i), gate_off // W)), par, par, ng,
                  pl.BlockSpec((None, H, HEAD_DIM, HEAD_DIM), lambda i: (rev(i), 0, 0, 0)), ws] + c_in,
        out_specs=[xs, col, col, ws, par, par, ng] + c_out,
        out_shape=[sd((S, 3 * W), f32), sd((H, S, 1), f32), sd((H, S, 1), f32), sd((S, W), f32), sd((H, 1, 1), f32),
                   sd((H, 1, 1), f32), sd((1, 1, HEAD_DIM), f32)] + c_shape,
        scratch_shapes=[pltpu.VMEM((H, HEAD_DIM, HEAD_DIM), f32)] + c_sems,
        compiler_params=_params("arbitrary"))(qkvc, al, bl, proj, a_log, dt_bias, norm_g, states, dy,
                                              *(cargo.arrays if cargo else ()))


def _tri(n, upper):
    r = lax.broadcasted_iota(jnp.int32, (n, n), 0)
    c = lax.broadcasted_iota(jnp.int32, (n, n), 1)
    return (r <= c if upper else r >= c).astype(f32)


def _fox_prep_fwd(tail, name):
    S = tail.shape[0]
    tb = _tile(S, 512)

    def body(x_ref, o_ref, carry):
        @pl.when(pl.program_id(0) == 0)
        def _():
            carry[...] = jnp.zeros_like(carry)

        x = x_ref[...]
        lf = jnp.minimum(x, 0.0) - jnp.log1p(jnp.exp(-jnp.abs(x)))
        o_ref[...] = jnp.dot(_tri(tb, False), lf, precision=HIGHEST, preferred_element_type=f32) + carry[...]
        carry[...] += jnp.sum(lf, axis=0, keepdims=True)

    blk = pl.BlockSpec((tb, LANES), lambda i: (i, 0))
    return pl.pallas_call(body, name=name, grid=(S // tb,), in_specs=[blk], out_specs=blk,
                          out_shape=jax.ShapeDtypeStruct((S, LANES), f32), scratch_shapes=[pltpu.VMEM((1, LANES), f32)],
                          compiler_params=_params("arbitrary"))(tail)


def _fox_prep_bwd(tail, dc_q, dc_k, name):
    S = tail.shape[0]
    tb = _tile(S, 512)
    nb = S // tb

    def body(x_ref, dq_ref, d_ref, o_ref, carry):
        @pl.when(pl.program_id(0) == 0)
        def _():
            carry[...] = jnp.zeros_like(carry)

        d = d_ref[...] + dq_ref[...]
        dlf = jnp.dot(_tri(tb, True), d, precision=HIGHEST, preferred_element_type=f32) + carry[...]
        carry[...] += jnp.sum(d, axis=0, keepdims=True)
        o_ref[...] = dlf * jax.nn.sigmoid(-x_ref[...])

    blk = pl.BlockSpec((tb, LANES), lambda i: (nb - 1 - i, 0))
    return pl.pallas_call(body, name=name, grid=(nb,), in_specs=[blk, blk, blk], out_specs=blk,
                          out_shape=jax.ShapeDtypeStruct((S, LANES), f32), scratch_shapes=[pltpu.VMEM((1, LANES), f32)],
                          compiler_params=_params("arbitrary"))(tail, dc_q, dc_k)


def _dot_nt(a, b):
    return lax.dot_general(a.astype(bf16), b.astype(bf16), _DIMS["nt"], preferred_element_type=f32)


def _dot_tn(a, b):
    return lax.dot_general(a.astype(bf16), b.astype(bf16), _DIMS["tn"], preferred_element_type=f32)


def _dot_nn(a, b):
    return lax.dot_general(a.astype(bf16), b.astype(bf16), _DIMS["nn"], preferred_element_type=f32)


def _fox_logits(q, k, cc, cr, T, diagonal):
    s = _dot_nt(q * (HEAD_DIM ** -0.5), k) + cc - cr
    if not diagonal:
        return s
    return jnp.where(lax.broadcasted_iota(jnp.int32, (T, T), 0) >= lax.broadcasted_iota(jnp.int32, (T, T), 1), s, NEG_BIG)


def _fox_fwd(proj, c_col, c_row, H, name, cargo=None):
    S = proj.shape[0]
    T = _tile(S, ATTN_TILE)
    nt = S // T

    def body(*refs):
        (q_ref, k_ref, v_ref, cc_ref, cr_ref), (o_ref, lse_ref), (m_s, l_s, acc_s), hold = _split_refs(refs, 5, 2, cargo)
        h, i, j = pl.program_id(0), pl.program_id(1), pl.program_id(2)
        _cargo_start(cargo, (h == 0) & (i == 0) & (j == 0), hold)

        @pl.when(j == 0)
        def _():
            m_s[...] = jnp.full_like(m_s, NEG_BIG)
            l_s[...] = jnp.zeros_like(l_s)
            acc_s[...] = jnp.zeros_like(acc_s)

        def block(diagonal):
            s = _fox_logits(q_ref[...], k_ref[...], cc_ref[...], cr_ref[...], T, diagonal)
            m_new = jnp.maximum(m_s[...], jnp.max(s, axis=-1, keepdims=True))
            p = jnp.exp(s - m_new)
            corr = jnp.exp(m_s[...] - m_new)
            l_s[...] = corr * l_s[...] + jnp.sum(p, axis=-1, keepdims=True)
            acc_s[...] = corr * acc_s[...] + _dot_nn(p, v_ref[...])
            m_s[...] = m_new

        @pl.when(j < i)
        def _():
            block(False)

        @pl.when(j == i)
        def _():
            block(True)
            o_ref[...] = acc_s[...] / l_s[...]
            lse_ref[...] = m_s[...] + jnp.log(l_s[...])

        _cargo_wait(cargo, (h == H - 1) & (i == nt - 1) & (j == nt - 1), hold)

    sd = jax.ShapeDtypeStruct
    c_in, c_out, c_shape, c_sems = _cargo_specs(cargo)
    sem = ("arbitrary",) * 3 if cargo else ("parallel", "parallel", "arbitrary")
    return pl.pallas_call(
        body, name=name, grid=(H, nt, nt),
        in_specs=[pl.BlockSpec((T, HEAD_DIM), lambda h, i, j: (i, h)),
                  pl.BlockSpec((T, HEAD_DIM), lambda h, i, j: (jnp.minimum(j, i), H + h)),
                  pl.BlockSpec((T, HEAD_DIM), lambda h, i, j: (jnp.minimum(j, i), 2 * H + h)),
                  pl.BlockSpec((None, T, 1), lambda h, i, j: (h, i, 0)),
                  pl.BlockSpec((None, 1, T), lambda h, i, j: (h, 0, jnp.minimum(j, i)))] + c_in,
        out_specs=[pl.BlockSpec((T, HEAD_DIM), lambda h, i, j: (i, h)), pl.BlockSpec((None, T, 1), lambda h, i, j: (h, i, 0))] + c_out,
        out_shape=[sd((S, H * HEAD_DIM), f32), sd((H, S, 1), f32)] + c_shape,
        scratch_shapes=[pltpu.VMEM((T, 1), f32), pltpu.VMEM((T, 1), f32), pltpu.VMEM((T, HEAD_DIM), f32)] + c_sems,
        compiler_params=_params(*sem, vmem=_attn_vmem(T)))(proj, proj, proj, c_col, c_row, *(cargo.arrays if cargo else ()))


def _fox_bwd_q(proj, c_col, c_row, o, do, lse, H, name):
    S = proj.shape[0]
    T = _tile(S, ATTN_TILE)
    nt = S // T

    def body(q_ref, k_ref, v_ref, cc_ref, cr_ref, o_ref, do_ref, lse_ref, dq_ref, dc_ref, acc_s, dc_s):
        i, j = pl.program_id(1), pl.program_id(2)

        @pl.when(j == 0)
        def _():
            acc_s[...] = jnp.zeros_like(acc_s)
            dc_s[...] = jnp.zeros_like(dc_s)

        def block(diagonal):
            s = _fox_logits(q_ref[...], k_ref[...], cc_ref[...], cr_ref[...], T, diagonal)
            p = jnp.exp(s - lse_ref[...])
            do_ = do_ref[...]
            delta = jnp.sum(o_ref[...] * do_, axis=-1, keepdims=True)
            ds = p * (_dot_nt(do_, v_ref[...]) - delta)
            acc_s[...] += _dot_nn(ds, k_ref[...])
            dc_s[...] += jnp.sum(ds, axis=-1, keepdims=True)

        @pl.when(j < i)
        def _():
            block(False)

        @pl.when(j == i)
        def _():
            block(True)
            dq_ref[...] = acc_s[...] * (HEAD_DIM ** -0.5)
            dc_ref[...] = dc_s[...]

    qb = pl.BlockSpec((T, HEAD_DIM), lambda h, i, j: (i, h))
    col = pl.BlockSpec((None, T, 1), lambda h, i, j: (h, i, 0))
    return pl.pallas_call(
        body, name=name, grid=(H, nt, nt),
        in_specs=[qb, pl.BlockSpec((T, HEAD_DIM), lambda h, i, j: (jnp.minimum(j, i), H + h)),
                  pl.BlockSpec((T, HEAD_DIM), lambda h, i, j: (jnp.minimum(j, i), 2 * H + h)),
                  col, pl.BlockSpec((None, 1, T), lambda h, i, j: (h, 0, jnp.minimum(j, i))), qb, qb, col],
        out_specs=[qb, col], out_shape=[jax.ShapeDtypeStruct((S, H * HEAD_DIM), f32), jax.ShapeDtypeStruct((H, S, 1), f32)],
        scratch_shapes=[pltpu.VMEM((T, HEAD_DIM), f32), pltpu.VMEM((T, 1), f32)],
        compiler_params=_params("parallel", "parallel", "arbitrary", vmem=_attn_vmem(T)))(proj, proj, proj, c_col, c_row, o, do, lse)


def _fox_bwd_kv(proj, c_col, c_row, o, do, lse, H, name):
    S = proj.shape[0]
    T = _tile(S, ATTN_TILE)
    nt = S // T

    def body(q_ref, k_ref, v_ref, cc_ref, cr_ref, o_ref, do_ref, lse_ref, dk_ref, dv_ref, dc_ref, dk_s, dv_s, dc_s):
        j, i = pl.program_id(1), pl.program_id(2)

        @pl.when(i == 0)
        def _():
            dk_s[...] = jnp.zeros_like(dk_s)
            dv_s[...] = jnp.zeros_like(dv_s)
            dc_s[...] = jnp.zeros_like(dc_s)

        def block(diagonal):
            s = _fox_logits(q_ref[...], k_ref[...], cc_ref[...], cr_ref[...], T, diagonal)
            p = jnp.exp(s - lse_ref[...])
            do_ = do_ref[...]
            delta = jnp.sum(o_ref[...] * do_, axis=-1, keepdims=True)
            ds = p * (_dot_nt(do_, v_ref[...]) - delta)
            dv_s[...] += _dot_tn(p, do_)
            dk_s[...] += _dot_tn(ds, q_ref[...])
            dc_s[...] -= jnp.sum(ds, axis=0, keepdims=True)

        @pl.when(i > j)
        def _():
            block(False)

        @pl.when(i == j)
        def _():
            block(True)

        @pl.when(i == nt - 1)
        def _():
            dk_ref[...] = dk_s[...] * (HEAD_DIM ** -0.5)
            dv_ref[...] = dv_s[...]
            dc_ref[...] = dc_s[...]

    qb = pl.BlockSpec((T, HEAD_DIM), lambda h, j, i: (jnp.maximum(i, j), h))
    col = pl.BlockSpec((None, T, 1), lambda h, j, i: (h, jnp.maximum(i, j), 0))
    kb = pl.BlockSpec((T, HEAD_DIM), lambda h, j, i: (j, h))
    sd = jax.ShapeDtypeStruct
    return pl.pallas_call(
        body, name=name, grid=(H, nt, nt),
        in_specs=[qb, pl.BlockSpec((T, HEAD_DIM), lambda h, j, i: (j, H + h)),
                  pl.BlockSpec((T, HEAD_DIM), lambda h, j, i: (j, 2 * H + h)),
                  col, pl.BlockSpec((None, 1, T), lambda h, j, i: (h, 0, j)), qb, qb, col],
        out_specs=[kb, kb, pl.BlockSpec((None, 1, T), lambda h, j, i: (h, 0, j))],
        out_shape=[sd((S, H * HEAD_DIM), f32), sd((S, H * HEAD_DIM), f32), sd((H, 1, S), f32)],
        scratch_shapes=[pltpu.VMEM((T, HEAD_DIM), f32), pltpu.VMEM((T, HEAD_DIM), f32), pltpu.VMEM((1, T), f32)],
        compiler_params=_params("parallel", "parallel", "arbitrary", vmem=_attn_vmem(T)))(proj, proj, proj, c_col, c_row, o, do, lse)


def _adamw(w, g, m, v, name):
    shape = w.shape
    cols = shape[-1]
    rows = w.size // cols
    ops = [t.reshape(rows, cols) for t in (w, g, m, v)]
    tr = rows
    if rows % 8 == 0:
        tr = 8
        while tr * 2 <= rows and rows % (tr * 2) == 0 and tr * 2 * cols * 4 <= (1 << 20):
            tr *= 2

    def body(w_ref, g_ref, m_ref, v_ref, d_ref, mo_ref, vo_ref):
        g_ = g_ref[...]
        m_ = ADAM_B1 * m_ref[...] + (1.0 - ADAM_B1) * g_
        v_ = ADAM_B2 * v_ref[...] + (1.0 - ADAM_B2) * (g_ * g_)
        m_hat = m_ / (1.0 - ADAM_B1 ** ADAM_STEP)
        v_hat = v_ / (1.0 - ADAM_B2 ** ADAM_STEP)
        d_ref[...] = -ADAM_LR * (m_hat / (jnp.sqrt(v_hat) + ADAM_EPS) + ADAM_WD * w_ref[...])
        mo_ref[...] = m_
        vo_ref[...] = v_

    blk = pl.BlockSpec((tr, cols), lambda i: (i, 0))
    outs = pl.pallas_call(body, name=name, grid=(rows // tr,), in_specs=[blk] * 4, out_specs=[blk] * 3,
                          out_shape=[jax.ShapeDtypeStruct((rows, cols), f32)] * 3, compiler_params=_params("parallel"))(*ops)
    return [o.reshape(shape) for o in outs]


def _row_tile(rows, row_bytes, budget=2 << 20):
    best = None
    for t in range(16, rows + 1, 16):
        if rows % t == 0 and t * row_bytes <= budget:
            best = t
    return best or rows


def _sum_leading(a, name):
    n, R, C = a.shape
    tr = _row_tile(R, n * C * 4)

    def body(a_ref, o_ref):
        acc = a_ref[0].astype(f32)
        for k in range(1, n):
            acc = acc + a_ref[k].astype(f32)
        o_ref[...] = acc

    return pl.pallas_call(body, name=name, grid=(R // tr,), in_specs=[pl.BlockSpec((n, tr, C), lambda i: (0, i, 0))],
                          out_specs=pl.BlockSpec((tr, C), lambda i: (i, 0)), out_shape=jax.ShapeDtypeStruct((R, C), f32),
                          compiler_params=_params("parallel"))(a)


def _add2(a, b, dtype, name):
    n, R, C = a.shape
    rows = n * R
    tr = _row_tile(rows, C * 4)

    def body(a_ref, b_ref, o_ref):
        o_ref[...] = (a_ref[...] + b_ref[...]).astype(dtype)

    blk = pl.BlockSpec((tr, C), lambda i: (i, 0))
    out = pl.pallas_call(body, name=name, grid=(rows // tr,), in_specs=[blk, blk], out_specs=blk,
                         out_shape=jax.ShapeDtypeStruct((rows, C), dtype),
                         compiler_params=_params("parallel"))(a.reshape(rows, C), b.reshape(rows, C))
    return out.reshape(n, R, C)


def _place():
    x, y, c = lax.axis_index("x"), lax.axis_index("y"), lax.axis_index("c")
    chips = [(1 - x, y), (x, 1 - y), (1 - x, 1 - y)]
    return x, y, c, chips


def _gather_copies(ws_refs, out_refs, send_sems, recv_sems):
    x, y, c, chips = _place()
    sends, lands = [], []
    for t, (ws_ref, out_ref) in enumerate(zip(ws_refs, out_refs)):
        for j, chip in enumerate(chips):
            make = functools.partial(pltpu.make_async_remote_copy, src_ref=ws_ref.at[c], send_sem=send_sems.at[3 * t + j],
                                     recv_sem=recv_sems.at[3 * t + j], device_id=(*chip, c), device_id_type=MESH)
            sends.append(functools.partial(make, dst_ref=out_ref.at[2 * x + y, c]))
            lands.append(functools.partial(make, dst_ref=out_ref.at[2 * chip[0] + chip[1], c]))
    return sends, lands


def _gather_cargo(blocks):
    return _Cargo(tuple(blocks), tuple(jax.ShapeDtypeStruct((N_CHIPS,) + b.shape, b.dtype) for b in blocks), _gather_copies,
                  3 * len(blocks))


def _sibling_forward(gathered, blocks, name):
    n = len(gathered)

    def body(*refs):
        ins, own, outs, send_sems, recv_sems = refs[:n], refs[n:2 * n], refs[2 * n:3 * n], refs[3 * n], refs[3 * n + 1]
        x, y, c, chips = _place()
        sends, lands = [], []
        for t in range(n):
            for j, chip in enumerate(chips):
                s = 2 * chip[0] + chip[1]
                make = functools.partial(pltpu.make_async_remote_copy, src_ref=ins[t].at[s, c], send_sem=send_sems.at[4 * t + j],
                                         recv_sem=recv_sems.at[4 * t + j], device_id=(x, y, 1 - c), device_id_type=MESH)
                sends.append(make(dst_ref=outs[t].at[s, c]))
                lands.append(make(dst_ref=outs[t].at[s, 1 - c]))
            make = functools.partial(pltpu.make_async_remote_copy, src_ref=own[t], dst_ref=outs[t].at[2 * x + y],
                                     send_sem=send_sems.at[4 * t + 3], recv_sem=recv_sems.at[4 * t + 3], device_id=(x, y, 1 - c),
                                     device_id_type=MESH)
            sends.append(make())
            lands.append(make())
        for cp in sends:
            cp.start()
        for cp in lands:
            cp.wait_recv()
        for cp in sends:
            cp.wait_send()

    return pl.pallas_call(body, name=name, out_shape=[jax.ShapeDtypeStruct(g.shape, g.dtype) for g in gathered],
                          in_specs=[ANY] * (2 * n), out_specs=[ANY] * n, input_output_aliases={t: t for t in range(n)},
                          scratch_shapes=[pltpu.SemaphoreType.DMA((4 * n,)), pltpu.SemaphoreType.DMA((4 * n,))])(*gathered, *blocks)


def _pair_exchange(g, name):
    n, _, R, C = g.shape

    def body(g_ref, out_ref, send_sems, recv_sems):
        x, y, c, _ = _place()
        cps = [pltpu.make_async_remote_copy(src_ref=g_ref.at[s, 1 - c], dst_ref=out_ref.at[s], send_sem=send_sems.at[s],
                                            recv_sem=recv_sems.at[s], device_id=(x, y, 1 - c), device_id_type=MESH)
               for s in range(n)]
        for cp in cps:
            cp.start()
        for cp in cps:
            cp.wait()

    return pl.pallas_call(body, name=name, out_shape=jax.ShapeDtypeStruct((n, R, C), g.dtype), in_specs=[ANY], out_specs=ANY,
                          scratch_shapes=[pltpu.SemaphoreType.DMA((n,)), pltpu.SemaphoreType.DMA((n,))])(g)


def _chip_copies(a_refs, out_refs, send_sems, recv_sems):
    x, y, c, chips = _place()
    me = 2 * x + y
    sends, lands = [], []
    for t, (a_ref, out_ref) in enumerate(zip(a_refs, out_refs)):
        for j, chip in enumerate(chips):
            them = 2 * chip[0] + chip[1]
            make = functools.partial(pltpu.make_async_remote_copy, send_sem=send_sems.at[3 * t + j], recv_sem=recv_sems.at[3 * t + j],
                                     device_id=(*chip, c), device_id_type=MESH)
            sends.append(functools.partial(make, src_ref=a_ref.at[them], dst_ref=out_ref.at[me]))
            lands.append(functools.partial(make, src_ref=a_ref.at[me], dst_ref=out_ref.at[them]))
    return sends, lands


def _chip_cargo(pairs):
    return _Cargo(tuple(pairs), tuple(jax.ShapeDtypeStruct(p.shape, p.dtype) for p in pairs), _chip_copies, 3 * len(pairs))


def _pair_swap(r, name):
    R, C = r.shape

    def body(r_ref, out_ref, send_sem, recv_sem):
        x, y, c, _ = _place()
        cp = pltpu.make_async_remote_copy(src_ref=r_ref, dst_ref=out_ref, send_sem=send_sem, recv_sem=recv_sem,
                                          device_id=(x, y, 1 - c), device_id_type=MESH)
        cp.start()
        cp.wait()

    return pl.pallas_call(body, name=name, out_shape=jax.ShapeDtypeStruct((R, C), r.dtype), in_specs=[ANY], out_specs=ANY,
                          scratch_shapes=[pltpu.SemaphoreType.DMA(()), pltpu.SemaphoreType.DMA(())])(r)


def _all_gather8(v, name):
    R, C = v.shape

    def body(v_ref, out_ref, send_sems, recv_sems):
        x, y, c, _ = _place()
        me = 4 * x + 2 * y + c
        peers = [(x ^ (k >> 2), y ^ ((k >> 1) & 1), c ^ (k & 1)) for k in range(1, N_DEV)]
        sends = [pltpu.make_async_remote_copy(src_ref=v_ref, dst_ref=out_ref.at[me], send_sem=send_sems.at[k], recv_sem=recv_sems.at[k],
                                              device_id=peer, device_id_type=MESH) for k, peer in enumerate(peers)]
        for cp in sends:
            cp.start()
        for k, (px, py, pc) in enumerate(peers):
            pltpu.make_async_remote_copy(src_ref=v_ref, dst_ref=out_ref.at[4 * px + 2 * py + pc], send_sem=send_sems.at[k],
                                         recv_sem=recv_sems.at[k], device_id=(px, py, pc), device_id_type=MESH).wait_recv()
        for cp in sends:
            cp.wait_send()

    return pl.pallas_call(body, name=name, out_shape=jax.ShapeDtypeStruct((N_DEV, R, C), v.dtype), in_specs=[ANY], out_specs=ANY,
                          scratch_shapes=[pltpu.SemaphoreType.DMA((7,)), pltpu.SemaphoreType.DMA((7,))])(v)


def _as_list(x):
    return list(x) if isinstance(x, (list, tuple)) else [x]


def _put(buf, block, index):
    return lax.dynamic_update_slice(buf, block[None], (index,) + (0,) * block.ndim)


def _all_reduce8(v, device, name):
    n = v.shape[0]
    rows = -(-n // (LANES * SUM_ROWS)) * SUM_ROWS
    padded = jnp.pad(v, (0, rows * LANES - n)).reshape(rows, LANES)
    return _sum_leading(_put(_all_gather8(padded, name + "_gather"), padded, device), name + "_sum").reshape(-1)[:n]


def _halves(w):
    R, C = w.shape
    return w.astype(bf16).reshape(2, R // 2, C)


def _finish_gather(blocks, landed):
    full = _sibling_forward(landed, blocks, "gather_sibling_forward")
    return [g.reshape(N_CHIPS, 2 * b.shape[1], b.shape[2]) for g, b in zip(full, blocks)]


def _pair_sums(g4, core, name):
    _, R, C = g4.shape
    g = g4.reshape(N_CHIPS, 2, R // 2, C)
    mine = lax.dynamic_index_in_dim(g, core, axis=1, keepdims=False)
    return _add2(mine, _pair_exchange(g, name + "_pair_exchange"), bf16, name + "_pair_sum")


def _finish_reduce(pair, landed, chip, core, name):
    _, R2, C = pair.shape
    own = lax.dynamic_index_in_dim(pair, chip, axis=0, keepdims=False)
    half = _sum_leading(_put(landed, own, chip), name + "_chip_sum")
    both = jnp.stack([half, _pair_swap(half, name + "_pair_swap")])
    return jnp.where(core == 0, both, both[::-1]).reshape(2 * R2, C)


def _segments(H, D):
    W = H * HEAD_DIM
    sizes = (3 * W, H, 2 * W, 3 * W, H, H, W, 3 * D)
    in_tail = (False, True, False, False, True, True, False, False)
    out, first, used = [], 0, [0, 0]
    for size, t in zip(sizes, in_tail):
        out.append((first, size, t, used[t]))
        first += size
        used[t] += size
    return out


def _main_tail_from_shards(g4, H, D):
    C = g4.shape[-1]
    parts = ([], [])
    for first, size, t, _ in _segments(H, D):
        for s in range(g4.shape[0]):
            a, b = max(first, s * C), min(first + size, (s + 1) * C)
            if a < b:
                parts[t].append(g4[s][..., a - s * C:b - s * C])
    parts[1].append(jnp.zeros(g4.shape[1:-1] + (LANES - 3 * H,), g4.dtype))
    return jnp.concatenate(parts[0], axis=-1), jnp.concatenate(parts[1], axis=-1)


def _shards_from_main_tail(main, tail, H, D):
    segs = _segments(H, D)
    C = sum(size for _, size, _, _ in segs) // N_CHIPS
    shards = []
    for s in range(N_CHIPS):
        pieces = []
        for first, size, t, there in segs:
            a, b = max(first, s * C), min(first + size, (s + 1) * C)
            if a < b:
                pieces.append((tail if t else main)[..., there + a - first:there + b - first])
        shards.append(jnp.concatenate(pieces, axis=-1))
    return jnp.stack(shards)


def _pad_cols(a, n):
    return jnp.pad(a, [(0, 0)] * (a.ndim - 1) + [(0, n - a.shape[-1])])


def kernel(x, w_in, b_in, sgu_ln_g, sgu_ln_b, sgu_w, sgu_b, gdn_conv_w, gdn_a_log, gdn_dt_bias, gdn_norm_g, w_proj_a, w_proj_b, w_proj_c, w_out, ln1_g, ln1_b, ffn_w_up, ffn_conv_w, ffn_conv_b, ffn_w_down, ln2_g, ln2_b, loss_target, m_w_in, m_b_in, m_sgu_ln_g, m_sgu_ln_b, m_sgu_w, m_sgu_b, m_gdn_conv_w, m_gdn_a_log, m_gdn_dt_bias, m_gdn_norm_g, m_w_proj_a, m_w_proj_b, m_w_proj_c, m_w_out, m_ln1_g, m_ln1_b, m_ffn_w_up, m_ffn_conv_w, m_ffn_conv_b, m_ffn_w_down, m_ln2_g, m_ln2_b, v_w_in, v_b_in, v_sgu_ln_g, v_sgu_ln_b, v_sgu_w, v_sgu_b, v_gdn_conv_w, v_gdn_a_log, v_gdn_dt_bias, v_gdn_norm_g, v_w_proj_a, v_w_proj_b, v_w_proj_c, v_w_out, v_ln1_g, v_ln1_b, v_ffn_w_up, v_ffn_conv_w, v_ffn_conv_b, v_ffn_w_down, v_ln2_g, v_ln2_b):
    P = dict(w_in=w_in, b_in=b_in, sgu_ln_g=sgu_ln_g, sgu_ln_b=sgu_ln_b, sgu_w=sgu_w, sgu_b=sgu_b, gdn_conv_w=gdn_conv_w,
             gdn_a_log=gdn_a_log, gdn_dt_bias=gdn_dt_bias, gdn_norm_g=gdn_norm_g, w_proj_a=w_proj_a, w_proj_b=w_proj_b,
             w_proj_c=w_proj_c, w_out=w_out, ln1_g=ln1_g, ln1_b=ln1_b, ffn_w_up=ffn_w_up, ffn_conv_w=ffn_conv_w,
             ffn_conv_b=ffn_conv_b, ffn_w_down=ffn_w_down, ln2_g=ln2_g, ln2_b=ln2_b)
    M1 = dict(w_in=m_w_in, b_in=m_b_in, sgu_ln_g=m_sgu_ln_g, sgu_ln_b=m_sgu_ln_b, sgu_w=m_sgu_w, sgu_b=m_sgu_b,
              gdn_conv_w=m_gdn_conv_w, gdn_a_log=m_gdn_a_log, gdn_dt_bias=m_gdn_dt_bias, gdn_norm_g=m_gdn_norm_g,
              w_proj_a=m_w_proj_a, w_proj_b=m_w_proj_b, w_proj_c=m_w_proj_c, w_out=m_w_out, ln1_g=m_ln1_g, ln1_b=m_ln1_b,
              ffn_w_up=m_ffn_w_up, ffn_conv_w=m_ffn_conv_w, ffn_conv_b=m_ffn_conv_b, ffn_w_down=m_ffn_w_down, ln2_g=m_ln2_g,
              ln2_b=m_ln2_b)
    M2 = dict(w_in=v_w_in, b_in=v_b_in, sgu_ln_g=v_sgu_ln_g, sgu_ln_b=v_sgu_ln_b, sgu_w=v_sgu_w, sgu_b=v_sgu_b,
              gdn_conv_w=v_gdn_conv_w, gdn_a_log=v_gdn_a_log, gdn_dt_bias=v_gdn_dt_bias, gdn_norm_g=v_gdn_norm_g,
              w_proj_a=v_w_proj_a, w_proj_b=v_w_proj_b, w_proj_c=v_w_proj_c, w_out=v_w_out, ln1_g=v_ln1_g, ln1_b=v_ln1_b,
              ffn_w_up=v_ffn_w_up, ffn_conv_w=v_ffn_conv_w, ffn_conv_b=v_ffn_conv_b, ffn_w_down=v_ffn_w_down, ln2_g=v_ln2_g,
              ln2_b=v_ln2_b)
    _, S, D = x.shape
    L = w_in.shape[0]
    N_IN = w_in.shape[2] * N_CHIPS
    H = (N_IN - 3 * D) // (9 * HEAD_DIM + 3)
    W = H * HEAD_DIM
    F = ffn_w_down.shape[1] * N_CHIPS
    Fp = -(-F // FF_ALIGN) * FF_ALIGN
    NM = 9 * W + 3 * D
    alpha = (2 * L) ** 0.25
    cx, cy, cc = lax.axis_index("x"), lax.axis_index("y"), lax.axis_index("c")
    chip = 2 * cx + cy

    def blocks_of(l):
        return [_halves(w_in[l]), _halves(jnp.concatenate([w_proj_a[l], w_proj_b[l], w_proj_c[l]], axis=0)),
                _halves(jnp.concatenate([w_out[l], ffn_w_down[l]], axis=0)), _halves(ffn_w_up[l])]

    def full_weights(blocks, landed):
        g_in, g_proj, g_rows, g_up = _finish_gather(blocks, landed)
        w_main, w_tail = _main_tail_from_shards(g_in, H, D)
        wa, wb, wc = [g_proj[:, k * W:(k + 1) * W].transpose(1, 0, 2).reshape(W, D) for k in range(3)]
        return dict(w_main=w_main, w_tail=w_tail, wa=wa, wb=wb, wc=wc, wo=g_rows[:, :D // N_CHIPS].reshape(D, D),
                    wd=jnp.pad(g_rows[:, D // N_CHIPS:].reshape(F, D), ((0, Fp - F), (0, 0))),
                    wg=_pad_cols(jnp.concatenate([g_up[0], g_up[1]], axis=1), Fp),
                    wv=_pad_cols(jnp.concatenate([g_up[2], g_up[3]], axis=1), Fp))

    blocks = blocks_of(0)
    weights = full_weights(blocks, _ship(_gather_cargo(blocks), "gather_first_layer"))
    gcw_cols, fcw_cols = gdn_conv_w.shape[2], ffn_conv_w.shape[2]
    only_south = (cc == 0).astype(f32)
    placed_g = lax.dynamic_update_slice(jnp.zeros((L, GDN_CONV, 3 * W), f32), gdn_conv_w * only_south, (0, 0, chip * gcw_cols))
    placed_f = lax.dynamic_update_slice(jnp.zeros((L, FFN_CONV, 2 * F), f32), ffn_conv_w * only_south, (0, 0, chip * fcw_cols))
    conv_all = _all_reduce8(jnp.concatenate([placed_g.reshape(-1), placed_f.reshape(-1)]), 2 * chip + cc, "conv_weights")
    gcw_full = conv_all[:L * GDN_CONV * 3 * W].reshape(L, GDN_CONV, 1, 3 * W)
    fcw_full = conv_all[L * GDN_CONV * 3 * W:].reshape(L, FFN_CONV, 1, 2 * F)

    saved = []
    h = x.reshape(S, D)
    for l in range(L):
        b_main, b_tail = _main_tail_from_shards(b_in[l][None, None, :], H, D)
        cwg, cwv = _pad_cols(fcw_full[l][..., :F], Fp), _pad_cols(fcw_full[l][..., F:], Fp)
        cbg, cbv = _pad_cols(ffn_conv_b[l][None, :F], Fp), _pad_cols(ffn_conv_b[l][None, F:], Fp)
        nxt = blocks_of(l + 1) if l + 1 < L else None
        ride = (lambda *idx: _gather_cargo([nxt[k] for k in idx])) if nxt else (lambda *idx: None)
        lw = dict(weights, cwg=cwg, cwv=cwv, cbg=cbg, cbv=cbv, gcw=gcw_full[l],
                  sgu_ln_g=sgu_ln_g[l][None, :], sgu_ln_b=sgu_ln_b[l][None, :], sgu_w=sgu_w[l], sgu_b=sgu_b[l][:, :, None],
                  a_log=gdn_a_log[l].reshape(H, 1, 1), dt_bias=gdn_dt_bias[l].reshape(H, 1, 1),
                  norm_g=gdn_norm_g[l].reshape(1, 1, HEAD_DIM), ln1_g=ln1_g[l][None, :], ln1_b=ln1_b[l][None, :],
                  ln2_g=ln2_g[l][None, :], ln2_b=ln2_b[l][None, :])
        tag = "_carrying" if nxt else ""
        proj, *land_in = _as_list(_matmul(h, lw["w_main"], "nn", "proj_main" + tag, bias=b_main, cargo=ride(0)))
        tail = _matmul(h, lw["w_tail"], "nn", "proj_tail", bias=b_tail)
        csum = _fox_prep_fwd(tail, "fox_prep")
        c_col = csum[:, :H].T[:, :, None]
        c_row = csum[:, :H].T[:, None, :]
        y_a, lse, *land_up = _fox_fwd(proj, c_col, c_row, H, "fox_fwd" + tag, cargo=ride(3))
        y_b = _sgu_fwd(proj, lw["sgu_ln_g"], lw["sgu_ln_b"], lw["sgu_w"], lw["sgu_b"], 3 * W, 4 * W, "sgu_fwd")
        qkvc = _gdn_pre_fwd(proj, lw["gcw"], 5 * W, 2 * H, "gdn_pre")
        al = tail[:, H:2 * H].T[:, :, None]
        bl = tail[:, 2 * H:3 * H].T[:, :, None]
        y_c, states, *land_rest = _gdn_scan_fwd(qkvc, al, bl, proj, lw["a_log"], lw["dt_bias"], lw["norm_g"], 8 * W, "gdn_scan" + tag,
                                                cargo=ride(1, 2))
        if nxt:
            weights = full_weights(nxt, land_in + [land_rest[0], land_rest[1]] + land_up)
        pa = _matmul(y_a, lw["wa"], "nn", "branch_proj")
        pb = _matmul(y_b, lw["wb"], "nn", "branch_proj")
        pc = _matmul(y_c, lw["wc"], "nn", "branch_proj")
        merged = _merge_fwd(proj, pa, pb, pc, 9 * W, "merge")
        mix = _matmul(merged, lw["wo"], "nn", "out_proj")
        x1 = _ln_fwd(h, mix, lw["ln1_g"], lw["ln1_b"], alpha, "ln")
        upg = _matmul(x1, lw["wg"], "nn", "ffn_up")
        upv = _matmul(x1, lw["wv"], "nn", "ffn_up")
        act = _ffn_act_fwd(upg, upv, cwg, cwv, cbg, cbv, "ffn_act")
        ffn = _matmul(act, lw["wd"], "nn", "ffn_down")
        x2 = _ln_fwd(x1, ffn, lw["ln2_g"], lw["ln2_b"], alpha, "ln")
        saved.append(dict(lw=lw, h=h, proj=proj, tail=tail, c_col=c_col, c_row=c_row, y_a=y_a, lse=lse, y_b=y_b, qkvc=qkvc, al=al,
                          bl=bl, y_c=y_c, states=states, pa=pa, pb=pb, pc=pc, merged=merged, mix=mix, x1=x1, upg=upg, upv=upv,
                          act=act, ffn=ffn))
        h = x2

    loss_part, dh = _loss_head(h, loss_target.reshape(S, D), "loss_head")
    loss = lax.psum(loss_part[0, 0], ("x", "y", "c"))

    reduced = [dict() for _ in range(L)]
    small_grads = [None] * L
    pending = None
    for l in reversed(range(L)):
        s = saved[l]
        lw = s["lw"]
        d_x1r, d_ffn, d_ln2g, d_ln2b = _ln_bwd(s["x1"], s["ffn"], lw["ln2_g"], lw["ln2_b"], dh, alpha, "ln_bwd")
        d_act = _matmul(d_ffn, lw["wd"], "nt", "ffn_down_dx")
        d_wd = _matmul(s["act"], d_ffn, "tn", "ffn_down_dw")
        dupg, dupv, dcwg, dcwv, dcbg, dcbv = _ffn_act_bwd(s["upg"], s["upv"], lw["cwg"], lw["cwv"], lw["cbg"], lw["cbv"], d_act,
                                                          "ffn_act_bwd")
        d_x1 = _matmul(dupg, lw["wg"], "nt", "ffn_up_dx", add=d_x1r)
        d_x1 = _matmul(dupv, lw["wv"], "nt", "ffn_up_dx", add=d_x1)
        d_wg = _matmul(s["x1"], dupg, "tn", "ffn_up_dw")
        d_wv = _matmul(s["x1"], dupv, "tn", "ffn_up_dw")
        d_hr, d_mix, d_ln1g, d_ln1b = _ln_bwd(s["h"], s["mix"], lw["ln1_g"], lw["ln1_b"], d_x1, alpha, "ln_bwd")
        d_merged = _matmul(d_mix, lw["wo"], "nt", "out_proj_dx")
        d_wo = _matmul(s["merged"], d_mix, "tn", "out_proj_dw")
        dg0, dg1, dg2, d_pa, d_pb, d_pc = _merge_bwd(s["proj"], s["pa"], s["pb"], s["pc"], d_merged, 9 * W, "merge_bwd")
        d_ya = _matmul(d_pa, lw["wa"], "nt", "branch_proj_dx")
        d_yb = _matmul(d_pb, lw["wb"], "nt", "branch_proj_dx")
        d_yc = _matmul(d_pc, lw["wc"], "nt", "branch_proj_dx")
        d_wa = _matmul(s["y_a"], d_pa, "tn", "branch_proj_dw")
        d_wb = _matmul(s["y_b"], d_pb, "tn", "branch_proj_dw")
        d_wc = _matmul(s["y_c"], d_pc, "tn", "branch_proj_dw")
        by_cols = lambda g: g.reshape(g.shape[0], N_CHIPS, g.shape[1] // N_CHIPS).transpose(1, 0, 2)
        early = dict(
            proj=jnp.concatenate([by_cols(d_wa), by_cols(d_wb), by_cols(d_wc)], axis=1),
            rows=jnp.concatenate([d_wo.reshape(N_CHIPS, D // N_CHIPS, D), d_wd[:F].reshape(N_CHIPS, F // N_CHIPS, D)], axis=1),
            w_up=jnp.stack([d_wg[:, :F // 2], d_wg[:, F // 2:F], d_wv[:, :F // 2], d_wv[:, F // 2:F]]))
        pairs = {k: _pair_sums(early[k], cc, "grad_" + k) for k in early}
        dqkvc, dal, dbl, dgate, d_alog, d_dt, d_ng, *landed = _gdn_scan_bwd(
            s["qkvc"], s["al"], s["bl"], s["proj"], lw["a_log"], lw["dt_bias"], lw["norm_g"], s["states"], d_yc, 8 * W,
            "gdn_scan_bwd", cargo=_chip_cargo([pairs[k] for k in early]))
        for k, got in zip(early, landed):
            reduced[l][k] = _finish_reduce(pairs[k], got, chip, cc, "grad_" + k)
        d_gqkv, d_gcw = _gdn_pre_bwd(s["proj"], lw["gcw"], dqkvc, 5 * W, 2 * H, "gdn_pre_bwd")
        d_u, d_v, d_slg, d_slb, d_sw, d_sb = _sgu_bwd(s["proj"], lw["sgu_ln_g"], lw["sgu_ln_b"], lw["sgu_w"], lw["sgu_b"], d_yb,
                                                      3 * W, 4 * W, "sgu_bwd")
        d_q, d_cq = _fox_bwd_q(s["proj"], s["c_col"], s["c_row"], s["y_a"], d_ya, s["lse"], H, "fox_bwd_q")
        d_k, d_v_att, d_c = _fox_bwd_kv(s["proj"], s["c_col"], s["c_row"], s["y_a"], d_ya, s["lse"], H, "fox_bwd_kv")
        d_f = _fox_prep_bwd(s["tail"], _pad_cols(d_cq[:, :, 0].T, LANES), _pad_cols(d_c[:, 0, :].T, LANES), "fox_prep_bwd")
        d_main = jnp.concatenate([d_q, d_k, d_v_att, d_u, d_v, d_gqkv, dgate, dg0, dg1, dg2], axis=1)
        d_tail = _pad_cols(jnp.concatenate([d_f[:, :H], dal[:, :, 0].T, dbl[:, :, 0].T], axis=1), LANES)
        d_wmain = _matmul(s["h"], d_main, "tn", "proj_main_dw")
        d_wtail = _matmul(s["h"], d_tail, "tn", "proj_tail_dw")
        d_bmain = _colsum(d_main, "proj_main_db")
        d_btail = _colsum(d_tail, "proj_tail_db")
        dh, *landed = _as_list(_matmul(d_main, lw["w_main"], "nt", "proj_main_dx" + ("_carrying" if pending else ""), add=d_hr,
                                       cargo=_chip_cargo([pending[1]]) if pending else None))
        if pending:
            reduced[pending[0]]["w_in"] = _finish_reduce(pending[1], landed[0], chip, cc, "grad_w_in")
        dh = _matmul(d_tail, lw["w_tail"], "nt", "proj_tail_dx", add=dh)
        pending = (l, _pair_sums(_shards_from_main_tail(d_wmain, d_wtail, H, D), cc, "grad_w_in"))
        small_grads[l] = dict(b_in=_shards_from_main_tail(d_bmain, d_btail, H, D).reshape(-1), sgu_ln_g=d_slg[0], sgu_ln_b=d_slb[0], sgu_w=d_sw,
                              sgu_b=d_sb[:, :, 0], gdn_conv_w=d_gcw[:, 0, :], gdn_a_log=d_alog[:, 0, 0], gdn_dt_bias=d_dt[:, 0, 0],
                              gdn_norm_g=d_ng[0, 0], ln1_g=d_ln1g[0], ln1_b=d_ln1b[0],
                              ffn_conv_w=jnp.concatenate([dcwg[:, 0, :F], dcwv[:, 0, :F]], axis=1),
                              ffn_conv_b=jnp.concatenate([dcbg[0, :F], dcbv[0, :F]]), ln2_g=d_ln2g[0], ln2_b=d_ln2b[0])
    grad_x = dh.reshape(1, S, D)

    landed = _ship(_chip_cargo([pending[1]]), "grad_chip_exchange")
    reduced[pending[0]]["w_in"] = _finish_reduce(pending[1], landed[0], chip, cc, "grad_w_in")
    grads = {n: [None] * L for n in WEIGHTS}
    for l in range(L):
        grads["w_in"][l] = reduced[l]["w_in"]
        for k, n in enumerate(("w_proj_a", "w_proj_b", "w_proj_c")):
            grads[n][l] = reduced[l]["proj"][k * W:(k + 1) * W]
        grads["w_out"][l] = reduced[l]["rows"][:D // N_CHIPS]
        grads["ffn_w_down"][l] = reduced[l]["rows"][D // N_CHIPS:]
        grads["ffn_w_up"][l] = reduced[l]["w_up"]
    small_shapes = {n: small_grads[0][n].shape for n in SMALL}
    small_flat = jnp.concatenate([small_grads[l][n].reshape(-1) for l in range(L) for n in SMALL])
    small_sum = _all_reduce8(small_flat, 2 * chip + cc, "small_grads")
    off = 0
    for l in range(L):
        for n in SMALL:
            size = math.prod(small_shapes[n])
            g = small_sum[off:off + size].reshape(small_shapes[n])
            off += size
            if n == "gdn_conv_w":
                g = lax.dynamic_slice_in_dim(g, chip * gcw_cols, gcw_cols, axis=1)
            elif n == "ffn_conv_w":
                g = lax.dynamic_slice_in_dim(g, chip * fcw_cols, fcw_cols, axis=1)
            grads[n][l] = g
    grads = {n: jnp.stack(grads[n]) for n in WEIGHTS}

    deltas, new_m, new_v = {}, {}, {}
    for n in WEIGHTS:
        deltas[n], new_m[n], new_v[n] = _adamw(P[n], grads[n], M1[n], M2[n], "adamw_" + n)
    return (loss, grad_x, *[grads[n] for n in WEIGHTS], *[deltas[n] for n in WEIGHTS], *[new_m[n] for n in WEIGHTS],
            *[new_v[n] for n in WEIGHTS])
```

```python
import functools
import math
from typing import NamedTuple

import jax
import jax.numpy as jnp
from jax import lax
from jax.experimental import pallas as pl
from jax.experimental.pallas import tpu as pltpu

f32 = jnp.float32
bf16 = jnp.bfloat16
HIGHEST = lax.Precision.HIGHEST
MESH = pl.DeviceIdType.MESH

HEAD_DIM = 128
CHUNK = 64
SGU_SPAN = 128
GDN_CONV = 4
FFN_CONV = 3
N_CHIPS = 4
N_DEV = 8
LN_EPS = 1e-5
RMS_EPS = 1e-6
ADAM_LR = 0.001
ADAM_B1 = 0.9
ADAM_B2 = 0.999
ADAM_EPS = 1e-08
ADAM_WD = 0.01
ADAM_STEP = 10
NEG_BIG = -1e30
LANES = 128
FF_ALIGN = 512
SUM_ROWS = 256
ATTN_TILE = 1024
VMEM_MARGIN = 12 << 20
VMEM_MOST = 60 << 20

SHARDED = ("w_in", "proj", "rows", "w_up")
SMALL = ("b_in", "sgu_ln_g", "sgu_ln_b", "sgu_w", "sgu_b", "gdn_conv_w", "gdn_a_log", "gdn_dt_bias", "gdn_norm_g",
         "ln1_g", "ln1_b", "ffn_conv_w", "ffn_conv_b", "ln2_g", "ln2_b")
WEIGHTS = ("w_in", "b_in", "sgu_ln_g", "sgu_ln_b", "sgu_w", "sgu_b", "gdn_conv_w", "gdn_a_log", "gdn_dt_bias", "gdn_norm_g",
           "w_proj_a", "w_proj_b", "w_proj_c", "w_out", "ln1_g", "ln1_b", "ffn_w_up", "ffn_conv_w", "ffn_conv_b", "ffn_w_down",
           "ln2_g", "ln2_b")

ANY = pl.BlockSpec(memory_space=pl.ANY)


def _tile(dim, pref):
    t = pref
    while t > 128 and dim % t:
        t //= 2
    return min(t, dim) if dim % min(t, dim) == 0 else dim


def _params(*sem, vmem=None):
    if vmem is None:
        return pltpu.CompilerParams(dimension_semantics=sem)
    return pltpu.CompilerParams(dimension_semantics=sem, vmem_limit_bytes=min(vmem + VMEM_MARGIN, VMEM_MOST))


def _attn_vmem(T):
    return 8 * T * T * 4


_DIMS = {"nn": (((1,), (0,)), ((), ())), "nt": (((1,), (1,)), ((), ())), "tn": (((0,), (0,)), ((), ()))}


def _pick(dim, most):
    for unit in (256, LANES):
        for t in range(min(most, dim) // unit * unit, 0, -unit):
            if dim % t == 0:
                return t
    return dim


class _Cargo(NamedTuple):
    arrays: tuple
    landing: tuple
    copies: object
    n: int


def _cargo_specs(cargo):
    if cargo is None:
        return [], [], [], []
    sems = [pltpu.SemaphoreType.DMA((cargo.n,)), pltpu.SemaphoreType.DMA((cargo.n,))]
    return [ANY] * len(cargo.arrays), [ANY] * len(cargo.landing), list(cargo.landing), sems


def _split_refs(refs, n_in, n_out, cargo):
    na, nl, ns = (len(cargo.arrays), len(cargo.landing), 2) if cargo else (0, 0, 0)
    a, b, c = n_in + na, n_in + na + n_out, n_in + na + n_out + nl
    return refs[:n_in], refs[a:b], refs[c:len(refs) - ns], (refs[n_in:a], refs[b:c], refs[len(refs) - ns:])


def _cargo_start(cargo, first, hold):
    if cargo is not None:
        @pl.when(first)
        def _():
            for make in cargo.copies(hold[0], hold[1], *hold[2])[0]:
                make().start()


def _cargo_wait(cargo, last, hold):
    if cargo is not None:
        @pl.when(last)
        def _():
            sends, lands = cargo.copies(hold[0], hold[1], *hold[2])
            for make in lands:
                make().wait_recv()
            for make in sends:
                make().wait_send()


def _ship(cargo, name):
    in_specs, out_specs, out_shape, sems = _cargo_specs(cargo)

    def body(*refs):
        _, _, _, hold = _split_refs(refs, 0, 0, cargo)
        sends, lands = cargo.copies(hold[0], hold[1], *hold[2])
        started = [make() for make in sends]
        for cp in started:
            cp.start()
        for make in lands:
            make().wait_recv()
        for cp in started:
            cp.wait_send()

    return pl.pallas_call(body, name=name, out_shape=out_shape, in_specs=in_specs, out_specs=out_specs,
                          scratch_shapes=sems)(*cargo.arrays)


def _matmul(a, b, mode, name, bias=None, add=None, cargo=None):
    a, b = a.astype(bf16), b.astype(bf16)
    if mode == "nn":
        (M, K), (_, N) = a.shape, b.shape
    elif mode == "nt":
        (M, K), (N, _) = a.shape, b.shape
    else:
        (K, M), (_, N) = a.shape, b.shape
    has_bias, has_add = bias is not None, add is not None
    tn = _pick(N, 2816)

    def need(tm, tk):
        return 2 * (tm * tk * a.dtype.itemsize + tk * tn * b.dtype.itemsize + tm * tn * 4 * (2 if has_add else 1)) + tm * tn * 4

    tm, tk = next((tm, tk) for tm, tk in ((1024, 1024), (1024, 512), (512, 512))
                  if need(_pick(M, tm), _pick(K, tk)) + VMEM_MARGIN <= VMEM_MOST or (tm, tk) == (512, 512))
    tm, tk = _pick(M, tm), _pick(K, tk)
    vmem = need(tm, tk)
    ni, nj, nk = M // tm, N // tn, K // tk

    def body(*refs):
        ins, (o_ref,), (acc_ref,), hold = _split_refs(refs, 2 + has_bias + has_add, 1, cargo)
        a_ref, b_ref = ins[0], ins[1]
        bias_ref = ins[2] if has_bias else None
        add_ref = ins[2 + has_bias] if has_add else None
        i, j, k = pl.program_id(0), pl.program_id(1), pl.program_id(2)
        _cargo_start(cargo, (i == 0) & (j == 0) & (k == 0), hold)

        @pl.when(k == 0)
        def _():
            acc_ref[...] = jnp.zeros_like(acc_ref)

        acc_ref[...] += lax.dot_general(a_ref[...], b_ref[...], _DIMS[mode], preferred_element_type=f32)

        @pl.when(k == nk - 1)
        def _():
            r = acc_ref[...]
            if has_bias:
                r = r + bias_ref[...]
            if has_add:
                r = r + add_ref[...]
            o_ref[...] = r

        _cargo_wait(cargo, (i == ni - 1) & (j == nj - 1) & (k == nk - 1), hold)

    if mode == "nn":
        specs = [pl.BlockSpec((tm, tk), lambda i, j, k: (i, k)), pl.BlockSpec((tk, tn), lambda i, j, k: (k, j))]
    elif mode == "nt":
        specs = [pl.BlockSpec((tm, tk), lambda i, j, k: (i, k)), pl.BlockSpec((tn, tk), lambda i, j, k: (j, k))]
    else:
        specs = [pl.BlockSpec((tk, tm), lambda i, j, k: (k, i)), pl.BlockSpec((tk, tn), lambda i, j, k: (k, j))]
    ops = [a, b]
    if has_bias:
        specs.append(pl.BlockSpec((1, tn), lambda i, j, k: (0, j)))
        ops.append(bias)
    if has_add:
        specs.append(pl.BlockSpec((tm, tn), lambda i, j, k: (i, j)))
        ops.append(add)
    c_in, c_out, c_shape, c_sems = _cargo_specs(cargo)
    outs = pl.pallas_call(
        body, name=name, grid=(ni, nj, nk), in_specs=specs + c_in,
        out_specs=[pl.BlockSpec((tm, tn), lambda i, j, k: (i, j))] + c_out,
        out_shape=[jax.ShapeDtypeStruct((M, N), f32)] + c_shape, scratch_shapes=[pltpu.VMEM((tm, tn), f32)] + c_sems,
        compiler_params=pltpu.CompilerParams(
            dimension_semantics=("arbitrary",) * 3 if cargo else ("parallel", "parallel", "arbitrary"),
            vmem_limit_bytes=min(vmem + VMEM_MARGIN, VMEM_MOST)))(*ops, *(cargo.arrays if cargo else ()))
    return outs if cargo else outs[0]


def _colsum(a, name):
    S, N = a.shape
    tn = _tile(N, 512)

    def body(a_ref, o_ref):
        o_ref[...] = jnp.sum(a_ref[...], axis=0, keepdims=True)

    return pl.pallas_call(body, name=name, grid=(N // tn,), in_specs=[pl.BlockSpec((S, tn), lambda j: (0, j))],
                          out_specs=pl.BlockSpec((1, tn), lambda j: (0, j)), out_shape=jax.ShapeDtypeStruct((1, N), f32),
                          compiler_params=_params("parallel"))(a)


def _ln_fn(alpha, x, y, g, b):
    z = alpha * x + y
    mu = jnp.mean(z, axis=-1, keepdims=True)
    zc = z - mu
    var = jnp.mean(zc * zc, axis=-1, keepdims=True)
    return zc * lax.rsqrt(var + LN_EPS) * g + b


def _ln_fwd(x, y, g, b, alpha, name):
    S, D = x.shape
    tr = _tile(S, 256)

    def body(x_ref, y_ref, g_ref, b_ref, o_ref):
        o_ref[...] = _ln_fn(alpha, x_ref[...], y_ref[...], g_ref[...], b_ref[...])

    row = pl.BlockSpec((tr, D), lambda i: (i, 0))
    par = pl.BlockSpec((1, D), lambda i: (0, 0))
    return pl.pallas_call(body, name=name, grid=(S // tr,), in_specs=[row, row, par, par], out_specs=row,
                          out_shape=jax.ShapeDtypeStruct((S, D), f32), compiler_params=_params("parallel"))(x, y, g, b)


def _ln_bwd(x, y, g, b, dout, alpha, name):
    S, D = x.shape
    tr = _tile(S, 256)

    def body(x_ref, y_ref, g_ref, b_ref, d_ref, dx_ref, dy_ref, dg_ref, db_ref):
        _, vjp = jax.vjp(functools.partial(_ln_fn, alpha), x_ref[...], y_ref[...], g_ref[...], b_ref[...])
        dx, dy, dg, db = vjp(d_ref[...])
        dx_ref[...] = dx
        dy_ref[...] = dy.astype(dy_ref.dtype)

        @pl.when(pl.program_id(0) == 0)
        def _():
            dg_ref[...] = jnp.zeros_like(dg_ref)
            db_ref[...] = jnp.zeros_like(db_ref)

        dg_ref[...] += dg
        db_ref[...] += db

    row = pl.BlockSpec((tr, D), lambda i: (i, 0))
    par = pl.BlockSpec((1, D), lambda i: (0, 0))
    sd = jax.ShapeDtypeStruct
    return pl.pallas_call(body, name=name, grid=(S // tr,), in_specs=[row, row, par, par, row],
                          out_specs=[row, row, par, par],
                          out_shape=[sd((S, D), f32), sd((S, D), bf16), sd((1, D), f32), sd((1, D), f32)],
                          compiler_params=_params("arbitrary"))(x, y, g, b, dout)


def _loss_head(y, t, name):
    S, D = y.shape
    tr = _tile(S, 256)

    def body(y_ref, t_ref, l_ref, d_ref):
        e = y_ref[...] - t_ref[...]
        d_ref[...] = e / D

        @pl.when(pl.program_id(0) == 0)
        def _():
            l_ref[...] = jnp.zeros_like(l_ref)

        l_ref[...] += 0.5 * jnp.sum(jnp.mean(e * e, axis=-1, keepdims=True))

    row = pl.BlockSpec((tr, D), lambda i: (i, 0))
    return pl.pallas_call(body, name=name, grid=(S // tr,), in_specs=[row, row],
                          out_specs=[pl.BlockSpec((8, LANES), lambda i: (0, 0)), row],
                          out_shape=[jax.ShapeDtypeStruct((8, LANES), f32), jax.ShapeDtypeStruct((S, D), f32)],
                          compiler_params=_params("arbitrary"))(y, t)


def _merge_fn(g0, g1, g2, pa, pb, pc):
    return jax.nn.sigmoid(g0) * pa + jax.nn.sigmoid(g1) * pb + jax.nn.sigmoid(g2) * pc


def _merge_specs(S, D, gate_off):
    tr = _tile(S, 512)
    tc = _tile(math.gcd(gate_off, D), 512)
    gates = [pl.BlockSpec((tr, tc), functools.partial(lambda k, i, j: (i, (gate_off + k * D) // tc + j), k)) for k in range(3)]
    tile = pl.BlockSpec((tr, tc), lambda i, j: (i, j))
    return tr, tc, gates, tile


def _merge_fwd(proj, pa, pb, pc, gate_off, name):
    S, D = pa.shape
    tr, tc, gates, tile = _merge_specs(S, D, gate_off)

    def body(g0, g1, g2, a, b, c, o_ref):
        o_ref[...] = _merge_fn(g0[...], g1[...], g2[...], a[...], b[...], c[...]).astype(o_ref.dtype)

    return pl.pallas_call(body, name=name, grid=(S // tr, D // tc), in_specs=gates + [tile] * 3, out_specs=tile,
                          out_shape=jax.ShapeDtypeStruct((S, D), bf16),
                          compiler_params=_params("parallel", "parallel"))(proj, proj, proj, pa, pb, pc)


def _merge_bwd(proj, pa, pb, pc, dm, gate_off, name):
    S, D = pa.shape
    tr, tc, gates, tile = _merge_specs(S, D, gate_off)

    def body(g0, g1, g2, a, b, c, d, dg0, dg1, dg2, da, db, dc):
        _, vjp = jax.vjp(_merge_fn, g0[...], g1[...], g2[...], a[...], b[...], c[...])
        for ref, val in zip((dg0, dg1, dg2, da, db, dc), vjp(d[...])):
            ref[...] = val.astype(ref.dtype)

    sd = jax.ShapeDtypeStruct
    return pl.pallas_call(body, name=name, grid=(S // tr, D // tc), in_specs=gates + [tile] * 4,
                          out_specs=[tile] * 6, out_shape=[sd((S, D), f32)] * 3 + [sd((S, D), bf16)] * 3,
                          compiler_params=_params("parallel", "parallel"))(proj, proj, proj, pa, pb, pc, dm)


def _sgu_fn(nb, u, v, ln_g, ln_b, w_s, b_s):
    mu = jnp.mean(v, axis=-1, keepdims=True)
    vc = v - mu
    var = jnp.mean(vc * vc, axis=-1, keepdims=True)
    vn = vc * lax.rsqrt(var + LN_EPS) * ln_g + ln_b
    r = lax.broadcasted_iota(jnp.int32, (SGU_SPAN, SGU_SPAN), 0) // CHUNK
    c = lax.broadcasted_iota(jnp.int32, (SGU_SPAN, SGU_SPAN), 1) // CHUNK
    wm = jnp.where(r >= c, w_s, 0.0)
    vn3 = vn.reshape(nb, SGU_SPAN, HEAD_DIM)
    mixed = lax.dot_general(jnp.broadcast_to(wm, (nb, SGU_SPAN, SGU_SPAN)), vn3, (((2,), (1,)), ((0,), (0,))),
                            preferred_element_type=f32)
    mixed = mixed + b_s
    return u * mixed.reshape(nb * SGU_SPAN, HEAD_DIM)


def _sgu_specs(S, G, u_off, v_off):
    nb = max(1, min(8, S // SGU_SPAN))
    rows = nb * SGU_SPAN
    ub = pl.BlockSpec((rows, HEAD_DIM), lambda g, n: (n, u_off // HEAD_DIM + g))
    vb = pl.BlockSpec((rows, HEAD_DIM), lambda g, n: (n, v_off // HEAD_DIM + g))
    lnb = pl.BlockSpec((1, HEAD_DIM), lambda g, n: (0, g))
    wb = pl.BlockSpec((None, SGU_SPAN, SGU_SPAN), lambda g, n: (g, 0, 0))
    bb = pl.BlockSpec((None, SGU_SPAN, 1), lambda g, n: (g, 0, 0))
    return nb, rows, ub, vb, lnb, wb, bb


def _sgu_fwd(proj, ln_g, ln_b, w_s, b_s, u_off, v_off, name):
    S = proj.shape[0]
    G = w_s.shape[0]
    nb, rows, ub, vb, lnb, wb, bb = _sgu_specs(S, G, u_off, v_off)

    def body(u, v, lg, lb, w, b, o_ref):
        o_ref[...] = _sgu_fn(nb, u[...], v[...], lg[...], lb[...], w[...], b[...]).astype(o_ref.dtype)

    return pl.pallas_call(body, name=name, grid=(G, S // rows), in_specs=[ub, vb, lnb, lnb, wb, bb],
                          out_specs=pl.BlockSpec((rows, HEAD_DIM), lambda g, n: (n, g)),
                          out_shape=jax.ShapeDtypeStruct((S, G * HEAD_DIM), bf16),
                          compiler_params=_params("parallel", "parallel"))(proj, proj, ln_g, ln_b, w_s, b_s)


def _sgu_bwd(proj, ln_g, ln_b, w_s, b_s, dy, u_off, v_off, name):
    S = proj.shape[0]
    G = w_s.shape[0]
    nb, rows, ub, vb, lnb, wb, bb = _sgu_specs(S, G, u_off, v_off)

    def body(u, v, lg, lb, w, b, d, du, dv, dlg, dlb, dw, db):
        _, vjp = jax.vjp(functools.partial(_sgu_fn, nb), u[...], v[...], lg[...], lb[...], w[...], b[...])
        gu, gv, glg, glb, gw, gb = vjp(d[...])
        du[...] = gu
        dv[...] = gv

        @pl.when(pl.program_id(1) == 0)
        def _():
            for ref in (dlg, dlb, dw, db):
                ref[...] = jnp.zeros_like(ref)

        dlg[...] += glg
        dlb[...] += glb
        dw[...] += gw
        db[...] += gb

    tile = pl.BlockSpec((rows, HEAD_DIM), lambda g, n: (n, g))
    sd = jax.ShapeDtypeStruct
    W = G * HEAD_DIM
    return pl.pallas_call(body, name=name, grid=(G, S // rows), in_specs=[ub, vb, lnb, lnb, wb, bb, tile],
                          out_specs=[tile, tile, lnb, lnb, wb, bb],
                          out_shape=[sd((S, W), f32), sd((S, W), f32), sd((1, W), f32), sd((1, W), f32),
                                     sd((G, SGU_SPAN, SGU_SPAN), f32), sd((G, SGU_SPAN, 1), f32)],
                          compiler_params=_params("parallel", "arbitrary"))(proj, proj, ln_g, ln_b, w_s, b_s, dy)


def _shift_down(x, k):
    if k == 0:
        return x
    rows = lax.broadcasted_iota(jnp.int32, x.shape, 0)
    return jnp.where(rows >= k, pltpu.roll(x, k, 0), 0.0)


def _shift_up(x, k):
    if k == 0:
        return x
    n = x.shape[0]
    rows = lax.broadcasted_iota(jnp.int32, x.shape, 0)
    return jnp.where(rows < n - k, pltpu.roll(x, n - k, 0), 0.0)


def _conv(x, w_ref, width):
    out = w_ref[width - 1] * x
    for j in range(width - 1):
        out = out + w_ref[j] * _shift_down(x, width - 1 - j)
    return out


def _conv_bwd(x, dz, w_ref, dw_ref, width):
    dx = w_ref[width - 1] * dz
    dw_ref[width - 1] = jnp.sum(dz * x, axis=0, keepdims=True)
    for j in range(width - 1):
        k = width - 1 - j
        dx = dx + w_ref[j] * _shift_up(dz, k)
        dw_ref[j] = jnp.sum(dz * _shift_down(x, k), axis=0, keepdims=True)
    return dx


def _silu(z):
    return z * jax.nn.sigmoid(z)


def _silu_and_slope(z):
    s = jax.nn.sigmoid(z)
    return z * s, s * (1.0 + z * (1.0 - s))


def _ffn_act_fwd(upg, upv, cwg, cwv, cbg, cbv, name):
    S, Fp = upg.shape

    def body(g_ref, v_ref, wg, wv, bg, bv, o_ref):
        hg = _conv(g_ref[...], wg, FFN_CONV) + bg[...]
        hv = _conv(v_ref[...], wv, FFN_CONV) + bv[...]
        o_ref[...] = (_silu(hg) * hv).astype(o_ref.dtype)

    col = pl.BlockSpec((S, LANES), lambda j: (0, j))
    wsp = pl.BlockSpec((FFN_CONV, 1, LANES), lambda j: (0, 0, j))
    bsp = pl.BlockSpec((1, LANES), lambda j: (0, j))
    return pl.pallas_call(body, name=name, grid=(Fp // LANES,), in_specs=[col, col, wsp, wsp, bsp, bsp], out_specs=col,
                          out_shape=jax.ShapeDtypeStruct((S, Fp), bf16),
                          compiler_params=_params("parallel"))(upg, upv, cwg, cwv, cbg, cbv)


def _ffn_act_bwd(upg, upv, cwg, cwv, cbg, cbv, dact, name):
    S, Fp = upg.shape

    def body(g_ref, v_ref, wg, wv, bg, bv, d_ref, dg_ref, dv_ref, dwg, dwv, dbg, dbv):
        xg, xv, d = g_ref[...], v_ref[...], d_ref[...]
        hg = _conv(xg, wg, FFN_CONV) + bg[...]
        hv = _conv(xv, wv, FFN_CONV) + bv[...]
        act_g, slope_g = _silu_and_slope(hg)
        dhg = d * hv * slope_g
        dhv = d * act_g
        dbg[...] = jnp.sum(dhg, axis=0, keepdims=True)
        dbv[...] = jnp.sum(dhv, axis=0, keepdims=True)
        dg_ref[...] = _conv_bwd(xg, dhg, wg, dwg, FFN_CONV).astype(dg_ref.dtype)
        dv_ref[...] = _conv_bwd(xv, dhv, wv, dwv, FFN_CONV).astype(dv_ref.dtype)

    col = pl.BlockSpec((S, LANES), lambda j: (0, j))
    wsp = pl.BlockSpec((FFN_CONV, 1, LANES), lambda j: (0, 0, j))
    bsp = pl.BlockSpec((1, LANES), lambda j: (0, j))
    sd = jax.ShapeDtypeStruct
    return pl.pallas_call(body, name=name, grid=(Fp // LANES,), in_specs=[col, col, wsp, wsp, bsp, bsp, col],
                          out_specs=[col, col, wsp, wsp, bsp, bsp],
                          out_shape=[sd((S, Fp), bf16), sd((S, Fp), bf16), sd((FFN_CONV, 1, Fp), f32), sd((FFN_CONV, 1, Fp), f32),
                                     sd((1, Fp), f32), sd((1, Fp), f32)],
                          compiler_params=_params("parallel"))(upg, upv, cwg, cwv, cbg, cbv, dact)


def _gdn_pre_fwd(proj, cw, x_off, n_norm, name):
    S = proj.shape[0]
    C = cw.shape[2]

    def body(x_ref, w_ref, o_ref):
        s = _silu(_conv(x_ref[...], w_ref, GDN_CONV))
        r = lax.rsqrt(jnp.sum(s * s, axis=-1, keepdims=True) + RMS_EPS)
        o_ref[...] = jnp.where(pl.program_id(0) < n_norm, s * r, s)

    xs = pl.BlockSpec((S, LANES), lambda j: (0, x_off // LANES + j))
    col = pl.BlockSpec((S, LANES), lambda j: (0, j))
    wsp = pl.BlockSpec((GDN_CONV, 1, LANES), lambda j: (0, 0, j))
    return pl.pallas_call(body, name=name, grid=(C // LANES,), in_specs=[xs, wsp], out_specs=col,
                          out_shape=jax.ShapeDtypeStruct((S, C), f32), compiler_params=_params("parallel"))(proj, cw)


def _gdn_pre_bwd(proj, cw, dout, x_off, n_norm, name):
    S = proj.shape[0]
    C = cw.shape[2]

    def body(x_ref, w_ref, d_ref, dx_ref, dw_ref):
        x, d = x_ref[...], d_ref[...]
        z = _conv(x, w_ref, GDN_CONV)
        s, slope = _silu_and_slope(z)
        r = lax.rsqrt(jnp.sum(s * s, axis=-1, keepdims=True) + RMS_EPS)
        ds_norm = d * r - s * (r * r * r) * jnp.sum(d * s, axis=-1, keepdims=True)
        ds = jnp.where(pl.program_id(0) < n_norm, ds_norm, d)
        dz = ds * slope
        dx_ref[...] = _conv_bwd(x, dz, w_ref, dw_ref, GDN_CONV)

    xs = pl.BlockSpec((S, LANES), lambda j: (0, x_off // LANES + j))
    col = pl.BlockSpec((S, LANES), lambda j: (0, j))
    wsp = pl.BlockSpec((GDN_CONV, 1, LANES), lambda j: (0, 0, j))
    return pl.pallas_call(body, name=name, grid=(C // LANES,), in_specs=[xs, wsp, col], out_specs=[col, wsp],
                          out_shape=[jax.ShapeDtypeStruct((S, C), f32), jax.ShapeDtypeStruct((GDN_CONV, 1, C), f32)],
                          compiler_params=_params("parallel"))(proj, cw, dout)


def _bmm(a, b, prec=None):
    return lax.dot_general(a, b, (((2,), (1,)), ((0,), (0,))), precision=prec, preferred_element_type=f32)


def _bmm_nt(a, b, prec=None):
    return lax.dot_general(a, b, (((2,), (2,)), ((0,), (0,))), precision=prec, preferred_element_type=f32)


def _bmm_tn(a, b, prec=None):
    return lax.dot_general(a, b, (((1,), (1,)), ((0,), (0,))), precision=prec, preferred_element_type=f32)


def _softplus(x):
    return jnp.maximum(x, 0.0) + jnp.log1p(jnp.exp(-jnp.abs(x)))


@jax.custom_vjp
def _unit_lower_inverse(a):
    H, C, _ = a.shape
    r = lax.broadcasted_iota(jnp.int32, (H, C, C), 1)
    c = lax.broadcasted_iota(jnp.int32, (H, C, C), 2)
    p = -a
    inv = (r == c).astype(f32) + p
    for _ in range(int(math.log2(C)) - 1):
        p = _bmm(p, p, HIGHEST)
        inv = inv + _bmm(inv, p, HIGHEST)
    return inv


def _unit_lower_inverse_fwd(a):
    inv = _unit_lower_inverse(a)
    return inv, inv


def _unit_lower_inverse_bwd(inv, d_inv):
    return (-_bmm_nt(_bmm_tn(inv, d_inv, HIGHEST), inv, HIGHEST),)


_unit_lower_inverse.defvjp(_unit_lower_inverse_fwd, _unit_lower_inverse_bwd)


def _gdn_chunk(q, k, v, al, bl, gate, a_log, dt_bias, norm_g, state):
    H, C, Dh = q.shape
    r = lax.broadcasted_iota(jnp.int32, (H, C, C), 1)
    c = lax.broadcasted_iota(jnp.int32, (H, C, C), 2)
    tril = r >= c
    strict = r > c
    lower = tril.astype(f32)
    upper = (r <= c).astype(f32)
    ones = jnp.ones((H, C, C), f32)
    g = -jnp.exp(a_log) * _softplus(al + dt_bias)
    beta = jax.nn.sigmoid(bl)
    g_lanes = jnp.broadcast_to(g, (H, C, Dh))
    g_sq = jnp.broadcast_to(g, (H, C, C))
    gc = _bmm(lower, g_lanes, HIGHEST)
    gc_i = _bmm(lower, g_sq, HIGHEST)
    gc_j = _bmm(ones, g_sq * upper, HIGHEST)
    decay = jnp.where(tril, jnp.exp(jnp.where(tril, gc_i - gc_j, 0.0)), 0.0)
    qs = q * (Dh ** -0.5)
    kb = k * beta
    a_kk = jnp.where(strict, _bmm_nt(kb, k) * decay, 0.0)
    rhs_u = v * beta
    rhs_w = kb * jnp.exp(gc)
    inv = _unit_lower_inverse(a_kk)
    u = _bmm(inv, rhs_u, HIGHEST)
    w = _bmm(inv, rhs_w, HIGHEST)
    qk = jnp.where(tril, _bmm_nt(qs, k) * decay, 0.0)
    g_last = jnp.sum(g, axis=1, keepdims=True)
    k_dec = k * jnp.exp(g_last - gc)
    q_dec = qs * jnp.exp(gc)
    v_new = u - _bmm(w, state)
    o = _bmm(q_dec, state) + _bmm(qk, v_new)
    new_state = state * jnp.exp(g_last) + _bmm_tn(k_dec, v_new)
    y = o * lax.rsqrt(jnp.mean(o * o, axis=-1, keepdims=True) + RMS_EPS) * norm_g * _silu(gate)
    return y, new_state


def _heads(ref, off, H):
    return jnp.stack([ref[:, off + h * HEAD_DIM: off + (h + 1) * HEAD_DIM] for h in range(H)])


def _gdn_scan_fwd(qkvc, al, bl, proj, a_log, dt_bias, norm_g, gate_off, name, cargo=None):
    S = qkvc.shape[0]
    H = al.shape[0]
    W = H * HEAD_DIM
    n = S // CHUNK

    def body(*refs):
        (x_ref, al_ref, bl_ref, gate_ref, alog_ref, dt_ref, ng_ref), (y_ref, st_ref), (state,), hold = _split_refs(refs, 7, 2, cargo)
        _cargo_start(cargo, pl.program_id(0) == 0, hold)

        @pl.when(pl.program_id(0) == 0)
        def _():
            state[...] = jnp.zeros_like(state)

        st_ref[...] = state[...]
        y, new = _gdn_chunk(_heads(x_ref, 0, H), _heads(x_ref, W, H), _heads(x_ref, 2 * W, H), al_ref[...], bl_ref[...],
                            _heads(gate_ref, 0, H), alog_ref[...], dt_ref[...], ng_ref[...], state[...])
        state[...] = new
        for h in range(H):
            y_ref[:, h * HEAD_DIM:(h + 1) * HEAD_DIM] = y[h].astype(y_ref.dtype)
        _cargo_wait(cargo, pl.program_id(0) == n - 1, hold)

    sd = jax.ShapeDtypeStruct
    col = pl.BlockSpec((H, CHUNK, 1), lambda i: (0, i, 0))
    par = pl.BlockSpec((H, 1, 1), lambda i: (0, 0, 0))
    c_in, c_out, c_shape, c_sems = _cargo_specs(cargo)
    return pl.pallas_call(
        body, name=name, grid=(n,),
        in_specs=[pl.BlockSpec((CHUNK, 3 * W), lambda i: (i, 0)), col, col,
                  pl.BlockSpec((CHUNK, W), lambda i: (i, gate_off // W)), par, par,
                  pl.BlockSpec((1, 1, HEAD_DIM), lambda i: (0, 0, 0))] + c_in,
        out_specs=[pl.BlockSpec((CHUNK, W), lambda i: (i, 0)),
                   pl.BlockSpec((None, H, HEAD_DIM, HEAD_DIM), lambda i: (i, 0, 0, 0))] + c_out,
        out_shape=[sd((S, W), bf16), sd((n, H, HEAD_DIM, HEAD_DIM), f32)] + c_shape,
        scratch_shapes=[pltpu.VMEM((H, HEAD_DIM, HEAD_DIM), f32)] + c_sems,
        compiler_params=_params("arbitrary"))(qkvc, al, bl, proj, a_log, dt_bias, norm_g, *(cargo.arrays if cargo else ()))


def _gdn_scan_bwd(qkvc, al, bl, proj, a_log, dt_bias, norm_g, states, dy, gate_off, name, cargo=None):
    S = qkvc.shape[0]
    H = al.shape[0]
    W = H * HEAD_DIM
    n = S // CHUNK

    def body(*refs):
        ins, outs, (dstate,), hold = _split_refs(refs, 9, 7, cargo)
        x_ref, al_ref, bl_ref, gate_ref, alog_ref, dt_ref, ng_ref, st_ref, dy_ref = ins
        dx_ref, dal_ref, dbl_ref, dgate_ref, dalog_ref, ddt_ref, dng_ref = outs
        _cargo_start(cargo, pl.program_id(0) == 0, hold)

        @pl.when(pl.program_id(0) == 0)
        def _():
            dstate[...] = jnp.zeros_like(dstate)
            for ref in (dalog_ref, ddt_ref, dng_ref):
                ref[...] = jnp.zeros_like(ref)

        _, vjp = jax.vjp(_gdn_chunk, _heads(x_ref, 0, H), _heads(x_ref, W, H), _heads(x_ref, 2 * W, H), al_ref[...],
                         bl_ref[...], _heads(gate_ref, 0, H), alog_ref[...], dt_ref[...], ng_ref[...], st_ref[...])
        dq, dk, dv, dal, dbl, dgate, dalog, ddt, dng, dst = vjp((_heads(dy_ref, 0, H), dstate[...]))
        dstate[...] = dst
        for h in range(H):
            lo, hi = h * HEAD_DIM, (h + 1) * HEAD_DIM
            dx_ref[:, lo:hi] = dq[h]
            dx_ref[:, W + lo:W + hi] = dk[h]
            dx_ref[:, 2 * W + lo:2 * W + hi] = dv[h]
            dgate_ref[:, lo:hi] = dgate[h]
        dal_ref[...] = dal
        dbl_ref[...] = dbl
        dalog_ref[...] += dalog
        ddt_ref[...] += ddt
        dng_ref[...] += dng

        _cargo_wait(cargo, pl.program_id(0) == n - 1, hold)

    sd = jax.ShapeDtypeStruct
    c_in, c_out, c_shape, c_sems = _cargo_specs(cargo)
    rev = lambda i: n - 1 - i
    col = pl.BlockSpec((H, CHUNK, 1), lambda i: (0, rev(i), 0))
    par = pl.BlockSpec((H, 1, 1), lambda i: (0, 0, 0))
    ng = pl.BlockSpec((1, 1, HEAD_DIM), lambda i: (0, 0, 0))
    xs = pl.BlockSpec((CHUNK, 3 * W), lambda i: (rev(i), 0))
    ws = pl.BlockSpec((CHUNK, W), lambda i: (rev(i), 0))
    return pl.pallas_call(
        body, name=name, grid=(n,),
        in_specs=[xs, col, col, pl.BlockSpec((CHUNK, W), lambda i: (rev(i), gate_off // W)), par, par, ng,
                  pl.BlockSpec((None, H, HEAD_DIM, HEAD_DIM), lambda i: (rev(i), 0, 0, 0)), ws] + c_in,
        out_specs=[xs, col, col, ws, par, par, ng] + c_out,
        out_shape=[sd((S, 3 * W), f32), sd((H, S, 1), f32), sd((H, S, 1), f32), sd((S, W), f32), sd((H, 1, 1), f32),
                   sd((H, 1, 1), f32), sd((1, 1, HEAD_DIM), f32)] + c_shape,
        scratch_shapes=[pltpu.VMEM((H, HEAD_DIM, HEAD_DIM), f32)] + c_sems,
        compiler_params=_params("arbitrary"))(qkvc, al, bl, proj, a_log, dt_bias, norm_g, states, dy,
                                              *(cargo.arrays if cargo else ()))


def _tri(n, upper):
    r = lax.broadcasted_iota(jnp.int32, (n, n), 0)
    c = lax.broadcasted_iota(jnp.int32, (n, n), 1)
    return (r <= c if upper else r >= c).astype(f32)


def _fox_prep_fwd(tail, name):
    S = tail.shape[0]
    tb = _tile(S, 512)

    def body(x_ref, o_ref, carry):
        @pl.when(pl.program_id(0) == 0)
        def _():
            carry[...] = jnp.zeros_like(carry)

        x = x_ref[...]
        lf = jnp.minimum(x, 0.0) - jnp.log1p(jnp.exp(-jnp.abs(x)))
        o_ref[...] = jnp.dot(_tri(tb, False), lf, precision=HIGHEST, preferred_element_type=f32) + carry[...]
        carry[...] += jnp.sum(lf, axis=0, keepdims=True)

    blk = pl.BlockSpec((tb, LANES), lambda i: (i, 0))
    return pl.pallas_call(body, name=name, grid=(S // tb,), in_specs=[blk], out_specs=blk,
                          out_shape=jax.ShapeDtypeStruct((S, LANES), f32), scratch_shapes=[pltpu.VMEM((1, LANES), f32)],
                          compiler_params=_params("arbitrary"))(tail)


def _fox_prep_bwd(tail, dc_q, dc_k, name):
    S = tail.shape[0]
    tb = _tile(S, 512)
    nb = S // tb

    def body(x_ref, dq_ref, d_ref, o_ref, carry):
        @pl.when(pl.program_id(0) == 0)
        def _():
            carry[...] = jnp.zeros_like(carry)

        d = d_ref[...] + dq_ref[...]
        dlf = jnp.dot(_tri(tb, True), d, precision=HIGHEST, preferred_element_type=f32) + carry[...]
        carry[...] += jnp.sum(d, axis=0, keepdims=True)
        o_ref[...] = dlf * jax.nn.sigmoid(-x_ref[...])

    blk = pl.BlockSpec((tb, LANES), lambda i: (nb - 1 - i, 0))
    return pl.pallas_call(body, name=name, grid=(nb,), in_specs=[blk, blk, blk], out_specs=blk,
                          out_shape=jax.ShapeDtypeStruct((S, LANES), f32), scratch_shapes=[pltpu.VMEM((1, LANES), f32)],
                          compiler_params=_params("arbitrary"))(tail, dc_q, dc_k)


def _dot_nt(a, b):
    return lax.dot_general(a.astype(bf16), b.astype(bf16), _DIMS["nt"], preferred_element_type=f32)


def _dot_tn(a, b):
    return lax.dot_general(a.astype(bf16), b.astype(bf16), _DIMS["tn"], preferred_element_type=f32)


def _dot_nn(a, b):
    return lax.dot_general(a.astype(bf16), b.astype(bf16), _DIMS["nn"], preferred_element_type=f32)


def _fox_logits(q, k, cc, cr, T, diagonal):
    s = _dot_nt(q * (HEAD_DIM ** -0.5), k) + cc - cr
    if not diagonal:
        return s
    return jnp.where(lax.broadcasted_iota(jnp.int32, (T, T), 0) >= lax.broadcasted_iota(jnp.int32, (T, T), 1), s, NEG_BIG)


def _fox_fwd(proj, c_col, c_row, H, name, cargo=None):
    S = proj.shape[0]
    T = _tile(S, ATTN_TILE)
    nt = S // T

    def body(*refs):
        (q_ref, k_ref, v_ref, cc_ref, cr_ref), (o_ref, lse_ref), (m_s, l_s, acc_s), hold = _split_refs(refs, 5, 2, cargo)
        h, i, j = pl.program_id(0), pl.program_id(1), pl.program_id(2)
        _cargo_start(cargo, (h == 0) & (i == 0) & (j == 0), hold)

        @pl.when(j == 0)
        def _():
            m_s[...] = jnp.full_like(m_s, NEG_BIG)
            l_s[...] = jnp.zeros_like(l_s)
            acc_s[...] = jnp.zeros_like(acc_s)

        def block(diagonal):
            s = _fox_logits(q_ref[...], k_ref[...], cc_ref[...], cr_ref[...], T, diagonal)
            m_new = jnp.maximum(m_s[...], jnp.max(s, axis=-1, keepdims=True))
            p = jnp.exp(s - m_new)
            corr = jnp.exp(m_s[...] - m_new)
            l_s[...] = corr * l_s[...] + jnp.sum(p, axis=-1, keepdims=True)
            acc_s[...] = corr * acc_s[...] + _dot_nn(p, v_ref[...])
            m_s[...] = m_new

        @pl.when(j < i)
        def _():
            block(False)

        @pl.when(j == i)
        def _():
            block(True)
            o_ref[...] = acc_s[...] / l_s[...]
            lse_ref[...] = m_s[...] + jnp.log(l_s[...])

        _cargo_wait(cargo, (h == H - 1) & (i == nt - 1) & (j == nt - 1), hold)

    sd = jax.ShapeDtypeStruct
    c_in, c_out, c_shape, c_sems = _cargo_specs(cargo)
    sem = ("arbitrary",) * 3 if cargo else ("parallel", "parallel", "arbitrary")
    return pl.pallas_call(
        body, name=name, grid=(H, nt, nt),
        in_specs=[pl.BlockSpec((T, HEAD_DIM), lambda h, i, j: (i, h)),
                  pl.BlockSpec((T, HEAD_DIM), lambda h, i, j: (jnp.minimum(j, i), H + h)),
                  pl.BlockSpec((T, HEAD_DIM), lambda h, i, j: (jnp.minimum(j, i), 2 * H + h)),
                  pl.BlockSpec((None, T, 1), lambda h, i, j: (h, i, 0)),
                  pl.BlockSpec((None, 1, T), lambda h, i, j: (h, 0, jnp.minimum(j, i)))] + c_in,
        out_specs=[pl.BlockSpec((T, HEAD_DIM), lambda h, i, j: (i, h)), pl.BlockSpec((None, T, 1), lambda h, i, j: (h, i, 0))] + c_out,
        out_shape=[sd((S, H * HEAD_DIM), f32), sd((H, S, 1), f32)] + c_shape,
        scratch_shapes=[pltpu.VMEM((T, 1), f32), pltpu.VMEM((T, 1), f32), pltpu.VMEM((T, HEAD_DIM), f32)] + c_sems,
        compiler_params=_params(*sem, vmem=_attn_vmem(T)))(proj, proj, proj, c_col, c_row, *(cargo.arrays if cargo else ()))


def _fox_bwd_q(proj, c_col, c_row, o, do, lse, H, name):
    S = proj.shape[0]
    T = _tile(S, ATTN_TILE)
    nt = S // T

    def body(q_ref, k_ref, v_ref, cc_ref, cr_ref, o_ref, do_ref, lse_ref, dq_ref, dc_ref, acc_s, dc_s):
        i, j = pl.program_id(1), pl.program_id(2)

        @pl.when(j == 0)
        def _():
            acc_s[...] = jnp.zeros_like(acc_s)
            dc_s[...] = jnp.zeros_like(dc_s)

        def block(diagonal):
            s = _fox_logits(q_ref[...], k_ref[...], cc_ref[...], cr_ref[...], T, diagonal)
            p = jnp.exp(s - lse_ref[...])
            do_ = do_ref[...]
            delta = jnp.sum(o_ref[...] * do_, axis=-1, keepdims=True)
            ds = p * (_dot_nt(do_, v_ref[...]) - delta)
            acc_s[...] += _dot_nn(ds, k_ref[...])
            dc_s[...] += jnp.sum(ds, axis=-1, keepdims=True)

        @pl.when(j < i)
        def _():
            block(False)

        @pl.when(j == i)
        def _():
            block(True)
            dq_ref[...] = acc_s[...] * (HEAD_DIM ** -0.5)
            dc_ref[...] = dc_s[...]

    qb = pl.BlockSpec((T, HEAD_DIM), lambda h, i, j: (i, h))
    col = pl.BlockSpec((None, T, 1), lambda h, i, j: (h, i, 0))
    return pl.pallas_call(
        body, name=name, grid=(H, nt, nt),
        in_specs=[qb, pl.BlockSpec((T, HEAD_DIM), lambda h, i, j: (jnp.minimum(j, i), H + h)),
                  pl.BlockSpec((T, HEAD_DIM), lambda h, i, j: (jnp.minimum(j, i), 2 * H + h)),
                  col, pl.BlockSpec((None, 1, T), lambda h, i, j: (h, 0, jnp.minimum(j, i))), qb, qb, col],
        out_specs=[qb, col], out_shape=[jax.ShapeDtypeStruct((S, H * HEAD_DIM), f32), jax.ShapeDtypeStruct((H, S, 1), f32)],
        scratch_shapes=[pltpu.VMEM((T, HEAD_DIM), f32), pltpu.VMEM((T, 1), f32)],
        compiler_params=_params("parallel", "parallel", "arbitrary", vmem=_attn_vmem(T)))(proj, proj, proj, c_col, c_row, o, do, lse)


def _fox_bwd_kv(proj, c_col, c_row, o, do, lse, H, name):
    S = proj.shape[0]
    T = _tile(S, ATTN_TILE)
    nt = S // T

    def body(q_ref, k_ref, v_ref, cc_ref, cr_ref, o_ref, do_ref, lse_ref, dk_ref, dv_ref, dc_ref, dk_s, dv_s, dc_s):
        j, i = pl.program_id(1), pl.program_id(2)

        @pl.when(i == 0)
        def _():
            dk_s[...] = jnp.zeros_like(dk_s)
            dv_s[...] = jnp.zeros_like(dv_s)
            dc_s[...] = jnp.zeros_like(dc_s)

        def block(diagonal):
            s = _fox_logits(q_ref[...], k_ref[...], cc_ref[...], cr_ref[...], T, diagonal)
            p = jnp.exp(s - lse_ref[...])
            do_ = do_ref[...]
            delta = jnp.sum(o_ref[...] * do_, axis=-1, keepdims=True)
            ds = p * (_dot_nt(do_, v_ref[...]) - delta)
            dv_s[...] += _dot_tn(p, do_)
            dk_s[...] += _dot_tn(ds, q_ref[...])
            dc_s[...] -= jnp.sum(ds, axis=0, keepdims=True)

        @pl.when(i > j)
        def _():
            block(False)

        @pl.when(i == j)
        def _():
            block(True)

        @pl.when(i == nt - 1)
        def _():
            dk_ref[...] = dk_s[...] * (HEAD_DIM ** -0.5)
            dv_ref[...] = dv_s[...]
            dc_ref[...] = dc_s[...]

    qb = pl.BlockSpec((T, HEAD_DIM), lambda h, j, i: (jnp.maximum(i, j), h))
    col = pl.BlockSpec((None, T, 1), lambda h, j, i: (h, jnp.maximum(i, j), 0))
    kb = pl.BlockSpec((T, HEAD_DIM), lambda h, j, i: (j, h))
    sd = jax.ShapeDtypeStruct
    return pl.pallas_call(
        body, name=name, grid=(H, nt, nt),
        in_specs=[qb, pl.BlockSpec((T, HEAD_DIM), lambda h, j, i: (j, H + h)),
                  pl.BlockSpec((T, HEAD_DIM), lambda h, j, i: (j, 2 * H + h)),
                  col, pl.BlockSpec((None, 1, T), lambda h, j, i: (h, 0, j)), qb, qb, col],
        out_specs=[kb, kb, pl.BlockSpec((None, 1, T), lambda h, j, i: (h, 0, j))],
        out_shape=[sd((S, H * HEAD_DIM), f32), sd((S, H * HEAD_DIM), f32), sd((H, 1, S), f32)],
        scratch_shapes=[pltpu.VMEM((T, HEAD_DIM), f32), pltpu.VMEM((T, HEAD_DIM), f32), pltpu.VMEM((1, T), f32)],
        compiler_params=_params("parallel", "parallel", "arbitrary", vmem=_attn_vmem(T)))(proj, proj, proj, c_col, c_row, o, do, lse)


def _adamw(w, g, m, v, name):
    shape = w.shape
    cols = shape[-1]
    rows = w.size // cols
    ops = [t.reshape(rows, cols) for t in (w, g, m, v)]
    tr = rows
    if rows % 8 == 0:
        tr = 8
        while tr * 2 <= rows and rows % (tr * 2) == 0 and tr * 2 * cols * 4 <= (1 << 20):
            tr *= 2

    def body(w_ref, g_ref, m_ref, v_ref, d_ref, mo_ref, vo_ref):
        g_ = g_ref[...]
        m_ = ADAM_B1 * m_ref[...] + (1.0 - ADAM_B1) * g_
        v_ = ADAM_B2 * v_ref[...] + (1.0 - ADAM_B2) * (g_ * g_)
        m_hat = m_ / (1.0 - ADAM_B1 ** ADAM_STEP)
        v_hat = v_ / (1.0 - ADAM_B2 ** ADAM_STEP)
        d_ref[...] = -ADAM_LR * (m_hat / (jnp.sqrt(v_hat) + ADAM_EPS) + ADAM_WD * w_ref[...])
        mo_ref[...] = m_
        vo_ref[...] = v_

    blk = pl.BlockSpec((tr, cols), lambda i: (i, 0))
    outs = pl.pallas_call(body, name=name, grid=(rows // tr,), in_specs=[blk] * 4, out_specs=[blk] * 3,
                          out_shape=[jax.ShapeDtypeStruct((rows, cols), f32)] * 3, compiler_params=_params("parallel"))(*ops)
    return [o.reshape(shape) for o in outs]


def _row_tile(rows, row_bytes, budget=2 << 20):
    best = None
    for t in range(16, rows + 1, 16):
        if rows % t == 0 and t * row_bytes <= budget:
            best = t
    return best or rows


def _sum_leading(a, name):
    n, R, C = a.shape
    tr = _row_tile(R, n * C * 4)

    def body(a_ref, o_ref):
        acc = a_ref[0].astype(f32)
        for k in range(1, n):
            acc = acc + a_ref[k].astype(f32)
        o_ref[...] = acc

    return pl.pallas_call(body, name=name, grid=(R // tr,), in_specs=[pl.BlockSpec((n, tr, C), lambda i: (0, i, 0))],
                          out_specs=pl.BlockSpec((tr, C), lambda i: (i, 0)), out_shape=jax.ShapeDtypeStruct((R, C), f32),
                          compiler_params=_params("parallel"))(a)


def _add2(a, b, dtype, name):
    n, R, C = a.shape
    rows = n * R
    tr = _row_tile(rows, C * 4)

    def body(a_ref, b_ref, o_ref):
        o_ref[...] = (a_ref[...] + b_ref[...]).astype(dtype)

    blk = pl.BlockSpec((tr, C), lambda i: (i, 0))
    out = pl.pallas_call(body, name=name, grid=(rows // tr,), in_specs=[blk, blk], out_specs=blk,
                         out_shape=jax.ShapeDtypeStruct((rows, C), dtype),
                         compiler_params=_params("parallel"))(a.reshape(rows, C), b.reshape(rows, C))
    return out.reshape(n, R, C)


def _place():
    x, y, c = lax.axis_index("x"), lax.axis_index("y"), lax.axis_index("c")
    chips = [(1 - x, y), (x, 1 - y), (1 - x, 1 - y)]
    return x, y, c, chips


def _gather_copies(ws_refs, out_refs, send_sems, recv_sems):
    x, y, c, chips = _place()
    sends, lands = [], []
    for t, (ws_ref, out_ref) in enumerate(zip(ws_refs, out_refs)):
        for j, chip in enumerate(chips):
            make = functools.partial(pltpu.make_async_remote_copy, src_ref=ws_ref.at[c], send_sem=send_sems.at[3 * t + j],
                                     recv_sem=recv_sems.at[3 * t + j], device_id=(*chip, c), device_id_type=MESH)
            sends.append(functools.partial(make, dst_ref=out_ref.at[2 * x + y, c]))
            lands.append(functools.partial(make, dst_ref=out_ref.at[2 * chip[0] + chip[1], c]))
    return sends, lands


def _gather_cargo(blocks):
    return _Cargo(tuple(blocks), tuple(jax.ShapeDtypeStruct((N_CHIPS,) + b.shape, b.dtype) for b in blocks), _gather_copies,
                  3 * len(blocks))


def _sibling_forward(gathered, blocks, name):
    n = len(gathered)

    def body(*refs):
        ins, own, outs, send_sems, recv_sems = refs[:n], refs[n:2 * n], refs[2 * n:3 * n], refs[3 * n], refs[3 * n + 1]
        x, y, c, chips = _place()
        sends, lands = [], []
        for t in range(n):
            for j, chip in enumerate(chips):
                s = 2 * chip[0] + chip[1]
                make = functools.partial(pltpu.make_async_remote_copy, src_ref=ins[t].at[s, c], send_sem=send_sems.at[4 * t + j],
                                         recv_sem=recv_sems.at[4 * t + j], device_id=(x, y, 1 - c), device_id_type=MESH)
                sends.append(make(dst_ref=outs[t].at[s, c]))
                lands.append(make(dst_ref=outs[t].at[s, 1 - c]))
            make = functools.partial(pltpu.make_async_remote_copy, src_ref=own[t], dst_ref=outs[t].at[2 * x + y],
                                     send_sem=send_sems.at[4 * t + 3], recv_sem=recv_sems.at[4 * t + 3], device_id=(x, y, 1 - c),
                                     device_id_type=MESH)
            sends.append(make())
            lands.append(make())
        for cp in sends:
            cp.start()
        for cp in lands:
            cp.wait_recv()
        for cp in sends:
            cp.wait_send()

    return pl.pallas_call(body, name=name, out_shape=[jax.ShapeDtypeStruct(g.shape, g.dtype) for g in gathered],
                          in_specs=[ANY] * (2 * n), out_specs=[ANY] * n, input_output_aliases={t: t for t in range(n)},
                          scratch_shapes=[pltpu.SemaphoreType.DMA((4 * n,)), pltpu.SemaphoreType.DMA((4 * n,))])(*gathered, *blocks)


def _pair_exchange(g, name):
    n, _, R, C = g.shape

    def body(g_ref, out_ref, send_sems, recv_sems):
        x, y, c, _ = _place()
        cps = [pltpu.make_async_remote_copy(src_ref=g_ref.at[s, 1 - c], dst_ref=out_ref.at[s], send_sem=send_sems.at[s],
                                            recv_sem=recv_sems.at[s], device_id=(x, y, 1 - c), device_id_type=MESH)
               for s in range(n)]
        for cp in cps:
            cp.start()
        for cp in cps:
            cp.wait()

    return pl.pallas_call(body, name=name, out_shape=jax.ShapeDtypeStruct((n, R, C), g.dtype), in_specs=[ANY], out_specs=ANY,
                          scratch_shapes=[pltpu.SemaphoreType.DMA((n,)), pltpu.SemaphoreType.DMA((n,))])(g)


def _chip_copies(a_refs, out_refs, send_sems, recv_sems):
    x, y, c, chips = _place()
    me = 2 * x + y
    sends, lands = [], []
    for t, (a_ref, out_ref) in enumerate(zip(a_refs, out_refs)):
        for j, chip in enumerate(chips):
            them = 2 * chip[0] + chip[1]
            make = functools.partial(pltpu.make_async_remote_copy, send_sem=send_sems.at[3 * t + j], recv_sem=recv_sems.at[3 * t + j],
                                     device_id=(*chip, c), device_id_type=MESH)
            sends.append(functools.partial(make, src_ref=a_ref.at[them], dst_ref=out_ref.at[me]))
            lands.append(functools.partial(make, src_ref=a_ref.at[me], dst_ref=out_ref.at[them]))
    return sends, lands


def _chip_cargo(pairs):
    return _Cargo(tuple(pairs), tuple(jax.ShapeDtypeStruct(p.shape, p.dtype) for p in pairs), _chip_copies, 3 * len(pairs))


def _pair_swap(r, name):
    R, C = r.shape

    def body(r_ref, out_ref, send_sem, recv_sem):
        x, y, c, _ = _place()
        cp = pltpu.make_async_remote_copy(src_ref=r_ref, dst_ref=out_ref, send_sem=send_sem, recv_sem=recv_sem,
                                          device_id=(x, y, 1 - c), device_id_type=MESH)
        cp.start()
        cp.wait()

    return pl.pallas_call(body, name=name, out_shape=jax.ShapeDtypeStruct((R, C), r.dtype), in_specs=[ANY], out_specs=ANY,
                          scratch_shapes=[pltpu.SemaphoreType.DMA(()), pltpu.SemaphoreType.DMA(())])(r)


def _all_gather8(v, name):
    R, C = v.shape

    def body(v_ref, out_ref, send_sems, recv_sems):
        x, y, c, _ = _place()
        me = 4 * x + 2 * y + c
        peers = [(x ^ (k >> 2), y ^ ((k >> 1) & 1), c ^ (k & 1)) for k in range(1, N_DEV)]
        sends = [pltpu.make_async_remote_copy(src_ref=v_ref, dst_ref=out_ref.at[me], send_sem=send_sems.at[k], recv_sem=recv_sems.at[k],
                                              device_id=peer, device_id_type=MESH) for k, peer in enumerate(peers)]
        for cp in sends:
            cp.start()
        for k, (px, py, pc) in enumerate(peers):
            pltpu.make_async_remote_copy(src_ref=v_ref, dst_ref=out_ref.at[4 * px + 2 * py + pc], send_sem=send_sems.at[k],
                                         recv_sem=recv_sems.at[k], device_id=(px, py, pc), device_id_type=MESH).wait_recv()
        for cp in sends:
            cp.wait_send()

    return pl.pallas_call(body, name=name, out_shape=jax.ShapeDtypeStruct((N_DEV, R, C), v.dtype), in_specs=[ANY], out_specs=ANY,
                          scratch_shapes=[pltpu.SemaphoreType.DMA((7,)), pltpu.SemaphoreType.DMA((7,))])(v)


def _as_list(x):
    return list(x) if isinstance(x, (list, tuple)) else [x]


def _put(buf, block, index):
    return lax.dynamic_update_slice(buf, block[None], (index,) + (0,) * block.ndim)


def _all_reduce8(v, device, name):
    n = v.shape[0]
    rows = -(-n // (LANES * SUM_ROWS)) * SUM_ROWS
    padded = jnp.pad(v, (0, rows * LANES - n)).reshape(rows, LANES)
    return _sum_leading(_put(_all_gather8(padded, name + "_gather"), padded, device), name + "_sum").reshape(-1)[:n]


def _halves(w):
    R, C = w.shape
    return w.astype(bf16).reshape(2, R // 2, C)


def _finish_gather(blocks, landed):
    full = _sibling_forward(landed, blocks, "gather_sibling_forward")
    return [g.reshape(N_CHIPS, 2 * b.shape[1], b.shape[2]) for g, b in zip(full, blocks)]


def _pair_sums(g4, core, name):
    _, R, C = g4.shape
    g = g4.reshape(N_CHIPS, 2, R // 2, C)
    mine = lax.dynamic_index_in_dim(g, core, axis=1, keepdims=False)
    return _add2(mine, _pair_exchange(g, name + "_pair_exchange"), bf16, name + "_pair_sum")


def _finish_reduce(pair, landed, chip, core, name):
    _, R2, C = pair.shape
    own = lax.dynamic_index_in_dim(pair, chip, axis=0, keepdims=False)
    half = _sum_leading(_put(landed, own, chip), name + "_chip_sum")
    both = jnp.stack([half, _pair_swap(half, name + "_pair_swap")])
    return jnp.where(core == 0, both, both[::-1]).reshape(2 * R2, C)


def _segments(H, D):
    W = H * HEAD_DIM
    sizes = (3 * W, H, 2 * W, 3 * W, H, H, W, 3 * D)
    in_tail = (False, True, False, False, True, True, False, False)
    out, first, used = [], 0, [0, 0]
    for size, t in zip(sizes, in_tail):
        out.append((first, size, t, used[t]))
        first += size
        used[t] += size
    return out


def _main_tail_from_shards(g4, H, D):
    C = g4.shape[-1]
    parts = ([], [])
    for first, size, t, _ in _segments(H, D):
        for s in range(g4.shape[0]):
            a, b = max(first, s * C), min(first + size, (s + 1) * C)
            if a < b:
                parts[t].append(g4[s][..., a - s * C:b - s * C])
    parts[1].append(jnp.zeros(g4.shape[1:-1] + (LANES - 3 * H,), g4.dtype))
    return jnp.concatenate(parts[0], axis=-1), jnp.concatenate(parts[1], axis=-1)


def _shards_from_main_tail(main, tail, H, D):
    segs = _segments(H, D)
    C = sum(size for _, size, _, _ in segs) // N_CHIPS
    shards = []
    for s in range(N_CHIPS):
        pieces = []
        for first, size, t, there in segs:
            a, b = max(first, s * C), min(first + size, (s + 1) * C)
            if a < b:
                pieces.append((tail if t else main)[..., there + a - first:there + b - first])
        shards.append(jnp.concatenate(pieces, axis=-1))
    return jnp.stack(shards)


def _pad_cols(a, n):
    return jnp.pad(a, [(0, 0)] * (a.ndim - 1) + [(0, n - a.shape[-1])])


def kernel(x, w_in, b_in, sgu_ln_g, sgu_ln_b, sgu_w, sgu_b, gdn_conv_w, gdn_a_log, gdn_dt_bias, gdn_norm_g, w_proj_a, w_proj_b, w_proj_c, w_out, ln1_g, ln1_b, ffn_w_up, ffn_conv_w, ffn_conv_b, ffn_w_down, ln2_g, ln2_b, loss_target, m_w_in, m_b_in, m_sgu_ln_g, m_sgu_ln_b, m_sgu_w, m_sgu_b, m_gdn_conv_w, m_gdn_a_log, m_gdn_dt_bias, m_gdn_norm_g, m_w_proj_a, m_w_proj_b, m_w_proj_c, m_w_out, m_ln1_g, m_ln1_b, m_ffn_w_up, m_ffn_conv_w, m_ffn_conv_b, m_ffn_w_down, m_ln2_g, m_ln2_b, v_w_in, v_b_in, v_sgu_ln_g, v_sgu_ln_b, v_sgu_w, v_sgu_b, v_gdn_conv_w, v_gdn_a_log, v_gdn_dt_bias, v_gdn_norm_g, v_w_proj_a, v_w_proj_b, v_w_proj_c, v_w_out, v_ln1_g, v_ln1_b, v_ffn_w_up, v_ffn_conv_w, v_ffn_conv_b, v_ffn_w_down, v_ln2_g, v_ln2_b):
    P = dict(w_in=w_in, b_in=b_in, sgu_ln_g=sgu_ln_g, sgu_ln_b=sgu_ln_b, sgu_w=sgu_w, sgu_b=sgu_b, gdn_conv_w=gdn_conv_w,
             gdn_a_log=gdn_a_log, gdn_dt_bias=gdn_dt_bias, gdn_norm_g=gdn_norm_g, w_proj_a=w_proj_a, w_proj_b=w_proj_b,
             w_proj_c=w_proj_c, w_out=w_out, ln1_g=ln1_g, ln1_b=ln1_b, ffn_w_up=ffn_w_up, ffn_conv_w=ffn_conv_w,
             ffn_conv_b=ffn_conv_b, ffn_w_down=ffn_w_down, ln2_g=ln2_g, ln2_b=ln2_b)
    M1 = dict(w_in=m_w_in, b_in=m_b_in, sgu_ln_g=m_sgu_ln_g, sgu_ln_b=m_sgu_ln_b, sgu_w=m_sgu_w, sgu_b=m_sgu_b,
              gdn_conv_w=m_gdn_conv_w, gdn_a_log=m_gdn_a_log, gdn_dt_bias=m_gdn_dt_bias, gdn_norm_g=m_gdn_norm_g,
              w_proj_a=m_w_proj_a, w_proj_b=m_w_proj_b, w_proj_c=m_w_proj_c, w_out=m_w_out, ln1_g=m_ln1_g, ln1_b=m_ln1_b,
              ffn_w_up=m_ffn_w_up, ffn_conv_w=m_ffn_conv_w, ffn_conv_b=m_ffn_conv_b, ffn_w_down=m_ffn_w_down, ln2_g=m_ln2_g,
              ln2_b=m_ln2_b)
    M2 = dict(w_in=v_w_in, b_in=v_b_in, sgu_ln_g=v_sgu_ln_g, sgu_ln_b=v_sgu_ln_b, sgu_w=v_sgu_w, sgu_b=v_sgu_b,
              gdn_conv_w=v_gdn_conv_w, gdn_a_log=v_gdn_a_log, gdn_dt_bias=v_gdn_dt_bias, gdn_norm_g=v_gdn_norm_g,
              w_proj_a=v_w_proj_a, w_proj_b=v_w_proj_b, w_proj_c=v_w_proj_c, w_out=v_w_out, ln1_g=v_ln1_g, ln1_b=v_ln1_b,
              ffn_w_up=v_ffn_w_up, ffn_conv_w=v_ffn_conv_w, ffn_conv_b=v_ffn_conv_b, ffn_w_down=v_ffn_w_down, ln2_g=v_ln2_g,
              ln2_b=v_ln2_b)
    _, S, D = x.shape
    L = w_in.shape[0]
    N_IN = w_in.shape[2] * N_CHIPS
    H = (N_IN - 3 * D) // (9 * HEAD_DIM + 3)
    W = H * HEAD_DIM
    F = ffn_w_down.shape[1] * N_CHIPS
    Fp = -(-F // FF_ALIGN) * FF_ALIGN
    NM = 9 * W + 3 * D
    alpha = (2 * L) ** 0.25
    cx, cy, cc = lax.axis_index("x"), lax.axis_index("y"), lax.axis_index("c")
    chip = 2 * cx + cy

    def blocks_of(l):
        return [_halves(w_in[l]), _halves(jnp.concatenate([w_proj_a[l], w_proj_b[l], w_proj_c[l]], axis=0)),
                _halves(jnp.concatenate([w_out[l], ffn_w_down[l]], axis=0)), _halves(ffn_w_up[l])]

    def full_weights(blocks, landed):
        g_in, g_proj, g_rows, g_up = _finish_gather(blocks, landed)
        w_main, w_tail = _main_tail_from_shards(g_in, H, D)
        wa, wb, wc = [g_proj[:, k * W:(k + 1) * W].transpose(1, 0, 2).reshape(W, D) for k in range(3)]
        return dict(w_main=w_main, w_tail=w_tail, wa=wa, wb=wb, wc=wc, wo=g_rows[:, :D // N_CHIPS].reshape(D, D),
                    wd=jnp.pad(g_rows[:, D // N_CHIPS:].reshape(F, D), ((0, Fp - F), (0, 0))),
                    wg=_pad_cols(jnp.concatenate([g_up[0], g_up[1]], axis=1), Fp),
                    wv=_pad_cols(jnp.concatenate([g_up[2], g_up[3]], axis=1), Fp))

    blocks = blocks_of(0)
    weights = full_weights(blocks, _ship(_gather_cargo(blocks), "gather_first_layer"))
    gcw_cols, fcw_cols = gdn_conv_w.shape[2], ffn_conv_w.shape[2]
    only_south = (cc == 0).astype(f32)
    placed_g = lax.dynamic_update_slice(jnp.zeros((L, GDN_CONV, 3 * W), f32), gdn_conv_w * only_south, (0, 0, chip * gcw_cols))
    placed_f = lax.dynamic_update_slice(jnp.zeros((L, FFN_CONV, 2 * F), f32), ffn_conv_w * only_south, (0, 0, chip * fcw_cols))
    conv_all = _all_reduce8(jnp.concatenate([placed_g.reshape(-1), placed_f.reshape(-1)]), 2 * chip + cc, "conv_weights")
    gcw_full = conv_all[:L * GDN_CONV * 3 * W].reshape(L, GDN_CONV, 1, 3 * W)
    fcw_full = conv_all[L * GDN_CONV * 3 * W:].reshape(L, FFN_CONV, 1, 2 * F)

    saved = []
    h = x.reshape(S, D)
    for l in range(L):
        b_main, b_tail = _main_tail_from_shards(b_in[l][None, None, :], H, D)
        cwg, cwv = _pad_cols(fcw_full[l][..., :F], Fp), _pad_cols(fcw_full[l][..., F:], Fp)
        cbg, cbv = _pad_cols(ffn_conv_b[l][None, :F], Fp), _pad_cols(ffn_conv_b[l][None, F:], Fp)
        nxt = blocks_of(l + 1) if l + 1 < L else None
        ride = (lambda *idx: _gather_cargo([nxt[k] for k in idx])) if nxt else (lambda *idx: None)
        lw = dict(weights, cwg=cwg, cwv=cwv, cbg=cbg, cbv=cbv, gcw=gcw_full[l],
                  sgu_ln_g=sgu_ln_g[l][None, :], sgu_ln_b=sgu_ln_b[l][None, :], sgu_w=sgu_w[l], sgu_b=sgu_b[l][:, :, None],
                  a_log=gdn_a_log[l].reshape(H, 1, 1), dt_bias=gdn_dt_bias[l].reshape(H, 1, 1),
                  norm_g=gdn_norm_g[l].reshape(1, 1, HEAD_DIM), ln1_g=ln1_g[l][None, :], ln1_b=ln1_b[l][None, :],
                  ln2_g=ln2_g[l][None, :], ln2_b=ln2_b[l][None, :])
        tag = "_carrying" if nxt else ""
        proj, *land_in = _as_list(_matmul(h, lw["w_main"], "nn", "proj_main" + tag, bias=b_main, cargo=ride(0)))
        tail = _matmul(h, lw["w_tail"], "nn", "proj_tail", bias=b_tail)
        csum = _fox_prep_fwd(tail, "fox_prep")
        c_col = csum[:, :H].T[:, :, None]
        c_row = csum[:, :H].T[:, None, :]
        y_a, lse, *land_up = _fox_fwd(proj, c_col, c_row, H, "fox_fwd" + tag, cargo=ride(3))
        y_b = _sgu_fwd(proj, lw["sgu_ln_g"], lw["sgu_ln_b"], lw["sgu_w"], lw["sgu_b"], 3 * W, 4 * W, "sgu_fwd")
        qkvc = _gdn_pre_fwd(proj, lw["gcw"], 5 * W, 2 * H, "gdn_pre")
        al = tail[:, H:2 * H].T[:, :, None]
        bl = tail[:, 2 * H:3 * H].T[:, :, None]
        y_c, states, *land_rest = _gdn_scan_fwd(qkvc, al, bl, proj, lw["a_log"], lw["dt_bias"], lw["norm_g"], 8 * W, "gdn_scan" + tag,
                                                cargo=ride(1, 2))
        if nxt:
            weights = full_weights(nxt, land_in + [land_rest[0], land_rest[1]] + land_up)
        pa = _matmul(y_a, lw["wa"], "nn", "branch_proj")
        pb = _matmul(y_b, lw["wb"], "nn", "branch_proj")
        pc = _matmul(y_c, lw["wc"], "nn", "branch_proj")
        merged = _merge_fwd(proj, pa, pb, pc, 9 * W, "merge")
        mix = _matmul(merged, lw["wo"], "nn", "out_proj")
        x1 = _ln_fwd(h, mix, lw["ln1_g"], lw["ln1_b"], alpha, "ln")
        upg = _matmul(x1, lw["wg"], "nn", "ffn_up")
        upv = _matmul(x1, lw["wv"], "nn", "ffn_up")
        act = _ffn_act_fwd(upg, upv, cwg, cwv, cbg, cbv, "ffn_act")
        ffn = _matmul(act, lw["wd"], "nn", "ffn_down")
        x2 = _ln_fwd(x1, ffn, lw["ln2_g"], lw["ln2_b"], alpha, "ln")
        saved.append(dict(lw=lw, h=h, proj=proj, tail=tail, c_col=c_col, c_row=c_row, y_a=y_a, lse=lse, y_b=y_b, qkvc=qkvc, al=al,
                          bl=bl, y_c=y_c, states=states, pa=pa, pb=pb, pc=pc, merged=merged, mix=mix, x1=x1, upg=upg, upv=upv,
                          act=act, ffn=ffn))
        h = x2

    loss_part, dh = _loss_head(h, loss_target.reshape(S, D), "loss_head")
    loss = lax.psum(loss_part[0, 0], ("x", "y", "c"))

    reduced = [dict() for _ in range(L)]
    small_grads = [None] * L
    pending = None
    for l in reversed(range(L)):
        s = saved[l]
        lw = s["lw"]
        d_x1r, d_ffn, d_ln2g, d_ln2b = _ln_bwd(s["x1"], s["ffn"], lw["ln2_g"], lw["ln2_b"], dh, alpha, "ln_bwd")
        d_act = _matmul(d_ffn, lw["wd"], "nt", "ffn_down_dx")
        d_wd = _matmul(s["act"], d_ffn, "tn", "ffn_down_dw")
        dupg, dupv, dcwg, dcwv, dcbg, dcbv = _ffn_act_bwd(s["upg"], s["upv"], lw["cwg"], lw["cwv"], lw["cbg"], lw["cbv"], d_act,
                                                          "ffn_act_bwd")
        d_x1 = _matmul(dupg, lw["wg"], "nt", "ffn_up_dx", add=d_x1r)
        d_x1 = _matmul(dupv, lw["wv"], "nt", "ffn_up_dx", add=d_x1)
        d_wg = _matmul(s["x1"], dupg, "tn", "ffn_up_dw")
        d_wv = _matmul(s["x1"], dupv, "tn", "ffn_up_dw")
        d_hr, d_mix, d_ln1g, d_ln1b = _ln_bwd(s["h"], s["mix"], lw["ln1_g"], lw["ln1_b"], d_x1, alpha, "ln_bwd")
        d_merged = _matmul(d_mix, lw["wo"], "nt", "out_proj_dx")
        d_wo = _matmul(s["merged"], d_mix, "tn", "out_proj_dw")
        dg0, dg1, dg2, d_pa, d_pb, d_pc = _merge_bwd(s["proj"], s["pa"], s["pb"], s["pc"], d_merged, 9 * W, "merge_bwd")
        d_ya = _matmul(d_pa, lw["wa"], "nt", "branch_proj_dx")
        d_yb = _matmul(d_pb, lw["wb"], "nt", "branch_proj_dx")
        d_yc = _matmul(d_pc, lw["wc"], "nt", "branch_proj_dx")
        d_wa = _matmul(s["y_a"], d_pa, "tn", "branch_proj_dw")
        d_wb = _matmul(s["y_b"], d_pb, "tn", "branch_proj_dw")
        d_wc = _matmul(s["y_c"], d_pc, "tn", "branch_proj_dw")
        by_cols = lambda g: g.reshape(g.shape[0], N_CHIPS, g.shape[1] // N_CHIPS).transpose(1, 0, 2)
        early = dict(
            proj=jnp.concatenate([by_cols(d_wa), by_cols(d_wb), by_cols(d_wc)], axis=1),
            rows=jnp.concatenate([d_wo.reshape(N_CHIPS, D // N_CHIPS, D), d_wd[:F].reshape(N_CHIPS, F // N_CHIPS, D)], axis=1),
            w_up=jnp.stack([d_wg[:, :F // 2], d_wg[:, F // 2:F], d_wv[:, :F // 2], d_wv[:, F // 2:F]]))
        pairs = {k: _pair_sums(early[k], cc, "grad_" + k) for k in early}
        dqkvc, dal, dbl, dgate, d_alog, d_dt, d_ng, *landed = _gdn_scan_bwd(
            s["qkvc"], s["al"], s["bl"], s["proj"], lw["a_log"], lw["dt_bias"], lw["norm_g"], s["states"], d_yc, 8 * W,
            "gdn_scan_bwd", cargo=_chip_cargo([pairs[k] for k in early]))
        for k, got in zip(early, landed):
            reduced[l][k] = _finish_reduce(pairs[k], got, chip, cc, "grad_" + k)
        d_gqkv, d_gcw = _gdn_pre_bwd(s["proj"], lw["gcw"], dqkvc, 5 * W, 2 * H, "gdn_pre_bwd")
        d_u, d_v, d_slg, d_slb, d_sw, d_sb = _sgu_bwd(s["proj"], lw["sgu_ln_g"], lw["sgu_ln_b"], lw["sgu_w"], lw["sgu_b"], d_yb,
                                                      3 * W, 4 * W, "sgu_bwd")
        d_q, d_cq = _fox_bwd_q(s["proj"], s["c_col"], s["c_row"], s["y_a"], d_ya, s["lse"], H, "fox_bwd_q")
        d_k, d_v_att, d_c = _fox_bwd_kv(s["proj"], s["c_col"], s["c_row"], s["y_a"], d_ya, s["lse"], H, "fox_bwd_kv")
        d_f = _fox_prep_bwd(s["tail"], _pad_cols(d_cq[:, :, 0].T, LANES), _pad_cols(d_c[:, 0, :].T, LANES), "fox_prep_bwd")
        d_main = jnp.concatenate([d_q, d_k, d_v_att, d_u, d_v, d_gqkv, dgate, dg0, dg1, dg2], axis=1)
        d_tail = _pad_cols(jnp.concatenate([d_f[:, :H], dal[:, :, 0].T, dbl[:, :, 0].T], axis=1), LANES)
        d_wmain = _matmul(s["h"], d_main, "tn", "proj_main_dw")
        d_wtail = _matmul(s["h"], d_tail, "tn", "proj_tail_dw")
        d_bmain = _colsum(d_main, "proj_main_db")
        d_btail = _colsum(d_tail, "proj_tail_db")
        dh, *landed = _as_list(_matmul(d_main, lw["w_main"], "nt", "proj_main_dx" + ("_carrying" if pending else ""), add=d_hr,
                                       cargo=_chip_cargo([pending[1]]) if pending else None))
        if pending:
            reduced[pending[0]]["w_in"] = _finish_reduce(pending[1], landed[0], chip, cc, "grad_w_in")
        dh = _matmul(d_tail, lw["w_tail"], "nt", "proj_tail_dx", add=dh)
        pending = (l, _pair_sums(_shards_from_main_tail(d_wmain, d_wtail, H, D), cc, "grad_w_in"))
        small_grads[l] = dict(b_in=_shards_from_main_tail(d_bmain, d_btail, H, D).reshape(-1), sgu_ln_g=d_slg[0], sgu_ln_b=d_slb[0], sgu_w=d_sw,
                              sgu_b=d_sb[:, :, 0], gdn_conv_w=d_gcw[:, 0, :], gdn_a_log=d_alog[:, 0, 0], gdn_dt_bias=d_dt[:, 0, 0],
                              gdn_norm_g=d_ng[0, 0], ln1_g=d_ln1g[0], ln1_b=d_ln1b[0],
                              ffn_conv_w=jnp.concatenate([dcwg[:, 0, :F], dcwv[:, 0, :F]], axis=1),
                              ffn_conv_b=jnp.concatenate([dcbg[0, :F], dcbv[0, :F]]), ln2_g=d_ln2g[0], ln2_b=d_ln2b[0])
    grad_x = dh.reshape(1, S, D)

    landed = _ship(_chip_cargo([pending[1]]), "grad_chip_exchange")
    reduced[pending[0]]["w_in"] = _finish_reduce(pending[1], landed[0], chip, cc, "grad_w_in")
    grads = {n: [None] * L for n in WEIGHTS}
    for l in range(L):
        grads["w_in"][l] = reduced[l]["w_in"]
        for k, n in enumerate(("w_proj_a", "w_proj_b", "w_proj_c")):
            grads[n][l] = reduced[l]["proj"][k * W:(k + 1) * W]
        grads["w_out"][l] = reduced[l]["rows"][:D // N_CHIPS]
        grads["ffn_w_down"][l] = reduced[l]["rows"][D // N_CHIPS:]
        grads["ffn_w_up"][l] = reduced[l]["w_up"]
    small_shapes = {n: small_grads[0][n].shape for n in SMALL}
    small_flat = jnp.concatenate([small_grads[l][n].reshape(-1) for l in range(L) for n in SMALL])
    small_sum = _all_reduce8(small_flat, 2 * chip + cc, "small_grads")
    off = 0
    for l in range(L):
        for n in SMALL:
            size = math.prod(small_shapes[n])
            g = small_sum[off:off + size].reshape(small_shapes[n])
            off += size
            if n == "gdn_conv_w":
                g = lax.dynamic_slice_in_dim(g, chip * gcw_cols, gcw_cols, axis=1)
            elif n == "ffn_conv_w":
                g = lax.dynamic_slice_in_dim(g, chip * fcw_cols, fcw_cols, axis=1)
            grads[n][l] = g
    grads = {n: jnp.stack(grads[n]) for n in WEIGHTS}

    deltas, new_m, new_v = {}, {}, {}
    for n in WEIGHTS:
        deltas[n], new_m[n], new_v[n] = _adamw(P[n], grads[n], M1[n], M2[n], "adamw_" + n)
    return (loss, grad_x, *[grads[n] for n in WEIGHTS], *[deltas[n] for n in WEIGHTS], *[new_m[n] for n in WEIGHTS],
            *[new_v[n] for n in WEIGHTS])
```
